```python
import jax, jax.numpy as jnp
from jax import lax
import numpy as np

D_MODEL = 1024
BATCH = 8
SEQ = 2048
DEPTH = 1
DEC_BATCH = 128
DEC_SEQ = 1
PAST_LEN = 2048
PAGE_SIZE = 128

D_MIX = D_MODEL
ATTN_WIDTH = D_MIX // 2
POOL_WIDTH = D_MIX - ATTN_WIDTH
HEAD_DIM = 64
N_HEADS = ATTN_WIDTH // HEAD_DIM
KV_HEADS = 2
IDX_HEADS = 8
IDX_DIM = 64
TOPK_MAX = 256
Q_BLOCK = 128
POOL_WINDOWS = (2, 4, 8, 16)
POOL_GROUPS = len(POOL_WINDOWS)
POOL_GC = POOL_WIDTH // POOL_GROUPS
POOL_HIST = max(POOL_WINDOWS) - 1
EPS = 1e-6
N_PAGES = PAST_LEN // PAGE_SIZE
N_PHYS_PAGES = (DEC_BATCH * N_PAGES * 5) // 4
COLS = (ATTN_WIDTH, KV_HEADS * HEAD_DIM, KV_HEADS * HEAD_DIM, IDX_HEADS * IDX_DIM, IDX_DIM, IDX_HEADS,
        ATTN_WIDTH, POOL_WIDTH, POOL_WIDTH)
N_IN = sum(COLS)
SPLITS = tuple(int(s) for s in np.cumsum(COLS)[:-1])

kernel_name = 'hymba_dsa_pool_adaln_step'


def rms_norm(x, w):
    xf = x.astype(jnp.float32)
    y = xf * lax.rsqrt(jnp.mean(xf * xf, axis=-1, keepdims=True) + EPS)
    return y.astype(x.dtype) * w


def _sparse_attend_block(q, qi, wi, k, v, ki, q_pos, topk):
    B, Q, H, Dh = q.shape
    L = k.shape[1]
    k_pos = jnp.arange(L)
    s = jnp.einsum('bqhd,bld->bqhl', qi, ki)
    score = jnp.einsum('bqh,bqhl->bql', wi, jax.nn.relu(s)).astype(jnp.float32)
    causal = k_pos[None, :] <= q_pos[:, None]
    score = jnp.where(causal[None], score, -jnp.inf)
    _, sel = lax.top_k(score, topk)
    valid = sel <= q_pos[None, :, None]
    gather = jax.vmap(lambda t, i: t[i])
    kg = gather(k, sel)
    vg = gather(v, sel)
    qg = q.reshape(B, Q, KV_HEADS, H // KV_HEADS, Dh)
    logits = jnp.einsum('bqgrd,bqkgd->bqgrk', qg, kg).astype(jnp.float32) * (Dh ** -0.5)
    logits = jnp.where(valid[:, :, None, None, :], logits, -jnp.inf)
    p = jax.nn.softmax(logits, axis=-1).astype(v.dtype)
    o = jnp.einsum('bqgrk,bqkgd->bqgrd', p, vg)
    return o.reshape(B, Q, H * Dh)


def sparse_attention(q, qi, wi, k, v, ki, q_pos, topk):
    B, T = q.shape[:2]
    if T <= Q_BLOCK or T % Q_BLOCK != 0:
        return _sparse_attend_block(q, qi, wi, k, v, ki, q_pos, topk)
    nb = T // Q_BLOCK

    def to_blocks(a):
        return jnp.moveaxis(a.reshape((B, nb, Q_BLOCK) + a.shape[2:]), 1, 0)

    def body(args):
        qb, qib, wib, pb = args
        return _sparse_attend_block(qb, qib, wib, k, v, ki, pb, topk)

    o = lax.map(body, (to_blocks(q), to_blocks(qi), to_blocks(wi), q_pos.reshape(nb, Q_BLOCK)))
    return jnp.moveaxis(o, 0, 1).reshape(B, T, -1)


def multi_scale_pool(u_ext, pos, w_pool, pool_scale):
    B, E, C = u_ext.shape
    T = E - POOL_HIST
    uf = u_ext.astype(jnp.float32)
    cs = jnp.concatenate([jnp.zeros((B, 1, C), jnp.float32), jnp.cumsum(uf, axis=1)], axis=1)
    u_new = uf[:, POOL_HIST:]
    outs = []
    for g, w in enumerate(POOL_WINDOWS):
        lo, hi = g * POOL_GC, (g + 1) * POOL_GC
        end = cs[:, POOL_HIST + 1:POOL_HIST + 1 + T, lo:hi]
        start = cs[:, POOL_HIST + 1 - w:POOL_HIST + 1 - w + T, lo:hi]
        cnt = jnp.minimum(pos + 1, w).astype(jnp.float32)[None, :, None]
        outs.append((end - start) / cnt - u_new[..., lo:hi])
    d = jnp.stack(outs, axis=2).astype(u_ext.dtype)
    y = jnp.einsum('btgc,gcd->btgd', d, w_pool).reshape(B, T, C)
    return y * pool_scale


def mixer_layer(x, c, u_hist, k_past, v_past, ki_past, pos, topk,
                norm_w, w_ada, b_ada, w_in, q_norm_w, k_norm_w, w_pool, pool_scale, w_out):
    B, T, _ = x.shape
    ada = jax.nn.silu(c) @ w_ada + b_ada
    shift, scale, gate = jnp.split(ada, 3, axis=-1)
    h = rms_norm(x, norm_w) * (1 + scale[:, None]) + shift[:, None]
    proj = h @ w_in
    q, k, v, qi, ki, wi, ga, u, gp = jnp.split(proj, SPLITS, axis=-1)
    q = rms_norm(q.reshape(B, T, N_HEADS, HEAD_DIM), q_norm_w)
    k = rms_norm(k.reshape(B, T, KV_HEADS, HEAD_DIM), k_norm_w)
    v = v.reshape(B, T, KV_HEADS, HEAD_DIM)
    qi = qi.reshape(B, T, IDX_HEADS, IDX_DIM)
    wi = wi * ((IDX_HEADS * IDX_DIM) ** -0.5)
    k_all = jnp.concatenate([k_past, k], axis=1)
    v_all = jnp.concatenate([v_past, v], axis=1)
    ki_all = jnp.concatenate([ki_past, ki], axis=1)
    a_out = sparse_attention(q, qi, wi, k_all, v_all, ki_all, pos, topk) * jax.nn.silu(ga)
    u_ext = jnp.concatenate([u_hist, u], axis=1)
    p_out = multi_scale_pool(u_ext, pos, w_pool, pool_scale) * jax.nn.silu(gp)
    y = jnp.concatenate([a_out, p_out], axis=-1) @ w_out
    return x + gate[:, None] * y, k, v, ki, u_ext[:, -POOL_HIST:]


def setup_inputs(seed: int = 0) -> dict:
    key = jax.random.key(seed)
    ks = jax.random.split(key, 20)
    f32 = jnp.float32
    nrm = lambda k, s, sc: jax.random.normal(k, s, f32) * sc
    perm = jax.random.permutation(ks[0], N_PHYS_PAGES)
    page_table = perm[:DEC_BATCH * N_PAGES].reshape(DEC_BATCH, N_PAGES).astype(jnp.int32)
    return {
        'x_prompt': nrm(ks[1], (BATCH, SEQ, D_MODEL), 1.0),
        'x_sample': nrm(ks[2], (DEC_BATCH, DEC_SEQ, D_MODEL), 1.0),
        'cache_k': nrm(ks[3], (DEPTH, N_PHYS_PAGES, PAGE_SIZE, KV_HEADS, HEAD_DIM), 1.0),
        'cache_v': nrm(ks[4], (DEPTH, N_PHYS_PAGES, PAGE_SIZE, KV_HEADS, HEAD_DIM), 1.0),
        'cache_kidx': nrm(ks[5], (DEPTH, N_PHYS_PAGES, PAGE_SIZE, IDX_DIM), 1.0),
        'state_pool': nrm(ks[6], (DEPTH, DEC_BATCH, POOL_HIST, POOL_WIDTH), 1.0),
        'page_table': page_table,
        'c_prompt': nrm(ks[7], (BATCH, D_MODEL), 1.0),
        'c_sample': nrm(ks[8], (DEC_BATCH, D_MODEL), 1.0),
        'norm_w': 1.0 + nrm(ks[9], (DEPTH, D_MODEL), 0.1),
        'w_ada': nrm(ks[10], (DEPTH, D_MODEL, 3 * D_MODEL), D_MODEL ** -0.5 * 0.5),
        'b_ada': nrm(ks[11], (DEPTH, 3 * D_MODEL), 0.02),
        'w_in': nrm(ks[12], (DEPTH, D_MODEL, N_IN), D_MODEL ** -0.5),
        'q_norm_w': 1.0 + nrm(ks[13], (DEPTH, HEAD_DIM), 0.1),
        'k_norm_w': 1.0 + nrm(ks[14], (DEPTH, HEAD_DIM), 0.1),
        'w_pool': nrm(ks[15], (DEPTH, POOL_GROUPS, POOL_GC, POOL_GC), POOL_GC ** -0.5),
        'pool_scale': 1.0 + nrm(ks[16], (DEPTH, POOL_WIDTH), 0.1),
        'w_out': nrm(ks[17], (DEPTH, D_MIX, D_MODEL), D_MIX ** -0.5),
    }


def reference(x_prompt, x_sample, cache_k, cache_v, cache_kidx, state_pool, page_table, c_prompt, c_sample,
              norm_w, w_ada, b_ada, w_in, q_norm_w, k_norm_w, w_pool, pool_scale, w_out):
    Bp, S, _ = x_prompt.shape
    Bs, T, _ = x_sample.shape
    n_pages = page_table.shape[1]
    past = n_pages * cache_k.shape[2]
    topk_p = min(TOPK_MAX, S // 4)
    topk_s = min(TOPK_MAX, (past + T) // 4)
    pos_p = jnp.arange(S)
    pos_s = past + jnp.arange(T)
    dt = x_prompt.dtype
    empty_kv = jnp.zeros((Bp, 0, KV_HEADS, HEAD_DIM), dt)
    empty_ki = jnp.zeros((Bp, 0, IDX_DIM), dt)
    zero_hist = jnp.zeros((Bp, POOL_HIST, POOL_WIDTH), dt)
    xp, xs = x_prompt, x_sample
    kp, vp, kip, up = [], [], [], []
    kss, vss, kiss, uss = [], [], [], []
    for l in range(DEPTH):
        params = (norm_w[l], w_ada[l], b_ada[l], w_in[l], q_norm_w[l], k_norm_w[l], w_pool[l], pool_scale[l], w_out[l])
        xp, k1, v1, ki1, u1 = mixer_layer(xp, c_prompt, zero_hist, empty_kv, empty_kv, empty_ki, pos_p, topk_p, *params)
        k_past = cache_k[l][page_table].reshape(Bs, past, KV_HEADS, HEAD_DIM)
        v_past = cache_v[l][page_table].reshape(Bs, past, KV_HEADS, HEAD_DIM)
        ki_past = cache_kidx[l][page_table].reshape(Bs, past, IDX_DIM)
        xs, k2, v2, ki2, u2 = mixer_layer(xs, c_sample, state_pool[l], k_past, v_past, ki_past, pos_s, topk_s, *params)
        kp.append(k1); vp.append(v1); kip.append(ki1); up.append(u1)
        kss.append(k2); vss.append(v2); kiss.append(ki2); uss.append(u2)
    return (xp, xs, jnp.stack(kp), jnp.stack(vp), jnp.stack(kip), jnp.stack(up),
            jnp.stack(kss), jnp.stack(vss), jnp.stack(kiss), jnp.stack(uss))
```

```python
import functools

import jax
import jax.numpy as jnp
import numpy as np
from jax import lax
from jax.experimental import pallas as pl
from jax.experimental.pallas import tpu as pltpu

F32 = jnp.float32
BF16 = jnp.bfloat16
I32 = jnp.int32

D_MODEL = 1024
ATTN_W = 512
POOL_W = 512
HEAD_DIM = 64
N_HEADS = 8
KV_HEADS = 2
IDX_HEADS = 8
IDX_DIM = 64
TOPK = 256
POOL_WINDOWS = (2, 4, 8, 16)
POOL_GC = 128
POOL_HIST = 15
EPS = 1e-6
PAGE = 128

LANES = 128
KEY_CHUNK = 256
TQ = 128
TM = 512
INT_MIN = np.int32(-2 ** 31)
NEG_BIG = -1e30
VMEM_LIMIT = 56 * 1024 * 1024

C_Q, C_K, C_V, C_QI, C_KW, C_GA, C_U, C_GP, N_PROJ = 0, 512, 640, 768, 1280, 1408, 1920, 2432, 2944


def _nt_dot(a, b):
    return lax.dot_general(a, b, (((1,), (1,)), ((), ())), preferred_element_type=F32)


def _dot(a, b):
    return jnp.dot(a, b, preferred_element_type=F32)


def _silu(z):
    return z / (1.0 + jnp.exp(-z))


def _lane_iota(shape):
    return lax.broadcasted_iota(I32, shape, len(shape) - 1)


def _seg_ones(n, seg):
    r = lax.broadcasted_iota(I32, (n, n), 0) // seg
    c = lax.broadcasted_iota(I32, (n, n), 1) // seg
    return jnp.where(r == c, 1.0, 0.0).astype(BF16)


def _head_rms(z, seg_mat, w):
    ss = _dot((z * z).astype(BF16), seg_mat)
    return z * lax.rsqrt(ss * (1.0 / HEAD_DIM) + EPS) * w


def _float_of_rank(u):
    key = u ^ INT_MIN
    bits = key ^ ((key >> 31) & np.int32(0x7FFFFFFF))
    return lax.bitcast_convert_type(bits, F32)


def _kth_largest(count_ge, shape):
    def bit_body(it, ans):
        cand = ans | jnp.left_shift(jnp.int32(1), 31 - it)
        return jnp.where(count_ge(_float_of_rank(cand)) >= float(TOPK), cand, ans)

    ans = lax.fori_loop(0, 32, bit_body, jnp.zeros(shape, I32))
    return jnp.where(ans == 0, -jnp.inf, _float_of_rank(ans))


def _ada_kernel(c_ref, w_ref, b_ref, o_ref):
    c = c_ref[...]
    o_ref[...] = _dot(_silu(c).astype(BF16), w_ref[...]) + b_ref[...]


def _ada(c_all, w_ada_b, b_ada):
    n = c_all.shape[0]
    return pl.pallas_call(
        _ada_kernel,
        out_shape=jax.ShapeDtypeStruct((n, 3 * D_MODEL), F32),
        grid=(3,),
        in_specs=[
            pl.BlockSpec((n, D_MODEL), lambda j: (0, 0)),
            pl.BlockSpec((D_MODEL, D_MODEL), lambda j: (0, j)),
            pl.BlockSpec((1, D_MODEL), lambda j: (0, j)),
        ],
        out_specs=pl.BlockSpec((n, D_MODEL), lambda j: (0, j)),
        compiler_params=pltpu.CompilerParams(dimension_semantics=("arbitrary",)),
        name="ada_ln",
    )(c_all, w_ada_b, b_ada)


def _modulated_norm(x, norm_w, scale, shift):
    ms = jnp.mean(x * x, axis=-1, keepdims=True)
    return (x * lax.rsqrt(ms + EPS)) * norm_w * (1.0 + scale) + shift


def _pool_mix(d, wpool_ref, pscale, gp):
    db = d.astype(BF16)
    cols = []
    for g in range(len(POOL_WINDOWS)):
        cols.append(_dot(db[:, g * POOL_GC:(g + 1) * POOL_GC], wpool_ref[g]))
    y = jnp.concatenate(cols, axis=1)
    return y * pscale * _silu(gp)


def _proj_prompt_kernel(x_ref, ada_ref, nw_ref, w_ref, qnw_ref, knw_ref, wpool_ref, ps_ref,
                        qs_ref, k_ref, kvar_ref, v_ref, vext_ref, qi_ref, ki_ref, kivar_ref,
                        wi_ref, sga_ref, pg_ref, ulast_ref,
                        ext_ref):
    t = pl.program_id(1)
    x = x_ref[0]
    ada = ada_ref[0]
    shift = ada[:, 0:D_MODEL]
    scale = ada[:, D_MODEL:2 * D_MODEL]
    hb = _modulated_norm(x, nw_ref[...], scale, shift).astype(BF16)

    lane = _lane_iota((TM, LANES))
    lo = lane < HEAD_DIM
    seg512 = _seg_ones(ATTN_W, HEAD_DIM)
    seg128 = _seg_ones(LANES, HEAD_DIM)

    q = _dot(hb, w_ref[:, C_Q:C_Q + ATTN_W])
    qs_ref[...] = _head_rms(q, seg512, qnw_ref[...]).astype(BF16)

    k = _head_rms(_dot(hb, w_ref[:, C_K:C_K + LANES]), seg128, knw_ref[...])
    k_ref[...] = k
    k_sw = pltpu.roll(k, HEAD_DIM, axis=1)
    kvar_ref[0] = jnp.where(lo, k, 0.0).astype(BF16)
    kvar_ref[1] = jnp.where(lo, 0.0, k_sw).astype(BF16)
    kvar_ref[2] = jnp.where(lo, k_sw, 0.0).astype(BF16)
    kvar_ref[3] = jnp.where(lo, 0.0, k).astype(BF16)

    v = _dot(hb, w_ref[:, C_V:C_V + LANES])
    v_ref[...] = v
    ones = jnp.ones((TM, LANES), BF16)
    vext_ref[0, :, 0:LANES] = v.astype(BF16)
    vext_ref[0, :, LANES:2 * LANES] = ones
    vext_ref[1, :, 0:LANES] = pltpu.roll(v, HEAD_DIM, axis=1).astype(BF16)
    vext_ref[1, :, LANES:2 * LANES] = ones

    qi_ref[...] = _dot(hb, w_ref[:, C_QI:C_QI + ATTN_W]).astype(BF16)
    kw = _dot(hb, w_ref[:, C_KW:C_KW + LANES])
    kw_sw = pltpu.roll(kw, HEAD_DIM, axis=1)
    ki_ref[...] = kw[:, 0:IDX_DIM]
    kivar_ref[0] = jnp.where(lo, kw, 0.0).astype(BF16)
    kivar_ref[1] = jnp.where(lo, 0.0, kw_sw).astype(BF16)
    wi_ref[...] = kw_sw[:, 0:IDX_HEADS] * ((IDX_HEADS * IDX_DIM) ** -0.5)

    sga_ref[...] = _silu(_dot(hb, w_ref[:, C_GA:C_GA + ATTN_W])).astype(BF16)

    u = _dot(hb, w_ref[:, C_U:C_U + POOL_W])
    gp = _dot(hb, w_ref[:, C_GP:C_GP + POOL_W])

    @pl.when(t == 0)
    def _():
        ext_ref[0:16, :] = jnp.zeros((16, POOL_W), F32)

    ext_ref[16:16 + TM, :] = u
    pos = t * TM + lax.broadcasted_iota(I32, (TM, POOL_GC), 0)
    ds = []
    for g, w in enumerate(POOL_WINDOWS):
        cs = slice(g * POOL_GC, (g + 1) * POOL_GC)
        s = u[:, cs]
        for j in range(1, w):
            s = s + ext_ref[16 - j:16 - j + TM, cs]
        cnt = jnp.minimum(pos + 1, w).astype(F32)
        ds.append(s / cnt - u[:, cs])
    d = jnp.concatenate(ds, axis=1)
    pg_ref[...] = _pool_mix(d, wpool_ref, ps_ref[...], gp).astype(BF16)

    tail = u[TM - 16:TM, :]
    ulast_ref[0] = tail
    ext_ref[0:16, :] = tail


def _proj_prompt(x, ada_p, norm_w, w_in_b, qnw, knw, wpool_b, pscale):
    b, s, _ = x.shape
    n = b * s
    nt = s // TM
    row = lambda bi, ti: (bi * nt + ti, 0)
    var = lambda bi, ti: (0, bi * nt + ti, 0)
    const2 = lambda bi, ti: (0, 0)
    const3 = lambda bi, ti: (0, 0, 0)
    out_shape = (
        jax.ShapeDtypeStruct((n, ATTN_W), BF16),
        jax.ShapeDtypeStruct((n, LANES), F32),
        jax.ShapeDtypeStruct((4, n, LANES), BF16),
        jax.ShapeDtypeStruct((n, LANES), F32),
        jax.ShapeDtypeStruct((2, n, 2 * LANES), BF16),
        jax.ShapeDtypeStruct((n, ATTN_W), BF16),
        jax.ShapeDtypeStruct((n, IDX_DIM), F32),
        jax.ShapeDtypeStruct((2, n, LANES), BF16),
        jax.ShapeDtypeStruct((n, IDX_HEADS), F32),
        jax.ShapeDtypeStruct((n, ATTN_W), BF16),
        jax.ShapeDtypeStruct((n, POOL_W), BF16),
        jax.ShapeDtypeStruct((b, 16, POOL_W), F32),
    )
    out_specs = (
        pl.BlockSpec((TM, ATTN_W), row),
        pl.BlockSpec((TM, LANES), row),
        pl.BlockSpec((4, TM, LANES), var),
        pl.BlockSpec((TM, LANES), row),
        pl.BlockSpec((2, TM, 2 * LANES), var),
        pl.BlockSpec((TM, ATTN_W), row),
        pl.BlockSpec((TM, IDX_DIM), row),
        pl.BlockSpec((2, TM, LANES), var),
        pl.BlockSpec((TM, IDX_HEADS), row),
        pl.BlockSpec((TM, ATTN_W), row),
        pl.BlockSpec((TM, POOL_W), row),
        pl.BlockSpec((1, 16, POOL_W), lambda bi, ti: (bi, 0, 0)),
    )
    in_specs = [
        pl.BlockSpec((1, TM, D_MODEL), lambda bi, ti: (bi, ti, 0)),
        pl.BlockSpec((1, 1, 3 * D_MODEL), lambda bi, ti: (bi, 0, 0)),
        pl.BlockSpec((1, D_MODEL), const2),
        pl.BlockSpec((D_MODEL, N_PROJ), const2),
        pl.BlockSpec((1, ATTN_W), const2),
        pl.BlockSpec((1, LANES), const2),
        pl.BlockSpec((4, POOL_GC, POOL_GC), const3),
        pl.BlockSpec((1, POOL_W), const2),
    ]
    return pl.pallas_call(
        _proj_prompt_kernel,
        out_shape=out_shape,
        grid=(b, nt),
        in_specs=in_specs,
        out_specs=out_specs,
        scratch_shapes=[pltpu.VMEM((16 + TM, POOL_W), F32)],
        compiler_params=pltpu.CompilerParams(
            dimension_semantics=("arbitrary", "arbitrary"), vmem_limit_bytes=VMEM_LIMIT),
        name="proj_prompt",
    )(x, ada_p, norm_w, w_in_b, qnw, knw, wpool_b, pscale)


def _attn_prompt_kernel(qi_ref, wi_ref, qs_ref, sga_ref, kiv_ref, kv_ref, vx_ref, ag_ref,
                        sc_ref, lg_ref, wib_ref, m_ref, acc_ref):
    i = pl.program_id(1)
    nch = (i + 2) // 2
    row0 = i * TQ
    shape2 = (TQ, KEY_CHUNK)

    wi = wi_ref[...]
    for h in range(IDX_HEADS):
        wib_ref[h] = jnp.broadcast_to(wi[:, h:h + 1], shape2)

    def score_chunk(c, carry):
        k0 = pl.multiple_of(c * KEY_CHUNK, KEY_CHUNK)
        acc = jnp.zeros(shape2, F32)
        for hp in range(IDX_HEADS // 2):
            qpair = qi_ref[:, hp * LANES:(hp + 1) * LANES]
            for par in range(2):
                s = _nt_dot(qpair, kiv_ref[par, pl.ds(k0, KEY_CHUNK), :])
                acc = acc + jnp.maximum(s, 0.0) * wib_ref[2 * hp + par]
        kpos = k0 + _lane_iota(shape2)
        qpos = row0 + lax.broadcasted_iota(I32, shape2, 0)
        sc_ref[c] = jnp.where(kpos <= qpos, acc, -jnp.inf)
        return carry

    lax.fori_loop(0, nch, score_chunk, 0)

    ones_mat = jnp.ones((LANES, LANES), BF16)

    def count_pass(pred):
        def body(c, acc):
            sc = sc_ref[c]
            return acc + pred(sc[:, 0:LANES]).astype(F32) + pred(sc[:, LANES:KEY_CHUNK]).astype(F32)
        part = lax.fori_loop(0, nch, body, jnp.zeros((TQ, LANES), F32))
        return _dot(part.astype(BF16), ones_mat)

    thr = _kth_largest(lambda t: count_pass(lambda x: x >= t), (TQ, LANES))
    need = float(TOPK) - count_pass(lambda x: x > thr)
    thr2 = jnp.concatenate([thr, thr], axis=1)
    need2 = jnp.concatenate([need, need], axis=1)

    tri = jnp.where(lax.broadcasted_iota(I32, (KEY_CHUNK, KEY_CHUNK), 0)
                    <= lax.broadcasted_iota(I32, (KEY_CHUNK, KEY_CHUNK), 1), 1.0, 0.0).astype(BF16)
    ones_cl = jnp.ones((KEY_CHUNK, LANES), BF16)

    for h in range(N_HEADS):
        m_ref[h] = jnp.full((TQ, LANES), -jnp.inf, F32)
        acc_ref[h] = jnp.zeros(shape2, F32)

    def logit_chunk(c, eq_before):
        k0 = pl.multiple_of(c * KEY_CHUNK, KEY_CHUNK)
        sc = sc_ref[c]
        eq = sc == thr2
        eqb = jnp.where(eq, 1.0, 0.0).astype(BF16)
        rank = _dot(eqb, tri) + jnp.concatenate([eq_before, eq_before], axis=1)
        sel = (sc > -jnp.inf) & ((sc > thr2) | (eq & (rank <= need2)))
        for h in range(N_HEADS):
            hp, par, g = h // 2, h % 2, h // (N_HEADS // KV_HEADS)
            qpair = qs_ref[:, hp * LANES:(hp + 1) * LANES]
            lg = _nt_dot(qpair, kv_ref[2 * g + par, pl.ds(k0, KEY_CHUNK), :])
            lg = jnp.where(sel, lg, -jnp.inf)
            lg_ref[h, c] = lg
            m_ref[h] = jnp.maximum(m_ref[h], jnp.maximum(lg[:, 0:LANES], lg[:, LANES:KEY_CHUNK]))
        return eq_before + _dot(eqb, ones_cl)

    lax.fori_loop(0, nch, logit_chunk, jnp.zeros((TQ, LANES), F32))

    for h in range(N_HEADS):
        m_ref[h] = jnp.broadcast_to(jnp.max(m_ref[h], axis=1, keepdims=True), (TQ, LANES))

    def pv_chunk(c, carry):
        k0 = pl.multiple_of(c * KEY_CHUNK, KEY_CHUNK)
        for h in range(N_HEADS):
            par, g = h % 2, h // (N_HEADS // KV_HEADS)
            m = m_ref[h]
            p = jnp.exp(lg_ref[h, c] - jnp.concatenate([m, m], axis=1)).astype(BF16)
            vx = vx_ref[0 if g == par else 1, pl.ds(k0, KEY_CHUNK), :]
            acc_ref[h] += _dot(p, vx)
        return carry

    lax.fori_loop(0, nch, pv_chunk, 0)

    lo = _lane_iota((TQ, LANES)) < HEAD_DIM
    for hp in range(N_HEADS // 2):
        a0 = acc_ref[2 * hp]
        a1 = acc_ref[2 * hp + 1]
        o0 = a0[:, 0:LANES] / a0[:, LANES:KEY_CHUNK]
        o1 = a1[:, 0:LANES] / a1[:, LANES:KEY_CHUNK]
        cols = slice(hp * LANES, (hp + 1) * LANES)
        ag_ref[:, cols] = (jnp.where(lo, o0, o1) * sga_ref[:, cols].astype(F32)).astype(BF16)


def _attn_prompt(qi, wi, qs, sga, kivar, kvar, vext, b, s):
    n = b * s
    nq = s // TQ
    nkc = s // KEY_CHUNK
    row = lambda bi, qi_: (bi * nq + qi_, 0)
    per_b = lambda bi, qi_: (0, bi, 0)
    return pl.pallas_call(
        _attn_prompt_kernel,
        out_shape=jax.ShapeDtypeStruct((n, ATTN_W), BF16),
        grid=(b, nq),
        in_specs=[
            pl.BlockSpec((TQ, ATTN_W), row),
            pl.BlockSpec((TQ, IDX_HEADS), row),
            pl.BlockSpec((TQ, ATTN_W), row),
            pl.BlockSpec((TQ, ATTN_W), row),
            pl.BlockSpec((2, s, LANES), per_b),
            pl.BlockSpec((4, s, LANES), per_b),
            pl.BlockSpec((2, s, 2 * LANES), per_b),
        ],
        out_specs=pl.BlockSpec((TQ, ATTN_W), row),
        scratch_shapes=[
            pltpu.VMEM((nkc, TQ, KEY_CHUNK), F32),
            pltpu.VMEM((N_HEADS, nkc, TQ, KEY_CHUNK), F32),
            pltpu.VMEM((IDX_HEADS, TQ, KEY_CHUNK), F32),
            pltpu.VMEM((N_HEADS, TQ, LANES), F32),
            pltpu.VMEM((N_HEADS, TQ, KEY_CHUNK), F32),
        ],
        compiler_params=pltpu.CompilerParams(
            dimension_semantics=("arbitrary", "arbitrary"), vmem_limit_bytes=VMEM_LIMIT),
        name="attn_prompt",
    )(qi, wi, qs, sga, kivar, kvar, vext)


def _out_proj_kernel(x_ref, a_ref, p_ref, gate_ref, w_ref, o_ref):
    y = _dot(a_ref[...], w_ref[0:ATTN_W, :]) + _dot(p_ref[...], w_ref[ATTN_W:ATTN_W + POOL_W, :])
    o_ref[0] = x_ref[0] + gate_ref[0] * y


def _out_proj_prompt(x, ag, pg, ada_p, w_out_b):
    b, s, _ = x.shape
    nt = s // TM
    row = lambda bi, ti: (bi * nt + ti, 0)
    return pl.pallas_call(
        _out_proj_kernel,
        out_shape=jax.ShapeDtypeStruct(x.shape, F32),
        grid=(b, nt),
        in_specs=[
            pl.BlockSpec((1, TM, D_MODEL), lambda bi, ti: (bi, ti, 0)),
            pl.BlockSpec((TM, ATTN_W), row),
            pl.BlockSpec((TM, POOL_W), row),
            pl.BlockSpec((1, 1, D_MODEL), lambda bi, ti: (bi, 0, 2)),
            pl.BlockSpec((D_MODEL, D_MODEL), lambda bi, ti: (0, 0)),
        ],
        out_specs=pl.BlockSpec((1, TM, D_MODEL), lambda bi, ti: (bi, ti, 0)),
        compiler_params=pltpu.CompilerParams(
            dimension_semantics=("arbitrary", "arbitrary"), vmem_limit_bytes=VMEM_LIMIT),
        name="out_proj_prompt",
    )(x, ag, pg, ada_p, w_out_b)


def _proj_sample_kernel(x_ref, ada_ref, nw_ref, w_ref, qnw_ref, knw_ref, wpool_ref, ps_ref, hist_ref,
                        qpad_ref, k_ref, v_ref, ki_ref, qi_ref, wi_ref, snew_ref, lnew_ref,
                        sga_ref, pg_ref, u_ref):
    nb = x_ref.shape[0]
    x = x_ref[...]
    shift = ada_ref[:, 0:D_MODEL]
    scale = ada_ref[:, D_MODEL:2 * D_MODEL]
    hb = _modulated_norm(x, nw_ref[...], scale, shift).astype(BF16)

    lane = _lane_iota((nb, LANES))
    lo = lane < HEAD_DIM
    seg512 = _seg_ones(ATTN_W, HEAD_DIM)
    seg128 = _seg_ones(LANES, HEAD_DIM)
    head_sel = jnp.where(lax.broadcasted_iota(I32, (ATTN_W, LANES), 0) // HEAD_DIM
                         == lax.broadcasted_iota(I32, (ATTN_W, LANES), 1), 1.0, 0.0).astype(BF16)

    def head_sums(prod):
        hi = prod.astype(BF16)
        rest = (prod - hi.astype(F32)).astype(BF16)
        return _dot(hi, head_sel) + _dot(rest, head_sel)

    q = _head_rms(_dot(hb, w_ref[:, C_Q:C_Q + ATTN_W]), seg512, qnw_ref[...])
    qb = q.astype(BF16)
    k = _head_rms(_dot(hb, w_ref[:, C_K:C_K + LANES]), seg128, knw_ref[...])
    k_ref[...] = k
    v = _dot(hb, w_ref[:, C_V:C_V + LANES])
    v_ref[...] = v

    for hp in range(N_HEADS // 2):
        pair = q[:, hp * LANES:(hp + 1) * LANES]
        pair_sw = pltpu.roll(pair, HEAD_DIM, axis=1)
        g = (2 * hp) // (N_HEADS // KV_HEADS)
        if g == 0:
            h_even, h_odd = jnp.where(lo, pair, 0.0), jnp.where(lo, pair_sw, 0.0)
        else:
            h_even, h_odd = jnp.where(lo, 0.0, pair_sw), jnp.where(lo, 0.0, pair)
        qpad_ref[:, (2 * hp) * LANES:(2 * hp + 1) * LANES] = h_even.astype(BF16)
        qpad_ref[:, (2 * hp + 1) * LANES:(2 * hp + 2) * LANES] = h_odd.astype(BF16)

    kq = k.astype(BF16).astype(F32)
    kq_sw = pltpu.roll(kq, HEAD_DIM, axis=1)
    k0t = jnp.where(lo, kq, kq_sw)
    k1t = jnp.where(lo, kq_sw, kq)
    qf = qb.astype(F32)
    prod = jnp.concatenate([qf[:, 0:LANES] * k0t, qf[:, LANES:2 * LANES] * k0t,
                            qf[:, 2 * LANES:3 * LANES] * k1t, qf[:, 3 * LANES:4 * LANES] * k1t], axis=1)
    lnew_ref[...] = head_sums(prod)

    qi = _dot(hb, w_ref[:, C_QI:C_QI + ATTN_W])
    qib = qi.astype(BF16)
    qi_ref[...] = qib
    kw = _dot(hb, w_ref[:, C_KW:C_KW + LANES])
    kw_sw = pltpu.roll(kw, HEAD_DIM, axis=1)
    ki_ref[...] = kw[:, 0:IDX_DIM]
    wi_full = jnp.where(lane < IDX_HEADS, kw_sw, 0.0) * ((IDX_HEADS * IDX_DIM) ** -0.5)
    wi_ref[...] = wi_full[:, 0:IDX_HEADS]

    kib = kw.astype(BF16).astype(F32)
    kit = jnp.where(lo, kib, pltpu.roll(kib, HEAD_DIM, axis=1))
    qif = qib.astype(F32)
    prod_i = jnp.concatenate([qif[:, j * LANES:(j + 1) * LANES] * kit for j in range(4)], axis=1)
    s_new = jnp.maximum(head_sums(prod_i), 0.0) * wi_full
    snew_ref[...] = jnp.broadcast_to(jnp.sum(s_new, axis=1, keepdims=True), (nb, LANES))

    sga_ref[...] = _silu(_dot(hb, w_ref[:, C_GA:C_GA + ATTN_W])).astype(BF16)

    u = _dot(hb, w_ref[:, C_U:C_U + POOL_W])
    gp = _dot(hb, w_ref[:, C_GP:C_GP + POOL_W])
    u_ref[...] = u
    ds = []
    for g, w in enumerate(POOL_WINDOWS):
        cs = slice(g * POOL_GC, (g + 1) * POOL_GC)
        s = u[:, cs]
        for j in range(1, w):
            s = s + hist_ref[POOL_HIST - j, :, cs]
        ds.append(s / float(w) - u[:, cs])
    d = jnp.concatenate(ds, axis=1)
    pg_ref[...] = _pool_mix(d, wpool_ref, ps_ref[...], gp).astype(BF16)


def _proj_sample(x, ada_s, norm_w, w_in_b, qnw, knw, wpool_b, pscale, hist_t):
    nb = x.shape[0]
    out_shape = (
        jax.ShapeDtypeStruct((nb, N_HEADS * LANES), BF16),
        jax.ShapeDtypeStruct((nb, LANES), F32),
        jax.ShapeDtypeStruct((nb, LANES), F32),
        jax.ShapeDtypeStruct((nb, IDX_DIM), F32),
        jax.ShapeDtypeStruct((nb, ATTN_W), BF16),
        jax.ShapeDtypeStruct((nb, IDX_HEADS), F32),
        jax.ShapeDtypeStruct((nb, LANES), F32),
        jax.ShapeDtypeStruct((nb, LANES), F32),
        jax.ShapeDtypeStruct((nb, ATTN_W), BF16),
        jax.ShapeDtypeStruct((nb, POOL_W), BF16),
        jax.ShapeDtypeStruct((nb, POOL_W), F32),
    )
    return pl.pallas_call(
        _proj_sample_kernel,
        out_shape=out_shape,
        compiler_params=pltpu.CompilerParams(vmem_limit_bytes=VMEM_LIMIT),
        name="proj_sample",
    )(x, ada_s, norm_w, w_in_b, qnw, knw, wpool_b, pscale, hist_t)


def _score_sample_kernel(pt_ref, qi_ref, wi_ref, page_ref, o_ref):
    s = _nt_dot(qi_ref[0], page_ref[0].astype(BF16))
    o_ref[0, 0] = jnp.sum(jnp.maximum(s, 0.0) * wi_ref[0], axis=0, keepdims=True)


def _score_sample(page_table, qi3, wi3, kidx_pages):
    nb, n_pages = page_table.shape
    return pl.pallas_call(
        _score_sample_kernel,
        out_shape=jax.ShapeDtypeStruct((nb, n_pages, 1, PAGE), F32),
        grid_spec=pltpu.PrefetchScalarGridSpec(
            num_scalar_prefetch=1,
            grid=(nb, n_pages),
            in_specs=[
                pl.BlockSpec((1, IDX_HEADS, IDX_DIM), lambda b, p, pt: (b, 0, 0)),
                pl.BlockSpec((1, IDX_HEADS, 1), lambda b, p, pt: (b, 0, 0)),
                pl.BlockSpec((1, PAGE, IDX_DIM), lambda b, p, pt: (pt[b, p], 0, 0)),
            ],
            out_specs=pl.BlockSpec((1, 1, 1, PAGE), lambda b, p, pt: (b, p, 0, 0)),
        ),
        compiler_params=pltpu.CompilerParams(dimension_semantics=("arbitrary", "arbitrary")),
        name="score_sample",
    )(page_table, qi3, wi3, kidx_pages)


def _select_sample_kernel(sc_ref, snew_ref, mask_ref, mnew_ref):
    nb, n_keys = sc_ref.shape
    nch = n_keys // KEY_CHUNK
    snew = snew_ref[...]
    ones_mat = jnp.ones((LANES, LANES), BF16)

    def count_pass(pred):
        acc = jnp.zeros((nb, LANES), F32)
        for c in range(n_keys // LANES):
            acc = acc + pred(sc_ref[:, c * LANES:(c + 1) * LANES]).astype(F32)
        return _dot(acc.astype(BF16), ones_mat) + pred(snew).astype(F32)

    thr = _kth_largest(lambda t: count_pass(lambda x: x >= t), (nb, LANES))
    need = float(TOPK) - count_pass(lambda x: x > thr)
    thr2 = jnp.concatenate([thr, thr], axis=1)
    need2 = jnp.concatenate([need, need], axis=1)
    tri = jnp.where(lax.broadcasted_iota(I32, (KEY_CHUNK, KEY_CHUNK), 0)
                    <= lax.broadcasted_iota(I32, (KEY_CHUNK, KEY_CHUNK), 1), 1.0, 0.0).astype(BF16)
    ones_cl = jnp.ones((KEY_CHUNK, LANES), BF16)

    eq_before = jnp.zeros((nb, LANES), F32)
    for c in range(nch):
        sc = sc_ref[:, c * KEY_CHUNK:(c + 1) * KEY_CHUNK]
        eq = sc == thr2
        eqb = jnp.where(eq, 1.0, 0.0).astype(BF16)
        rank = _dot(eqb, tri) + jnp.concatenate([eq_before, eq_before], axis=1)
        sel = (sc > thr2) | (eq & (rank <= need2))
        mask_ref[:, c * KEY_CHUNK:(c + 1) * KEY_CHUNK] = jnp.where(sel, 0.0, -jnp.inf)
        eq_before = eq_before + _dot(eqb, ones_cl)
    sel_new = (snew > thr) | ((snew == thr) & (eq_before + 1.0 <= need))
    mnew_ref[...] = jnp.where(sel_new, 0.0, -jnp.inf)


def _select_sample(scores, snew):
    nb, n_keys = scores.shape
    return pl.pallas_call(
        _select_sample_kernel,
        out_shape=(jax.ShapeDtypeStruct((nb, n_keys), F32), jax.ShapeDtypeStruct((nb, LANES), F32)),
        name="select_sample",
    )(scores, snew)


def _attn_sample_kernel(pt_ref, q_ref, kpage_ref, vpage_ref, mask_ref, lnew_ref, mnew_ref, vnew_ref,
                        o_ref, m_ref, l_ref, acc_ref):
    p_idx = pl.program_id(1)

    @pl.when(p_idx == 0)
    def _():
        m_ref[...] = jnp.full(m_ref.shape, NEG_BIG, F32)
        l_ref[...] = jnp.zeros(l_ref.shape, F32)
        acc_ref[...] = jnp.zeros(acc_ref.shape, F32)

    lg = _nt_dot(q_ref[0], kpage_ref[0].astype(BF16)) + mask_ref[0, 0]
    m_old = m_ref[...]
    m_new = jnp.maximum(m_old, jnp.max(lg, axis=1, keepdims=True))
    alpha = jnp.exp(m_old - m_new)
    p = jnp.exp(lg - m_new)
    l_ref[...] = alpha * l_ref[...] + jnp.sum(p, axis=1, keepdims=True)
    acc_ref[...] = alpha * acc_ref[...] + _dot(p.astype(BF16), vpage_ref[0].astype(BF16))
    m_ref[...] = m_new

    @pl.when(p_idx == pl.num_programs(1) - 1)
    def _():
        lg_n = lnew_ref[0] + mnew_ref[0]
        m_o = m_ref[...]
        m_f = jnp.maximum(m_o, lg_n)
        a_f = jnp.exp(m_o - m_f)
        p_n = jnp.exp(lg_n - m_f)
        l_f = a_f * l_ref[...] + p_n
        acc_f = a_f * acc_ref[...] + p_n * vnew_ref[0]
        o_ref[0] = acc_f / l_f


def _attn_sample(page_table, qpad3, k_pages, v_pages, mask4, lnew3, mnew3, vnew3):
    nb, n_pages = page_table.shape
    per_b = lambda b, p, pt: (b, 0, 0)
    return pl.pallas_call(
        _attn_sample_kernel,
        out_shape=jax.ShapeDtypeStruct((nb, N_HEADS, LANES), F32),
        grid_spec=pltpu.PrefetchScalarGridSpec(
            num_scalar_prefetch=1,
            grid=(nb, n_pages),
            in_specs=[
                pl.BlockSpec((1, N_HEADS, LANES), per_b),
                pl.BlockSpec((1, PAGE, LANES), lambda b, p, pt: (pt[b, p], 0, 0)),
                pl.BlockSpec((1, PAGE, LANES), lambda b, p, pt: (pt[b, p], 0, 0)),
                pl.BlockSpec((1, 1, 1, PAGE), lambda b, p, pt: (b, p, 0, 0)),
                pl.BlockSpec((1, N_HEADS, 1), per_b),
                pl.BlockSpec((1, N_HEADS, 1), per_b),
                pl.BlockSpec((1, 1, LANES), per_b),
            ],
            out_specs=pl.BlockSpec((1, N_HEADS, LANES), per_b),
            scratch_shapes=[
                pltpu.VMEM((N_HEADS, 1), F32),
                pltpu.VMEM((N_HEADS, 1), F32),
                pltpu.VMEM((N_HEADS, LANES), F32),
            ],
        ),
        compiler_params=pltpu.CompilerParams(dimension_semantics=("arbitrary", "arbitrary")),
        name="attn_sample",
    )(page_table, qpad3, k_pages, v_pages, mask4, lnew3, mnew3, vnew3)


def _out_proj_sample_kernel(x_ref, a_ref, sga_ref, p_ref, gate_ref, w_ref, o_ref):
    lo = _lane_iota((x_ref.shape[0], LANES)) < HEAD_DIM
    pairs = []
    for hp in range(N_HEADS // 2):
        even = a_ref[:, (2 * hp) * LANES:(2 * hp + 1) * LANES]
        odd = a_ref[:, (2 * hp + 1) * LANES:(2 * hp + 2) * LANES]
        if (2 * hp) // (N_HEADS // KV_HEADS) == 0:
            pairs.append(jnp.where(lo, even, pltpu.roll(odd, HEAD_DIM, axis=1)))
        else:
            pairs.append(jnp.where(lo, pltpu.roll(even, HEAD_DIM, axis=1), odd))
    a = jnp.concatenate(pairs, axis=1)
    ag = (a * sga_ref[...].astype(F32)).astype(BF16)
    y = _dot(ag, w_ref[0:ATTN_W, :]) + _dot(p_ref[...], w_ref[ATTN_W:ATTN_W + POOL_W, :])
    o_ref[...] = x_ref[...] + gate_ref[...] * y


def _out_proj_sample(x, a, sga, pg, gate, w_out_b):
    return pl.pallas_call(
        _out_proj_sample_kernel,
        out_shape=jax.ShapeDtypeStruct(x.shape, F32),
        compiler_params=pltpu.CompilerParams(vmem_limit_bytes=VMEM_LIMIT),
        name="out_proj_sample",
    )(x, a, sga, pg, gate, w_out_b)


def _permute_w_in(w_in):
    q, k, v, qi, ki, wi, ga, u, gp = jnp.split(
        w_in, np.cumsum([512, 128, 128, 512, 64, 8, 512, 512]).tolist(), axis=1)
    pad = jnp.zeros((w_in.shape[0], LANES - IDX_DIM - IDX_HEADS), w_in.dtype)
    return jnp.concatenate([q, k, v, qi, ki, wi, pad, ga, u, gp], axis=1).astype(BF16)


def kernel(x_prompt, x_sample, cache_k, cache_v, cache_kidx, state_pool, page_table, c_prompt, c_sample,
           norm_w, w_ada, b_ada, w_in, q_norm_w, k_norm_w, w_pool, pool_scale, w_out):
    bp, s, _ = x_prompt.shape
    bs = x_sample.shape[0]
    assert w_in.shape[0] == 1 and x_sample.shape[1] == 1, "single layer, single decode token"
    n_phys = cache_k.shape[1]

    w_in_b = _permute_w_in(w_in[0])
    w_ada_b = w_ada[0].astype(BF16)
    w_out_b = w_out[0].astype(BF16)
    wpool_b = w_pool[0].astype(BF16)
    qnw = jnp.tile(q_norm_w[0], N_HEADS)[None, :] * (HEAD_DIM ** -0.5)
    knw = jnp.tile(k_norm_w[0], KV_HEADS)[None, :]
    nw = norm_w[0][None, :]
    pscale = pool_scale[0][None, :]

    ada = _ada(jnp.concatenate([c_prompt, c_sample], axis=0), w_ada_b, b_ada)
    ada_p = ada[:bp].reshape(bp, 1, 3 * D_MODEL)
    ada_s = ada[bp:]

    (qs, k_p, kvar, v_p, vext, qi, ki_p, kivar, wi, sga, pg, ulast) = _proj_prompt(
        x_prompt, ada_p, nw, w_in_b, qnw, knw, wpool_b, pscale)
    ag = _attn_prompt(qi, wi, qs, sga, kivar, kvar, vext, bp, s)
    y_prompt = _out_proj_prompt(x_prompt, ag, pg, ada_p, w_out_b)

    hist_t = jnp.transpose(state_pool[0], (1, 0, 2))
    (qpad, k_s, v_s, ki_s, qi_s, wi_s, snew, lnew, sga_s, pg_s, u_s) = _proj_sample(
        x_sample[:, 0, :], ada_s, nw, w_in_b, qnw, knw, wpool_b, pscale, hist_t)
    n_pages = page_table.shape[1]
    scores = _score_sample(page_table, qi_s.reshape(bs, IDX_HEADS, IDX_DIM), wi_s.reshape(bs, IDX_HEADS, 1),
                           cache_kidx[0])
    mask, mnew = _select_sample(scores.reshape(bs, n_pages * PAGE), snew)
    o_s = _attn_sample(
        page_table, qpad.reshape(bs, N_HEADS, LANES),
        cache_k[0].reshape(n_phys, PAGE, LANES), cache_v[0].reshape(n_phys, PAGE, LANES),
        mask.reshape(bs, n_pages, 1, PAGE),
        lnew[:, :N_HEADS].reshape(bs, N_HEADS, 1),
        jnp.broadcast_to(mnew[:, :1], (bs, N_HEADS)).reshape(bs, N_HEADS, 1),
        v_s.reshape(bs, 1, LANES))
    y_sample = _out_proj_sample(x_sample[:, 0, :], o_s.reshape(bs, N_HEADS * LANES), sga_s, pg_s,
                                ada_s[:, 2 * D_MODEL:], w_out_b)

    return (
        y_prompt,
        y_sample[:, None, :],
        k_p.reshape(1, bp, s, KV_HEADS, HEAD_DIM),
        v_p.reshape(1, bp, s, KV_HEADS, HEAD_DIM),
        ki_p.reshape(1, bp, s, IDX_DIM),
        ulast[:, 1:, :][None],
        k_s.reshape(1, bs, 1, KV_HEADS, HEAD_DIM),
        v_s.reshape(1, bs, 1, KV_HEADS, HEAD_DIM),
        ki_s.reshape(1, bs, 1, IDX_DIM),
        jnp.concatenate([state_pool[0][:, 1:, :], u_s[:, None, :]], axis=1)[None],
    )
```

```python
import functools

import jax
import jax.numpy as jnp
import numpy as np
from jax import lax
from jax.experimental import pallas as pl
from jax.experimental.pallas import tpu as pltpu

F32 = jnp.float32
BF16 = jnp.bfloat16
I32 = jnp.int32

D_MODEL = 1024
ATTN_W = 512
POOL_W = 512
HEAD_DIM = 64
N_HEADS = 8
KV_HEADS = 2
IDX_HEADS = 8
IDX_DIM = 64
TOPK = 256
POOL_WINDOWS = (2, 4, 8, 16)
POOL_GC = 128
POOL_HIST = 15
EPS = 1e-6
PAGE = 128

LANES = 128
KEY_CHUNK = 256
TQ = 128
TM = 512
INT_MIN = np.int32(-2 ** 31)
NEG_BIG = -1e30
VMEM_LIMIT = 56 * 1024 * 1024

C_Q, C_K, C_V, C_QI, C_KW, C_GA, C_U, C_GP, N_PROJ = 0, 512, 640, 768, 1280, 1408, 1920, 2432, 2944


def _nt_dot(a, b):
    return lax.dot_general(a, b, (((1,), (1,)), ((), ())), preferred_element_type=F32)


def _dot(a, b):
    return jnp.dot(a, b, preferred_element_type=F32)


def _silu(z):
    return z / (1.0 + jnp.exp(-z))


def _lane_iota(shape):
    return lax.broadcasted_iota(I32, shape, len(shape) - 1)


def _seg_ones(n, seg):
    r = lax.broadcasted_iota(I32, (n, n), 0) // seg
    c = lax.broadcasted_iota(I32, (n, n), 1) // seg
    return jnp.where(r == c, 1.0, 0.0).astype(BF16)


def _head_rms(z, seg_mat, w):
    ss = _dot((z * z).astype(BF16), seg_mat)
    return z * lax.rsqrt(ss * (1.0 / HEAD_DIM) + EPS) * w


def _float_of_rank(u):
    key = u ^ INT_MIN
    bits = key ^ ((key >> 31) & np.int32(0x7FFFFFFF))
    return lax.bitcast_convert_type(bits, F32)


def _kth_largest(count_ge, shape):
    def bit_body(it, ans):
        cand = ans | jnp.left_shift(jnp.int32(1), 31 - it)
        return jnp.where(count_ge(_float_of_rank(cand)) >= float(TOPK), cand, ans)

    ans = lax.fori_loop(0, 32, bit_body, jnp.zeros(shape, I32))
    return jnp.where(ans == 0, -jnp.inf, _float_of_rank(ans))


def _ada_kernel(c_ref, w_ref, b_ref, o_ref):
    c = c_ref[...]
    o_ref[...] = _dot(_silu(c).astype(BF16), w_ref[...]) + b_ref[...]


def _ada(c_all, w_ada_b, b_ada):
    n = c_all.shape[0]
    return pl.pallas_call(
        _ada_kernel,
        out_shape=jax.ShapeDtypeStruct((n, 3 * D_MODEL), F32),
        grid=(3,),
        in_specs=[
            pl.BlockSpec((n, D_MODEL), lambda j: (0, 0)),
            pl.BlockSpec((D_MODEL, D_MODEL), lambda j: (0, j)),
            pl.BlockSpec((1, D_MODEL), lambda j: (0, j)),
        ],
        out_specs=pl.BlockSpec((n, D_MODEL), lambda j: (0, j)),
        compiler_params=pltpu.CompilerParams(dimension_semantics=("arbitrary",)),
        name="ada_ln",
    )(c_all, w_ada_b, b_ada)


def _modulated_norm(x, norm_w, scale, shift):
    ms = jnp.mean(x * x, axis=-1, keepdims=True)
    return (x * lax.rsqrt(ms + EPS)) * norm_w * (1.0 + scale) + shift


def _pool_mix(d, wpool_ref, pscale, gp):
    db = d.astype(BF16)
    cols = []
    for g in range(len(POOL_WINDOWS)):
        cols.append(_dot(db[:, g * POOL_GC:(g + 1) * POOL_GC], wpool_ref[g]))
    y = jnp.concatenate(cols, axis=1)
    return y * pscale * _silu(gp)


def _proj_prompt_kernel(x_ref, ada_ref, nw_ref, w_ref, qnw_ref, knw_ref, wpool_ref, ps_ref,
                        qs_ref, k_ref, kvar_ref, v_ref, vext_ref, qi_ref, ki_ref, kivar_ref,
                        wi_ref, sga_ref, pg_ref, ulast_ref,
                        ext_ref):
    t = pl.program_id(1)
    x = x_ref[0]
    ada = ada_ref[0]
    shift = ada[:, 0:D_MODEL]
    scale = ada[:, D_MODEL:2 * D_MODEL]
    hb = _modulated_norm(x, nw_ref[...], scale, shift).astype(BF16)

    lane = _lane_iota((TM, LANES))
    lo = lane < HEAD_DIM
    seg512 = _seg_ones(ATTN_W, HEAD_DIM)
    seg128 = _seg_ones(LANES, HEAD_DIM)

    q = _dot(hb, w_ref[:, C_Q:C_Q + ATTN_W])
    qs_ref[...] = _head_rms(q, seg512, qnw_ref[...]).astype(BF16)

    k = _head_rms(_dot(hb, w_ref[:, C_K:C_K + LANES]), seg128, knw_ref[...])
    k_ref[...] = k
    k_sw = pltpu.roll(k, HEAD_DIM, axis=1)
    kvar_ref[0] = jnp.where(lo, k, 0.0).astype(BF16)
    kvar_ref[1] = jnp.where(lo, 0.0, k_sw).astype(BF16)
    kvar_ref[2] = jnp.where(lo, k_sw, 0.0).astype(BF16)
    kvar_ref[3] = jnp.where(lo, 0.0, k).astype(BF16)

    v = _dot(hb, w_ref[:, C_V:C_V + LANES])
    v_ref[...] = v
    ones = jnp.ones((TM, LANES), BF16)
    vext_ref[0, :, 0:LANES] = v.astype(BF16)
    vext_ref[0, :, LANES:2 * LANES] = ones
    vext_ref[1, :, 0:LANES] = pltpu.roll(v, HEAD_DIM, axis=1).astype(BF16)
    vext_ref[1, :, LANES:2 * LANES] = ones

    qi_ref[...] = _dot(hb, w_ref[:, C_QI:C_QI + ATTN_W]).astype(BF16)
    kw = _dot(hb, w_ref[:, C_KW:C_KW + LANES])
    kw_sw = pltpu.roll(kw, HEAD_DIM, axis=1)
    ki_ref[...] = kw[:, 0:IDX_DIM]
    kivar_ref[0] = jnp.where(lo, kw, 0.0).astype(BF16)
    kivar_ref[1] = jnp.where(lo, 0.0, kw_sw).astype(BF16)
    wi_ref[...] = kw_sw[:, 0:IDX_HEADS] * ((IDX_HEADS * IDX_DIM) ** -0.5)

    sga_ref[...] = _silu(_dot(hb, w_ref[:, C_GA:C_GA + ATTN_W])).astype(BF16)

    u = _dot(hb, w_ref[:, C_U:C_U + POOL_W])
    gp = _dot(hb, w_ref[:, C_GP:C_GP + POOL_W])

    @pl.when(t == 0)
    def _():
        ext_ref[0:16, :] = jnp.zeros((16, POOL_W), F32)

    ext_ref[16:16 + TM, :] = u
    pos = t * TM + lax.broadcasted_iota(I32, (TM, POOL_GC), 0)
    ds = []
    for g, w in enumerate(POOL_WINDOWS):
        cs = slice(g * POOL_GC, (g + 1) * POOL_GC)
        s = u[:, cs]
        for j in range(1, w):
            s = s + ext_ref[16 - j:16 - j + TM, cs]
        cnt = jnp.minimum(pos + 1, w).astype(F32)
        ds.append(s / cnt - u[:, cs])
    d = jnp.concatenate(ds, axis=1)
    pg_ref[...] = _pool_mix(d, wpool_ref, ps_ref[...], gp).astype(BF16)

    tail = u[TM - 16:TM, :]
    ulast_ref[0] = tail
    ext_ref[0:16, :] = tail


def _proj_prompt(x, ada_p, norm_w, w_in_b, qnw, knw, wpool_b, pscale):
    b, s, _ = x.shape
    n = b * s
    nt = s // TM
    row = lambda bi, ti: (bi * nt + ti, 0)
    var = lambda bi, ti: (0, bi * nt + ti, 0)
    const2 = lambda bi, ti: (0, 0)
    const3 = lambda bi, ti: (0, 0, 0)
    out_shape = (
        jax.ShapeDtypeStruct((n, ATTN_W), BF16),
        jax.ShapeDtypeStruct((n, LANES), F32),
        jax.ShapeDtypeStruct((4, n, LANES), BF16),
        jax.ShapeDtypeStruct((n, LANES), F32),
        jax.ShapeDtypeStruct((2, n, 2 * LANES), BF16),
        jax.ShapeDtypeStruct((n, ATTN_W), BF16),
        jax.ShapeDtypeStruct((n, IDX_DIM), F32),
        jax.ShapeDtypeStruct((2, n, LANES), BF16),
        jax.ShapeDtypeStruct((n, IDX_HEADS), F32),
        jax.ShapeDtypeStruct((n, ATTN_W), BF16),
        jax.ShapeDtypeStruct((n, POOL_W), BF16),
        jax.ShapeDtypeStruct((b, 16, POOL_W), F32),
    )
    out_specs = (
        pl.BlockSpec((TM, ATTN_W), row),
        pl.BlockSpec((TM, LANES), row),
        pl.BlockSpec((4, TM, LANES), var),
        pl.BlockSpec((TM, LANES), row),
        pl.BlockSpec((2, TM, 2 * LANES), var),
        pl.BlockSpec((TM, ATTN_W), row),
        pl.BlockSpec((TM, IDX_DIM), row),
        pl.BlockSpec((2, TM, LANES), var),
        pl.BlockSpec((TM, IDX_HEADS), row),
        pl.BlockSpec((TM, ATTN_W), row),
        pl.BlockSpec((TM, POOL_W), row),
        pl.BlockSpec((1, 16, POOL_W), lambda bi, ti: (bi, 0, 0)),
    )
    in_specs = [
        pl.BlockSpec((1, TM, D_MODEL), lambda bi, ti: (bi, ti, 0)),
        pl.BlockSpec((1, 1, 3 * D_MODEL), lambda bi, ti: (bi, 0, 0)),
        pl.BlockSpec((1, D_MODEL), const2),
        pl.BlockSpec((D_MODEL, N_PROJ), const2),
        pl.BlockSpec((1, ATTN_W), const2),
        pl.BlockSpec((1, LANES), const2),
        pl.BlockSpec((4, POOL_GC, POOL_GC), const3),
        pl.BlockSpec((1, POOL_W), const2),
    ]
    return pl.pallas_call(
        _proj_prompt_kernel,
        out_shape=out_shape,
        grid=(b, nt),
        in_specs=in_specs,
        out_specs=out_specs,
        scratch_shapes=[pltpu.VMEM((16 + TM, POOL_W), F32)],
        compiler_params=pltpu.CompilerParams(
            dimension_semantics=("arbitrary", "arbitrary"), vmem_limit_bytes=VMEM_LIMIT),
        name="proj_prompt",
    )(x, ada_p, norm_w, w_in_b, qnw, knw, wpool_b, pscale)


def _attn_prompt_kernel(qi_ref, wi_ref, qs_ref, sga_ref, kiv_ref, kv_ref, vx_ref, ag_ref,
                        sc_ref, lg_ref, wib_ref, m_ref, acc_ref):
    i = pl.program_id(1)
    nch = (i + 2) // 2
    row0 = i * TQ
    shape2 = (TQ, KEY_CHUNK)

    wi = wi_ref[...]
    for h in range(IDX_HEADS):
        wib_ref[h] = jnp.broadcast_to(wi[:, h:h + 1], shape2)

    def score_chunk(c, carry):
        k0 = pl.multiple_of(c * KEY_CHUNK, KEY_CHUNK)
        acc = jnp.zeros(shape2, F32)
        for hp in range(IDX_HEADS // 2):
            qpair = qi_ref[:, hp * LANES:(hp + 1) * LANES]
            for par in range(2):
                s = _nt_dot(qpair, kiv_ref[par, pl.ds(k0, KEY_CHUNK), :])
                acc = acc + jnp.maximum(s, 0.0) * wib_ref[2 * hp + par]
        kpos = k0 + _lane_iota(shape2)
        qpos = row0 + lax.broadcasted_iota(I32, shape2, 0)
        sc_ref[c] = jnp.where(kpos <= qpos, acc, -jnp.inf)
        return carry

    lax.fori_loop(0, nch, score_chunk, 0)

    ones_mat = jnp.ones((LANES, LANES), BF16)

    def count_pass(pred):
        def body(c, acc):
            sc = sc_ref[c]
            return acc + pred(sc[:, 0:LANES]).astype(F32) + pred(sc[:, LANES:KEY_CHUNK]).astype(F32)
        part = lax.fori_loop(0, nch, body, jnp.zeros((TQ, LANES), F32))
        return _dot(part.astype(BF16), ones_mat)

    thr = _kth_largest(lambda t: count_pass(lambda x: x >= t), (TQ, LANES))
    need = float(TOPK) - count_pass(lambda x: x > thr)
    thr2 = jnp.concatenate([thr, thr], axis=1)
    need2 = jnp.concatenate([need, need], axis=1)

    tri = jnp.where(lax.broadcasted_iota(I32, (KEY_CHUNK, KEY_CHUNK), 0)
                    <= lax.broadcasted_iota(I32, (KEY_CHUNK, KEY_CHUNK), 1), 1.0, 0.0).astype(BF16)
    ones_cl = jnp.ones((KEY_CHUNK, LANES), BF16)

    for h in range(N_HEADS):
        m_ref[h] = jnp.full((TQ, LANES), -jnp.inf, F32)
        acc_ref[h] = jnp.zeros(shape2, F32)

    def logit_chunk(c, eq_before):
        k0 = pl.multiple_of(c * KEY_CHUNK, KEY_CHUNK)
        sc = sc_ref[c]
        eq = sc == thr2
        eqb = jnp.where(eq, 1.0, 0.0).astype(BF16)
        rank = _dot(eqb, tri) + jnp.concatenate([eq_before, eq_before], axis=1)
        sel = (sc > -jnp.inf) & ((sc > thr2) | (eq & (rank <= need2)))
        for h in range(N_HEADS):
            hp, par, g = h // 2, h % 2, h // (N_HEADS // KV_HEADS)
            qpair = qs_ref[:, hp * LANES:(hp + 1) * LANES]
            lg = _nt_dot(qpair, kv_ref[2 * g + par, pl.ds(k0, KEY_CHUNK), :])
            lg = jnp.where(sel, lg, -jnp.inf)
            lg_ref[h, c] = lg
            m_ref[h] = jnp.maximum(m_ref[h], jnp.maximum(lg[:, 0:LANES], lg[:, LANES:KEY_CHUNK]))
        return eq_before + _dot(eqb, ones_cl)

    lax.fori_loop(0, nch, logit_chunk, jnp.zeros((TQ, LANES), F32))

    for h in range(N_HEADS):
        m_ref[h] = jnp.broadcast_to(jnp.max(m_ref[h], axis=1, keepdims=True), (TQ, LANES))

    def pv_chunk(c, carry):
        k0 = pl.multiple_of(c * KEY_CHUNK, KEY_CHUNK)
        for h in range(N_HEADS):
            par, g = h % 2, h // (N_HEADS // KV_HEADS)
            m = m_ref[h]
            p = jnp.exp(lg_ref[h, c] - jnp.concatenate([m, m], axis=1)).astype(BF16)
            vx = vx_ref[0 if g == par else 1, pl.ds(k0, KEY_CHUNK), :]
            acc_ref[h] += _dot(p, vx)
        return carry

    lax.fori_loop(0, nch, pv_chunk, 0)

    lo = _lane_iota((TQ, LANES)) < HEAD_DIM
    for hp in range(N_HEADS // 2):
        a0 = acc_ref[2 * hp]
        a1 = acc_ref[2 * hp + 1]
        o0 = a0[:, 0:LANES] / a0[:, LANES:KEY_CHUNK]
        o1 = a1[:, 0:LANES] / a1[:, LANES:KEY_CHUNK]
        cols = slice(hp * LANES, (hp + 1) * LANES)
        ag_ref[:, cols] = (jnp.where(lo, o0, o1) * sga_ref[:, cols].astype(F32)).astype(BF16)


def _attn_prompt(qi, wi, qs, sga, kivar, kvar, vext, b, s):
    n = b * s
    nq = s // TQ
    nkc = s // KEY_CHUNK
    row = lambda bi, qi_: (bi * nq + qi_, 0)
    per_b = lambda bi, qi_: (0, bi, 0)
    return pl.pallas_call(
        _attn_prompt_kernel,
        out_shape=jax.ShapeDtypeStruct((n, ATTN_W), BF16),
        grid=(b, nq),
        in_specs=[
            pl.BlockSpec((TQ, ATTN_W), row),
            pl.BlockSpec((TQ, IDX_HEADS), row),
            pl.BlockSpec((TQ, ATTN_W), row),
            pl.BlockSpec((TQ, ATTN_W), row),
            pl.BlockSpec((2, s, LANES), per_b),
            pl.BlockSpec((4, s, LANES), per_b),
            pl.BlockSpec((2, s, 2 * LANES), per_b),
        ],
        out_specs=pl.BlockSpec((TQ, ATTN_W), row),
        scratch_shapes=[
            pltpu.VMEM((nkc, TQ, KEY_CHUNK), F32),
            pltpu.VMEM((N_HEADS, nkc, TQ, KEY_CHUNK), F32),
            pltpu.VMEM((IDX_HEADS, TQ, KEY_CHUNK), F32),
            pltpu.VMEM((N_HEADS, TQ, LANES), F32),
            pltpu.VMEM((N_HEADS, TQ, KEY_CHUNK), F32),
        ],
        compiler_params=pltpu.CompilerParams(
            dimension_semantics=("arbitrary", "arbitrary"), vmem_limit_bytes=VMEM_LIMIT),
        name="attn_prompt",
    )(qi, wi, qs, sga, kivar, kvar, vext)


def _out_proj_kernel(x_ref, a_ref, p_ref, gate_ref, w_ref, o_ref):
    y = _dot(a_ref[...], w_ref[0:ATTN_W, :]) + _dot(p_ref[...], w_ref[ATTN_W:ATTN_W + POOL_W, :])
    o_ref[0] = x_ref[0] + gate_ref[0] * y


def _out_proj_prompt(x, ag, pg, ada_p, w_out_b):
    b, s, _ = x.shape
    nt = s // TM
    row = lambda bi, ti: (bi * nt + ti, 0)
    return pl.pallas_call(
        _out_proj_kernel,
        out_shape=jax.ShapeDtypeStruct(x.shape, F32),
        grid=(b, nt),
        in_specs=[
            pl.BlockSpec((1, TM, D_MODEL), lambda bi, ti: (bi, ti, 0)),
            pl.BlockSpec((TM, ATTN_W), row),
            pl.BlockSpec((TM, POOL_W), row),
            pl.BlockSpec((1, 1, D_MODEL), lambda bi, ti: (bi, 0, 2)),
            pl.BlockSpec((D_MODEL, D_MODEL), lambda bi, ti: (0, 0)),
        ],
        out_specs=pl.BlockSpec((1, TM, D_MODEL), lambda bi, ti: (bi, ti, 0)),
        compiler_params=pltpu.CompilerParams(
            dimension_semantics=("arbitrary", "arbitrary"), vmem_limit_bytes=VMEM_LIMIT),
        name="out_proj_prompt",
    )(x, ag, pg, ada_p, w_out_b)


def _proj_sample_kernel(x_ref, ada_ref, nw_ref, w_ref, qnw_ref, knw_ref, wpool_ref, ps_ref, hist_ref,
                        qpad_ref, k_ref, v_ref, ki_ref, qi_ref, wi_ref, snew_ref, lnew_ref,
                        sga_ref, pg_ref, u_ref):
    nb = x_ref.shape[0]
    x = x_ref[...]
    shift = ada_ref[:, 0:D_MODEL]
    scale = ada_ref[:, D_MODEL:2 * D_MODEL]
    hb = _modulated_norm(x, nw_ref[...], scale, shift).astype(BF16)

    lane = _lane_iota((nb, LANES))
    lo = lane < HEAD_DIM
    seg512 = _seg_ones(ATTN_W, HEAD_DIM)
    seg128 = _seg_ones(LANES, HEAD_DIM)
    head_sel = jnp.where(lax.broadcasted_iota(I32, (ATTN_W, LANES), 0) // HEAD_DIM
                         == lax.broadcasted_iota(I32, (ATTN_W, LANES), 1), 1.0, 0.0).astype(BF16)

    def head_sums(prod):
        hi = prod.astype(BF16)
        rest = (prod - hi.astype(F32)).astype(BF16)
        return _dot(hi, head_sel) + _dot(rest, head_sel)

    q = _head_rms(_dot(hb, w_ref[:, C_Q:C_Q + ATTN_W]), seg512, qnw_ref[...])
    qb = q.astype(BF16)
    k = _head_rms(_dot(hb, w_ref[:, C_K:C_K + LANES]), seg128, knw_ref[...])
    k_ref[...] = k
    v = _dot(hb, w_ref[:, C_V:C_V + LANES])
    v_ref[...] = v

    for hp in range(N_HEADS // 2):
        pair = q[:, hp * LANES:(hp + 1) * LANES]
        pair_sw = pltpu.roll(pair, HEAD_DIM, axis=1)
        g = (2 * hp) // (N_HEADS // KV_HEADS)
        if g == 0:
            h_even, h_odd = jnp.where(lo, pair, 0.0), jnp.where(lo, pair_sw, 0.0)
        else:
            h_even, h_odd = jnp.where(lo, 0.0, pair_sw), jnp.where(lo, 0.0, pair)
        qpad_ref[:, (2 * hp) * LANES:(2 * hp + 1) * LANES] = h_even.astype(BF16)
        qpad_ref[:, (2 * hp + 1) * LANES:(2 * hp + 2) * LANES] = h_odd.astype(BF16)

    kq = k.astype(BF16).astype(F32)
    kq_sw = pltpu.roll(kq, HEAD_DIM, axis=1)
    k0t = jnp.where(lo, kq, kq_sw)
    k1t = jnp.where(lo, kq_sw, kq)
    qf = qb.astype(F32)
    prod = jnp.concatenate([qf[:, 0:LANES] * k0t, qf[:, LANES:2 * LANES] * k0t,
                            qf[:, 2 * LANES:3 * LANES] * k1t, qf[:, 3 * LANES:4 * LANES] * k1t], axis=1)
    lnew_ref[...] = head_sums(prod)

    qi = _dot(hb, w_ref[:, C_QI:C_QI + ATTN_W])
    qib = qi.astype(BF16)
    qi_ref[...] = qib
    kw = _dot(hb, w_ref[:, C_KW:C_KW + LANES])
    kw_sw = pltpu.roll(kw, HEAD_DIM, axis=1)
    ki_ref[...] = kw[:, 0:IDX_DIM]
    wi_full = jnp.where(lane < IDX_HEADS, kw_sw, 0.0) * ((IDX_HEADS * IDX_DIM) ** -0.5)
    wi_ref[...] = wi_full[:, 0:IDX_HEADS]

    kib = kw.astype(BF16).astype(F32)
    kit = jnp.where(lo, kib, pltpu.roll(kib, HEAD_DIM, axis=1))
    qif = qib.astype(F32)
    prod_i = jnp.concatenate([qif[:, j * LANES:(j + 1) * LANES] * kit for j in range(4)], axis=1)
    s_new = jnp.maximum(head_sums(prod_i), 0.0) * wi_full
    snew_ref[...] = jnp.broadcast_to(jnp.sum(s_new, axis=1, keepdims=True), (nb, LANES))

    sga_ref[...] = _silu(_dot(hb, w_ref[:, C_GA:C_GA + ATTN_W])).astype(BF16)

    u = _dot(hb, w_ref[:, C_U:C_U + POOL_W])
    gp = _dot(hb, w_ref[:, C_GP:C_GP + POOL_W])
    u_ref[...] = u
    ds = []
    for g, w in enumerate(POOL_WINDOWS):
        cs = slice(g * POOL_GC, (g + 1) * POOL_GC)
        s = u[:, cs]
        for j in range(1, w):
            s = s + hist_ref[POOL_HIST - j, :, cs]
        ds.append(s / float(w) - u[:, cs])
    d = jnp.concatenate(ds, axis=1)
    pg_ref[...] = _pool_mix(d, wpool_ref, ps_ref[...], gp).astype(BF16)


def _proj_sample(x, ada_s, norm_w, w_in_b, qnw, knw, wpool_b, pscale, hist_t):
    nb = x.shape[0]
    out_shape = (
        jax.ShapeDtypeStruct((nb, N_HEADS * LANES), BF16),
        jax.ShapeDtypeStruct((nb, LANES), F32),
        jax.ShapeDtypeStruct((nb, LANES), F32),
        jax.ShapeDtypeStruct((nb, IDX_DIM), F32),
        jax.ShapeDtypeStruct((nb, ATTN_W), BF16),
        jax.ShapeDtypeStruct((nb, IDX_HEADS), F32),
        jax.ShapeDtypeStruct((nb, LANES), F32),
        jax.ShapeDtypeStruct((nb, LANES), F32),
        jax.ShapeDtypeStruct((nb, ATTN_W), BF16),
        jax.ShapeDtypeStruct((nb, POOL_W), BF16),
        jax.ShapeDtypeStruct((nb, POOL_W), F32),
    )
    return pl.pallas_call(
        _proj_sample_kernel,
        out_shape=out_shape,
        compiler_params=pltpu.CompilerParams(vmem_limit_bytes=VMEM_LIMIT),
        name="proj_sample",
    )(x, ada_s, norm_w, w_in_b, qnw, knw, wpool_b, pscale, hist_t)


def _page_copies(pt_ref, pages_hbm, buf_ref, sem_ref, step, slot, group, n_pages):
    copies = []
    for j in range(group):
        for p in range(n_pages):
            page = pt_ref[step * group + j, p]
            copies.append(pltpu.make_async_copy(
                pages_hbm.at[page], buf_ref.at[slot, j, :, pl.ds(p * PAGE, PAGE)], sem_ref.at[slot]))
    return copies


def _score_sample_kernel(pt_ref, qi_ref, wi_ref, kidx_hbm, o_ref, buf_ref, sem_ref, *, group, n_pages):
    step = pl.program_id(0)
    slot = lax.rem(step, 2)

    def copies(st, sl):
        return _page_copies(pt_ref, kidx_hbm, buf_ref, sem_ref, st, sl, group, n_pages)

    @pl.when(step == 0)
    def _():
        for cp in copies(step, slot):
            cp.start()

    @pl.when(step + 1 < pl.num_programs(0))
    def _():
        for cp in copies(step + 1, 1 - slot):
            cp.start()

    for cp in copies(step, slot):
        cp.wait()

    for j in range(group):
        ki_t = buf_ref[slot, j].astype(BF16)
        s = _dot(qi_ref[j], ki_t)
        o_ref[j:j + 1, :] = jnp.sum(jnp.maximum(s, 0.0) * wi_ref[j], axis=0, keepdims=True)


SCORE_GROUP = 8


def _score_sample(page_table, qi3, wi3, kidx_t):
    nb, n_pages = page_table.shape
    n_keys = n_pages * PAGE
    g = SCORE_GROUP
    return pl.pallas_call(
        functools.partial(_score_sample_kernel, group=g, n_pages=n_pages),
        out_shape=jax.ShapeDtypeStruct((nb, n_keys), F32),
        grid_spec=pltpu.PrefetchScalarGridSpec(
            num_scalar_prefetch=1,
            grid=(nb // g,),
            in_specs=[
                pl.BlockSpec((g, IDX_HEADS, IDX_DIM), lambda s, pt: (s, 0, 0)),
                pl.BlockSpec((g, IDX_HEADS, 1), lambda s, pt: (s, 0, 0)),
                pl.BlockSpec(memory_space=pl.ANY),
            ],
            out_specs=pl.BlockSpec((g, n_keys), lambda s, pt: (s, 0)),
            scratch_shapes=[
                pltpu.VMEM((2, g, IDX_DIM, n_keys), F32),
                pltpu.SemaphoreType.DMA((2,)),
            ],
        ),
        compiler_params=pltpu.CompilerParams(
            dimension_semantics=("arbitrary",), vmem_limit_bytes=VMEM_LIMIT),
        name="score_sample",
    )(page_table, qi3, wi3, kidx_t)


def _select_sample_kernel(sc_ref, snew_ref, mask_ref, mnew_ref):
    nb, n_keys = sc_ref.shape
    nch = n_keys // KEY_CHUNK
    snew = snew_ref[...]
    ones_mat = jnp.ones((LANES, LANES), BF16)

    def count_pass(pred):
        acc = jnp.zeros((nb, LANES), F32)
        for c in range(n_keys // LANES):
            acc = acc + pred(sc_ref[:, c * LANES:(c + 1) * LANES]).astype(F32)
        return _dot(acc.astype(BF16), ones_mat) + pred(snew).astype(F32)

    thr = _kth_largest(lambda t: count_pass(lambda x: x >= t), (nb, LANES))
    need = float(TOPK) - count_pass(lambda x: x > thr)
    thr2 = jnp.concatenate([thr, thr], axis=1)
    need2 = jnp.concatenate([need, need], axis=1)
    tri = jnp.where(lax.broadcasted_iota(I32, (KEY_CHUNK, KEY_CHUNK), 0)
                    <= lax.broadcasted_iota(I32, (KEY_CHUNK, KEY_CHUNK), 1), 1.0, 0.0).astype(BF16)
    ones_cl = jnp.ones((KEY_CHUNK, LANES), BF16)

    eq_before = jnp.zeros((nb, LANES), F32)
    for c in range(nch):
        sc = sc_ref[:, c * KEY_CHUNK:(c + 1) * KEY_CHUNK]
        eq = sc == thr2
        eqb = jnp.where(eq, 1.0, 0.0).astype(BF16)
        rank = _dot(eqb, tri) + jnp.concatenate([eq_before, eq_before], axis=1)
        sel = (sc > thr2) | (eq & (rank <= need2))
        mask_ref[:, c * KEY_CHUNK:(c + 1) * KEY_CHUNK] = jnp.where(sel, 0.0, -jnp.inf)
        eq_before = eq_before + _dot(eqb, ones_cl)
    sel_new = (snew > thr) | ((snew == thr) & (eq_before + 1.0 <= need))
    mnew_ref[...] = jnp.where(sel_new, 0.0, -jnp.inf)


def _select_sample(scores, snew):
    nb, n_keys = scores.shape
    return pl.pallas_call(
        _select_sample_kernel,
        out_shape=(jax.ShapeDtypeStruct((nb, n_keys), F32), jax.ShapeDtypeStruct((nb, LANES), F32)),
        name="select_sample",
    )(scores, snew)


def _attn_sample_kernel(pt_ref, q_ref, mask_ref, lnew_ref, mnew_ref, vnew_ref, k_hbm, v_hbm,
                        o_ref, kbuf_ref, vbuf_ref, ksem_ref, vsem_ref, *, group, n_pages):
    step = pl.program_id(0)
    slot = lax.rem(step, 2)

    def copies(st, sl):
        return (_page_copies(pt_ref, k_hbm, kbuf_ref, ksem_ref, st, sl, group, n_pages)
                + _page_copies(pt_ref, v_hbm, vbuf_ref, vsem_ref, st, sl, group, n_pages))

    @pl.when(step == 0)
    def _():
        for cp in copies(step, slot):
            cp.start()

    @pl.when(step + 1 < pl.num_programs(0))
    def _():
        for cp in copies(step + 1, 1 - slot):
            cp.start()

    for cp in copies(step, slot):
        cp.wait()

    for j in range(group):
        k_t = kbuf_ref[slot, j].astype(BF16)
        lg = _dot(q_ref[j], k_t) + mask_ref[j]
        lg_n = lnew_ref[j] + mnew_ref[j]
        m = jnp.maximum(jnp.max(lg, axis=1, keepdims=True), lg_n)
        p = jnp.exp(lg - m)
        p_n = jnp.exp(lg_n - m)
        l = jnp.sum(p, axis=1, keepdims=True) + p_n
        acc = _nt_dot(p.astype(BF16), vbuf_ref[slot, j].astype(BF16)) + p_n * vnew_ref[j]
        o_ref[j] = acc / l


ATTN_GROUP = 4


def _attn_sample(page_table, qpad3, mask3, lnew3, mnew3, vnew3, k_t, v_t):
    nb, n_pages = page_table.shape
    n_keys = n_pages * PAGE
    g = ATTN_GROUP
    per_s = lambda s, pt: (s, 0, 0)
    return pl.pallas_call(
        functools.partial(_attn_sample_kernel, group=g, n_pages=n_pages),
        out_shape=jax.ShapeDtypeStruct((nb, N_HEADS, LANES), F32),
        grid_spec=pltpu.PrefetchScalarGridSpec(
            num_scalar_prefetch=1,
            grid=(nb // g,),
            in_specs=[
                pl.BlockSpec((g, N_HEADS, LANES), per_s),
                pl.BlockSpec((g, 1, n_keys), per_s),
                pl.BlockSpec((g, N_HEADS, 1), per_s),
                pl.BlockSpec((g, N_HEADS, 1), per_s),
                pl.BlockSpec((g, 1, LANES), per_s),
                pl.BlockSpec(memory_space=pl.ANY),
                pl.BlockSpec(memory_space=pl.ANY),
            ],
            out_specs=pl.BlockSpec((g, N_HEADS, LANES), per_s),
            scratch_shapes=[
                pltpu.VMEM((2, g, LANES, n_keys), F32),
                pltpu.VMEM((2, g, LANES, n_keys), F32),
                pltpu.SemaphoreType.DMA((2,)),
                pltpu.SemaphoreType.DMA((2,)),
            ],
        ),
        compiler_params=pltpu.CompilerParams(
            dimension_semantics=("arbitrary",), vmem_limit_bytes=VMEM_LIMIT),
        name="attn_sample",
    )(page_table, qpad3, mask3, lnew3, mnew3, vnew3, k_t, v_t)


def _out_proj_sample_kernel(x_ref, a_ref, sga_ref, p_ref, gate_ref, w_ref, o_ref):
    lo = _lane_iota((x_ref.shape[0], LANES)) < HEAD_DIM
    pairs = []
    for hp in range(N_HEADS // 2):
        even = a_ref[:, (2 * hp) * LANES:(2 * hp + 1) * LANES]
        odd = a_ref[:, (2 * hp + 1) * LANES:(2 * hp + 2) * LANES]
        if (2 * hp) // (N_HEADS // KV_HEADS) == 0:
            pairs.append(jnp.where(lo, even, pltpu.roll(odd, HEAD_DIM, axis=1)))
        else:
            pairs.append(jnp.where(lo, pltpu.roll(even, HEAD_DIM, axis=1), odd))
    a = jnp.concatenate(pairs, axis=1)
    ag = (a * sga_ref[...].astype(F32)).astype(BF16)
    y = _dot(ag, w_ref[0:ATTN_W, :]) + _dot(p_ref[...], w_ref[ATTN_W:ATTN_W + POOL_W, :])
    o_ref[...] = x_ref[...] + gate_ref[...] * y


def _out_proj_sample(x, a, sga, pg, gate, w_out_b):
    return pl.pallas_call(
        _out_proj_sample_kernel,
        out_shape=jax.ShapeDtypeStruct(x.shape, F32),
        compiler_params=pltpu.CompilerParams(vmem_limit_bytes=VMEM_LIMIT),
        name="out_proj_sample",
    )(x, a, sga, pg, gate, w_out_b)


def _permute_w_in(w_in):
    q, k, v, qi, ki, wi, ga, u, gp = jnp.split(
        w_in, np.cumsum([512, 128, 128, 512, 64, 8, 512, 512]).tolist(), axis=1)
    pad = jnp.zeros((w_in.shape[0], LANES - IDX_DIM - IDX_HEADS), w_in.dtype)
    return jnp.concatenate([q, k, v, qi, ki, wi, pad, ga, u, gp], axis=1).astype(BF16)


def kernel(x_prompt, x_sample, cache_k, cache_v, cache_kidx, state_pool, page_table, c_prompt, c_sample,
           norm_w, w_ada, b_ada, w_in, q_norm_w, k_norm_w, w_pool, pool_scale, w_out):
    bp, s, _ = x_prompt.shape
    bs = x_sample.shape[0]
    assert w_in.shape[0] == 1 and x_sample.shape[1] == 1, "single layer, single decode token"
    n_phys = cache_k.shape[1]

    w_in_b = _permute_w_in(w_in[0])
    w_ada_b = w_ada[0].astype(BF16)
    w_out_b = w_out[0].astype(BF16)
    wpool_b = w_pool[0].astype(BF16)
    qnw = jnp.tile(q_norm_w[0], N_HEADS)[None, :] * (HEAD_DIM ** -0.5)
    knw = jnp.tile(k_norm_w[0], KV_HEADS)[None, :]
    nw = norm_w[0][None, :]
    pscale = pool_scale[0][None, :]

    ada = _ada(jnp.concatenate([c_prompt, c_sample], axis=0), w_ada_b, b_ada)
    ada_p = ada[:bp].reshape(bp, 1, 3 * D_MODEL)
    ada_s = ada[bp:]

    (qs, k_p, kvar, v_p, vext, qi, ki_p, kivar, wi, sga, pg, ulast) = _proj_prompt(
        x_prompt, ada_p, nw, w_in_b, qnw, knw, wpool_b, pscale)
    ag = _attn_prompt(qi, wi, qs, sga, kivar, kvar, vext, bp, s)
    y_prompt = _out_proj_prompt(x_prompt, ag, pg, ada_p, w_out_b)

    hist_t = jnp.transpose(state_pool[0], (1, 0, 2))
    (qpad, k_s, v_s, ki_s, qi_s, wi_s, snew, lnew, sga_s, pg_s, u_s) = _proj_sample(
        x_sample[:, 0, :], ada_s, nw, w_in_b, qnw, knw, wpool_b, pscale, hist_t)
    n_pages = page_table.shape[1]
    kidx_t = jnp.transpose(cache_kidx[0], (0, 2, 1))
    k_t = jnp.transpose(cache_k[0], (0, 2, 3, 1)).reshape(n_phys, LANES, PAGE)
    v_t = jnp.transpose(cache_v[0], (0, 2, 3, 1)).reshape(n_phys, LANES, PAGE)
    scores = _score_sample(page_table, qi_s.reshape(bs, IDX_HEADS, IDX_DIM), wi_s.reshape(bs, IDX_HEADS, 1),
                           kidx_t)
    mask, mnew = _select_sample(scores, snew)
    o_s = _attn_sample(
        page_table, qpad.reshape(bs, N_HEADS, LANES),
        mask.reshape(bs, 1, n_pages * PAGE),
        lnew[:, :N_HEADS].reshape(bs, N_HEADS, 1),
        jnp.broadcast_to(mnew[:, :1], (bs, N_HEADS)).reshape(bs, N_HEADS, 1),
        v_s.reshape(bs, 1, LANES), k_t, v_t)
    y_sample = _out_proj_sample(x_sample[:, 0, :], o_s.reshape(bs, N_HEADS * LANES), sga_s, pg_s,
                                ada_s[:, 2 * D_MODEL:], w_out_b)

    return (
        y_prompt,
        y_sample[:, None, :],
        k_p.reshape(1, bp, s, KV_HEADS, HEAD_DIM),
        v_p.reshape(1, bp, s, KV_HEADS, HEAD_DIM),
        ki_p.reshape(1, bp, s, IDX_DIM),
        ulast[:, 1:, :][None],
        k_s.reshape(1, bs, 1, KV_HEADS, HEAD_DIM),
        v_s.reshape(1, bs, 1, KV_HEADS, HEAD_DIM),
        ki_s.reshape(1, bs, 1, IDX_DIM),
        jnp.concatenate([state_pool[0][:, 1:, :], u_s[:, None, :]], axis=1)[None],
    )
```

```python
import functools

import jax
import jax.numpy as jnp
import numpy as np
from jax import lax
from jax.experimental import pallas as pl
from jax.experimental.pallas import tpu as pltpu

F32 = jnp.float32
BF16 = jnp.bfloat16
I32 = jnp.int32

D_MODEL = 1024
ATTN_W = 512
POOL_W = 512
HEAD_DIM = 64
N_HEADS = 8
KV_HEADS = 2
IDX_HEADS = 8
IDX_DIM = 64
TOPK = 256
POOL_WINDOWS = (2, 4, 8, 16)
POOL_GC = 128
POOL_HIST = 15
EPS = 1e-6
PAGE = 128

LANES = 128
SUBLANES = 8
BF16_ROWS = 16
KEY_CHUNK = 256
TQ = 128
TM = 512
VT_ROWS = HEAD_DIM + BF16_ROWS
SCORE_GROUP = 8
ATTN_GROUP = 4
INT_MIN = np.int32(-2 ** 31)
VMEM_LIMIT = 56 * 1024 * 1024

C_Q, C_K, C_V, C_QI, C_KW, C_GA, C_U, C_GP, N_PROJ = 0, 512, 640, 768, 1280, 1408, 1920, 2432, 2944


def _nt_dot(a, b):
    return lax.dot_general(a, b, (((1,), (1,)), ((), ())), preferred_element_type=F32)


def _dot(a, b):
    return jnp.dot(a, b, preferred_element_type=F32)


def _silu(z):
    return z / (1.0 + jnp.exp(-z))


def _lane_iota(shape):
    return lax.broadcasted_iota(I32, shape, len(shape) - 1)


def _seg_ones(n, seg):
    r = lax.broadcasted_iota(I32, (n, n), 0) // seg
    c = lax.broadcasted_iota(I32, (n, n), 1) // seg
    return jnp.where(r == c, 1.0, 0.0).astype(BF16)


def _head_rms(z, seg_mat, w):
    ss = _dot((z * z).astype(BF16), seg_mat)
    return z * lax.rsqrt(ss * (1.0 / HEAD_DIM) + EPS) * w


def _float_of_rank(u):
    key = u ^ INT_MIN
    bits = key ^ ((key >> 31) & np.int32(0x7FFFFFFF))
    return lax.bitcast_convert_type(bits, F32)


def _kth_largest(count_ge, shape):
    def bit_body(it, ans):
        cand = ans | jnp.left_shift(jnp.int32(1), 31 - it)
        return jnp.where(count_ge(_float_of_rank(cand)) >= float(TOPK), cand, ans)

    ans = lax.fori_loop(0, 32, bit_body, jnp.zeros(shape, I32))
    return jnp.where(ans == 0, -jnp.inf, _float_of_rank(ans))


def _fold_rows(x, op):
    parts = [x[r:r + SUBLANES] for r in range(0, x.shape[0], SUBLANES)]
    while len(parts) > 1:
        parts = [op(parts[i], parts[i + 1]) for i in range(0, len(parts), 2)]
    return parts[0]


def _ada_kernel(c_ref, w_ref, b_ref, o_ref):
    c = c_ref[...]
    o_ref[...] = _dot(_silu(c).astype(BF16), w_ref[...]) + b_ref[...]


def _ada(c_all, w_ada_b, b_ada):
    n = c_all.shape[0]
    return pl.pallas_call(
        _ada_kernel,
        out_shape=jax.ShapeDtypeStruct((n, 3 * D_MODEL), F32),
        grid=(3,),
        in_specs=[
            pl.BlockSpec((n, D_MODEL), lambda j: (0, 0)),
            pl.BlockSpec((D_MODEL, D_MODEL), lambda j: (0, j)),
            pl.BlockSpec((1, D_MODEL), lambda j: (0, j)),
        ],
        out_specs=pl.BlockSpec((n, D_MODEL), lambda j: (0, j)),
        compiler_params=pltpu.CompilerParams(dimension_semantics=("arbitrary",)),
        name="ada_ln",
    )(c_all, w_ada_b, b_ada)


def _modulated_norm(x, norm_w, scale, shift):
    ms = jnp.mean(x * x, axis=-1, keepdims=True)
    return (x * lax.rsqrt(ms + EPS)) * norm_w * (1.0 + scale) + shift


def _pool_mix(d, wpool_ref, pscale, gp):
    db = d.astype(BF16)
    cols = []
    for g in range(len(POOL_WINDOWS)):
        cols.append(_dot(db[:, g * POOL_GC:(g + 1) * POOL_GC], wpool_ref[g]))
    y = jnp.concatenate(cols, axis=1)
    return y * pscale * _silu(gp)


def _proj_prompt_kernel(x_ref, ada_ref, nw_ref, w_ref, qnw_ref, knw_ref, wpool_ref, ps_ref,
                        qs_ref, kt_ref, kk_ref, vt_ref, vtx_ref, qi_ref, kit_ref, kik_ref,
                        wit_ref, sga_ref, pg_ref, ulast_ref,
                        ext_ref):
    t = pl.program_id(1)
    x = x_ref[0]
    ada = ada_ref[0]
    shift = ada[:, 0:D_MODEL]
    scale = ada[:, D_MODEL:2 * D_MODEL]
    hb = _modulated_norm(x, nw_ref[...], scale, shift).astype(BF16)

    lo = _lane_iota((TM, LANES)) < HEAD_DIM
    seg512 = _seg_ones(ATTN_W, HEAD_DIM)
    seg128 = _seg_ones(LANES, HEAD_DIM)

    q = _dot(hb, w_ref[:, C_Q:C_Q + ATTN_W])
    qs_ref[...] = _head_rms(q, seg512, qnw_ref[...]).astype(BF16)

    k = _head_rms(_dot(hb, w_ref[:, C_K:C_K + LANES]), seg128, knw_ref[...])
    kt_ref[0] = k.T
    k_sw = pltpu.roll(k, HEAD_DIM, axis=1)
    kk_ref[0] = jnp.where(lo, k, k_sw).astype(BF16)
    kk_ref[1] = jnp.where(lo, k_sw, k).astype(BF16)

    v_t = _dot(hb, w_ref[:, C_V:C_V + LANES]).T
    vt_ref[0] = v_t
    ones = jnp.ones((BF16_ROWS, KEY_CHUNK), BF16)
    for g in range(KV_HEADS):
        for c in range(TM // KEY_CHUNK):
            vtx_ref[0, g, c, 0:HEAD_DIM, :] = v_t[g * HEAD_DIM:(g + 1) * HEAD_DIM,
                                                  c * KEY_CHUNK:(c + 1) * KEY_CHUNK].astype(BF16)
            vtx_ref[0, g, c, HEAD_DIM:VT_ROWS, :] = ones

    qi_ref[...] = _dot(hb, w_ref[:, C_QI:C_QI + ATTN_W]).astype(BF16)
    kw = _dot(hb, w_ref[:, C_KW:C_KW + LANES])
    kw_t = kw.T
    kit_ref[0] = kw_t[0:IDX_DIM]
    wit_ref[...] = kw_t[IDX_DIM:IDX_DIM + IDX_HEADS] * ((IDX_HEADS * IDX_DIM) ** -0.5)
    kik_ref[...] = jnp.where(lo, kw, pltpu.roll(kw, HEAD_DIM, axis=1)).astype(BF16)

    sga_ref[...] = _silu(_dot(hb, w_ref[:, C_GA:C_GA + ATTN_W])).astype(BF16)

    u = _dot(hb, w_ref[:, C_U:C_U + POOL_W])
    gp = _dot(hb, w_ref[:, C_GP:C_GP + POOL_W])

    @pl.when(t == 0)
    def _():
        ext_ref[0:16, :] = jnp.zeros((16, POOL_W), F32)

    ext_ref[16:16 + TM, :] = u
    pos = t * TM + lax.broadcasted_iota(I32, (TM, POOL_GC), 0)
    ds = []
    for g, w in enumerate(POOL_WINDOWS):
        cs = slice(g * POOL_GC, (g + 1) * POOL_GC)
        s = u[:, cs]
        for j in range(1, w):
            s = s + ext_ref[16 - j:16 - j + TM, cs]
        cnt = jnp.minimum(pos + 1, w).astype(F32)
        ds.append(s / cnt - u[:, cs])
    d = jnp.concatenate(ds, axis=1)
    pg_ref[...] = _pool_mix(d, wpool_ref, ps_ref[...], gp).astype(BF16)

    tail = u[TM - 16:TM, :]
    ulast_ref[0] = tail
    ext_ref[0:16, :] = tail


def _proj_prompt(x, ada_p, norm_w, w_in_b, qnw, knw, wpool_b, pscale):
    b, s, _ = x.shape
    n = b * s
    nt = s // TM
    cpt = TM // KEY_CHUNK
    row = lambda bi, ti: (bi * nt + ti, 0)
    tok = lambda bi, ti: (bi, 0, ti)
    const2 = lambda bi, ti: (0, 0)
    const3 = lambda bi, ti: (0, 0, 0)
    out_shape = (
        jax.ShapeDtypeStruct((n, ATTN_W), BF16),
        jax.ShapeDtypeStruct((b, LANES, s), F32),
        jax.ShapeDtypeStruct((KV_HEADS, n, LANES), BF16),
        jax.ShapeDtypeStruct((b, LANES, s), F32),
        jax.ShapeDtypeStruct((b, KV_HEADS, s // KEY_CHUNK, VT_ROWS, KEY_CHUNK), BF16),
        jax.ShapeDtypeStruct((n, ATTN_W), BF16),
        jax.ShapeDtypeStruct((b, IDX_DIM, s), F32),
        jax.ShapeDtypeStruct((n, LANES), BF16),
        jax.ShapeDtypeStruct((IDX_HEADS, n), F32),
        jax.ShapeDtypeStruct((n, ATTN_W), BF16),
        jax.ShapeDtypeStruct((n, POOL_W), BF16),
        jax.ShapeDtypeStruct((b, 16, POOL_W), F32),
    )
    out_specs = (
        pl.BlockSpec((TM, ATTN_W), row),
        pl.BlockSpec((1, LANES, TM), tok),
        pl.BlockSpec((KV_HEADS, TM, LANES), lambda bi, ti: (0, bi * nt + ti, 0)),
        pl.BlockSpec((1, LANES, TM), tok),
        pl.BlockSpec((1, KV_HEADS, cpt, VT_ROWS, KEY_CHUNK), lambda bi, ti: (bi, 0, ti, 0, 0)),
        pl.BlockSpec((TM, ATTN_W), row),
        pl.BlockSpec((1, IDX_DIM, TM), tok),
        pl.BlockSpec((TM, LANES), row),
        pl.BlockSpec((IDX_HEADS, TM), lambda bi, ti: (0, bi * nt + ti)),
        pl.BlockSpec((TM, ATTN_W), row),
        pl.BlockSpec((TM, POOL_W), row),
        pl.BlockSpec((1, 16, POOL_W), lambda bi, ti: (bi, 0, 0)),
    )
    in_specs = [
        pl.BlockSpec((1, TM, D_MODEL), lambda bi, ti: (bi, ti, 0)),
        pl.BlockSpec((1, 1, 3 * D_MODEL), lambda bi, ti: (bi, 0, 0)),
        pl.BlockSpec((1, D_MODEL), const2),
        pl.BlockSpec((D_MODEL, N_PROJ), const2),
        pl.BlockSpec((1, ATTN_W), const2),
        pl.BlockSpec((1, LANES), const2),
        pl.BlockSpec((4, POOL_GC, POOL_GC), const3),
        pl.BlockSpec((1, POOL_W), const2),
    ]
    return pl.pallas_call(
        _proj_prompt_kernel,
        out_shape=out_shape,
        grid=(b, nt),
        in_specs=in_specs,
        out_specs=out_specs,
        scratch_shapes=[pltpu.VMEM((16 + TM, POOL_W), F32)],
        compiler_params=pltpu.CompilerParams(
            dimension_semantics=("arbitrary", "arbitrary"), vmem_limit_bytes=VMEM_LIMIT),
        name="proj_prompt",
    )(x, ada_p, norm_w, w_in_b, qnw, knw, wpool_b, pscale)


def _attn_prompt_kernel(qi_ref, wit_ref, qs_ref, sga_ref, kik_ref, kk_ref, vtx_ref, ag_ref,
                        sc_ref, lg_ref, wq_ref, m_ref, acc_ref):
    i = pl.program_id(1)
    nch = (i + 2) // 2
    row0 = i * TQ
    shape2 = (KEY_CHUNK, TQ)
    n_pairs = N_HEADS // 2
    lo = _lane_iota((TQ, LANES)) < HEAD_DIM

    for kind, ref in enumerate((qi_ref, qs_ref)):
        for hp in range(n_pairs):
            pair = ref[:, hp * LANES:(hp + 1) * LANES].astype(F32)
            wq_ref[kind, hp, 0:TQ, :] = jnp.where(lo, pair, 0.0).astype(BF16)
            wq_ref[kind, hp, TQ:2 * TQ, :] = jnp.where(lo, 0.0, pair).astype(BF16)

    def score_chunk(c, carry):
        k0 = pl.multiple_of(c * KEY_CHUNK, KEY_CHUNK)
        kk = kik_ref[pl.ds(k0, KEY_CHUNK), :]
        acc = jnp.zeros(shape2, F32)
        for hp in range(n_pairs):
            s2 = _nt_dot(kk, wq_ref[0, hp])
            acc = acc + jnp.maximum(s2[:, 0:TQ], 0.0) * wit_ref[2 * hp:2 * hp + 1, :]
            acc = acc + jnp.maximum(s2[:, TQ:2 * TQ], 0.0) * wit_ref[2 * hp + 1:2 * hp + 2, :]
        kpos = k0 + lax.broadcasted_iota(I32, shape2, 0)
        qpos = row0 + lax.broadcasted_iota(I32, shape2, 1)
        sc_ref[c] = jnp.where(kpos <= qpos, acc, -jnp.inf)
        return carry

    lax.fori_loop(0, nch, score_chunk, 0)

    def count_pass(pred):
        def body(c, acc):
            return acc + _fold_rows(pred(sc_ref[c]).astype(F32), jnp.add)
        part = lax.fori_loop(0, nch, body, jnp.zeros((SUBLANES, TQ), F32))
        return jnp.sum(part, axis=0, keepdims=True)

    thr = _kth_largest(lambda t: count_pass(lambda x: x >= t), (1, TQ))
    need = float(TOPK) - count_pass(lambda x: x > thr)

    tri = jnp.where(lax.broadcasted_iota(I32, (KEY_CHUNK, KEY_CHUNK), 1)
                    <= lax.broadcasted_iota(I32, (KEY_CHUNK, KEY_CHUNK), 0), 1.0, 0.0).astype(BF16)

    for h in range(N_HEADS):
        m_ref[h] = jnp.full((SUBLANES, TQ), -jnp.inf, F32)
    for hp in range(n_pairs):
        acc_ref[hp] = jnp.zeros((VT_ROWS, 2 * TQ), F32)

    def logit_chunk(c, eq_before):
        k0 = pl.multiple_of(c * KEY_CHUNK, KEY_CHUNK)
        sc = sc_ref[c]
        eq = sc == thr
        rank = _dot(tri, jnp.where(eq, 1.0, 0.0).astype(BF16)) + eq_before
        sel = (sc > -jnp.inf) & ((sc > thr) | (eq & (rank <= need)))
        for hp in range(n_pairs):
            g = (2 * hp) // (N_HEADS // KV_HEADS)
            lg2 = _nt_dot(kk_ref[g, pl.ds(k0, KEY_CHUNK), :], wq_ref[1, hp])
            for par in range(2):
                h = 2 * hp + par
                lg = jnp.where(sel, lg2[:, par * TQ:(par + 1) * TQ], -jnp.inf)
                lg_ref[h, c] = lg
                m_ref[h] = jnp.maximum(m_ref[h], _fold_rows(lg, jnp.maximum))
        return rank[KEY_CHUNK - 1:KEY_CHUNK, :]

    lax.fori_loop(0, nch, logit_chunk, jnp.zeros((1, TQ), F32))

    m_rows = [jnp.max(m_ref[h], axis=0, keepdims=True) for h in range(N_HEADS)]

    def pv_chunk(c, carry):
        for hp in range(n_pairs):
            g = (2 * hp) // (N_HEADS // KV_HEADS)
            p2 = jnp.concatenate(
                [jnp.exp(lg_ref[2 * hp + par, c] - m_rows[2 * hp + par]).astype(BF16) for par in range(2)],
                axis=1)
            acc_ref[hp] += _dot(vtx_ref[0, g, c], p2)
        return carry

    lax.fori_loop(0, nch, pv_chunk, 0)

    for hp in range(n_pairs):
        a = acc_ref[hp]
        o0 = a[0:HEAD_DIM, 0:TQ] / a[HEAD_DIM:HEAD_DIM + 1, 0:TQ]
        o1 = a[0:HEAD_DIM, TQ:2 * TQ] / a[HEAD_DIM:HEAD_DIM + 1, TQ:2 * TQ]
        pair = jnp.concatenate([o0, o1], axis=0).T
        cols = slice(hp * LANES, (hp + 1) * LANES)
        ag_ref[:, cols] = (pair * sga_ref[:, cols].astype(F32)).astype(BF16)


def _attn_prompt(qi, wit, qs, sga, kik, kk, vtx, b, s):
    n = b * s
    nq = s // TQ
    nkc = s // KEY_CHUNK
    row = lambda bi, qi_: (bi * nq + qi_, 0)
    return pl.pallas_call(
        _attn_prompt_kernel,
        out_shape=jax.ShapeDtypeStruct((n, ATTN_W), BF16),
        grid=(b, nq),
        in_specs=[
            pl.BlockSpec((TQ, ATTN_W), row),
            pl.BlockSpec((IDX_HEADS, TQ), lambda bi, qi_: (0, bi * nq + qi_)),
            pl.BlockSpec((TQ, ATTN_W), row),
            pl.BlockSpec((TQ, ATTN_W), row),
            pl.BlockSpec((s, LANES), lambda bi, qi_: (bi, 0)),
            pl.BlockSpec((KV_HEADS, s, LANES), lambda bi, qi_: (0, bi, 0)),
            pl.BlockSpec((1, KV_HEADS, nkc, VT_ROWS, KEY_CHUNK), lambda bi, qi_: (bi, 0, 0, 0, 0)),
        ],
        out_specs=pl.BlockSpec((TQ, ATTN_W), row),
        scratch_shapes=[
            pltpu.VMEM((nkc, KEY_CHUNK, TQ), F32),
            pltpu.VMEM((N_HEADS, nkc, KEY_CHUNK, TQ), F32),
            pltpu.VMEM((2, N_HEADS // 2, 2 * TQ, LANES), BF16),
            pltpu.VMEM((N_HEADS, SUBLANES, TQ), F32),
            pltpu.VMEM((N_HEADS // 2, VT_ROWS, 2 * TQ), F32),
        ],
        compiler_params=pltpu.CompilerParams(
            dimension_semantics=("arbitrary", "arbitrary"), vmem_limit_bytes=VMEM_LIMIT),
        name="attn_prompt",
    )(qi, wit, qs, sga, kik, kk, vtx)


def _out_proj_kernel(x_ref, a_ref, p_ref, gate_ref, w_ref, o_ref):
    y = _dot(a_ref[...], w_ref[0:ATTN_W, :]) + _dot(p_ref[...], w_ref[ATTN_W:ATTN_W + POOL_W, :])
    o_ref[0] = x_ref[0] + gate_ref[0] * y


def _out_proj_prompt(x, ag, pg, ada_p, w_out_b):
    b, s, _ = x.shape
    nt = s // TM
    row = lambda bi, ti: (bi * nt + ti, 0)
    return pl.pallas_call(
        _out_proj_kernel,
        out_shape=jax.ShapeDtypeStruct(x.shape, F32),
        grid=(b, nt),
        in_specs=[
            pl.BlockSpec((1, TM, D_MODEL), lambda bi, ti: (bi, ti, 0)),
            pl.BlockSpec((TM, ATTN_W), row),
            pl.BlockSpec((TM, POOL_W), row),
            pl.BlockSpec((1, 1, D_MODEL), lambda bi, ti: (bi, 0, 2)),
            pl.BlockSpec((D_MODEL, D_MODEL), lambda bi, ti: (0, 0)),
        ],
        out_specs=pl.BlockSpec((1, TM, D_MODEL), lambda bi, ti: (bi, ti, 0)),
        compiler_params=pltpu.CompilerParams(
            dimension_semantics=("arbitrary", "arbitrary"), vmem_limit_bytes=VMEM_LIMIT),
        name="out_proj_prompt",
    )(x, ag, pg, ada_p, w_out_b)


def _proj_sample_kernel(x_ref, ada_ref, nw_ref, w_ref, qnw_ref, knw_ref, wpool_ref, ps_ref, hist_ref,
                        qpad_ref, k_ref, v_ref, ki_ref, qi_ref, wi_ref, snew_ref, lnew_ref,
                        sga_ref, pg_ref, u_ref):
    nb = x_ref.shape[0]
    x = x_ref[...]
    shift = ada_ref[:, 0:D_MODEL]
    scale = ada_ref[:, D_MODEL:2 * D_MODEL]
    hb = _modulated_norm(x, nw_ref[...], scale, shift).astype(BF16)

    lane = _lane_iota((nb, LANES))
    lo = lane < HEAD_DIM
    seg512 = _seg_ones(ATTN_W, HEAD_DIM)
    seg128 = _seg_ones(LANES, HEAD_DIM)
    head_sel = jnp.where(lax.broadcasted_iota(I32, (ATTN_W, LANES), 0) // HEAD_DIM
                         == lax.broadcasted_iota(I32, (ATTN_W, LANES), 1), 1.0, 0.0).astype(BF16)

    def head_sums(prod):
        hi = prod.astype(BF16)
        rest = (prod - hi.astype(F32)).astype(BF16)
        return _dot(hi, head_sel) + _dot(rest, head_sel)

    q = _head_rms(_dot(hb, w_ref[:, C_Q:C_Q + ATTN_W]), seg512, qnw_ref[...])
    qb = q.astype(BF16)
    k = _head_rms(_dot(hb, w_ref[:, C_K:C_K + LANES]), seg128, knw_ref[...])
    k_ref[...] = k
    v = _dot(hb, w_ref[:, C_V:C_V + LANES])
    v_ref[...] = v

    for hp in range(N_HEADS // 2):
        pair = q[:, hp * LANES:(hp + 1) * LANES]
        pair_sw = pltpu.roll(pair, HEAD_DIM, axis=1)
        g = (2 * hp) // (N_HEADS // KV_HEADS)
        if g == 0:
            h_even, h_odd = jnp.where(lo, pair, 0.0), jnp.where(lo, pair_sw, 0.0)
        else:
            h_even, h_odd = jnp.where(lo, 0.0, pair_sw), jnp.where(lo, 0.0, pair)
        qpad_ref[:, (2 * hp) * LANES:(2 * hp + 1) * LANES] = h_even.astype(BF16)
        qpad_ref[:, (2 * hp + 1) * LANES:(2 * hp + 2) * LANES] = h_odd.astype(BF16)

    kq = k.astype(BF16).astype(F32)
    kq_sw = pltpu.roll(kq, HEAD_DIM, axis=1)
    k0t = jnp.where(lo, kq, kq_sw)
    k1t = jnp.where(lo, kq_sw, kq)
    qf = qb.astype(F32)
    prod = jnp.concatenate([qf[:, 0:LANES] * k0t, qf[:, LANES:2 * LANES] * k0t,
                            qf[:, 2 * LANES:3 * LANES] * k1t, qf[:, 3 * LANES:4 * LANES] * k1t], axis=1)
    lnew_ref[...] = head_sums(prod)

    qi = _dot(hb, w_ref[:, C_QI:C_QI + ATTN_W])
    qib = qi.astype(BF16)
    qi_ref[...] = qib
    kw = _dot(hb, w_ref[:, C_KW:C_KW + LANES])
    kw_sw = pltpu.roll(kw, HEAD_DIM, axis=1)
    ki_ref[...] = kw[:, 0:IDX_DIM]
    wi_full = jnp.where(lane < IDX_HEADS, kw_sw, 0.0) * ((IDX_HEADS * IDX_DIM) ** -0.5)
    wi_ref[...] = wi_full[:, 0:IDX_HEADS]

    kib = kw.astype(BF16).astype(F32)
    kit = jnp.where(lo, kib, pltpu.roll(kib, HEAD_DIM, axis=1))
    qif = qib.astype(F32)
    prod_i = jnp.concatenate([qif[:, j * LANES:(j + 1) * LANES] * kit for j in range(4)], axis=1)
    s_new = jnp.maximum(head_sums(prod_i), 0.0) * wi_full
    snew_ref[...] = jnp.broadcast_to(jnp.sum(s_new, axis=1, keepdims=True), (nb, LANES))

    sga_ref[...] = _silu(_dot(hb, w_ref[:, C_GA:C_GA + ATTN_W])).astype(BF16)

    u = _dot(hb, w_ref[:, C_U:C_U + POOL_W])
    gp = _dot(hb, w_ref[:, C_GP:C_GP + POOL_W])
    u_ref[...] = u
    ds = []
    for g, w in enumerate(POOL_WINDOWS):
        cs = slice(g * POOL_GC, (g + 1) * POOL_GC)
        s = u[:, cs]
        for j in range(1, w):
            s = s + hist_ref[POOL_HIST - j, :, cs]
        ds.append(s / float(w) - u[:, cs])
    d = jnp.concatenate(ds, axis=1)
    pg_ref[...] = _pool_mix(d, wpool_ref, ps_ref[...], gp).astype(BF16)


def _proj_sample(x, ada_s, norm_w, w_in_b, qnw, knw, wpool_b, pscale, hist_t):
    nb = x.shape[0]
    out_shape = (
        jax.ShapeDtypeStruct((nb, N_HEADS * LANES), BF16),
        jax.ShapeDtypeStruct((nb, LANES), F32),
        jax.ShapeDtypeStruct((nb, LANES), F32),
        jax.ShapeDtypeStruct((nb, IDX_DIM), F32),
        jax.ShapeDtypeStruct((nb, ATTN_W), BF16),
        jax.ShapeDtypeStruct((nb, IDX_HEADS), F32),
        jax.ShapeDtypeStruct((nb, LANES), F32),
        jax.ShapeDtypeStruct((nb, LANES), F32),
        jax.ShapeDtypeStruct((nb, ATTN_W), BF16),
        jax.ShapeDtypeStruct((nb, POOL_W), BF16),
        jax.ShapeDtypeStruct((nb, POOL_W), F32),
    )
    return pl.pallas_call(
        _proj_sample_kernel,
        out_shape=out_shape,
        compiler_params=pltpu.CompilerParams(vmem_limit_bytes=VMEM_LIMIT),
        name="proj_sample",
    )(x, ada_s, norm_w, w_in_b, qnw, knw, wpool_b, pscale, hist_t)


def _page_copies(pt_ref, pages_hbm, buf_ref, sem_ref, step, slot, group, n_pages):
    copies = []
    for j in range(group):
        for p in range(n_pages):
            page = pt_ref[step * group + j, p]
            copies.append(pltpu.make_async_copy(
                pages_hbm.at[page], buf_ref.at[slot, j, :, pl.ds(p * PAGE, PAGE)], sem_ref.at[slot]))
    return copies


def _score_sample_kernel(pt_ref, qi_ref, wi_ref, kidx_hbm, o_ref, buf_ref, sem_ref, *, group, n_pages):
    step = pl.program_id(0)
    slot = lax.rem(step, 2)

    def copies(st, sl):
        return _page_copies(pt_ref, kidx_hbm, buf_ref, sem_ref, st, sl, group, n_pages)

    @pl.when(step == 0)
    def _():
        for cp in copies(step, slot):
            cp.start()

    @pl.when(step + 1 < pl.num_programs(0))
    def _():
        for cp in copies(step + 1, 1 - slot):
            cp.start()

    for cp in copies(step, slot):
        cp.wait()

    for j in range(group):
        ki_t = buf_ref[slot, j].astype(BF16)
        s = _dot(qi_ref[j], ki_t)
        o_ref[j:j + 1, :] = jnp.sum(jnp.maximum(s, 0.0) * wi_ref[j], axis=0, keepdims=True)


def _score_sample(page_table, qi3, wi3, kidx_t):
    nb, n_pages = page_table.shape
    n_keys = n_pages * PAGE
    g = SCORE_GROUP
    return pl.pallas_call(
        functools.partial(_score_sample_kernel, group=g, n_pages=n_pages),
        out_shape=jax.ShapeDtypeStruct((nb, n_keys), F32),
        grid_spec=pltpu.PrefetchScalarGridSpec(
            num_scalar_prefetch=1,
            grid=(nb // g,),
            in_specs=[
                pl.BlockSpec((g, IDX_HEADS, IDX_DIM), lambda s, pt: (s, 0, 0)),
                pl.BlockSpec((g, IDX_HEADS, 1), lambda s, pt: (s, 0, 0)),
                pl.BlockSpec(memory_space=pl.ANY),
            ],
            out_specs=pl.BlockSpec((g, n_keys), lambda s, pt: (s, 0)),
            scratch_shapes=[
                pltpu.VMEM((2, g, IDX_DIM, n_keys), F32),
                pltpu.SemaphoreType.DMA((2,)),
            ],
        ),
        compiler_params=pltpu.CompilerParams(
            dimension_semantics=("arbitrary",), vmem_limit_bytes=VMEM_LIMIT),
        name="score_sample",
    )(page_table, qi3, wi3, kidx_t)


def _select_sample_kernel(sc_ref, snew_ref, mask_ref, mnew_ref):
    nb, n_keys = sc_ref.shape
    nch = n_keys // KEY_CHUNK
    snew = snew_ref[...]
    ones_mat = jnp.ones((LANES, LANES), BF16)

    def count_pass(pred):
        acc = jnp.zeros((nb, LANES), F32)
        for c in range(n_keys // LANES):
            acc = acc + pred(sc_ref[:, c * LANES:(c + 1) * LANES]).astype(F32)
        return _dot(acc.astype(BF16), ones_mat) + pred(snew).astype(F32)

    thr = _kth_largest(lambda t: count_pass(lambda x: x >= t), (nb, LANES))
    need = float(TOPK) - count_pass(lambda x: x > thr)
    thr2 = jnp.concatenate([thr, thr], axis=1)
    need2 = jnp.concatenate([need, need], axis=1)
    tri = jnp.where(lax.broadcasted_iota(I32, (KEY_CHUNK, KEY_CHUNK), 0)
                    <= lax.broadcasted_iota(I32, (KEY_CHUNK, KEY_CHUNK), 1), 1.0, 0.0).astype(BF16)
    ones_cl = jnp.ones((KEY_CHUNK, LANES), BF16)

    eq_before = jnp.zeros((nb, LANES), F32)
    for c in range(nch):
        sc = sc_ref[:, c * KEY_CHUNK:(c + 1) * KEY_CHUNK]
        eq = sc == thr2
        eqb = jnp.where(eq, 1.0, 0.0).astype(BF16)
        rank = _dot(eqb, tri) + jnp.concatenate([eq_before, eq_before], axis=1)
        sel = (sc > thr2) | (eq & (rank <= need2))
        mask_ref[:, c * KEY_CHUNK:(c + 1) * KEY_CHUNK] = jnp.where(sel, 0.0, -jnp.inf)
        eq_before = eq_before + _dot(eqb, ones_cl)
    sel_new = (snew > thr) | ((snew == thr) & (eq_before + 1.0 <= need))
    mnew_ref[...] = jnp.where(sel_new, 0.0, -jnp.inf)


def _select_sample(scores, snew):
    nb, n_keys = scores.shape
    return pl.pallas_call(
        _select_sample_kernel,
        out_shape=(jax.ShapeDtypeStruct((nb, n_keys), F32), jax.ShapeDtypeStruct((nb, LANES), F32)),
        name="select_sample",
    )(scores, snew)


def _attn_sample_kernel(pt_ref, q_ref, mask_ref, lnew_ref, mnew_ref, vnew_ref, k_hbm, v_hbm,
                        o_ref, kbuf_ref, vbuf_ref, ksem_ref, vsem_ref, *, group, n_pages):
    step = pl.program_id(0)
    slot = lax.rem(step, 2)

    def copies(st, sl):
        return (_page_copies(pt_ref, k_hbm, kbuf_ref, ksem_ref, st, sl, group, n_pages)
                + _page_copies(pt_ref, v_hbm, vbuf_ref, vsem_ref, st, sl, group, n_pages))

    @pl.when(step == 0)
    def _():
        for cp in copies(step, slot):
            cp.start()

    @pl.when(step + 1 < pl.num_programs(0))
    def _():
        for cp in copies(step + 1, 1 - slot):
            cp.start()

    for cp in copies(step, slot):
        cp.wait()

    for j in range(group):
        k_t = kbuf_ref[slot, j].astype(BF16)
        lg = _dot(q_ref[j], k_t) + mask_ref[j]
        lg_n = lnew_ref[j] + mnew_ref[j]
        m = jnp.maximum(jnp.max(lg, axis=1, keepdims=True), lg_n)
        p = jnp.exp(lg - m)
        p_n = jnp.exp(lg_n - m)
        l = jnp.sum(p, axis=1, keepdims=True) + p_n
        acc = _nt_dot(p.astype(BF16), vbuf_ref[slot, j].astype(BF16)) + p_n * vnew_ref[j]
        o_ref[j] = acc / l


def _attn_sample(page_table, qpad3, mask3, lnew3, mnew3, vnew3, k_t, v_t):
    nb, n_pages = page_table.shape
    n_keys = n_pages * PAGE
    g = ATTN_GROUP
    per_s = lambda s, pt: (s, 0, 0)
    return pl.pallas_call(
        functools.partial(_attn_sample_kernel, group=g, n_pages=n_pages),
        out_shape=jax.ShapeDtypeStruct((nb, N_HEADS, LANES), F32),
        grid_spec=pltpu.PrefetchScalarGridSpec(
            num_scalar_prefetch=1,
            grid=(nb // g,),
            in_specs=[
                pl.BlockSpec((g, N_HEADS, LANES), per_s),
                pl.BlockSpec((g, 1, n_keys), per_s),
                pl.BlockSpec((g, N_HEADS, 1), per_s),
                pl.BlockSpec((g, N_HEADS, 1), per_s),
                pl.BlockSpec((g, 1, LANES), per_s),
                pl.BlockSpec(memory_space=pl.ANY),
                pl.BlockSpec(memory_space=pl.ANY),
            ],
            out_specs=pl.BlockSpec((g, N_HEADS, LANES), per_s),
            scratch_shapes=[
                pltpu.VMEM((2, g, LANES, n_keys), F32),
                pltpu.VMEM((2, g, LANES, n_keys), F32),
                pltpu.SemaphoreType.DMA((2,)),
                pltpu.SemaphoreType.DMA((2,)),
            ],
        ),
        compiler_params=pltpu.CompilerParams(
            dimension_semantics=("arbitrary",), vmem_limit_bytes=VMEM_LIMIT),
        name="attn_sample",
    )(page_table, qpad3, mask3, lnew3, mnew3, vnew3, k_t, v_t)


def _out_proj_sample_kernel(x_ref, a_ref, sga_ref, p_ref, gate_ref, w_ref, o_ref):
    lo = _lane_iota((x_ref.shape[0], LANES)) < HEAD_DIM
    pairs = []
    for hp in range(N_HEADS // 2):
        even = a_ref[:, (2 * hp) * LANES:(2 * hp + 1) * LANES]
        odd = a_ref[:, (2 * hp + 1) * LANES:(2 * hp + 2) * LANES]
        if (2 * hp) // (N_HEADS // KV_HEADS) == 0:
            pairs.append(jnp.where(lo, even, pltpu.roll(odd, HEAD_DIM, axis=1)))
        else:
            pairs.append(jnp.where(lo, pltpu.roll(even, HEAD_DIM, axis=1), odd))
    a = jnp.concatenate(pairs, axis=1)
    ag = (a * sga_ref[...].astype(F32)).astype(BF16)
    y = _dot(ag, w_ref[0:ATTN_W, :]) + _dot(p_ref[...], w_ref[ATTN_W:ATTN_W + POOL_W, :])
    o_ref[...] = x_ref[...] + gate_ref[...] * y


def _out_proj_sample(x, a, sga, pg, gate, w_out_b):
    return pl.pallas_call(
        _out_proj_sample_kernel,
        out_shape=jax.ShapeDtypeStruct(x.shape, F32),
        compiler_params=pltpu.CompilerParams(vmem_limit_bytes=VMEM_LIMIT),
        name="out_proj_sample",
    )(x, a, sga, pg, gate, w_out_b)


def _permute_w_in(w_in):
    q, k, v, qi, ki, wi, ga, u, gp = jnp.split(
        w_in, np.cumsum([512, 128, 128, 512, 64, 8, 512, 512]).tolist(), axis=1)
    pad = jnp.zeros((w_in.shape[0], LANES - IDX_DIM - IDX_HEADS), w_in.dtype)
    return jnp.concatenate([q, k, v, qi, ki, wi, pad, ga, u, gp], axis=1).astype(BF16)


def kernel(x_prompt, x_sample, cache_k, cache_v, cache_kidx, state_pool, page_table, c_prompt, c_sample,
           norm_w, w_ada, b_ada, w_in, q_norm_w, k_norm_w, w_pool, pool_scale, w_out):
    bp, s, _ = x_prompt.shape
    bs = x_sample.shape[0]
    assert w_in.shape[0] == 1 and x_sample.shape[1] == 1, "single layer, single decode token"
    n_phys = cache_k.shape[1]

    w_in_b = _permute_w_in(w_in[0])
    w_ada_b = w_ada[0].astype(BF16)
    w_out_b = w_out[0].astype(BF16)
    wpool_b = w_pool[0].astype(BF16)
    qnw = jnp.tile(q_norm_w[0], N_HEADS)[None, :] * (HEAD_DIM ** -0.5)
    knw = jnp.tile(k_norm_w[0], KV_HEADS)[None, :]
    nw = norm_w[0][None, :]
    pscale = pool_scale[0][None, :]

    ada = _ada(jnp.concatenate([c_prompt, c_sample], axis=0), w_ada_b, b_ada)
    ada_p = ada[:bp].reshape(bp, 1, 3 * D_MODEL)
    ada_s = ada[bp:]

    (qs, k_t_p, kk, v_t_p, vtx, qi, ki_t_p, kik, wit, sga, pg, ulast) = _proj_prompt(
        x_prompt, ada_p, nw, w_in_b, qnw, knw, wpool_b, pscale)
    ag = _attn_prompt(qi, wit, qs, sga, kik, kk, vtx, bp, s)
    y_prompt = _out_proj_prompt(x_prompt, ag, pg, ada_p, w_out_b)

    hist_t = jnp.transpose(state_pool[0], (1, 0, 2))
    (qpad, k_s, v_s, ki_s, qi_s, wi_s, snew, lnew, sga_s, pg_s, u_s) = _proj_sample(
        x_sample[:, 0, :], ada_s, nw, w_in_b, qnw, knw, wpool_b, pscale, hist_t)
    n_pages = page_table.shape[1]
    kidx_t = jnp.transpose(cache_kidx[0], (0, 2, 1))
    k_t = jnp.transpose(cache_k[0], (0, 2, 3, 1)).reshape(n_phys, LANES, PAGE)
    v_t = jnp.transpose(cache_v[0], (0, 2, 3, 1)).reshape(n_phys, LANES, PAGE)
    scores = _score_sample(page_table, qi_s.reshape(bs, IDX_HEADS, IDX_DIM), wi_s.reshape(bs, IDX_HEADS, 1),
                           kidx_t)
    mask, mnew = _select_sample(scores, snew)
    o_s = _attn_sample(
        page_table, qpad.reshape(bs, N_HEADS, LANES),
        mask.reshape(bs, 1, n_pages * PAGE),
        lnew[:, :N_HEADS].reshape(bs, N_HEADS, 1),
        jnp.broadcast_to(mnew[:, :1], (bs, N_HEADS)).reshape(bs, N_HEADS, 1),
        v_s.reshape(bs, 1, LANES), k_t, v_t)
    y_sample = _out_proj_sample(x_sample[:, 0, :], o_s.reshape(bs, N_HEADS * LANES), sga_s, pg_s,
                                ada_s[:, 2 * D_MODEL:], w_out_b)

    to_heads = lambda a: jnp.transpose(a.reshape(bp, KV_HEADS, HEAD_DIM, s), (0, 3, 1, 2))[None]
    return (
        y_prompt,
        y_sample[:, None, :],
        to_heads(k_t_p),
        to_heads(v_t_p),
        jnp.transpose(ki_t_p, (0, 2, 1))[None],
        ulast[:, 1:, :][None],
        k_s.reshape(1, bs, 1, KV_HEADS, HEAD_DIM),
        v_s.reshape(1, bs, 1, KV_HEADS, HEAD_DIM),
        ki_s.reshape(1, bs, 1, IDX_DIM),
        jnp.concatenate([state_pool[0][:, 1:, :], u_s[:, None, :]], axis=1)[None],
    )
```

```python
import functools

import jax
import jax.numpy as jnp
import numpy as np
from jax import lax
from jax.experimental import pallas as pl
from jax.experimental.pallas import tpu as pltpu

F32 = jnp.float32
BF16 = jnp.bfloat16
I32 = jnp.int32

D_MODEL = 1024
ATTN_W = 512
POOL_W = 512
HEAD_DIM = 64
N_HEADS = 8
KV_HEADS = 2
IDX_HEADS = 8
IDX_DIM = 64
TOPK = 256
POOL_WINDOWS = (2, 4, 8, 16)
POOL_GC = 128
POOL_HIST = 15
EPS = 1e-6
PAGE = 128

LANES = 128
SUBLANES = 8
BF16_ROWS = 16
KEY_CHUNK = 256
TQ = 128
TM = 512
VT_ROWS = HEAD_DIM + BF16_ROWS
SCORE_GROUP = 8
ATTN_GROUP = 4
INT_MIN = np.int32(-2 ** 31)
VMEM_LIMIT = 56 * 1024 * 1024

C_Q, C_K, C_V, C_QI, C_KW, C_GA, C_U, C_GP, N_PROJ = 0, 512, 640, 768, 1280, 1408, 1920, 2432, 2944


def _nt_dot(a, b):
    return lax.dot_general(a, b, (((1,), (1,)), ((), ())), preferred_element_type=F32)


def _dot(a, b):
    return jnp.dot(a, b, preferred_element_type=F32)


def _silu(z):
    return z / (1.0 + jnp.exp(-z))


def _lane_iota(shape):
    return lax.broadcasted_iota(I32, shape, len(shape) - 1)


def _seg_ones(n, seg):
    r = lax.broadcasted_iota(I32, (n, n), 0) // seg
    c = lax.broadcasted_iota(I32, (n, n), 1) // seg
    return jnp.where(r == c, 1.0, 0.0).astype(BF16)


def _head_rms(z, seg_mat, w):
    ss = _dot((z * z).astype(BF16), seg_mat)
    return z * lax.rsqrt(ss * (1.0 / HEAD_DIM) + EPS) * w


def _float_of_rank(u):
    key = u ^ INT_MIN
    bits = key ^ ((key >> 31) & np.int32(0x7FFFFFFF))
    return lax.bitcast_convert_type(bits, F32)


def _kth_largest(count_ge, shape):
    def bit_body(it, ans):
        cand = ans | jnp.left_shift(jnp.int32(1), 31 - it)
        return jnp.where(count_ge(_float_of_rank(cand)) >= float(TOPK), cand, ans)

    ans = lax.fori_loop(0, 32, bit_body, jnp.zeros(shape, I32))
    return jnp.where(ans == 0, -jnp.inf, _float_of_rank(ans))


def _fold_rows(x, op):
    parts = [x[r:r + SUBLANES] for r in range(0, x.shape[0], SUBLANES)]
    while len(parts) > 1:
        parts = [op(parts[i], parts[i + 1]) for i in range(0, len(parts), 2)]
    return parts[0]


def _ada_kernel(c_ref, w_ref, b_ref, o_ref):
    c = c_ref[...]
    o_ref[...] = _dot(_silu(c).astype(BF16), w_ref[...]) + b_ref[...]


def _ada(c_all, w_ada_b, b_ada):
    n = c_all.shape[0]
    return pl.pallas_call(
        _ada_kernel,
        out_shape=jax.ShapeDtypeStruct((n, 3 * D_MODEL), F32),
        grid=(3,),
        in_specs=[
            pl.BlockSpec((n, D_MODEL), lambda j: (0, 0)),
            pl.BlockSpec((D_MODEL, D_MODEL), lambda j: (0, j)),
            pl.BlockSpec((1, D_MODEL), lambda j: (0, j)),
        ],
        out_specs=pl.BlockSpec((n, D_MODEL), lambda j: (0, j)),
        compiler_params=pltpu.CompilerParams(dimension_semantics=("arbitrary",)),
        name="ada_ln",
    )(c_all, w_ada_b, b_ada)


def _modulated_norm(x, norm_w, scale, shift):
    ms = jnp.mean(x * x, axis=-1, keepdims=True)
    return (x * lax.rsqrt(ms + EPS)) * norm_w * (1.0 + scale) + shift


def _pool_mix(d, wpool_ref, pscale, gp):
    db = d.astype(BF16)
    cols = []
    for g in range(len(POOL_WINDOWS)):
        cols.append(_dot(db[:, g * POOL_GC:(g + 1) * POOL_GC], wpool_ref[g]))
    y = jnp.concatenate(cols, axis=1)
    return y * pscale * _silu(gp)


def _proj_prompt_kernel(x_ref, ada_ref, nw_ref, w_ref, qnw_ref, knw_ref, wpool_ref, ps_ref,
                        qs_ref, kt_ref, kk_ref, vt_ref, vtx_ref, qi_ref, kit_ref, kik_ref,
                        wit_ref, sga_ref, pg_ref, ulast_ref,
                        ext_ref):
    t = pl.program_id(1)
    x = x_ref[0]
    ada = ada_ref[0]
    shift = ada[:, 0:D_MODEL]
    scale = ada[:, D_MODEL:2 * D_MODEL]
    hb = _modulated_norm(x, nw_ref[...], scale, shift).astype(BF16)

    lo = _lane_iota((TM, LANES)) < HEAD_DIM
    seg512 = _seg_ones(ATTN_W, HEAD_DIM)
    seg128 = _seg_ones(LANES, HEAD_DIM)

    q = _dot(hb, w_ref[:, C_Q:C_Q + ATTN_W])
    qs_ref[...] = _head_rms(q, seg512, qnw_ref[...]).astype(BF16)

    k = _head_rms(_dot(hb, w_ref[:, C_K:C_K + LANES]), seg128, knw_ref[...])
    kt_ref[0] = k.T
    k_sw = pltpu.roll(k, HEAD_DIM, axis=1)
    kk_ref[0] = jnp.where(lo, k, k_sw).astype(BF16)
    kk_ref[1] = jnp.where(lo, k_sw, k).astype(BF16)

    v_t = _dot(hb, w_ref[:, C_V:C_V + LANES]).T
    vt_ref[0] = v_t
    ones = jnp.ones((BF16_ROWS, KEY_CHUNK), BF16)
    for g in range(KV_HEADS):
        for c in range(TM // KEY_CHUNK):
            vtx_ref[0, g, c, 0:HEAD_DIM, :] = v_t[g * HEAD_DIM:(g + 1) * HEAD_DIM,
                                                  c * KEY_CHUNK:(c + 1) * KEY_CHUNK].astype(BF16)
            vtx_ref[0, g, c, HEAD_DIM:VT_ROWS, :] = ones

    qi_ref[...] = _dot(hb, w_ref[:, C_QI:C_QI + ATTN_W]).astype(BF16)
    kw = _dot(hb, w_ref[:, C_KW:C_KW + LANES])
    kw_t = kw.T
    kit_ref[0] = kw_t[0:IDX_DIM]
    wit_ref[...] = kw_t[IDX_DIM:IDX_DIM + IDX_HEADS] * ((IDX_HEADS * IDX_DIM) ** -0.5)
    kik_ref[...] = jnp.where(lo, kw, pltpu.roll(kw, HEAD_DIM, axis=1)).astype(BF16)

    sga_ref[...] = _silu(_dot(hb, w_ref[:, C_GA:C_GA + ATTN_W])).astype(BF16)

    u = _dot(hb, w_ref[:, C_U:C_U + POOL_W])
    gp = _dot(hb, w_ref[:, C_GP:C_GP + POOL_W])

    @pl.when(t == 0)
    def _():
        ext_ref[0:16, :] = jnp.zeros((16, POOL_W), F32)

    ext_ref[16:16 + TM, :] = u
    pos = t * TM + lax.broadcasted_iota(I32, (TM, POOL_GC), 0)
    ds = []
    for g, w in enumerate(POOL_WINDOWS):
        cs = slice(g * POOL_GC, (g + 1) * POOL_GC)
        s = u[:, cs]
        for j in range(1, w):
            s = s + ext_ref[16 - j:16 - j + TM, cs]
        cnt = jnp.minimum(pos + 1, w).astype(F32)
        ds.append(s / cnt - u[:, cs])
    d = jnp.concatenate(ds, axis=1)
    pg_ref[...] = _pool_mix(d, wpool_ref, ps_ref[...], gp).astype(BF16)

    tail = u[TM - 16:TM, :]
    ulast_ref[0] = tail
    ext_ref[0:16, :] = tail


def _proj_prompt(x, ada_p, norm_w, w_in_b, qnw, knw, wpool_b, pscale):
    b, s, _ = x.shape
    n = b * s
    nt = s // TM
    cpt = TM // KEY_CHUNK
    row = lambda bi, ti: (bi * nt + ti, 0)
    tok = lambda bi, ti: (bi, 0, ti)
    const2 = lambda bi, ti: (0, 0)
    const3 = lambda bi, ti: (0, 0, 0)
    out_shape = (
        jax.ShapeDtypeStruct((n, ATTN_W), BF16),
        jax.ShapeDtypeStruct((b, LANES, s), F32),
        jax.ShapeDtypeStruct((KV_HEADS, n, LANES), BF16),
        jax.ShapeDtypeStruct((b, LANES, s), F32),
        jax.ShapeDtypeStruct((b, KV_HEADS, s // KEY_CHUNK, VT_ROWS, KEY_CHUNK), BF16),
        jax.ShapeDtypeStruct((n, ATTN_W), BF16),
        jax.ShapeDtypeStruct((b, IDX_DIM, s), F32),
        jax.ShapeDtypeStruct((n, LANES), BF16),
        jax.ShapeDtypeStruct((IDX_HEADS, n), F32),
        jax.ShapeDtypeStruct((n, ATTN_W), BF16),
        jax.ShapeDtypeStruct((n, POOL_W), BF16),
        jax.ShapeDtypeStruct((b, 16, POOL_W), F32),
    )
    out_specs = (
        pl.BlockSpec((TM, ATTN_W), row),
        pl.BlockSpec((1, LANES, TM), tok),
        pl.BlockSpec((KV_HEADS, TM, LANES), lambda bi, ti: (0, bi * nt + ti, 0)),
        pl.BlockSpec((1, LANES, TM), tok),
        pl.BlockSpec((1, KV_HEADS, cpt, VT_ROWS, KEY_CHUNK), lambda bi, ti: (bi, 0, ti, 0, 0)),
        pl.BlockSpec((TM, ATTN_W), row),
        pl.BlockSpec((1, IDX_DIM, TM), tok),
        pl.BlockSpec((TM, LANES), row),
        pl.BlockSpec((IDX_HEADS, TM), lambda bi, ti: (0, bi * nt + ti)),
        pl.BlockSpec((TM, ATTN_W), row),
        pl.BlockSpec((TM, POOL_W), row),
        pl.BlockSpec((1, 16, POOL_W), lambda bi, ti: (bi, 0, 0)),
    )
    in_specs = [
        pl.BlockSpec((1, TM, D_MODEL), lambda bi, ti: (bi, ti, 0)),
        pl.BlockSpec((1, 1, 3 * D_MODEL), lambda bi, ti: (bi, 0, 0)),
        pl.BlockSpec((1, D_MODEL), const2),
        pl.BlockSpec((D_MODEL, N_PROJ), const2),
        pl.BlockSpec((1, ATTN_W), const2),
        pl.BlockSpec((1, LANES), const2),
        pl.BlockSpec((4, POOL_GC, POOL_GC), const3),
        pl.BlockSpec((1, POOL_W), const2),
    ]
    return pl.pallas_call(
        _proj_prompt_kernel,
        out_shape=out_shape,
        grid=(b, nt),
        in_specs=in_specs,
        out_specs=out_specs,
        scratch_shapes=[pltpu.VMEM((16 + TM, POOL_W), F32)],
        compiler_params=pltpu.CompilerParams(
            dimension_semantics=("arbitrary", "arbitrary"), vmem_limit_bytes=VMEM_LIMIT),
        name="proj_prompt",
    )(x, ada_p, norm_w, w_in_b, qnw, knw, wpool_b, pscale)


def _attn_tile(nch, row0, wit_ref, kik_ref, kk_ref, vtx_ref, sc_ref, lg_ref, wq_ref, acc_ref):
    shape2 = (KEY_CHUNK, TQ)
    n_pairs = N_HEADS // 2
    group_of = lambda hp: (2 * hp) // (N_HEADS // KV_HEADS)

    for c in range(nch):
        kk = kik_ref[c * KEY_CHUNK:(c + 1) * KEY_CHUNK, :]
        acc = jnp.zeros(shape2, F32)
        for hp in range(n_pairs):
            s2 = _nt_dot(kk, wq_ref[0, hp])
            acc = acc + jnp.maximum(s2[:, 0:TQ], 0.0) * wit_ref[2 * hp:2 * hp + 1, :]
            acc = acc + jnp.maximum(s2[:, TQ:2 * TQ], 0.0) * wit_ref[2 * hp + 1:2 * hp + 2, :]
        if c == nch - 1:
            kpos = c * KEY_CHUNK + lax.broadcasted_iota(I32, shape2, 0)
            qpos = row0 + lax.broadcasted_iota(I32, shape2, 1)
            acc = jnp.where(kpos <= qpos, acc, -jnp.inf)
        sc_ref[c] = acc
        for hp in range(n_pairs):
            lg_ref[hp, c] = _nt_dot(kk_ref[group_of(hp), c * KEY_CHUNK:(c + 1) * KEY_CHUNK, :], wq_ref[1, hp])

    def count_pass(pred):
        parts = [_fold_rows(pred(sc_ref[c]).astype(F32), jnp.add) for c in range(nch)]
        while len(parts) > 1:
            parts = [sum(parts[j:j + 2]) for j in range(0, len(parts), 2)]
        return jnp.sum(parts[0], axis=0, keepdims=True)

    thr = _kth_largest(lambda t: count_pass(lambda x: x >= t), (1, TQ))
    need = float(TOPK) - count_pass(lambda x: x > thr)

    tri = jnp.where(lax.broadcasted_iota(I32, (KEY_CHUNK, KEY_CHUNK), 1)
                    <= lax.broadcasted_iota(I32, (KEY_CHUNK, KEY_CHUNK), 0), 1.0, 0.0).astype(BF16)

    m_part = [jnp.full((SUBLANES, TQ), -jnp.inf, F32) for _ in range(N_HEADS)]
    eq_before = jnp.zeros((1, TQ), F32)
    for c in range(nch):
        sc = sc_ref[c]
        eq = sc == thr
        rank = _dot(tri, jnp.where(eq, 1.0, 0.0).astype(BF16)) + eq_before
        sel = (sc > thr) | (eq & (rank <= need))
        if c == nch - 1:
            sel = sel & (sc > -jnp.inf)
        eq_before = rank[KEY_CHUNK - 1:KEY_CHUNK, :]
        for hp in range(n_pairs):
            for par in range(2):
                cols = slice(par * TQ, (par + 1) * TQ)
                lg = jnp.where(sel, lg_ref[hp, c, :, cols], -jnp.inf)
                lg_ref[hp, c, :, cols] = lg
                m_part[2 * hp + par] = jnp.maximum(m_part[2 * hp + par], _fold_rows(lg, jnp.maximum))
    m_rows = [jnp.max(m, axis=0, keepdims=True) for m in m_part]

    for hp in range(n_pairs):
        m2 = jnp.concatenate([m_rows[2 * hp], m_rows[2 * hp + 1]], axis=1)
        acc = jnp.zeros((VT_ROWS, 2 * TQ), F32)
        for c in range(nch):
            p2 = jnp.exp(lg_ref[hp, c] - m2).astype(BF16)
            acc = acc + _dot(vtx_ref[0, group_of(hp), c], p2)
        acc_ref[hp] = acc


def _attn_prompt_kernel(qi_ref, wit_ref, qs_ref, sga_ref, kik_ref, kk_ref, vtx_ref, ag_ref,
                        sc_ref, lg_ref, wq_ref, acc_ref):
    i = pl.program_id(1)
    nch = (i + 2) // 2
    row0 = i * TQ
    n_pairs = N_HEADS // 2
    lo = _lane_iota((TQ, LANES)) < HEAD_DIM

    for kind, ref in enumerate((qi_ref, qs_ref)):
        for hp in range(n_pairs):
            pair = ref[:, hp * LANES:(hp + 1) * LANES].astype(F32)
            wq_ref[kind, hp, 0:TQ, :] = jnp.where(lo, pair, 0.0).astype(BF16)
            wq_ref[kind, hp, TQ:2 * TQ, :] = jnp.where(lo, 0.0, pair).astype(BF16)

    for n_static in range(1, sc_ref.shape[0] + 1):
        @pl.when(nch == n_static)
        def _(n_static=n_static):
            _attn_tile(n_static, row0, wit_ref, kik_ref, kk_ref, vtx_ref, sc_ref, lg_ref, wq_ref, acc_ref)

    for hp in range(n_pairs):
        a = acc_ref[hp]
        o0 = a[0:HEAD_DIM, 0:TQ] / a[HEAD_DIM:HEAD_DIM + 1, 0:TQ]
        o1 = a[0:HEAD_DIM, TQ:2 * TQ] / a[HEAD_DIM:HEAD_DIM + 1, TQ:2 * TQ]
        pair = jnp.concatenate([o0, o1], axis=0).T
        cols = slice(hp * LANES, (hp + 1) * LANES)
        ag_ref[:, cols] = (pair * sga_ref[:, cols].astype(F32)).astype(BF16)


def _attn_prompt(qi, wit, qs, sga, kik, kk, vtx, b, s):
    n = b * s
    nq = s // TQ
    nkc = s // KEY_CHUNK
    row = lambda bi, qi_: (bi * nq + qi_, 0)
    return pl.pallas_call(
        _attn_prompt_kernel,
        out_shape=jax.ShapeDtypeStruct((n, ATTN_W), BF16),
        grid=(b, nq),
        in_specs=[
            pl.BlockSpec((TQ, ATTN_W), row),
            pl.BlockSpec((IDX_HEADS, TQ), lambda bi, qi_: (0, bi * nq + qi_)),
            pl.BlockSpec((TQ, ATTN_W), row),
            pl.BlockSpec((TQ, ATTN_W), row),
            pl.BlockSpec((s, LANES), lambda bi, qi_: (bi, 0)),
            pl.BlockSpec((KV_HEADS, s, LANES), lambda bi, qi_: (0, bi, 0)),
            pl.BlockSpec((1, KV_HEADS, nkc, VT_ROWS, KEY_CHUNK), lambda bi, qi_: (bi, 0, 0, 0, 0)),
        ],
        out_specs=pl.BlockSpec((TQ, ATTN_W), row),
        scratch_shapes=[
            pltpu.VMEM((nkc, KEY_CHUNK, TQ), F32),
            pltpu.VMEM((N_HEADS // 2, nkc, KEY_CHUNK, 2 * TQ), F32),
            pltpu.VMEM((2, N_HEADS // 2, 2 * TQ, LANES), BF16),
            pltpu.VMEM((N_HEADS // 2, VT_ROWS, 2 * TQ), F32),
        ],
        compiler_params=pltpu.CompilerParams(
            dimension_semantics=("arbitrary", "arbitrary"), vmem_limit_bytes=VMEM_LIMIT),
        name="attn_prompt",
    )(qi, wit, qs, sga, kik, kk, vtx)


def _out_proj_kernel(x_ref, a_ref, p_ref, gate_ref, w_ref, o_ref):
    y = _dot(a_ref[...], w_ref[0:ATTN_W, :]) + _dot(p_ref[...], w_ref[ATTN_W:ATTN_W + POOL_W, :])
    o_ref[0] = x_ref[0] + gate_ref[0] * y


def _out_proj_prompt(x, ag, pg, ada_p, w_out_b):
    b, s, _ = x.shape
    nt = s // TM
    row = lambda bi, ti: (bi * nt + ti, 0)
    return pl.pallas_call(
        _out_proj_kernel,
        out_shape=jax.ShapeDtypeStruct(x.shape, F32),
        grid=(b, nt),
        in_specs=[
            pl.BlockSpec((1, TM, D_MODEL), lambda bi, ti: (bi, ti, 0)),
            pl.BlockSpec((TM, ATTN_W), row),
            pl.BlockSpec((TM, POOL_W), row),
            pl.BlockSpec((1, 1, D_MODEL), lambda bi, ti: (bi, 0, 2)),
            pl.BlockSpec((D_MODEL, D_MODEL), lambda bi, ti: (0, 0)),
        ],
        out_specs=pl.BlockSpec((1, TM, D_MODEL), lambda bi, ti: (bi, ti, 0)),
        compiler_params=pltpu.CompilerParams(
            dimension_semantics=("arbitrary", "arbitrary"), vmem_limit_bytes=VMEM_LIMIT),
        name="out_proj_prompt",
    )(x, ag, pg, ada_p, w_out_b)


def _proj_sample_kernel(x_ref, ada_ref, nw_ref, w_ref, qnw_ref, knw_ref, wpool_ref, ps_ref, hist_ref,
                        qpad_ref, k_ref, v_ref, ki_ref, qi_ref, wi_ref, snew_ref, lnew_ref,
                        sga_ref, pg_ref, u_ref):
    nb = x_ref.shape[0]
    x = x_ref[...]
    shift = ada_ref[:, 0:D_MODEL]
    scale = ada_ref[:, D_MODEL:2 * D_MODEL]
    hb = _modulated_norm(x, nw_ref[...], scale, shift).astype(BF16)

    lane = _lane_iota((nb, LANES))
    lo = lane < HEAD_DIM
    seg512 = _seg_ones(ATTN_W, HEAD_DIM)
    seg128 = _seg_ones(LANES, HEAD_DIM)
    head_sel = jnp.where(lax.broadcasted_iota(I32, (ATTN_W, LANES), 0) // HEAD_DIM
                         == lax.broadcasted_iota(I32, (ATTN_W, LANES), 1), 1.0, 0.0).astype(BF16)

    def head_sums(prod):
        hi = prod.astype(BF16)
        rest = (prod - hi.astype(F32)).astype(BF16)
        return _dot(hi, head_sel) + _dot(rest, head_sel)

    q = _head_rms(_dot(hb, w_ref[:, C_Q:C_Q + ATTN_W]), seg512, qnw_ref[...])
    qb = q.astype(BF16)
    k = _head_rms(_dot(hb, w_ref[:, C_K:C_K + LANES]), seg128, knw_ref[...])
    k_ref[...] = k
    v = _dot(hb, w_ref[:, C_V:C_V + LANES])
    v_ref[...] = v

    for hp in range(N_HEADS // 2):
        pair = q[:, hp * LANES:(hp + 1) * LANES]
        pair_sw = pltpu.roll(pair, HEAD_DIM, axis=1)
        g = (2 * hp) // (N_HEADS // KV_HEADS)
        if g == 0:
            h_even, h_odd = jnp.where(lo, pair, 0.0), jnp.where(lo, pair_sw, 0.0)
        else:
            h_even, h_odd = jnp.where(lo, 0.0, pair_sw), jnp.where(lo, 0.0, pair)
        qpad_ref[:, (2 * hp) * LANES:(2 * hp + 1) * LANES] = h_even.astype(BF16)
        qpad_ref[:, (2 * hp + 1) * LANES:(2 * hp + 2) * LANES] = h_odd.astype(BF16)

    kq = k.astype(BF16).astype(F32)
    kq_sw = pltpu.roll(kq, HEAD_DIM, axis=1)
    k0t = jnp.where(lo, kq, kq_sw)
    k1t = jnp.where(lo, kq_sw, kq)
    qf = qb.astype(F32)
    prod = jnp.concatenate([qf[:, 0:LANES] * k0t, qf[:, LANES:2 * LANES] * k0t,
                            qf[:, 2 * LANES:3 * LANES] * k1t, qf[:, 3 * LANES:4 * LANES] * k1t], axis=1)
    lnew_ref[...] = head_sums(prod)

    qi = _dot(hb, w_ref[:, C_QI:C_QI + ATTN_W])
    qib = qi.astype(BF16)
    qi_ref[...] = qib
    kw = _dot(hb, w_ref[:, C_KW:C_KW + LANES])
    kw_sw = pltpu.roll(kw, HEAD_DIM, axis=1)
    ki_ref[...] = kw[:, 0:IDX_DIM]
    wi_full = jnp.where(lane < IDX_HEADS, kw_sw, 0.0) * ((IDX_HEADS * IDX_DIM) ** -0.5)
    wi_ref[...] = wi_full[:, 0:IDX_HEADS]

    kib = kw.astype(BF16).astype(F32)
    kit = jnp.where(lo, kib, pltpu.roll(kib, HEAD_DIM, axis=1))
    qif = qib.astype(F32)
    prod_i = jnp.concatenate([qif[:, j * LANES:(j + 1) * LANES] * kit for j in range(4)], axis=1)
    s_new = jnp.maximum(head_sums(prod_i), 0.0) * wi_full
    snew_ref[...] = jnp.broadcast_to(jnp.sum(s_new, axis=1, keepdims=True), (nb, LANES))

    sga_ref[...] = _silu(_dot(hb, w_ref[:, C_GA:C_GA + ATTN_W])).astype(BF16)

    u = _dot(hb, w_ref[:, C_U:C_U + POOL_W])
    gp = _dot(hb, w_ref[:, C_GP:C_GP + POOL_W])
    u_ref[...] = u
    ds = []
    for g, w in enumerate(POOL_WINDOWS):
        cs = slice(g * POOL_GC, (g + 1) * POOL_GC)
        s = u[:, cs]
        for j in range(1, w):
            s = s + hist_ref[POOL_HIST - j, :, cs]
        ds.append(s / float(w) - u[:, cs])
    d = jnp.concatenate(ds, axis=1)
    pg_ref[...] = _pool_mix(d, wpool_ref, ps_ref[...], gp).astype(BF16)


def _proj_sample(x, ada_s, norm_w, w_in_b, qnw, knw, wpool_b, pscale, hist_t):
    nb = x.shape[0]
    out_shape = (
        jax.ShapeDtypeStruct((nb, N_HEADS * LANES), BF16),
        jax.ShapeDtypeStruct((nb, LANES), F32),
        jax.ShapeDtypeStruct((nb, LANES), F32),
        jax.ShapeDtypeStruct((nb, IDX_DIM), F32),
        jax.ShapeDtypeStruct((nb, ATTN_W), BF16),
        jax.ShapeDtypeStruct((nb, IDX_HEADS), F32),
        jax.ShapeDtypeStruct((nb, LANES), F32),
        jax.ShapeDtypeStruct((nb, LANES), F32),
        jax.ShapeDtypeStruct((nb, ATTN_W), BF16),
        jax.ShapeDtypeStruct((nb, POOL_W), BF16),
        jax.ShapeDtypeStruct((nb, POOL_W), F32),
    )
    return pl.pallas_call(
        _proj_sample_kernel,
        out_shape=out_shape,
        compiler_params=pltpu.CompilerParams(vmem_limit_bytes=VMEM_LIMIT),
        name="proj_sample",
    )(x, ada_s, norm_w, w_in_b, qnw, knw, wpool_b, pscale, hist_t)


def _page_copies(pt_ref, pages_hbm, buf_ref, sem_ref, step, slot, group, n_pages):
    copies = []
    for j in range(group):
        for p in range(n_pages):
            page = pt_ref[step * group + j, p]
            copies.append(pltpu.make_async_copy(
                pages_hbm.at[page], buf_ref.at[slot, j, :, pl.ds(p * PAGE, PAGE)], sem_ref.at[slot]))
    return copies


def _score_sample_kernel(pt_ref, qi_ref, wi_ref, kidx_hbm, o_ref, buf_ref, sem_ref, *, group, n_pages):
    step = pl.program_id(0)
    slot = lax.rem(step, 2)

    def copies(st, sl):
        return _page_copies(pt_ref, kidx_hbm, buf_ref, sem_ref, st, sl, group, n_pages)

    @pl.when(step == 0)
    def _():
        for cp in copies(step, slot):
            cp.start()

    @pl.when(step + 1 < pl.num_programs(0))
    def _():
        for cp in copies(step + 1, 1 - slot):
            cp.start()

    for cp in copies(step, slot):
        cp.wait()

    for j in range(group):
        ki_t = buf_ref[slot, j].astype(BF16)
        s = _dot(qi_ref[j], ki_t)
        o_ref[j:j + 1, :] = jnp.sum(jnp.maximum(s, 0.0) * wi_ref[j], axis=0, keepdims=True)


def _score_sample(page_table, qi3, wi3, kidx_t):
    nb, n_pages = page_table.shape
    n_keys = n_pages * PAGE
    g = SCORE_GROUP
    return pl.pallas_call(
        functools.partial(_score_sample_kernel, group=g, n_pages=n_pages),
        out_shape=jax.ShapeDtypeStruct((nb, n_keys), F32),
        grid_spec=pltpu.PrefetchScalarGridSpec(
            num_scalar_prefetch=1,
            grid=(nb // g,),
            in_specs=[
                pl.BlockSpec((g, IDX_HEADS, IDX_DIM), lambda s, pt: (s, 0, 0)),
                pl.BlockSpec((g, IDX_HEADS, 1), lambda s, pt: (s, 0, 0)),
                pl.BlockSpec(memory_space=pl.ANY),
            ],
            out_specs=pl.BlockSpec((g, n_keys), lambda s, pt: (s, 0)),
            scratch_shapes=[
                pltpu.VMEM((2, g, IDX_DIM, n_keys), F32),
                pltpu.SemaphoreType.DMA((2,)),
            ],
        ),
        compiler_params=pltpu.CompilerParams(
            dimension_semantics=("arbitrary",), vmem_limit_bytes=VMEM_LIMIT),
        name="score_sample",
    )(page_table, qi3, wi3, kidx_t)


def _select_sample_kernel(sc_ref, snew_ref, mask_ref, mnew_ref):
    nb, n_keys = sc_ref.shape
    nch = n_keys // KEY_CHUNK
    snew = snew_ref[...]
    ones_mat = jnp.ones((LANES, LANES), BF16)

    def count_pass(pred):
        acc = jnp.zeros((nb, LANES), F32)
        for c in range(n_keys // LANES):
            acc = acc + pred(sc_ref[:, c * LANES:(c + 1) * LANES]).astype(F32)
        return _dot(acc.astype(BF16), ones_mat) + pred(snew).astype(F32)

    thr = _kth_largest(lambda t: count_pass(lambda x: x >= t), (nb, LANES))
    need = float(TOPK) - count_pass(lambda x: x > thr)
    thr2 = jnp.concatenate([thr, thr], axis=1)
    need2 = jnp.concatenate([need, need], axis=1)
    tri = jnp.where(lax.broadcasted_iota(I32, (KEY_CHUNK, KEY_CHUNK), 0)
                    <= lax.broadcasted_iota(I32, (KEY_CHUNK, KEY_CHUNK), 1), 1.0, 0.0).astype(BF16)
    ones_cl = jnp.ones((KEY_CHUNK, LANES), BF16)

    eq_before = jnp.zeros((nb, LANES), F32)
    for c in range(nch):
        sc = sc_ref[:, c * KEY_CHUNK:(c + 1) * KEY_CHUNK]
        eq = sc == thr2
        eqb = jnp.where(eq, 1.0, 0.0).astype(BF16)
        rank = _dot(eqb, tri) + jnp.concatenate([eq_before, eq_before], axis=1)
        sel = (sc > thr2) | (eq & (rank <= need2))
        mask_ref[:, c * KEY_CHUNK:(c + 1) * KEY_CHUNK] = jnp.where(sel, 0.0, -jnp.inf)
        eq_before = eq_before + _dot(eqb, ones_cl)
    sel_new = (snew > thr) | ((snew == thr) & (eq_before + 1.0 <= need))
    mnew_ref[...] = jnp.where(sel_new, 0.0, -jnp.inf)


def _select_sample(scores, snew):
    nb, n_keys = scores.shape
    return pl.pallas_call(
        _select_sample_kernel,
        out_shape=(jax.ShapeDtypeStruct((nb, n_keys), F32), jax.ShapeDtypeStruct((nb, LANES), F32)),
        name="select_sample",
    )(scores, snew)


def _attn_sample_kernel(pt_ref, q_ref, mask_ref, lnew_ref, mnew_ref, vnew_ref, k_hbm, v_hbm,
                        o_ref, kbuf_ref, vbuf_ref, ksem_ref, vsem_ref, *, group, n_pages):
    step = pl.program_id(0)
    slot = lax.rem(step, 2)

    def copies(st, sl):
        return (_page_copies(pt_ref, k_hbm, kbuf_ref, ksem_ref, st, sl, group, n_pages)
                + _page_copies(pt_ref, v_hbm, vbuf_ref, vsem_ref, st, sl, group, n_pages))

    @pl.when(step == 0)
    def _():
        for cp in copies(step, slot):
            cp.start()

    @pl.when(step + 1 < pl.num_programs(0))
    def _():
        for cp in copies(step + 1, 1 - slot):
            cp.start()

    for cp in copies(step, slot):
        cp.wait()

    for j in range(group):
        k_t = kbuf_ref[slot, j].astype(BF16)
        lg = _dot(q_ref[j], k_t) + mask_ref[j]
        lg_n = lnew_ref[j] + mnew_ref[j]
        m = jnp.maximum(jnp.max(lg, axis=1, keepdims=True), lg_n)
        p = jnp.exp(lg - m)
        p_n = jnp.exp(lg_n - m)
        l = jnp.sum(p, axis=1, keepdims=True) + p_n
        acc = _nt_dot(p.astype(BF16), vbuf_ref[slot, j].astype(BF16)) + p_n * vnew_ref[j]
        o_ref[j] = acc / l


def _attn_sample(page_table, qpad3, mask3, lnew3, mnew3, vnew3, k_t, v_t):
    nb, n_pages = page_table.shape
    n_keys = n_pages * PAGE
    g = ATTN_GROUP
    per_s = lambda s, pt: (s, 0, 0)
    return pl.pallas_call(
        functools.partial(_attn_sample_kernel, group=g, n_pages=n_pages),
        out_shape=jax.ShapeDtypeStruct((nb, N_HEADS, LANES), F32),
        grid_spec=pltpu.PrefetchScalarGridSpec(
            num_scalar_prefetch=1,
            grid=(nb // g,),
            in_specs=[
                pl.BlockSpec((g, N_HEADS, LANES), per_s),
                pl.BlockSpec((g, 1, n_keys), per_s),
                pl.BlockSpec((g, N_HEADS, 1), per_s),
                pl.BlockSpec((g, N_HEADS, 1), per_s),
                pl.BlockSpec((g, 1, LANES), per_s),
                pl.BlockSpec(memory_space=pl.ANY),
                pl.BlockSpec(memory_space=pl.ANY),
            ],
            out_specs=pl.BlockSpec((g, N_HEADS, LANES), per_s),
            scratch_shapes=[
                pltpu.VMEM((2, g, LANES, n_keys), F32),
                pltpu.VMEM((2, g, LANES, n_keys), F32),
                pltpu.SemaphoreType.DMA((2,)),
                pltpu.SemaphoreType.DMA((2,)),
            ],
        ),
        compiler_params=pltpu.CompilerParams(
            dimension_semantics=("arbitrary",), vmem_limit_bytes=VMEM_LIMIT),
        name="attn_sample",
    )(page_table, qpad3, mask3, lnew3, mnew3, vnew3, k_t, v_t)


def _out_proj_sample_kernel(x_ref, a_ref, sga_ref, p_ref, gate_ref, w_ref, o_ref):
    lo = _lane_iota((x_ref.shape[0], LANES)) < HEAD_DIM
    pairs = []
    for hp in range(N_HEADS // 2):
        even = a_ref[:, (2 * hp) * LANES:(2 * hp + 1) * LANES]
        odd = a_ref[:, (2 * hp + 1) * LANES:(2 * hp + 2) * LANES]
        if (2 * hp) // (N_HEADS // KV_HEADS) == 0:
            pairs.append(jnp.where(lo, even, pltpu.roll(odd, HEAD_DIM, axis=1)))
        else:
            pairs.append(jnp.where(lo, pltpu.roll(even, HEAD_DIM, axis=1), odd))
    a = jnp.concatenate(pairs, axis=1)
    ag = (a * sga_ref[...].astype(F32)).astype(BF16)
    y = _dot(ag, w_ref[0:ATTN_W, :]) + _dot(p_ref[...], w_ref[ATTN_W:ATTN_W + POOL_W, :])
    o_ref[...] = x_ref[...] + gate_ref[...] * y


def _out_proj_sample(x, a, sga, pg, gate, w_out_b):
    return pl.pallas_call(
        _out_proj_sample_kernel,
        out_shape=jax.ShapeDtypeStruct(x.shape, F32),
        compiler_params=pltpu.CompilerParams(vmem_limit_bytes=VMEM_LIMIT),
        name="out_proj_sample",
    )(x, a, sga, pg, gate, w_out_b)


def _permute_w_in(w_in):
    q, k, v, qi, ki, wi, ga, u, gp = jnp.split(
        w_in, np.cumsum([512, 128, 128, 512, 64, 8, 512, 512]).tolist(), axis=1)
    pad = jnp.zeros((w_in.shape[0], LANES - IDX_DIM - IDX_HEADS), w_in.dtype)
    return jnp.concatenate([q, k, v, qi, ki, wi, pad, ga, u, gp], axis=1).astype(BF16)


def kernel(x_prompt, x_sample, cache_k, cache_v, cache_kidx, state_pool, page_table, c_prompt, c_sample,
           norm_w, w_ada, b_ada, w_in, q_norm_w, k_norm_w, w_pool, pool_scale, w_out):
    bp, s, _ = x_prompt.shape
    bs = x_sample.shape[0]
    assert w_in.shape[0] == 1 and x_sample.shape[1] == 1, "single layer, single decode token"
    n_phys = cache_k.shape[1]

    w_in_b = _permute_w_in(w_in[0])
    w_ada_b = w_ada[0].astype(BF16)
    w_out_b = w_out[0].astype(BF16)
    wpool_b = w_pool[0].astype(BF16)
    qnw = jnp.tile(q_norm_w[0], N_HEADS)[None, :] * (HEAD_DIM ** -0.5)
    knw = jnp.tile(k_norm_w[0], KV_HEADS)[None, :]
    nw = norm_w[0][None, :]
    pscale = pool_scale[0][None, :]

    ada = _ada(jnp.concatenate([c_prompt, c_sample], axis=0), w_ada_b, b_ada)
    ada_p = ada[:bp].reshape(bp, 1, 3 * D_MODEL)
    ada_s = ada[bp:]

    (qs, k_t_p, kk, v_t_p, vtx, qi, ki_t_p, kik, wit, sga, pg, ulast) = _proj_prompt(
        x_prompt, ada_p, nw, w_in_b, qnw, knw, wpool_b, pscale)
    ag = _attn_prompt(qi, wit, qs, sga, kik, kk, vtx, bp, s)
    y_prompt = _out_proj_prompt(x_prompt, ag, pg, ada_p, w_out_b)

    hist_t = jnp.transpose(state_pool[0], (1, 0, 2))
    (qpad, k_s, v_s, ki_s, qi_s, wi_s, snew, lnew, sga_s, pg_s, u_s) = _proj_sample(
        x_sample[:, 0, :], ada_s, nw, w_in_b, qnw, knw, wpool_b, pscale, hist_t)
    n_pages = page_table.shape[1]
    kidx_t = jnp.transpose(cache_kidx[0], (0, 2, 1))
    k_t = jnp.transpose(cache_k[0], (0, 2, 3, 1)).reshape(n_phys, LANES, PAGE)
    v_t = jnp.transpose(cache_v[0], (0, 2, 3, 1)).reshape(n_phys, LANES, PAGE)
    scores = _score_sample(page_table, qi_s.reshape(bs, IDX_HEADS, IDX_DIM), wi_s.reshape(bs, IDX_HEADS, 1),
                           kidx_t)
    mask, mnew = _select_sample(scores, snew)
    o_s = _attn_sample(
        page_table, qpad.reshape(bs, N_HEADS, LANES),
        mask.reshape(bs, 1, n_pages * PAGE),
        lnew[:, :N_HEADS].reshape(bs, N_HEADS, 1),
        jnp.broadcast_to(mnew[:, :1], (bs, N_HEADS)).reshape(bs, N_HEADS, 1),
        v_s.reshape(bs, 1, LANES), k_t, v_t)
    y_sample = _out_proj_sample(x_sample[:, 0, :], o_s.reshape(bs, N_HEADS * LANES), sga_s, pg_s,
                                ada_s[:, 2 * D_MODEL:], w_out_b)

    to_heads = lambda a: jnp.transpose(a.reshape(bp, KV_HEADS, HEAD_DIM, s), (0, 3, 1, 2))[None]
    return (
        y_prompt,
        y_sample[:, None, :],
        to_heads(k_t_p),
        to_heads(v_t_p),
        jnp.transpose(ki_t_p, (0, 2, 1))[None],
        ulast[:, 1:, :][None],
        k_s.reshape(1, bs, 1, KV_HEADS, HEAD_DIM),
        v_s.reshape(1, bs, 1, KV_HEADS, HEAD_DIM),
        ki_s.reshape(1, bs, 1, IDX_DIM),
        jnp.concatenate([state_pool[0][:, 1:, :], u_s[:, None, :]], axis=1)[None],
    )
```

```python
import functools

import jax
import jax.numpy as jnp
import numpy as np
from jax import lax
from jax.experimental import pallas as pl
from jax.experimental.pallas import tpu as pltpu

F32 = jnp.float32
BF16 = jnp.bfloat16
I32 = jnp.int32

D_MODEL = 1024
ATTN_W = 512
POOL_W = 512
HEAD_DIM = 64
N_HEADS = 8
KV_HEADS = 2
IDX_HEADS = 8
IDX_DIM = 64
TOPK = 256
POOL_WINDOWS = (2, 4, 8, 16)
POOL_GC = 128
POOL_HIST = 15
EPS = 1e-6
PAGE = 128

LANES = 128
SUBLANES = 8
BF16_ROWS = 16
KEY_CHUNK = 256
TQ = 128
TM = 512
VT_ROWS = HEAD_DIM + BF16_ROWS
SCORE_GROUP = 8
ATTN_GROUP = 4
INT_MIN = np.int32(-2 ** 31)
VMEM_LIMIT = 56 * 1024 * 1024

C_Q, C_K, C_V, C_QI, C_KW, C_GA, C_U, C_GP, N_PROJ = 0, 512, 640, 768, 1280, 1408, 1920, 2432, 2944


def _nt_dot(a, b):
    return lax.dot_general(a, b, (((1,), (1,)), ((), ())), preferred_element_type=F32)


def _dot(a, b):
    return jnp.dot(a, b, preferred_element_type=F32)


def _silu(z):
    return z / (1.0 + jnp.exp(-z))


def _lane_iota(shape):
    return lax.broadcasted_iota(I32, shape, len(shape) - 1)


def _seg_ones(n, seg):
    r = lax.broadcasted_iota(I32, (n, n), 0) // seg
    c = lax.broadcasted_iota(I32, (n, n), 1) // seg
    return jnp.where(r == c, 1.0, 0.0).astype(BF16)


def _head_rms(z, seg_mat, w):
    n = seg_mat.shape[0]
    sq = (z * z).astype(BF16)
    ss = jnp.concatenate([_dot(sq[:, j:j + n], seg_mat) for j in range(0, z.shape[1], n)], axis=1)
    return z * lax.rsqrt(ss * (1.0 / HEAD_DIM) + EPS) * w


def _float_of_rank(u):
    key = u ^ INT_MIN
    bits = key ^ ((key >> 31) & np.int32(0x7FFFFFFF))
    return lax.bitcast_convert_type(bits, F32)


def _kth_largest(count_ge, shape):
    def bit_body(it, ans):
        cand = ans | jnp.left_shift(jnp.int32(1), 31 - it)
        return jnp.where(count_ge(_float_of_rank(cand)) >= float(TOPK), cand, ans)

    ans = lax.fori_loop(0, 32, bit_body, jnp.zeros(shape, I32))
    return jnp.where(ans == 0, -jnp.inf, _float_of_rank(ans))


def _fold_rows(x, op):
    parts = [x[r:r + SUBLANES] for r in range(0, x.shape[0], SUBLANES)]
    while len(parts) > 1:
        parts = [op(parts[i], parts[i + 1]) for i in range(0, len(parts), 2)]
    return parts[0]


def _ada_kernel(cp_ref, cs_ref, w_ref, b_ref, op_ref, os_ref):
    w = w_ref[0].astype(BF16)
    op_ref[...] = _dot(_silu(cp_ref[...]).astype(BF16), w) + b_ref[...]
    os_ref[...] = _dot(_silu(cs_ref[...]).astype(BF16), w) + b_ref[...]


def _ada(c_prompt, c_sample, w_ada, b_ada):
    bp, bs = c_prompt.shape[0], c_sample.shape[0]
    return pl.pallas_call(
        _ada_kernel,
        out_shape=(jax.ShapeDtypeStruct((bp, 3 * D_MODEL), F32), jax.ShapeDtypeStruct((bs, 3 * D_MODEL), F32)),
        grid=(3,),
        in_specs=[
            pl.BlockSpec((bp, D_MODEL), lambda j: (0, 0)),
            pl.BlockSpec((bs, D_MODEL), lambda j: (0, 0)),
            pl.BlockSpec((1, D_MODEL, D_MODEL), lambda j: (0, 0, j)),
            pl.BlockSpec((1, D_MODEL), lambda j: (0, j)),
        ],
        out_specs=(pl.BlockSpec((bp, D_MODEL), lambda j: (0, j)), pl.BlockSpec((bs, D_MODEL), lambda j: (0, j))),
        compiler_params=pltpu.CompilerParams(dimension_semantics=("arbitrary",)),
        name="ada_ln",
    )(c_prompt, c_sample, w_ada, b_ada)


def _modulated_norm(x, norm_w, scale, shift):
    ms = jnp.mean(x * x, axis=-1, keepdims=True)
    return (x * lax.rsqrt(ms + EPS)) * norm_w * (1.0 + scale) + shift


def _pool_mix(d, wpool_ref, pscale, gp):
    db = d.astype(BF16)
    wide = 2 * POOL_GC
    y = jnp.concatenate([_dot(db[:, p * wide:(p + 1) * wide], wpool_ref[p])
                         for p in range(len(POOL_WINDOWS) // 2)], axis=1)
    return y * pscale * _silu(gp)


def _proj_prompt_kernel(x_ref, ada_ref, nw_ref, w_ref, qnw_ref, knw_ref, wpool_ref, ps_ref,
                        qs_ref, kt_ref, kk_ref, vt_ref, vtx_ref, qi_ref, kit_ref, kik_ref,
                        wit_ref, sga_ref, pg_ref, ulast_ref,
                        ext_ref):
    t = pl.program_id(1)
    ada = ada_ref[0]
    shift = ada[:, 0:D_MODEL]
    scale = ada[:, D_MODEL:2 * D_MODEL]
    sub_rows = KEY_CHUNK
    lo = _lane_iota((sub_rows, LANES)) < HEAD_DIM
    seg256 = _seg_ones(2 * LANES, HEAD_DIM)
    seg128 = _seg_ones(LANES, HEAD_DIM)
    ones = jnp.ones((BF16_ROWS, KEY_CHUNK), BF16)

    @pl.when(t == 0)
    def _():
        ext_ref[0:16, :] = jnp.zeros((16, POOL_W), F32)

    for sub in range(TM // sub_rows):
        rows = slice(sub * sub_rows, (sub + 1) * sub_rows)
        hb = _modulated_norm(x_ref[0, rows, :], nw_ref[...], scale, shift).astype(BF16)

        q = _dot(hb, w_ref[:, C_Q:C_Q + ATTN_W])
        qs_ref[rows, :] = _head_rms(q, seg256, qnw_ref[...]).astype(BF16)

        kv = _dot(hb, w_ref[:, C_K:C_K + 2 * LANES])
        k = _head_rms(kv[:, 0:LANES], seg128, knw_ref[...])
        kt_ref[0, :, rows] = k.T
        k_sw = pltpu.roll(k, HEAD_DIM, axis=1)
        kk_ref[0, rows, :] = jnp.where(lo, k, k_sw).astype(BF16)
        kk_ref[1, rows, :] = jnp.where(lo, k_sw, k).astype(BF16)

        v_t = kv[:, LANES:2 * LANES].T
        vt_ref[0, :, rows] = v_t
        for g in range(KV_HEADS):
            vtx_ref[0, g, sub, 0:HEAD_DIM, :] = v_t[g * HEAD_DIM:(g + 1) * HEAD_DIM].astype(BF16)
            vtx_ref[0, g, sub, HEAD_DIM:VT_ROWS, :] = ones

        qi_ref[rows, :] = _dot(hb, w_ref[:, C_QI:C_QI + ATTN_W]).astype(BF16)
        kw = _dot(hb, w_ref[:, C_KW:C_KW + LANES])
        kw_t = kw.T
        kit_ref[0, :, rows] = kw_t[0:IDX_DIM]
        wit_ref[:, rows] = kw_t[IDX_DIM:IDX_DIM + IDX_HEADS] * ((IDX_HEADS * IDX_DIM) ** -0.5)
        kik_ref[rows, :] = jnp.where(lo, kw, pltpu.roll(kw, HEAD_DIM, axis=1)).astype(BF16)

        sga_ref[rows, :] = _silu(_dot(hb, w_ref[:, C_GA:C_GA + ATTN_W])).astype(BF16)

        u = _dot(hb, w_ref[:, C_U:C_U + POOL_W])
        gp = _dot(hb, w_ref[:, C_GP:C_GP + POOL_W])
        base = 16 + sub * sub_rows
        ext_ref[base:base + sub_rows, :] = u
        pos = t * TM + sub * sub_rows + lax.broadcasted_iota(I32, (sub_rows, POOL_GC), 0)
        ds = []
        for g, w in enumerate(POOL_WINDOWS):
            cs = slice(g * POOL_GC, (g + 1) * POOL_GC)
            s = u[:, cs]
            for j in range(1, w):
                s = s + ext_ref[base - j:base - j + sub_rows, cs]
            cnt = jnp.minimum(pos + 1, w).astype(F32)
            ds.append(s / cnt - u[:, cs])
        d = jnp.concatenate(ds, axis=1)
        pg_ref[rows, :] = _pool_mix(d, wpool_ref, ps_ref[...], gp).astype(BF16)

    tail = ext_ref[TM:TM + 16, :]
    ulast_ref[0] = tail
    ext_ref[0:16, :] = tail


def _proj_prompt(x, ada_p, norm_w, w_in_b, qnw, knw, wpool_b, pscale):
    b, s, _ = x.shape
    n = b * s
    nt = s // TM
    cpt = TM // KEY_CHUNK
    row = lambda bi, ti: (bi * nt + ti, 0)
    tok = lambda bi, ti: (bi, 0, ti)
    const2 = lambda bi, ti: (0, 0)
    const3 = lambda bi, ti: (0, 0, 0)
    out_shape = (
        jax.ShapeDtypeStruct((n, ATTN_W), BF16),
        jax.ShapeDtypeStruct((b, LANES, s), F32),
        jax.ShapeDtypeStruct((KV_HEADS, n, LANES), BF16),
        jax.ShapeDtypeStruct((b, LANES, s), F32),
        jax.ShapeDtypeStruct((b, KV_HEADS, s // KEY_CHUNK, VT_ROWS, KEY_CHUNK), BF16),
        jax.ShapeDtypeStruct((n, ATTN_W), BF16),
        jax.ShapeDtypeStruct((b, IDX_DIM, s), F32),
        jax.ShapeDtypeStruct((n, LANES), BF16),
        jax.ShapeDtypeStruct((IDX_HEADS, n), F32),
        jax.ShapeDtypeStruct((n, ATTN_W), BF16),
        jax.ShapeDtypeStruct((n, POOL_W), BF16),
        jax.ShapeDtypeStruct((b, 16, POOL_W), F32),
    )
    out_specs = (
        pl.BlockSpec((TM, ATTN_W), row),
        pl.BlockSpec((1, LANES, TM), tok),
        pl.BlockSpec((KV_HEADS, TM, LANES), lambda bi, ti: (0, bi * nt + ti, 0)),
        pl.BlockSpec((1, LANES, TM), tok),
        pl.BlockSpec((1, KV_HEADS, cpt, VT_ROWS, KEY_CHUNK), lambda bi, ti: (bi, 0, ti, 0, 0)),
        pl.BlockSpec((TM, ATTN_W), row),
        pl.BlockSpec((1, IDX_DIM, TM), tok),
        pl.BlockSpec((TM, LANES), row),
        pl.BlockSpec((IDX_HEADS, TM), lambda bi, ti: (0, bi * nt + ti)),
        pl.BlockSpec((TM, ATTN_W), row),
        pl.BlockSpec((TM, POOL_W), row),
        pl.BlockSpec((1, 16, POOL_W), lambda bi, ti: (bi, 0, 0)),
    )
    in_specs = [
        pl.BlockSpec((1, TM, D_MODEL), lambda bi, ti: (bi, ti, 0)),
        pl.BlockSpec((1, 1, 3 * D_MODEL), lambda bi, ti: (bi, 0, 0)),
        pl.BlockSpec((1, D_MODEL), const2),
        pl.BlockSpec((D_MODEL, N_PROJ), const2),
        pl.BlockSpec((1, ATTN_W), const2),
        pl.BlockSpec((1, LANES), const2),
        pl.BlockSpec((2, 2 * POOL_GC, 2 * POOL_GC), const3),
        pl.BlockSpec((1, POOL_W), const2),
    ]
    return pl.pallas_call(
        _proj_prompt_kernel,
        out_shape=out_shape,
        grid=(b, nt),
        in_specs=in_specs,
        out_specs=out_specs,
        scratch_shapes=[pltpu.VMEM((16 + TM, POOL_W), F32)],
        compiler_params=pltpu.CompilerParams(
            dimension_semantics=("arbitrary", "arbitrary"), vmem_limit_bytes=VMEM_LIMIT),
        name="proj_prompt",
    )(x, ada_p, norm_w, w_in_b, qnw, knw, wpool_b, pscale)


def _attn_tile(nch, row0, wit_ref, kik_ref, kk_ref, vtx_ref, sc_ref, lg_ref, wq_ref, acc_ref):
    shape2 = (KEY_CHUNK, TQ)
    n_pairs = N_HEADS // 2
    group_of = lambda hp: (2 * hp) // (N_HEADS // KV_HEADS)

    for c in range(nch):
        kk = kik_ref[c * KEY_CHUNK:(c + 1) * KEY_CHUNK, :]
        acc = jnp.zeros(shape2, F32)
        for hp in range(n_pairs):
            s2 = _nt_dot(kk, wq_ref[0, hp])
            acc = acc + jnp.maximum(s2[:, 0:TQ], 0.0) * wit_ref[2 * hp:2 * hp + 1, :]
            acc = acc + jnp.maximum(s2[:, TQ:2 * TQ], 0.0) * wit_ref[2 * hp + 1:2 * hp + 2, :]
        if c == nch - 1:
            kpos = c * KEY_CHUNK + lax.broadcasted_iota(I32, shape2, 0)
            qpos = row0 + lax.broadcasted_iota(I32, shape2, 1)
            acc = jnp.where(kpos <= qpos, acc, -jnp.inf)
        sc_ref[c] = acc
        for hp in range(n_pairs):
            lg_ref[hp, c] = _nt_dot(kk_ref[group_of(hp), c * KEY_CHUNK:(c + 1) * KEY_CHUNK, :], wq_ref[1, hp])

    def count_pass(pred):
        parts = [_fold_rows(pred(sc_ref[c]).astype(F32), jnp.add) for c in range(nch)]
        while len(parts) > 1:
            parts = [sum(parts[j:j + 2]) for j in range(0, len(parts), 2)]
        return jnp.sum(parts[0], axis=0, keepdims=True)

    thr = _kth_largest(lambda t: count_pass(lambda x: x >= t), (1, TQ))
    need = float(TOPK) - count_pass(lambda x: x > thr)

    tri = jnp.where(lax.broadcasted_iota(I32, (KEY_CHUNK, KEY_CHUNK), 1)
                    <= lax.broadcasted_iota(I32, (KEY_CHUNK, KEY_CHUNK), 0), 1.0, 0.0).astype(BF16)

    m_part = [jnp.full((SUBLANES, TQ), -jnp.inf, F32) for _ in range(N_HEADS)]
    eq_before = jnp.zeros((1, TQ), F32)
    for c in range(nch):
        sc = sc_ref[c]
        eq = sc == thr
        rank = _dot(tri, jnp.where(eq, 1.0, 0.0).astype(BF16)) + eq_before
        sel = (sc > thr) | (eq & (rank <= need))
        if c == nch - 1:
            sel = sel & (sc > -jnp.inf)
        eq_before = rank[KEY_CHUNK - 1:KEY_CHUNK, :]
        for hp in range(n_pairs):
            for par in range(2):
                cols = slice(par * TQ, (par + 1) * TQ)
                lg = jnp.where(sel, lg_ref[hp, c, :, cols], -jnp.inf)
                lg_ref[hp, c, :, cols] = lg
                m_part[2 * hp + par] = jnp.maximum(m_part[2 * hp + par], _fold_rows(lg, jnp.maximum))
    m_rows = [jnp.max(m, axis=0, keepdims=True) for m in m_part]

    for hp in range(n_pairs):
        m2 = jnp.concatenate([m_rows[2 * hp], m_rows[2 * hp + 1]], axis=1)
        acc = jnp.zeros((VT_ROWS, 2 * TQ), F32)
        for c in range(nch):
            p2 = jnp.exp(lg_ref[hp, c] - m2).astype(BF16)
            acc = acc + _dot(vtx_ref[0, group_of(hp), c], p2)
        acc_ref[hp] = acc


def _attn_prompt_kernel(qi_ref, wit_ref, qs_ref, sga_ref, kik_ref, kk_ref, vtx_ref, ag_ref,
                        sc_ref, lg_ref, wq_ref, acc_ref):
    i = pl.program_id(1)
    nch = (i + 2) // 2
    row0 = i * TQ
    n_pairs = N_HEADS // 2
    lo = _lane_iota((TQ, LANES)) < HEAD_DIM

    for kind, ref in enumerate((qi_ref, qs_ref)):
        for hp in range(n_pairs):
            pair = ref[:, hp * LANES:(hp + 1) * LANES].astype(F32)
            wq_ref[kind, hp, 0:TQ, :] = jnp.where(lo, pair, 0.0).astype(BF16)
            wq_ref[kind, hp, TQ:2 * TQ, :] = jnp.where(lo, 0.0, pair).astype(BF16)

    for n_static in range(1, sc_ref.shape[0] + 1):
        @pl.when(nch == n_static)
        def _(n_static=n_static):
            _attn_tile(n_static, row0, wit_ref, kik_ref, kk_ref, vtx_ref, sc_ref, lg_ref, wq_ref, acc_ref)

    for hp in range(n_pairs):
        a = acc_ref[hp]
        o0 = a[0:HEAD_DIM, 0:TQ] / a[HEAD_DIM:HEAD_DIM + 1, 0:TQ]
        o1 = a[0:HEAD_DIM, TQ:2 * TQ] / a[HEAD_DIM:HEAD_DIM + 1, TQ:2 * TQ]
        pair = jnp.concatenate([o0, o1], axis=0).T
        cols = slice(hp * LANES, (hp + 1) * LANES)
        ag_ref[:, cols] = (pair * sga_ref[:, cols].astype(F32)).astype(BF16)


def _attn_prompt(qi, wit, qs, sga, kik, kk, vtx, b, s):
    n = b * s
    nq = s // TQ
    nkc = s // KEY_CHUNK
    row = lambda bi, qi_: (bi * nq + qi_, 0)
    return pl.pallas_call(
        _attn_prompt_kernel,
        out_shape=jax.ShapeDtypeStruct((n, ATTN_W), BF16),
        grid=(b, nq),
        in_specs=[
            pl.BlockSpec((TQ, ATTN_W), row),
            pl.BlockSpec((IDX_HEADS, TQ), lambda bi, qi_: (0, bi * nq + qi_)),
            pl.BlockSpec((TQ, ATTN_W), row),
            pl.BlockSpec((TQ, ATTN_W), row),
            pl.BlockSpec((s, LANES), lambda bi, qi_: (bi, 0)),
            pl.BlockSpec((KV_HEADS, s, LANES), lambda bi, qi_: (0, bi, 0)),
            pl.BlockSpec((1, KV_HEADS, nkc, VT_ROWS, KEY_CHUNK), lambda bi, qi_: (bi, 0, 0, 0, 0)),
        ],
        out_specs=pl.BlockSpec((TQ, ATTN_W), row),
        scratch_shapes=[
            pltpu.VMEM((nkc, KEY_CHUNK, TQ), F32),
            pltpu.VMEM((N_HEADS // 2, nkc, KEY_CHUNK, 2 * TQ), F32),
            pltpu.VMEM((2, N_HEADS // 2, 2 * TQ, LANES), BF16),
            pltpu.VMEM((N_HEADS // 2, VT_ROWS, 2 * TQ), F32),
        ],
        compiler_params=pltpu.CompilerParams(
            dimension_semantics=("arbitrary", "arbitrary"), vmem_limit_bytes=VMEM_LIMIT),
        name="attn_prompt",
    )(qi, wit, qs, sga, kik, kk, vtx)


def _out_proj_kernel(x_ref, a_ref, p_ref, gate_ref, w_ref, o_ref):
    y = _dot(a_ref[...], w_ref[0:ATTN_W, :]) + _dot(p_ref[...], w_ref[ATTN_W:ATTN_W + POOL_W, :])
    o_ref[0] = x_ref[0] + gate_ref[0] * y


def _out_proj_prompt(x, ag, pg, ada_p, w_out_b):
    b, s, _ = x.shape
    nt = s // TM
    row = lambda bi, ti: (bi * nt + ti, 0)
    return pl.pallas_call(
        _out_proj_kernel,
        out_shape=jax.ShapeDtypeStruct(x.shape, F32),
        grid=(b, nt),
        in_specs=[
            pl.BlockSpec((1, TM, D_MODEL), lambda bi, ti: (bi, ti, 0)),
            pl.BlockSpec((TM, ATTN_W), row),
            pl.BlockSpec((TM, POOL_W), row),
            pl.BlockSpec((1, 1, D_MODEL), lambda bi, ti: (bi, 0, 2)),
            pl.BlockSpec((D_MODEL, D_MODEL), lambda bi, ti: (0, 0)),
        ],
        out_specs=pl.BlockSpec((1, TM, D_MODEL), lambda bi, ti: (bi, ti, 0)),
        compiler_params=pltpu.CompilerParams(
            dimension_semantics=("arbitrary", "arbitrary"), vmem_limit_bytes=VMEM_LIMIT),
        name="out_proj_prompt",
    )(x, ag, pg, ada_p, w_out_b)


def _proj_sample_kernel(x_ref, ada_ref, nw_ref, w_ref, qnw_ref, knw_ref, wpool_ref, ps_ref, hist_ref,
                        qpad_ref, k_ref, v_ref, ki_ref, qi_ref, wi_ref, snew_ref, lnew_ref,
                        sga_ref, pg_ref, pool_ref):
    nb = x_ref.shape[0]
    x = x_ref[...]
    shift = ada_ref[:, 0:D_MODEL]
    scale = ada_ref[:, D_MODEL:2 * D_MODEL]
    hb = _modulated_norm(x, nw_ref[...], scale, shift).astype(BF16)

    lane = _lane_iota((nb, LANES))
    lo = lane < HEAD_DIM
    seg256 = _seg_ones(2 * LANES, HEAD_DIM)
    seg128 = _seg_ones(LANES, HEAD_DIM)
    head_sel = jnp.where(lax.broadcasted_iota(I32, (ATTN_W, LANES), 0) // HEAD_DIM
                         == lax.broadcasted_iota(I32, (ATTN_W, LANES), 1), 1.0, 0.0).astype(BF16)

    def head_sums(prod):
        hi = prod.astype(BF16)
        rest = (prod - hi.astype(F32)).astype(BF16)
        return _dot(hi, head_sel) + _dot(rest, head_sel)

    q = _head_rms(_dot(hb, w_ref[:, C_Q:C_Q + ATTN_W]), seg256, qnw_ref[...])
    qb = q.astype(BF16)
    k = _head_rms(_dot(hb, w_ref[:, C_K:C_K + LANES]), seg128, knw_ref[...])
    k_ref[...] = k
    v = _dot(hb, w_ref[:, C_V:C_V + LANES])
    v_ref[...] = v

    for hp in range(N_HEADS // 2):
        pair = q[:, hp * LANES:(hp + 1) * LANES]
        pair_sw = pltpu.roll(pair, HEAD_DIM, axis=1)
        g = (2 * hp) // (N_HEADS // KV_HEADS)
        if g == 0:
            h_even, h_odd = jnp.where(lo, pair, 0.0), jnp.where(lo, pair_sw, 0.0)
        else:
            h_even, h_odd = jnp.where(lo, 0.0, pair_sw), jnp.where(lo, 0.0, pair)
        qpad_ref[:, (2 * hp) * LANES:(2 * hp + 1) * LANES] = h_even.astype(BF16)
        qpad_ref[:, (2 * hp + 1) * LANES:(2 * hp + 2) * LANES] = h_odd.astype(BF16)

    kq = k.astype(BF16).astype(F32)
    kq_sw = pltpu.roll(kq, HEAD_DIM, axis=1)
    k0t = jnp.where(lo, kq, kq_sw)
    k1t = jnp.where(lo, kq_sw, kq)
    qf = qb.astype(F32)
    prod = jnp.concatenate([qf[:, 0:LANES] * k0t, qf[:, LANES:2 * LANES] * k0t,
                            qf[:, 2 * LANES:3 * LANES] * k1t, qf[:, 3 * LANES:4 * LANES] * k1t], axis=1)
    lnew_ref[...] = head_sums(prod)

    qi = _dot(hb, w_ref[:, C_QI:C_QI + ATTN_W])
    qib = qi.astype(BF16)
    qi_ref[...] = qib
    kw = _dot(hb, w_ref[:, C_KW:C_KW + LANES])
    kw_sw = pltpu.roll(kw, HEAD_DIM, axis=1)
    ki_ref[...] = kw[:, 0:IDX_DIM]
    wi_full = jnp.where(lane < IDX_HEADS, kw_sw, 0.0) * ((IDX_HEADS * IDX_DIM) ** -0.5)
    wi_ref[...] = wi_full[:, 0:IDX_HEADS]

    kib = kw.astype(BF16).astype(F32)
    kit = jnp.where(lo, kib, pltpu.roll(kib, HEAD_DIM, axis=1))
    qif = qib.astype(F32)
    prod_i = jnp.concatenate([qif[:, j * LANES:(j + 1) * LANES] * kit for j in range(4)], axis=1)
    s_new = jnp.maximum(head_sums(prod_i), 0.0) * wi_full
    snew_ref[...] = jnp.broadcast_to(jnp.sum(s_new, axis=1, keepdims=True), (nb, LANES))

    sga_ref[...] = _silu(_dot(hb, w_ref[:, C_GA:C_GA + ATTN_W])).astype(BF16)

    u = _dot(hb, w_ref[:, C_U:C_U + POOL_W])
    gp = _dot(hb, w_ref[:, C_GP:C_GP + POOL_W])
    for j in range(POOL_HIST - 1):
        pool_ref[j] = hist_ref[j + 1]
    pool_ref[POOL_HIST - 1] = u
    ds = []
    for g, w in enumerate(POOL_WINDOWS):
        cs = slice(g * POOL_GC, (g + 1) * POOL_GC)
        s = u[:, cs]
        for j in range(1, w):
            s = s + hist_ref[POOL_HIST - j, :, cs]
        ds.append(s / float(w) - u[:, cs])
    d = jnp.concatenate(ds, axis=1)
    pg_ref[...] = _pool_mix(d, wpool_ref, ps_ref[...], gp).astype(BF16)


def _proj_sample(x, ada_s, norm_w, w_in_b, qnw, knw, wpool_b, pscale, hist_t):
    nb = x.shape[0]
    out_shape = (
        jax.ShapeDtypeStruct((nb, N_HEADS * LANES), BF16),
        jax.ShapeDtypeStruct((nb, LANES), F32),
        jax.ShapeDtypeStruct((nb, LANES), F32),
        jax.ShapeDtypeStruct((nb, IDX_DIM), F32),
        jax.ShapeDtypeStruct((nb, ATTN_W), BF16),
        jax.ShapeDtypeStruct((nb, IDX_HEADS), F32),
        jax.ShapeDtypeStruct((nb, LANES), F32),
        jax.ShapeDtypeStruct((nb, LANES), F32),
        jax.ShapeDtypeStruct((nb, ATTN_W), BF16),
        jax.ShapeDtypeStruct((nb, POOL_W), BF16),
        jax.ShapeDtypeStruct((POOL_HIST, nb, POOL_W), F32),
    )
    return pl.pallas_call(
        _proj_sample_kernel,
        out_shape=out_shape,
        compiler_params=pltpu.CompilerParams(vmem_limit_bytes=VMEM_LIMIT),
        name="proj_sample",
    )(x, ada_s, norm_w, w_in_b, qnw, knw, wpool_b, pscale, hist_t)


def _page_copies(pt_ref, pages_hbm, buf_ref, sem_ref, step, slot, group, n_pages):
    copies = []
    for j in range(group):
        for p in range(n_pages):
            page = pt_ref[step * group + j, p]
            copies.append(pltpu.make_async_copy(
                pages_hbm.at[page], buf_ref.at[slot, j, :, pl.ds(p * PAGE, PAGE)], sem_ref.at[slot]))
    return copies


def _score_sample_kernel(pt_ref, qi_ref, wi_ref, kidx_hbm, o_ref, buf_ref, sem_ref, *, group, n_pages):
    step = pl.program_id(0)
    slot = lax.rem(step, 2)

    def copies(st, sl):
        return _page_copies(pt_ref, kidx_hbm, buf_ref, sem_ref, st, sl, group, n_pages)

    @pl.when(step == 0)
    def _():
        for cp in copies(step, slot):
            cp.start()

    @pl.when(step + 1 < pl.num_programs(0))
    def _():
        for cp in copies(step + 1, 1 - slot):
            cp.start()

    for cp in copies(step, slot):
        cp.wait()

    for j in range(group):
        ki_t = buf_ref[slot, j].astype(BF16)
        s = _dot(qi_ref[j], ki_t)
        o_ref[j:j + 1, :] = jnp.sum(jnp.maximum(s, 0.0) * wi_ref[j], axis=0, keepdims=True)


def _score_sample(page_table, qi3, wi3, kidx_t):
    nb, n_pages = page_table.shape
    n_keys = n_pages * PAGE
    g = SCORE_GROUP
    return pl.pallas_call(
        functools.partial(_score_sample_kernel, group=g, n_pages=n_pages),
        out_shape=jax.ShapeDtypeStruct((nb, n_keys), F32),
        grid_spec=pltpu.PrefetchScalarGridSpec(
            num_scalar_prefetch=1,
            grid=(nb // g,),
            in_specs=[
                pl.BlockSpec((g, IDX_HEADS, IDX_DIM), lambda s, pt: (s, 0, 0)),
                pl.BlockSpec((g, IDX_HEADS, 1), lambda s, pt: (s, 0, 0)),
                pl.BlockSpec(memory_space=pl.ANY),
            ],
            out_specs=pl.BlockSpec((g, n_keys), lambda s, pt: (s, 0)),
            scratch_shapes=[
                pltpu.VMEM((2, g, IDX_DIM, n_keys), F32),
                pltpu.SemaphoreType.DMA((2,)),
            ],
        ),
        compiler_params=pltpu.CompilerParams(
            dimension_semantics=("arbitrary",), vmem_limit_bytes=VMEM_LIMIT),
        name="score_sample",
    )(page_table, qi3, wi3, kidx_t)


def _select_sample_kernel(sc_ref, snew_ref, mask_ref, mnew_ref):
    nb, n_keys = sc_ref.shape
    nch = n_keys // KEY_CHUNK
    snew = snew_ref[...]
    ones_mat = jnp.ones((LANES, LANES), BF16)

    def count_pass(pred):
        acc = jnp.zeros((nb, LANES), F32)
        for c in range(n_keys // LANES):
            acc = acc + pred(sc_ref[:, c * LANES:(c + 1) * LANES]).astype(F32)
        return _dot(acc.astype(BF16), ones_mat) + pred(snew).astype(F32)

    thr = _kth_largest(lambda t: count_pass(lambda x: x >= t), (nb, LANES))
    need = float(TOPK) - count_pass(lambda x: x > thr)
    thr2 = jnp.concatenate([thr, thr], axis=1)
    need2 = jnp.concatenate([need, need], axis=1)
    tri = jnp.where(lax.broadcasted_iota(I32, (KEY_CHUNK, KEY_CHUNK), 0)
                    <= lax.broadcasted_iota(I32, (KEY_CHUNK, KEY_CHUNK), 1), 1.0, 0.0).astype(BF16)
    ones_cl = jnp.ones((KEY_CHUNK, LANES), BF16)

    eq_before = jnp.zeros((nb, LANES), F32)
    for c in range(nch):
        sc = sc_ref[:, c * KEY_CHUNK:(c + 1) * KEY_CHUNK]
        eq = sc == thr2
        eqb = jnp.where(eq, 1.0, 0.0).astype(BF16)
        rank = _dot(eqb, tri) + jnp.concatenate([eq_before, eq_before], axis=1)
        sel = (sc > thr2) | (eq & (rank <= need2))
        mask_ref[:, c * KEY_CHUNK:(c + 1) * KEY_CHUNK] = jnp.where(sel, 0.0, -jnp.inf)
        eq_before = eq_before + _dot(eqb, ones_cl)
    sel_new = (snew > thr) | ((snew == thr) & (eq_before + 1.0 <= need))
    mnew_ref[...] = jnp.where(sel_new, 0.0, -jnp.inf)


def _select_sample(scores, snew):
    nb, n_keys = scores.shape
    return pl.pallas_call(
        _select_sample_kernel,
        out_shape=(jax.ShapeDtypeStruct((nb, n_keys), F32), jax.ShapeDtypeStruct((nb, LANES), F32)),
        name="select_sample",
    )(scores, snew)


def _attn_sample_kernel(pt_ref, q_ref, mask_ref, lnew_ref, mnew_ref, vnew_ref, k_hbm, v_hbm,
                        o_ref, kbuf_ref, vbuf_ref, ksem_ref, vsem_ref, *, group, n_pages):
    step = pl.program_id(0)
    slot = lax.rem(step, 2)

    def copies(st, sl):
        return (_page_copies(pt_ref, k_hbm, kbuf_ref, ksem_ref, st, sl, group, n_pages)
                + _page_copies(pt_ref, v_hbm, vbuf_ref, vsem_ref, st, sl, group, n_pages))

    @pl.when(step == 0)
    def _():
        for cp in copies(step, slot):
            cp.start()

    @pl.when(step + 1 < pl.num_programs(0))
    def _():
        for cp in copies(step + 1, 1 - slot):
            cp.start()

    for cp in copies(step, slot):
        cp.wait()

    for j in range(group):
        k_t = kbuf_ref[slot, j].astype(BF16)
        lg = _dot(q_ref[j], k_t) + mask_ref[j]
        lg_n = lnew_ref[j] + mnew_ref[j]
        m = jnp.maximum(jnp.max(lg, axis=1, keepdims=True), lg_n)
        p = jnp.exp(lg - m)
        p_n = jnp.exp(lg_n - m)
        l = jnp.sum(p, axis=1, keepdims=True) + p_n
        acc = _nt_dot(p.astype(BF16), vbuf_ref[slot, j].astype(BF16)) + p_n * vnew_ref[j]
        o_ref[j] = acc / l


def _attn_sample(page_table, qpad3, mask3, lnew3, mnew3, vnew3, k_t, v_t):
    nb, n_pages = page_table.shape
    n_keys = n_pages * PAGE
    g = ATTN_GROUP
    per_s = lambda s, pt: (s, 0, 0)
    return pl.pallas_call(
        functools.partial(_attn_sample_kernel, group=g, n_pages=n_pages),
        out_shape=jax.ShapeDtypeStruct((nb, N_HEADS, LANES), F32),
        grid_spec=pltpu.PrefetchScalarGridSpec(
            num_scalar_prefetch=1,
            grid=(nb // g,),
            in_specs=[
                pl.BlockSpec((g, N_HEADS, LANES), per_s),
                pl.BlockSpec((g, 1, n_keys), per_s),
                pl.BlockSpec((g, N_HEADS, 1), per_s),
                pl.BlockSpec((g, N_HEADS, 1), per_s),
                pl.BlockSpec((g, 1, LANES), per_s),
                pl.BlockSpec(memory_space=pl.ANY),
                pl.BlockSpec(memory_space=pl.ANY),
            ],
            out_specs=pl.BlockSpec((g, N_HEADS, LANES), per_s),
            scratch_shapes=[
                pltpu.VMEM((2, g, LANES, n_keys), F32),
                pltpu.VMEM((2, g, LANES, n_keys), F32),
                pltpu.SemaphoreType.DMA((2,)),
                pltpu.SemaphoreType.DMA((2,)),
            ],
        ),
        compiler_params=pltpu.CompilerParams(
            dimension_semantics=("arbitrary",), vmem_limit_bytes=VMEM_LIMIT),
        name="attn_sample",
    )(page_table, qpad3, mask3, lnew3, mnew3, vnew3, k_t, v_t)


def _out_proj_sample_kernel(x_ref, a_ref, sga_ref, p_ref, ada_ref, w_ref, o_ref):
    lo = _lane_iota((x_ref.shape[0], LANES)) < HEAD_DIM
    pairs = []
    for hp in range(N_HEADS // 2):
        even = a_ref[:, (2 * hp) * LANES:(2 * hp + 1) * LANES]
        odd = a_ref[:, (2 * hp + 1) * LANES:(2 * hp + 2) * LANES]
        if (2 * hp) // (N_HEADS // KV_HEADS) == 0:
            pairs.append(jnp.where(lo, even, pltpu.roll(odd, HEAD_DIM, axis=1)))
        else:
            pairs.append(jnp.where(lo, pltpu.roll(even, HEAD_DIM, axis=1), odd))
    a = jnp.concatenate(pairs, axis=1)
    ag = (a * sga_ref[...].astype(F32)).astype(BF16)
    y = _dot(ag, w_ref[0:ATTN_W, :]) + _dot(p_ref[...], w_ref[ATTN_W:ATTN_W + POOL_W, :])
    o_ref[...] = x_ref[...] + ada_ref[:, 2 * D_MODEL:3 * D_MODEL] * y


def _out_proj_sample(x, a, sga, pg, ada_s, w_out_b):
    return pl.pallas_call(
        _out_proj_sample_kernel,
        out_shape=jax.ShapeDtypeStruct(x.shape, F32),
        compiler_params=pltpu.CompilerParams(vmem_limit_bytes=VMEM_LIMIT),
        name="out_proj_sample",
    )(x, a, sga, pg, ada_s, w_out_b)


def _permute_w_in(w_in):
    q, k, v, qi, ki, wi, ga, u, gp = jnp.split(
        w_in, np.cumsum([512, 128, 128, 512, 64, 8, 512, 512]).tolist(), axis=1)
    pad = jnp.zeros((w_in.shape[0], LANES - IDX_DIM - IDX_HEADS), w_in.dtype)
    return jnp.concatenate([q, k, v, qi, ki, wi, pad, ga, u, gp], axis=1).astype(BF16)


def kernel(x_prompt, x_sample, cache_k, cache_v, cache_kidx, state_pool, page_table, c_prompt, c_sample,
           norm_w, w_ada, b_ada, w_in, q_norm_w, k_norm_w, w_pool, pool_scale, w_out):
    bp, s, _ = x_prompt.shape
    bs = x_sample.shape[0]
    assert w_in.shape[0] == 1 and x_sample.shape[1] == 1, "single layer, single decode token"
    n_phys = cache_k.shape[1]

    w_in_b = _permute_w_in(w_in[0])
    w_out_b = w_out[0].astype(BF16)
    zero_blk = jnp.zeros((POOL_GC, POOL_GC), w_pool.dtype)
    wpool_b = jnp.stack([jnp.block([[w_pool[0, 2 * p], zero_blk], [zero_blk, w_pool[0, 2 * p + 1]]])
                         for p in range(len(POOL_WINDOWS) // 2)]).astype(BF16)
    qnw = jnp.tile(q_norm_w[0], N_HEADS)[None, :] * (HEAD_DIM ** -0.5)
    knw = jnp.tile(k_norm_w[0], KV_HEADS)[None, :]
    nw = norm_w[0][None, :]
    pscale = pool_scale[0][None, :]

    ada_p, ada_s = _ada(c_prompt, c_sample, w_ada, b_ada)
    ada_p = ada_p.reshape(bp, 1, 3 * D_MODEL)

    (qs, k_t_p, kk, v_t_p, vtx, qi, ki_t_p, kik, wit, sga, pg, ulast) = _proj_prompt(
        x_prompt, ada_p, nw, w_in_b, qnw, knw, wpool_b, pscale)
    ag = _attn_prompt(qi, wit, qs, sga, kik, kk, vtx, bp, s)
    y_prompt = _out_proj_prompt(x_prompt, ag, pg, ada_p, w_out_b)

    hist_t = jnp.transpose(state_pool[0], (1, 0, 2))
    (qpad, k_s, v_s, ki_s, qi_s, wi_s, snew, lnew, sga_s, pg_s, pool_s) = _proj_sample(
        x_sample[:, 0, :], ada_s, nw, w_in_b, qnw, knw, wpool_b, pscale, hist_t)
    n_pages = page_table.shape[1]
    kidx_t = jnp.transpose(cache_kidx[0], (0, 2, 1))
    k_t = jnp.transpose(cache_k[0], (0, 2, 3, 1)).reshape(n_phys, LANES, PAGE)
    v_t = jnp.transpose(cache_v[0], (0, 2, 3, 1)).reshape(n_phys, LANES, PAGE)
    scores = _score_sample(page_table, qi_s.reshape(bs, IDX_HEADS, IDX_DIM), wi_s.reshape(bs, IDX_HEADS, 1),
                           kidx_t)
    mask, mnew = _select_sample(scores, snew)
    o_s = _attn_sample(
        page_table, qpad.reshape(bs, N_HEADS, LANES),
        mask.reshape(bs, 1, n_pages * PAGE),
        lnew[:, :N_HEADS].reshape(bs, N_HEADS, 1),
        jnp.broadcast_to(mnew[:, :1], (bs, N_HEADS)).reshape(bs, N_HEADS, 1),
        v_s.reshape(bs, 1, LANES), k_t, v_t)
    y_sample = _out_proj_sample(x_sample[:, 0, :], o_s.reshape(bs, N_HEADS * LANES), sga_s, pg_s, ada_s, w_out_b)

    to_heads = lambda a: jnp.transpose(a.reshape(bp, KV_HEADS, HEAD_DIM, s), (0, 3, 1, 2))[None]
    return (
        y_prompt,
        y_sample[:, None, :],
        to_heads(k_t_p),
        to_heads(v_t_p),
        jnp.transpose(ki_t_p, (0, 2, 1))[None],
        ulast[:, 1:, :][None],
        k_s.reshape(1, bs, 1, KV_HEADS, HEAD_DIM),
        v_s.reshape(1, bs, 1, KV_HEADS, HEAD_DIM),
        ki_s.reshape(1, bs, 1, IDX_DIM),
        jnp.transpose(pool_s, (1, 0, 2))[None],
    )
```

```python
import functools

import jax
import jax.numpy as jnp
import numpy as np
from jax import lax
from jax.experimental import pallas as pl
from jax.experimental.pallas import tpu as pltpu

F32 = jnp.float32
BF16 = jnp.bfloat16
I32 = jnp.int32

D_MODEL = 1024
ATTN_W = 512
POOL_W = 512
HEAD_DIM = 64
N_HEADS = 8
KV_HEADS = 2
IDX_HEADS = 8
IDX_DIM = 64
TOPK = 256
POOL_WINDOWS = (2, 4, 8, 16)
POOL_GC = 128
POOL_HIST = 15
EPS = 1e-6
PAGE = 128

LANES = 128
SUBLANES = 8
BF16_ROWS = 16
KEY_CHUNK = 256
TQ = 128
TM = 512
VT_ROWS = HEAD_DIM + BF16_ROWS
SCORE_GROUP = 8
ATTN_GROUP = 4
INT_MIN = np.int32(-2 ** 31)
VMEM_LIMIT = 56 * 1024 * 1024

C_Q, C_K, C_V, C_QI, C_KW, C_GA, C_U, C_GP, N_PROJ = 0, 512, 640, 768, 1280, 1408, 1920, 2432, 2944


def _nt_dot(a, b):
    return lax.dot_general(a, b, (((1,), (1,)), ((), ())), preferred_element_type=F32)


def _dot(a, b):
    return jnp.dot(a, b, preferred_element_type=F32)


def _silu(z):
    return z / (1.0 + jnp.exp(-z))


def _lane_iota(shape):
    return lax.broadcasted_iota(I32, shape, len(shape) - 1)


def _seg_ones(n, seg):
    r = lax.broadcasted_iota(I32, (n, n), 0) // seg
    c = lax.broadcasted_iota(I32, (n, n), 1) // seg
    return jnp.where(r == c, 1.0, 0.0).astype(BF16)


def _head_rms(z, seg_mat, w):
    n = seg_mat.shape[0]
    sq = (z * z).astype(BF16)
    ss = jnp.concatenate([_dot(sq[:, j:j + n], seg_mat) for j in range(0, z.shape[1], n)], axis=1)
    return z * lax.rsqrt(ss * (1.0 / HEAD_DIM) + EPS) * w


def _float_of_rank(u):
    key = u ^ INT_MIN
    bits = key ^ ((key >> 31) & np.int32(0x7FFFFFFF))
    return lax.bitcast_convert_type(bits, F32)


def _kth_largest(count_ge, shape):
    def bit_body(it, ans):
        cand = ans | jnp.left_shift(jnp.int32(1), 31 - it)
        return jnp.where(count_ge(_float_of_rank(cand)) >= float(TOPK), cand, ans)

    ans = lax.fori_loop(0, 32, bit_body, jnp.zeros(shape, I32))
    return jnp.where(ans == 0, -jnp.inf, _float_of_rank(ans))


def _fold_rows(x, op):
    parts = [x[r:r + SUBLANES] for r in range(0, x.shape[0], SUBLANES)]
    while len(parts) > 1:
        parts = [op(parts[i], parts[i + 1]) for i in range(0, len(parts), 2)]
    return parts[0]


def _ada_kernel(cp_ref, cs_ref, w_ref, b_ref, op_ref, os_ref):
    w = w_ref[0].astype(BF16)
    op_ref[...] = _dot(_silu(cp_ref[...]).astype(BF16), w) + b_ref[...]
    os_ref[...] = _dot(_silu(cs_ref[...]).astype(BF16), w) + b_ref[...]


def _ada(c_prompt, c_sample, w_ada, b_ada):
    bp, bs = c_prompt.shape[0], c_sample.shape[0]
    return pl.pallas_call(
        _ada_kernel,
        out_shape=(jax.ShapeDtypeStruct((bp, 3 * D_MODEL), F32), jax.ShapeDtypeStruct((bs, 3 * D_MODEL), F32)),
        grid=(3,),
        in_specs=[
            pl.BlockSpec((bp, D_MODEL), lambda j: (0, 0)),
            pl.BlockSpec((bs, D_MODEL), lambda j: (0, 0)),
            pl.BlockSpec((1, D_MODEL, D_MODEL), lambda j: (0, 0, j)),
            pl.BlockSpec((1, D_MODEL), lambda j: (0, j)),
        ],
        out_specs=(pl.BlockSpec((bp, D_MODEL), lambda j: (0, j)), pl.BlockSpec((bs, D_MODEL), lambda j: (0, j))),
        compiler_params=pltpu.CompilerParams(dimension_semantics=("arbitrary",)),
        name="ada_ln",
    )(c_prompt, c_sample, w_ada, b_ada)


def _modulated_norm(x, norm_w, scale, shift):
    ms = jnp.mean(x * x, axis=-1, keepdims=True)
    return (x * lax.rsqrt(ms + EPS)) * norm_w * (1.0 + scale) + shift


def _pool_mix(d, wpool_ref, pscale, gp):
    db = d.astype(BF16)
    wide = 2 * POOL_GC
    y = jnp.concatenate([_dot(db[:, p * wide:(p + 1) * wide], wpool_ref[p])
                         for p in range(len(POOL_WINDOWS) // 2)], axis=1)
    return y * pscale * _silu(gp)


def _proj_prompt_kernel(x_ref, ada_ref, nw_ref, w_ref, qnw_ref, knw_ref, wpool_ref, ps_ref,
                        qs_ref, kt_ref, kk_ref, vt_ref, vtx_ref, qi_ref, kit_ref, kik_ref,
                        wit_ref, sga_ref, pg_ref, ulast_ref,
                        ext_ref):
    t = pl.program_id(1)
    ada = ada_ref[0]
    shift = ada[:, 0:D_MODEL]
    scale = ada[:, D_MODEL:2 * D_MODEL]
    sub_rows = KEY_CHUNK
    lo = _lane_iota((sub_rows, LANES)) < HEAD_DIM
    seg256 = _seg_ones(2 * LANES, HEAD_DIM)
    seg128 = _seg_ones(LANES, HEAD_DIM)
    ones = jnp.ones((BF16_ROWS, KEY_CHUNK), BF16)

    @pl.when(t == 0)
    def _():
        ext_ref[0:16, :] = jnp.zeros((16, POOL_W), F32)

    for sub in range(TM // sub_rows):
        rows = slice(sub * sub_rows, (sub + 1) * sub_rows)
        hb = _modulated_norm(x_ref[0, rows, :], nw_ref[...], scale, shift).astype(BF16)

        q = _dot(hb, w_ref[:, C_Q:C_Q + ATTN_W])
        qs_ref[rows, :] = _head_rms(q, seg256, qnw_ref[...]).astype(BF16)

        kv = _dot(hb, w_ref[:, C_K:C_K + 2 * LANES])
        k = _head_rms(kv[:, 0:LANES], seg128, knw_ref[...])
        kt_ref[0, :, rows] = k.T
        k_sw = pltpu.roll(k, HEAD_DIM, axis=1)
        kk_ref[0, rows, :] = jnp.where(lo, k, k_sw).astype(BF16)
        kk_ref[1, rows, :] = jnp.where(lo, k_sw, k).astype(BF16)

        v_t = kv[:, LANES:2 * LANES].T
        vt_ref[0, :, rows] = v_t
        for g in range(KV_HEADS):
            vtx_ref[0, g, sub, 0:HEAD_DIM, :] = v_t[g * HEAD_DIM:(g + 1) * HEAD_DIM].astype(BF16)
            vtx_ref[0, g, sub, HEAD_DIM:VT_ROWS, :] = ones

        qi_ref[rows, :] = _dot(hb, w_ref[:, C_QI:C_QI + ATTN_W]).astype(BF16)
        kw = _dot(hb, w_ref[:, C_KW:C_KW + LANES])
        kw_t = kw.T
        kit_ref[0, :, rows] = kw_t[0:IDX_DIM]
        wit_ref[:, rows] = kw_t[IDX_DIM:IDX_DIM + IDX_HEADS] * ((IDX_HEADS * IDX_DIM) ** -0.5)
        kik_ref[rows, :] = jnp.where(lo, kw, pltpu.roll(kw, HEAD_DIM, axis=1)).astype(BF16)

        sga_ref[rows, :] = _silu(_dot(hb, w_ref[:, C_GA:C_GA + ATTN_W])).astype(BF16)

        u = _dot(hb, w_ref[:, C_U:C_U + POOL_W])
        gp = _dot(hb, w_ref[:, C_GP:C_GP + POOL_W])
        base = 16 + sub * sub_rows
        ext_ref[base:base + sub_rows, :] = u
        pos = t * TM + sub * sub_rows + lax.broadcasted_iota(I32, (sub_rows, POOL_GC), 0)
        ds = []
        for g, w in enumerate(POOL_WINDOWS):
            cs = slice(g * POOL_GC, (g + 1) * POOL_GC)
            s = u[:, cs]
            for j in range(1, w):
                s = s + ext_ref[base - j:base - j + sub_rows, cs]
            cnt = jnp.minimum(pos + 1, w).astype(F32)
            ds.append(s / cnt - u[:, cs])
        d = jnp.concatenate(ds, axis=1)
        pg_ref[rows, :] = _pool_mix(d, wpool_ref, ps_ref[...], gp).astype(BF16)

    tail = ext_ref[TM:TM + 16, :]
    ulast_ref[0] = tail
    ext_ref[0:16, :] = tail


def _proj_prompt(x, ada_p, norm_w, w_in_b, qnw, knw, wpool_b, pscale):
    b, s, _ = x.shape
    n = b * s
    nt = s // TM
    cpt = TM // KEY_CHUNK
    row = lambda bi, ti: (bi * nt + ti, 0)
    tok = lambda bi, ti: (bi, 0, ti)
    const2 = lambda bi, ti: (0, 0)
    const3 = lambda bi, ti: (0, 0, 0)
    out_shape = (
        jax.ShapeDtypeStruct((n, ATTN_W), BF16),
        jax.ShapeDtypeStruct((b, LANES, s), F32),
        jax.ShapeDtypeStruct((KV_HEADS, n, LANES), BF16),
        jax.ShapeDtypeStruct((b, LANES, s), F32),
        jax.ShapeDtypeStruct((b, KV_HEADS, s // KEY_CHUNK, VT_ROWS, KEY_CHUNK), BF16),
        jax.ShapeDtypeStruct((n, ATTN_W), BF16),
        jax.ShapeDtypeStruct((b, IDX_DIM, s), F32),
        jax.ShapeDtypeStruct((n, LANES), BF16),
        jax.ShapeDtypeStruct((IDX_HEADS, n), F32),
        jax.ShapeDtypeStruct((n, ATTN_W), BF16),
        jax.ShapeDtypeStruct((n, POOL_W), BF16),
        jax.ShapeDtypeStruct((b, 16, POOL_W), F32),
    )
    out_specs = (
        pl.BlockSpec((TM, ATTN_W), row),
        pl.BlockSpec((1, LANES, TM), tok),
        pl.BlockSpec((KV_HEADS, TM, LANES), lambda bi, ti: (0, bi * nt + ti, 0)),
        pl.BlockSpec((1, LANES, TM), tok),
        pl.BlockSpec((1, KV_HEADS, cpt, VT_ROWS, KEY_CHUNK), lambda bi, ti: (bi, 0, ti, 0, 0)),
        pl.BlockSpec((TM, ATTN_W), row),
        pl.BlockSpec((1, IDX_DIM, TM), tok),
        pl.BlockSpec((TM, LANES), row),
        pl.BlockSpec((IDX_HEADS, TM), lambda bi, ti: (0, bi * nt + ti)),
        pl.BlockSpec((TM, ATTN_W), row),
        pl.BlockSpec((TM, POOL_W), row),
        pl.BlockSpec((1, 16, POOL_W), lambda bi, ti: (bi, 0, 0)),
    )
    in_specs = [
        pl.BlockSpec((1, TM, D_MODEL), lambda bi, ti: (bi, ti, 0)),
        pl.BlockSpec((1, 1, 3 * D_MODEL), lambda bi, ti: (bi, 0, 0)),
        pl.BlockSpec((1, D_MODEL), const2),
        pl.BlockSpec((D_MODEL, N_PROJ), const2),
        pl.BlockSpec((1, ATTN_W), const2),
        pl.BlockSpec((1, LANES), const2),
        pl.BlockSpec((2, 2 * POOL_GC, 2 * POOL_GC), const3),
        pl.BlockSpec((1, POOL_W), const2),
    ]
    return pl.pallas_call(
        _proj_prompt_kernel,
        out_shape=out_shape,
        grid=(b, nt),
        in_specs=in_specs,
        out_specs=out_specs,
        scratch_shapes=[pltpu.VMEM((16 + TM, POOL_W), F32)],
        compiler_params=pltpu.CompilerParams(
            dimension_semantics=("arbitrary", "arbitrary"), vmem_limit_bytes=VMEM_LIMIT),
        name="proj_prompt",
    )(x, ada_p, norm_w, w_in_b, qnw, knw, wpool_b, pscale)


def _attn_tile(tile, wit_ref, kik_ref, kk_ref, vtx_ref, sc_ref, lg_ref, wq_ref, acc_ref):
    n_pairs = N_HEADS // 2
    group_of = lambda hp: (2 * hp) // (N_HEADS // KV_HEADS)
    n_keys = (tile + 1) * TQ
    nch = -(-n_keys // KEY_CHUNK)
    rows_of = lambda c: min(KEY_CHUNK, n_keys - c * KEY_CHUNK)

    for c in range(nch):
        rows = rows_of(c)
        keys = slice(c * KEY_CHUNK, c * KEY_CHUNK + rows)
        kk = kik_ref[keys, :]
        acc = jnp.zeros((rows, TQ), F32)
        for hp in range(n_pairs):
            s2 = _nt_dot(kk, wq_ref[0, hp])
            acc = acc + jnp.maximum(s2[:, 0:TQ], 0.0) * wit_ref[2 * hp:2 * hp + 1, :]
            acc = acc + jnp.maximum(s2[:, TQ:2 * TQ], 0.0) * wit_ref[2 * hp + 1:2 * hp + 2, :]
        if c == nch - 1:
            kpos = c * KEY_CHUNK + lax.broadcasted_iota(I32, (rows, TQ), 0)
            qpos = tile * TQ + lax.broadcasted_iota(I32, (rows, TQ), 1)
            acc = jnp.where(kpos <= qpos, acc, -jnp.inf)
        sc_ref[c, 0:rows, :] = acc
        for hp in range(n_pairs):
            lg_ref[hp, c, 0:rows, :] = _nt_dot(kk_ref[group_of(hp), keys, :], wq_ref[1, hp])

    def count_pass(pred):
        parts = [_fold_rows(pred(sc_ref[c, 0:rows_of(c), :]).astype(F32), jnp.add) for c in range(nch)]
        while len(parts) > 1:
            parts = [sum(parts[j:j + 2]) for j in range(0, len(parts), 2)]
        return jnp.sum(parts[0], axis=0, keepdims=True)

    if n_keys <= TOPK:
        thr = jnp.full((1, TQ), -jnp.inf, F32)
        need = jnp.zeros((1, TQ), F32)
    else:
        thr = _kth_largest(lambda t: count_pass(lambda x: x >= t), (1, TQ))
        need = float(TOPK) - count_pass(lambda x: x > thr)

    tri = jnp.where(lax.broadcasted_iota(I32, (KEY_CHUNK, KEY_CHUNK), 1)
                    <= lax.broadcasted_iota(I32, (KEY_CHUNK, KEY_CHUNK), 0), 1.0, 0.0).astype(BF16)

    m_part = [jnp.full((SUBLANES, TQ), -jnp.inf, F32) for _ in range(N_HEADS)]
    eq_before = jnp.zeros((1, TQ), F32)
    for c in range(nch):
        rows = rows_of(c)
        sc = sc_ref[c, 0:rows, :]
        eq = sc == thr
        rank = _dot(tri[0:rows, 0:rows], jnp.where(eq, 1.0, 0.0).astype(BF16)) + eq_before
        sel = (sc > thr) | (eq & (rank <= need))
        if c == nch - 1:
            sel = sel & (sc > -jnp.inf)
        eq_before = rank[rows - 1:rows, :]
        for hp in range(n_pairs):
            for par in range(2):
                cols = slice(par * TQ, (par + 1) * TQ)
                lg = jnp.where(sel, lg_ref[hp, c, 0:rows, cols], -jnp.inf)
                lg_ref[hp, c, 0:rows, cols] = lg
                m_part[2 * hp + par] = jnp.maximum(m_part[2 * hp + par], _fold_rows(lg, jnp.maximum))
    m_rows = [jnp.max(m, axis=0, keepdims=True) for m in m_part]

    for hp in range(n_pairs):
        m2 = jnp.concatenate([m_rows[2 * hp], m_rows[2 * hp + 1]], axis=1)
        acc = jnp.zeros((VT_ROWS, 2 * TQ), F32)
        for c in range(nch):
            rows = rows_of(c)
            p2 = jnp.exp(lg_ref[hp, c, 0:rows, :] - m2).astype(BF16)
            acc = acc + _dot(vtx_ref[0, group_of(hp), c, :, 0:rows], p2)
        acc_ref[hp] = acc


def _attn_prompt_kernel(qi_ref, wit_ref, qs_ref, sga_ref, kik_ref, kk_ref, vtx_ref, ag_ref,
                        sc_ref, lg_ref, wq_ref, acc_ref):
    i = pl.program_id(1)
    n_pairs = N_HEADS // 2
    lo = _lane_iota((TQ, LANES)) < HEAD_DIM

    for kind, ref in enumerate((qi_ref, qs_ref)):
        for hp in range(n_pairs):
            pair = ref[:, hp * LANES:(hp + 1) * LANES].astype(F32)
            wq_ref[kind, hp, 0:TQ, :] = jnp.where(lo, pair, 0.0).astype(BF16)
            wq_ref[kind, hp, TQ:2 * TQ, :] = jnp.where(lo, 0.0, pair).astype(BF16)

    for tile in range(sc_ref.shape[0] * KEY_CHUNK // TQ):
        @pl.when(i == tile)
        def _(tile=tile):
            _attn_tile(tile, wit_ref, kik_ref, kk_ref, vtx_ref, sc_ref, lg_ref, wq_ref, acc_ref)

    for hp in range(n_pairs):
        a = acc_ref[hp]
        o0 = a[0:HEAD_DIM, 0:TQ] / a[HEAD_DIM:HEAD_DIM + 1, 0:TQ]
        o1 = a[0:HEAD_DIM, TQ:2 * TQ] / a[HEAD_DIM:HEAD_DIM + 1, TQ:2 * TQ]
        pair = jnp.concatenate([o0, o1], axis=0).T
        cols = slice(hp * LANES, (hp + 1) * LANES)
        ag_ref[:, cols] = (pair * sga_ref[:, cols].astype(F32)).astype(BF16)


def _attn_prompt(qi, wit, qs, sga, kik, kk, vtx, b, s):
    n = b * s
    nq = s // TQ
    nkc = s // KEY_CHUNK
    row = lambda bi, qi_: (bi * nq + qi_, 0)
    return pl.pallas_call(
        _attn_prompt_kernel,
        out_shape=jax.ShapeDtypeStruct((n, ATTN_W), BF16),
        grid=(b, nq),
        in_specs=[
            pl.BlockSpec((TQ, ATTN_W), row),
            pl.BlockSpec((IDX_HEADS, TQ), lambda bi, qi_: (0, bi * nq + qi_)),
            pl.BlockSpec((TQ, ATTN_W), row),
            pl.BlockSpec((TQ, ATTN_W), row),
            pl.BlockSpec((s, LANES), lambda bi, qi_: (bi, 0)),
            pl.BlockSpec((KV_HEADS, s, LANES), lambda bi, qi_: (0, bi, 0)),
            pl.BlockSpec((1, KV_HEADS, nkc, VT_ROWS, KEY_CHUNK), lambda bi, qi_: (bi, 0, 0, 0, 0)),
        ],
        out_specs=pl.BlockSpec((TQ, ATTN_W), row),
        scratch_shapes=[
            pltpu.VMEM((nkc, KEY_CHUNK, TQ), F32),
            pltpu.VMEM((N_HEADS // 2, nkc, KEY_CHUNK, 2 * TQ), F32),
            pltpu.VMEM((2, N_HEADS // 2, 2 * TQ, LANES), BF16),
            pltpu.VMEM((N_HEADS // 2, VT_ROWS, 2 * TQ), F32),
        ],
        compiler_params=pltpu.CompilerParams(
            dimension_semantics=("arbitrary", "arbitrary"), vmem_limit_bytes=VMEM_LIMIT),
        name="attn_prompt",
    )(qi, wit, qs, sga, kik, kk, vtx)


def _out_proj_kernel(x_ref, a_ref, p_ref, gate_ref, w_ref, o_ref):
    y = _dot(a_ref[...], w_ref[0:ATTN_W, :]) + _dot(p_ref[...], w_ref[ATTN_W:ATTN_W + POOL_W, :])
    o_ref[0] = x_ref[0] + gate_ref[0] * y


def _out_proj_prompt(x, ag, pg, ada_p, w_out_b):
    b, s, _ = x.shape
    nt = s // TM
    row = lambda bi, ti: (bi * nt + ti, 0)
    return pl.pallas_call(
        _out_proj_kernel,
        out_shape=jax.ShapeDtypeStruct(x.shape, F32),
        grid=(b, nt),
        in_specs=[
            pl.BlockSpec((1, TM, D_MODEL), lambda bi, ti: (bi, ti, 0)),
            pl.BlockSpec((TM, ATTN_W), row),
            pl.BlockSpec((TM, POOL_W), row),
            pl.BlockSpec((1, 1, D_MODEL), lambda bi, ti: (bi, 0, 2)),
            pl.BlockSpec((D_MODEL, D_MODEL), lambda bi, ti: (0, 0)),
        ],
        out_specs=pl.BlockSpec((1, TM, D_MODEL), lambda bi, ti: (bi, ti, 0)),
        compiler_params=pltpu.CompilerParams(
            dimension_semantics=("arbitrary", "arbitrary"), vmem_limit_bytes=VMEM_LIMIT),
        name="out_proj_prompt",
    )(x, ag, pg, ada_p, w_out_b)


def _proj_sample_kernel(x_ref, ada_ref, nw_ref, w_ref, qnw_ref, knw_ref, wpool_ref, ps_ref, hist_ref,
                        qpad_ref, k_ref, v_ref, ki_ref, qi_ref, wi_ref, snew_ref, lnew_ref,
                        sga_ref, pg_ref, pool_ref):
    nb = x_ref.shape[0]
    x = x_ref[...]
    shift = ada_ref[:, 0:D_MODEL]
    scale = ada_ref[:, D_MODEL:2 * D_MODEL]
    hb = _modulated_norm(x, nw_ref[...], scale, shift).astype(BF16)

    lane = _lane_iota((nb, LANES))
    lo = lane < HEAD_DIM
    seg256 = _seg_ones(2 * LANES, HEAD_DIM)
    seg128 = _seg_ones(LANES, HEAD_DIM)
    head_sel = jnp.where(lax.broadcasted_iota(I32, (ATTN_W, LANES), 0) // HEAD_DIM
                         == lax.broadcasted_iota(I32, (ATTN_W, LANES), 1), 1.0, 0.0).astype(BF16)

    def head_sums(prod):
        hi = prod.astype(BF16)
        rest = (prod - hi.astype(F32)).astype(BF16)
        return _dot(hi, head_sel) + _dot(rest, head_sel)

    q = _head_rms(_dot(hb, w_ref[:, C_Q:C_Q + ATTN_W]), seg256, qnw_ref[...])
    qb = q.astype(BF16)
    k = _head_rms(_dot(hb, w_ref[:, C_K:C_K + LANES]), seg128, knw_ref[...])
    k_ref[...] = k
    v = _dot(hb, w_ref[:, C_V:C_V + LANES])
    v_ref[...] = v

    for hp in range(N_HEADS // 2):
        pair = q[:, hp * LANES:(hp + 1) * LANES]
        pair_sw = pltpu.roll(pair, HEAD_DIM, axis=1)
        g = (2 * hp) // (N_HEADS // KV_HEADS)
        if g == 0:
            h_even, h_odd = jnp.where(lo, pair, 0.0), jnp.where(lo, pair_sw, 0.0)
        else:
            h_even, h_odd = jnp.where(lo, 0.0, pair_sw), jnp.where(lo, 0.0, pair)
        qpad_ref[:, (2 * hp) * LANES:(2 * hp + 1) * LANES] = h_even.astype(BF16)
        qpad_ref[:, (2 * hp + 1) * LANES:(2 * hp + 2) * LANES] = h_odd.astype(BF16)

    kq = k.astype(BF16).astype(F32)
    kq_sw = pltpu.roll(kq, HEAD_DIM, axis=1)
    k0t = jnp.where(lo, kq, kq_sw)
    k1t = jnp.where(lo, kq_sw, kq)
    qf = qb.astype(F32)
    prod = jnp.concatenate([qf[:, 0:LANES] * k0t, qf[:, LANES:2 * LANES] * k0t,
                            qf[:, 2 * LANES:3 * LANES] * k1t, qf[:, 3 * LANES:4 * LANES] * k1t], axis=1)
    lnew_ref[...] = head_sums(prod)

    qi = _dot(hb, w_ref[:, C_QI:C_QI + ATTN_W])
    qib = qi.astype(BF16)
    qi_ref[...] = qib
    kw = _dot(hb, w_ref[:, C_KW:C_KW + LANES])
    kw_sw = pltpu.roll(kw, HEAD_DIM, axis=1)
    ki_ref[...] = kw[:, 0:IDX_DIM]
    wi_full = jnp.where(lane < IDX_HEADS, kw_sw, 0.0) * ((IDX_HEADS * IDX_DIM) ** -0.5)
    wi_ref[...] = wi_full[:, 0:IDX_HEADS]

    kib = kw.astype(BF16).astype(F32)
    kit = jnp.where(lo, kib, pltpu.roll(kib, HEAD_DIM, axis=1))
    qif = qib.astype(F32)
    prod_i = jnp.concatenate([qif[:, j * LANES:(j + 1) * LANES] * kit for j in range(4)], axis=1)
    s_new = jnp.maximum(head_sums(prod_i), 0.0) * wi_full
    snew_ref[...] = jnp.broadcast_to(jnp.sum(s_new, axis=1, keepdims=True), (nb, LANES))

    sga_ref[...] = _silu(_dot(hb, w_ref[:, C_GA:C_GA + ATTN_W])).astype(BF16)

    u = _dot(hb, w_ref[:, C_U:C_U + POOL_W])
    gp = _dot(hb, w_ref[:, C_GP:C_GP + POOL_W])
    for j in range(POOL_HIST - 1):
        pool_ref[j] = hist_ref[j + 1]
    pool_ref[POOL_HIST - 1] = u
    ds = []
    for g, w in enumerate(POOL_WINDOWS):
        cs = slice(g * POOL_GC, (g + 1) * POOL_GC)
        s = u[:, cs]
        for j in range(1, w):
            s = s + hist_ref[POOL_HIST - j, :, cs]
        ds.append(s / float(w) - u[:, cs])
    d = jnp.concatenate(ds, axis=1)
    pg_ref[...] = _pool_mix(d, wpool_ref, ps_ref[...], gp).astype(BF16)


def _proj_sample(x, ada_s, norm_w, w_in_b, qnw, knw, wpool_b, pscale, hist_t):
    nb = x.shape[0]
    out_shape = (
        jax.ShapeDtypeStruct((nb, N_HEADS * LANES), BF16),
        jax.ShapeDtypeStruct((nb, LANES), F32),
        jax.ShapeDtypeStruct((nb, LANES), F32),
        jax.ShapeDtypeStruct((nb, IDX_DIM), F32),
        jax.ShapeDtypeStruct((nb, ATTN_W), BF16),
        jax.ShapeDtypeStruct((nb, IDX_HEADS), F32),
        jax.ShapeDtypeStruct((nb, LANES), F32),
        jax.ShapeDtypeStruct((nb, LANES), F32),
        jax.ShapeDtypeStruct((nb, ATTN_W), BF16),
        jax.ShapeDtypeStruct((nb, POOL_W), BF16),
        jax.ShapeDtypeStruct((POOL_HIST, nb, POOL_W), F32),
    )
    return pl.pallas_call(
        _proj_sample_kernel,
        out_shape=out_shape,
        compiler_params=pltpu.CompilerParams(vmem_limit_bytes=VMEM_LIMIT),
        name="proj_sample",
    )(x, ada_s, norm_w, w_in_b, qnw, knw, wpool_b, pscale, hist_t)


def _page_copies(pt_ref, pages_hbm, buf_ref, sem_ref, step, slot, group, n_pages):
    copies = []
    for j in range(group):
        for p in range(n_pages):
            page = pt_ref[step * group + j, p]
            copies.append(pltpu.make_async_copy(
                pages_hbm.at[page], buf_ref.at[slot, j, :, pl.ds(p * PAGE, PAGE)], sem_ref.at[slot]))
    return copies


def _score_sample_kernel(pt_ref, qi_ref, wi_ref, kidx_hbm, o_ref, buf_ref, sem_ref, *, group, n_pages):
    step = pl.program_id(0)
    slot = lax.rem(step, 2)

    def copies(st, sl):
        return _page_copies(pt_ref, kidx_hbm, buf_ref, sem_ref, st, sl, group, n_pages)

    @pl.when(step == 0)
    def _():
        for cp in copies(step, slot):
            cp.start()

    @pl.when(step + 1 < pl.num_programs(0))
    def _():
        for cp in copies(step + 1, 1 - slot):
            cp.start()

    for cp in copies(step, slot):
        cp.wait()

    for j in range(group):
        ki_t = buf_ref[slot, j].astype(BF16)
        s = _dot(qi_ref[j], ki_t)
        o_ref[j:j + 1, :] = jnp.sum(jnp.maximum(s, 0.0) * wi_ref[j], axis=0, keepdims=True)


def _score_sample(page_table, qi3, wi3, kidx_t):
    nb, n_pages = page_table.shape
    n_keys = n_pages * PAGE
    g = SCORE_GROUP
    return pl.pallas_call(
        functools.partial(_score_sample_kernel, group=g, n_pages=n_pages),
        out_shape=jax.ShapeDtypeStruct((nb, n_keys), F32),
        grid_spec=pltpu.PrefetchScalarGridSpec(
            num_scalar_prefetch=1,
            grid=(nb // g,),
            in_specs=[
                pl.BlockSpec((g, IDX_HEADS, IDX_DIM), lambda s, pt: (s, 0, 0)),
                pl.BlockSpec((g, IDX_HEADS, 1), lambda s, pt: (s, 0, 0)),
                pl.BlockSpec(memory_space=pl.ANY),
            ],
            out_specs=pl.BlockSpec((g, n_keys), lambda s, pt: (s, 0)),
            scratch_shapes=[
                pltpu.VMEM((2, g, IDX_DIM, n_keys), F32),
                pltpu.SemaphoreType.DMA((2,)),
            ],
        ),
        compiler_params=pltpu.CompilerParams(
            dimension_semantics=("arbitrary",), vmem_limit_bytes=VMEM_LIMIT),
        name="score_sample",
    )(page_table, qi3, wi3, kidx_t)


def _select_sample_kernel(sc_ref, snew_ref, mask_ref, mnew_ref):
    nb, n_keys = sc_ref.shape
    nch = n_keys // KEY_CHUNK
    snew = snew_ref[...]
    ones_mat = jnp.ones((LANES, LANES), BF16)

    def count_pass(pred):
        acc = jnp.zeros((nb, LANES), F32)
        for c in range(n_keys // LANES):
            acc = acc + pred(sc_ref[:, c * LANES:(c + 1) * LANES]).astype(F32)
        return _dot(acc.astype(BF16), ones_mat) + pred(snew).astype(F32)

    thr = _kth_largest(lambda t: count_pass(lambda x: x >= t), (nb, LANES))
    need = float(TOPK) - count_pass(lambda x: x > thr)
    thr2 = jnp.concatenate([thr, thr], axis=1)
    need2 = jnp.concatenate([need, need], axis=1)
    tri = jnp.where(lax.broadcasted_iota(I32, (KEY_CHUNK, KEY_CHUNK), 0)
                    <= lax.broadcasted_iota(I32, (KEY_CHUNK, KEY_CHUNK), 1), 1.0, 0.0).astype(BF16)
    ones_cl = jnp.ones((KEY_CHUNK, LANES), BF16)

    eq_before = jnp.zeros((nb, LANES), F32)
    for c in range(nch):
        sc = sc_ref[:, c * KEY_CHUNK:(c + 1) * KEY_CHUNK]
        eq = sc == thr2
        eqb = jnp.where(eq, 1.0, 0.0).astype(BF16)
        rank = _dot(eqb, tri) + jnp.concatenate([eq_before, eq_before], axis=1)
        sel = (sc > thr2) | (eq & (rank <= need2))
        mask_ref[:, c * KEY_CHUNK:(c + 1) * KEY_CHUNK] = jnp.where(sel, 0.0, -jnp.inf)
        eq_before = eq_before + _dot(eqb, ones_cl)
    sel_new = (snew > thr) | ((snew == thr) & (eq_before + 1.0 <= need))
    mnew_ref[...] = jnp.where(sel_new, 0.0, -jnp.inf)


def _select_sample(scores, snew):
    nb, n_keys = scores.shape
    return pl.pallas_call(
        _select_sample_kernel,
        out_shape=(jax.ShapeDtypeStruct((nb, n_keys), F32), jax.ShapeDtypeStruct((nb, LANES), F32)),
        name="select_sample",
    )(scores, snew)


def _attn_sample_kernel(pt_ref, q_ref, mask_ref, lnew_ref, mnew_ref, vnew_ref, k_hbm, v_hbm,
                        o_ref, kbuf_ref, vbuf_ref, ksem_ref, vsem_ref, *, group, n_pages):
    step = pl.program_id(0)
    slot = lax.rem(step, 2)

    def copies(st, sl):
        return (_page_copies(pt_ref, k_hbm, kbuf_ref, ksem_ref, st, sl, group, n_pages)
                + _page_copies(pt_ref, v_hbm, vbuf_ref, vsem_ref, st, sl, group, n_pages))

    @pl.when(step == 0)
    def _():
        for cp in copies(step, slot):
            cp.start()

    @pl.when(step + 1 < pl.num_programs(0))
    def _():
        for cp in copies(step + 1, 1 - slot):
            cp.start()

    for cp in copies(step, slot):
        cp.wait()

    for j in range(group):
        k_t = kbuf_ref[slot, j].astype(BF16)
        lg = _dot(q_ref[j], k_t) + mask_ref[j]
        lg_n = lnew_ref[j] + mnew_ref[j]
        m = jnp.maximum(jnp.max(lg, axis=1, keepdims=True), lg_n)
        p = jnp.exp(lg - m)
        p_n = jnp.exp(lg_n - m)
        l = jnp.sum(p, axis=1, keepdims=True) + p_n
        acc = _nt_dot(p.astype(BF16), vbuf_ref[slot, j].astype(BF16)) + p_n * vnew_ref[j]
        o_ref[j] = acc / l


def _attn_sample(page_table, qpad3, mask3, lnew3, mnew3, vnew3, k_t, v_t):
    nb, n_pages = page_table.shape
    n_keys = n_pages * PAGE
    g = ATTN_GROUP
    per_s = lambda s, pt: (s, 0, 0)
    return pl.pallas_call(
        functools.partial(_attn_sample_kernel, group=g, n_pages=n_pages),
        out_shape=jax.ShapeDtypeStruct((nb, N_HEADS, LANES), F32),
        grid_spec=pltpu.PrefetchScalarGridSpec(
            num_scalar_prefetch=1,
            grid=(nb // g,),
            in_specs=[
                pl.BlockSpec((g, N_HEADS, LANES), per_s),
                pl.BlockSpec((g, 1, n_keys), per_s),
                pl.BlockSpec((g, N_HEADS, 1), per_s),
                pl.BlockSpec((g, N_HEADS, 1), per_s),
                pl.BlockSpec((g, 1, LANES), per_s),
                pl.BlockSpec(memory_space=pl.ANY),
                pl.BlockSpec(memory_space=pl.ANY),
            ],
            out_specs=pl.BlockSpec((g, N_HEADS, LANES), per_s),
            scratch_shapes=[
                pltpu.VMEM((2, g, LANES, n_keys), F32),
                pltpu.VMEM((2, g, LANES, n_keys), F32),
                pltpu.SemaphoreType.DMA((2,)),
                pltpu.SemaphoreType.DMA((2,)),
            ],
        ),
        compiler_params=pltpu.CompilerParams(
            dimension_semantics=("arbitrary",), vmem_limit_bytes=VMEM_LIMIT),
        name="attn_sample",
    )(page_table, qpad3, mask3, lnew3, mnew3, vnew3, k_t, v_t)


def _out_proj_sample_kernel(x_ref, a_ref, sga_ref, p_ref, ada_ref, w_ref, o_ref):
    lo = _lane_iota((x_ref.shape[0], LANES)) < HEAD_DIM
    pairs = []
    for hp in range(N_HEADS // 2):
        even = a_ref[:, (2 * hp) * LANES:(2 * hp + 1) * LANES]
        odd = a_ref[:, (2 * hp + 1) * LANES:(2 * hp + 2) * LANES]
        if (2 * hp) // (N_HEADS // KV_HEADS) == 0:
            pairs.append(jnp.where(lo, even, pltpu.roll(odd, HEAD_DIM, axis=1)))
        else:
            pairs.append(jnp.where(lo, pltpu.roll(even, HEAD_DIM, axis=1), odd))
    a = jnp.concatenate(pairs, axis=1)
    ag = (a * sga_ref[...].astype(F32)).astype(BF16)
    y = _dot(ag, w_ref[0:ATTN_W, :]) + _dot(p_ref[...], w_ref[ATTN_W:ATTN_W + POOL_W, :])
    o_ref[...] = x_ref[...] + ada_ref[:, 2 * D_MODEL:3 * D_MODEL] * y


def _out_proj_sample(x, a, sga, pg, ada_s, w_out_b):
    return pl.pallas_call(
        _out_proj_sample_kernel,
        out_shape=jax.ShapeDtypeStruct(x.shape, F32),
        compiler_params=pltpu.CompilerParams(vmem_limit_bytes=VMEM_LIMIT),
        name="out_proj_sample",
    )(x, a, sga, pg, ada_s, w_out_b)


def _permute_w_in(w_in):
    q, k, v, qi, ki, wi, ga, u, gp = jnp.split(
        w_in, np.cumsum([512, 128, 128, 512, 64, 8, 512, 512]).tolist(), axis=1)
    pad = jnp.zeros((w_in.shape[0], LANES - IDX_DIM - IDX_HEADS), w_in.dtype)
    return jnp.concatenate([q, k, v, qi, ki, wi, pad, ga, u, gp], axis=1).astype(BF16)


def kernel(x_prompt, x_sample, cache_k, cache_v, cache_kidx, state_pool, page_table, c_prompt, c_sample,
           norm_w, w_ada, b_ada, w_in, q_norm_w, k_norm_w, w_pool, pool_scale, w_out):
    bp, s, _ = x_prompt.shape
    bs = x_sample.shape[0]
    assert w_in.shape[0] == 1 and x_sample.shape[1] == 1, "single layer, single decode token"
    n_phys = cache_k.shape[1]

    w_in_b = _permute_w_in(w_in[0])
    w_out_b = w_out[0].astype(BF16)
    zero_blk = jnp.zeros((POOL_GC, POOL_GC), w_pool.dtype)
    wpool_b = jnp.stack([jnp.block([[w_pool[0, 2 * p], zero_blk], [zero_blk, w_pool[0, 2 * p + 1]]])
                         for p in range(len(POOL_WINDOWS) // 2)]).astype(BF16)
    qnw = jnp.tile(q_norm_w[0], N_HEADS)[None, :] * (HEAD_DIM ** -0.5)
    knw = jnp.tile(k_norm_w[0], KV_HEADS)[None, :]
    nw = norm_w[0][None, :]
    pscale = pool_scale[0][None, :]

    ada_p, ada_s = _ada(c_prompt, c_sample, w_ada, b_ada)
    ada_p = ada_p.reshape(bp, 1, 3 * D_MODEL)

    (qs, k_t_p, kk, v_t_p, vtx, qi, ki_t_p, kik, wit, sga, pg, ulast) = _proj_prompt(
        x_prompt, ada_p, nw, w_in_b, qnw, knw, wpool_b, pscale)
    ag = _attn_prompt(qi, wit, qs, sga, kik, kk, vtx, bp, s)
    y_prompt = _out_proj_prompt(x_prompt, ag, pg, ada_p, w_out_b)

    hist_t = jnp.transpose(state_pool[0], (1, 0, 2))
    (qpad, k_s, v_s, ki_s, qi_s, wi_s, snew, lnew, sga_s, pg_s, pool_s) = _proj_sample(
        x_sample[:, 0, :], ada_s, nw, w_in_b, qnw, knw, wpool_b, pscale, hist_t)
    n_pages = page_table.shape[1]
    kidx_t = jnp.transpose(cache_kidx[0], (0, 2, 1))
    k_t = jnp.transpose(cache_k[0], (0, 2, 3, 1)).reshape(n_phys, LANES, PAGE)
    v_t = jnp.transpose(cache_v[0], (0, 2, 3, 1)).reshape(n_phys, LANES, PAGE)
    scores = _score_sample(page_table, qi_s.reshape(bs, IDX_HEADS, IDX_DIM), wi_s.reshape(bs, IDX_HEADS, 1),
                           kidx_t)
    mask, mnew = _select_sample(scores, snew)
    o_s = _attn_sample(
        page_table, qpad.reshape(bs, N_HEADS, LANES),
        mask.reshape(bs, 1, n_pages * PAGE),
        lnew[:, :N_HEADS].reshape(bs, N_HEADS, 1),
        jnp.broadcast_to(mnew[:, :1], (bs, N_HEADS)).reshape(bs, N_HEADS, 1),
        v_s.reshape(bs, 1, LANES), k_t, v_t)
    y_sample = _out_proj_sample(x_sample[:, 0, :], o_s.reshape(bs, N_HEADS * LANES), sga_s, pg_s, ada_s, w_out_b)

    to_heads = lambda a: jnp.transpose(a.reshape(bp, KV_HEADS, HEAD_DIM, s), (0, 3, 1, 2))[None]
    return (
        y_prompt,
        y_sample[:, None, :],
        to_heads(k_t_p),
        to_heads(v_t_p),
        jnp.transpose(ki_t_p, (0, 2, 1))[None],
        ulast[:, 1:, :][None],
        k_s.reshape(1, bs, 1, KV_HEADS, HEAD_DIM),
        v_s.reshape(1, bs, 1, KV_HEADS, HEAD_DIM),
        ki_s.reshape(1, bs, 1, IDX_DIM),
        jnp.transpose(pool_s, (1, 0, 2))[None],
    )
```

```python
import functools

import jax
import jax.numpy as jnp
import numpy as np
from jax import lax
from jax.experimental import pallas as pl
from jax.experimental.pallas import tpu as pltpu

F32 = jnp.float32
BF16 = jnp.bfloat16
I32 = jnp.int32

D_MODEL = 1024
ATTN_W = 512
POOL_W = 512
HEAD_DIM = 64
N_HEADS = 8
KV_HEADS = 2
IDX_HEADS = 8
IDX_DIM = 64
TOPK = 256
POOL_WINDOWS = (2, 4, 8, 16)
POOL_GC = 128
POOL_HIST = 15
EPS = 1e-6
PAGE = 128

LANES = 128
SUBLANES = 8
BF16_ROWS = 16
KEY_CHUNK = 256
TQ = 128
TM = 512
VT_ROWS = HEAD_DIM + BF16_ROWS
SCORE_GROUP = 8
ATTN_GROUP = 4
INT_MIN = np.int32(-2 ** 31)
VMEM_LIMIT = 56 * 1024 * 1024

C_Q, C_K, C_V, C_QI, C_KW, C_GA, C_U, C_GP, N_PROJ = 0, 512, 640, 768, 1280, 1408, 1920, 2432, 2944


def _nt_dot(a, b):
    return lax.dot_general(a, b, (((1,), (1,)), ((), ())), preferred_element_type=F32)


def _dot(a, b):
    return jnp.dot(a, b, preferred_element_type=F32)


def _silu(z):
    return z / (1.0 + jnp.exp(-z))


def _lane_iota(shape):
    return lax.broadcasted_iota(I32, shape, len(shape) - 1)


def _seg_ones(n, seg):
    r = lax.broadcasted_iota(I32, (n, n), 0) // seg
    c = lax.broadcasted_iota(I32, (n, n), 1) // seg
    return jnp.where(r == c, 1.0, 0.0).astype(BF16)


def _head_rms(z, seg_mat, w):
    n = seg_mat.shape[0]
    sq = (z * z).astype(BF16)
    ss = jnp.concatenate([_dot(sq[:, j:j + n], seg_mat) for j in range(0, z.shape[1], n)], axis=1)
    return z * lax.rsqrt(ss * (1.0 / HEAD_DIM) + EPS) * w


def _float_of_rank(u):
    key = u ^ INT_MIN
    bits = key ^ ((key >> 31) & np.int32(0x7FFFFFFF))
    return lax.bitcast_convert_type(bits, F32)


def _kth_largest(count_ge, shape):
    def bit_body(it, ans):
        cand = ans | jnp.left_shift(jnp.int32(1), 31 - it)
        return jnp.where(count_ge(_float_of_rank(cand)) >= float(TOPK), cand, ans)

    ans = lax.fori_loop(0, 32, bit_body, jnp.zeros(shape, I32))
    return jnp.where(ans == 0, -jnp.inf, _float_of_rank(ans))


def _kth_largest_two_step(count_ge_coarse, count_ge, shape):
    def rank_of16(u16):
        return (u16 << 16) | jnp.where(u16 < 0x8000, 0xFFFF, 0)

    def coarse(it, a16):
        cand = a16 | jnp.left_shift(jnp.int32(1), 15 - it)
        return jnp.where(count_ge_coarse(_float_of_rank(rank_of16(cand))) >= float(TOPK), cand, a16)

    a16 = lax.fori_loop(0, 16, coarse, jnp.zeros(shape, I32))

    def fine(it, acc):
        cand = acc + jnp.left_shift(jnp.int32(1), 16 - it)
        return jnp.where(count_ge(_float_of_rank(cand)) >= float(TOPK), cand, acc)

    acc = lax.fori_loop(0, 17, fine, rank_of16(a16) - (1 << 16))
    return jnp.where(a16 == 0, -jnp.inf, _float_of_rank(acc))


def _fold_rows(x, op):
    parts = [x[r:r + SUBLANES] for r in range(0, x.shape[0], SUBLANES)]
    while len(parts) > 1:
        parts = [op(parts[i], parts[i + 1]) for i in range(0, len(parts), 2)]
    return parts[0]


def _ada_kernel(cp_ref, cs_ref, w_ref, b_ref, op_ref, os_ref):
    w = w_ref[0].astype(BF16)
    op_ref[...] = _dot(_silu(cp_ref[...]).astype(BF16), w) + b_ref[...]
    os_ref[...] = _dot(_silu(cs_ref[...]).astype(BF16), w) + b_ref[...]


def _ada(c_prompt, c_sample, w_ada, b_ada):
    bp, bs = c_prompt.shape[0], c_sample.shape[0]
    return pl.pallas_call(
        _ada_kernel,
        out_shape=(jax.ShapeDtypeStruct((bp, 3 * D_MODEL), F32), jax.ShapeDtypeStruct((bs, 3 * D_MODEL), F32)),
        grid=(3,),
        in_specs=[
            pl.BlockSpec((bp, D_MODEL), lambda j: (0, 0)),
            pl.BlockSpec((bs, D_MODEL), lambda j: (0, 0)),
            pl.BlockSpec((1, D_MODEL, D_MODEL), lambda j: (0, 0, j)),
            pl.BlockSpec((1, D_MODEL), lambda j: (0, j)),
        ],
        out_specs=(pl.BlockSpec((bp, D_MODEL), lambda j: (0, j)), pl.BlockSpec((bs, D_MODEL), lambda j: (0, j))),
        compiler_params=pltpu.CompilerParams(dimension_semantics=("arbitrary",)),
        name="ada_ln",
    )(c_prompt, c_sample, w_ada, b_ada)


def _modulated_norm(x, norm_w, scale, shift):
    ms = jnp.mean(x * x, axis=-1, keepdims=True)
    return (x * lax.rsqrt(ms + EPS)) * norm_w * (1.0 + scale) + shift


def _pool_mix(d, wpool_ref, pscale, gp):
    db = d.astype(BF16)
    wide = 2 * POOL_GC
    y = jnp.concatenate([_dot(db[:, p * wide:(p + 1) * wide], wpool_ref[p])
                         for p in range(len(POOL_WINDOWS) // 2)], axis=1)
    return y * pscale * _silu(gp)


def _proj_prompt_kernel(x_ref, ada_ref, nw_ref, w_ref, qnw_ref, knw_ref, wpool_ref, ps_ref,
                        qs_ref, kt_ref, kk_ref, vt_ref, vtx_ref, qi_ref, kit_ref, kik_ref,
                        wit_ref, sga_ref, pg_ref, ulast_ref,
                        ext_ref):
    t = pl.program_id(1)
    ada = ada_ref[0]
    shift = ada[:, 0:D_MODEL]
    scale = ada[:, D_MODEL:2 * D_MODEL]
    sub_rows = KEY_CHUNK
    lo = _lane_iota((sub_rows, LANES)) < HEAD_DIM
    seg256 = _seg_ones(2 * LANES, HEAD_DIM)
    seg128 = _seg_ones(LANES, HEAD_DIM)
    ones = jnp.ones((BF16_ROWS, KEY_CHUNK), BF16)

    @pl.when(t == 0)
    def _():
        ext_ref[0:16, :] = jnp.zeros((16, POOL_W), F32)

    for sub in range(TM // sub_rows):
        rows = slice(sub * sub_rows, (sub + 1) * sub_rows)
        hb = _modulated_norm(x_ref[0, rows, :], nw_ref[...], scale, shift).astype(BF16)

        q = _dot(hb, w_ref[:, C_Q:C_Q + ATTN_W])
        qs_ref[rows, :] = _head_rms(q, seg256, qnw_ref[...]).astype(BF16)

        kv = _dot(hb, w_ref[:, C_K:C_K + 2 * LANES])
        k = _head_rms(kv[:, 0:LANES], seg128, knw_ref[...])
        kt_ref[0, :, rows] = k.T
        k_sw = pltpu.roll(k, HEAD_DIM, axis=1)
        kk_ref[0, rows, :] = jnp.where(lo, k, k_sw).astype(BF16)
        kk_ref[1, rows, :] = jnp.where(lo, k_sw, k).astype(BF16)

        v_t = kv[:, LANES:2 * LANES].T
        vt_ref[0, :, rows] = v_t
        for g in range(KV_HEADS):
            vtx_ref[0, g, sub, 0:HEAD_DIM, :] = v_t[g * HEAD_DIM:(g + 1) * HEAD_DIM].astype(BF16)
            vtx_ref[0, g, sub, HEAD_DIM:VT_ROWS, :] = ones

        qi_ref[rows, :] = _dot(hb, w_ref[:, C_QI:C_QI + ATTN_W]).astype(BF16)
        kw = _dot(hb, w_ref[:, C_KW:C_KW + LANES])
        kw_t = kw.T
        kit_ref[0, :, rows] = kw_t[0:IDX_DIM]
        wit_ref[:, rows] = kw_t[IDX_DIM:IDX_DIM + IDX_HEADS] * ((IDX_HEADS * IDX_DIM) ** -0.5)
        kik_ref[rows, :] = jnp.where(lo, kw, pltpu.roll(kw, HEAD_DIM, axis=1)).astype(BF16)

        sga_ref[rows, :] = _silu(_dot(hb, w_ref[:, C_GA:C_GA + ATTN_W])).astype(BF16)

        u = _dot(hb, w_ref[:, C_U:C_U + POOL_W])
        gp = _dot(hb, w_ref[:, C_GP:C_GP + POOL_W])
        base = 16 + sub * sub_rows
        ext_ref[base:base + sub_rows, :] = u
        pos = t * TM + sub * sub_rows + lax.broadcasted_iota(I32, (sub_rows, POOL_GC), 0)
        ds = []
        for g, w in enumerate(POOL_WINDOWS):
            cs = slice(g * POOL_GC, (g + 1) * POOL_GC)
            s = u[:, cs]
            for j in range(1, w):
                s = s + ext_ref[base - j:base - j + sub_rows, cs]
            cnt = jnp.minimum(pos + 1, w).astype(F32)
            ds.append(s / cnt - u[:, cs])
        d = jnp.concatenate(ds, axis=1)
        pg_ref[rows, :] = _pool_mix(d, wpool_ref, ps_ref[...], gp).astype(BF16)

    tail = ext_ref[TM:TM + 16, :]
    ulast_ref[0] = tail
    ext_ref[0:16, :] = tail


def _proj_prompt(x, ada_p, norm_w, w_in_b, qnw, knw, wpool_b, pscale):
    b, s, _ = x.shape
    n = b * s
    nt = s // TM
    cpt = TM // KEY_CHUNK
    row = lambda bi, ti: (bi * nt + ti, 0)
    tok = lambda bi, ti: (bi, 0, ti)
    const2 = lambda bi, ti: (0, 0)
    const3 = lambda bi, ti: (0, 0, 0)
    out_shape = (
        jax.ShapeDtypeStruct((n, ATTN_W), BF16),
        jax.ShapeDtypeStruct((b, LANES, s), F32),
        jax.ShapeDtypeStruct((KV_HEADS, n, LANES), BF16),
        jax.ShapeDtypeStruct((b, LANES, s), F32),
        jax.ShapeDtypeStruct((b, KV_HEADS, s // KEY_CHUNK, VT_ROWS, KEY_CHUNK), BF16),
        jax.ShapeDtypeStruct((n, ATTN_W), BF16),
        jax.ShapeDtypeStruct((b, IDX_DIM, s), F32),
        jax.ShapeDtypeStruct((n, LANES), BF16),
        jax.ShapeDtypeStruct((IDX_HEADS, n), F32),
        jax.ShapeDtypeStruct((n, ATTN_W), BF16),
        jax.ShapeDtypeStruct((n, POOL_W), BF16),
        jax.ShapeDtypeStruct((b, 16, POOL_W), F32),
    )
    out_specs = (
        pl.BlockSpec((TM, ATTN_W), row),
        pl.BlockSpec((1, LANES, TM), tok),
        pl.BlockSpec((KV_HEADS, TM, LANES), lambda bi, ti: (0, bi * nt + ti, 0)),
        pl.BlockSpec((1, LANES, TM), tok),
        pl.BlockSpec((1, KV_HEADS, cpt, VT_ROWS, KEY_CHUNK), lambda bi, ti: (bi, 0, ti, 0, 0)),
        pl.BlockSpec((TM, ATTN_W), row),
        pl.BlockSpec((1, IDX_DIM, TM), tok),
        pl.BlockSpec((TM, LANES), row),
        pl.BlockSpec((IDX_HEADS, TM), lambda bi, ti: (0, bi * nt + ti)),
        pl.BlockSpec((TM, ATTN_W), row),
        pl.BlockSpec((TM, POOL_W), row),
        pl.BlockSpec((1, 16, POOL_W), lambda bi, ti: (bi, 0, 0)),
    )
    in_specs = [
        pl.BlockSpec((1, TM, D_MODEL), lambda bi, ti: (bi, ti, 0)),
        pl.BlockSpec((1, 1, 3 * D_MODEL), lambda bi, ti: (bi, 0, 0)),
        pl.BlockSpec((1, D_MODEL), const2),
        pl.BlockSpec((D_MODEL, N_PROJ), const2),
        pl.BlockSpec((1, ATTN_W), const2),
        pl.BlockSpec((1, LANES), const2),
        pl.BlockSpec((2, 2 * POOL_GC, 2 * POOL_GC), const3),
        pl.BlockSpec((1, POOL_W), const2),
    ]
    return pl.pallas_call(
        _proj_prompt_kernel,
        out_shape=out_shape,
        grid=(b, nt),
        in_specs=in_specs,
        out_specs=out_specs,
        scratch_shapes=[pltpu.VMEM((16 + TM, POOL_W), F32)],
        compiler_params=pltpu.CompilerParams(
            dimension_semantics=("arbitrary", "arbitrary"), vmem_limit_bytes=VMEM_LIMIT),
        name="proj_prompt",
    )(x, ada_p, norm_w, w_in_b, qnw, knw, wpool_b, pscale)


def _attn_tile(nch, row0, wit_ref, kik_ref, kk_ref, vtx_ref, sc_ref, scb_ref, lg_ref, wq_ref, acc_ref):
    n_pairs = N_HEADS // 2
    group_of = lambda hp: (2 * hp) // (N_HEADS // KV_HEADS)
    n_keys = nch * KEY_CHUNK
    rows_of = lambda c: KEY_CHUNK

    for c in range(nch):
        rows = rows_of(c)
        keys = slice(c * KEY_CHUNK, c * KEY_CHUNK + rows)
        kk = kik_ref[keys, :]
        acc = jnp.zeros((rows, TQ), F32)
        for hp in range(n_pairs):
            s2 = _nt_dot(kk, wq_ref[0, hp])
            acc = acc + jnp.maximum(s2[:, 0:TQ], 0.0) * wit_ref[2 * hp:2 * hp + 1, :]
            acc = acc + jnp.maximum(s2[:, TQ:2 * TQ], 0.0) * wit_ref[2 * hp + 1:2 * hp + 2, :]
        if c == nch - 1:
            kpos = c * KEY_CHUNK + lax.broadcasted_iota(I32, (rows, TQ), 0)
            qpos = row0 + lax.broadcasted_iota(I32, (rows, TQ), 1)
            acc = jnp.where(kpos <= qpos, acc, -jnp.inf)
        sc_ref[c, 0:rows, :] = acc
        scb_ref[c, 0:rows, :] = acc.astype(BF16)
        for hp in range(n_pairs):
            lg_ref[hp, c, 0:rows, :] = _nt_dot(kk_ref[group_of(hp), keys, :], wq_ref[1, hp])

    def tree_sum(parts):
        while len(parts) > 1:
            parts = [sum(parts[j:j + 2]) for j in range(0, len(parts), 2)]
        return parts[0]

    def count_pass(pred):
        part = tree_sum([_fold_rows(pred(sc_ref[c, 0:rows_of(c), :]).astype(F32), jnp.add) for c in range(nch)])
        return jnp.sum(part, axis=0, keepdims=True)

    one_b, zero_b = jnp.ones((BF16_ROWS, TQ), BF16), jnp.zeros((BF16_ROWS, TQ), BF16)

    def count_ge_rounded(t):
        tb = jnp.broadcast_to(t, (BF16_ROWS, TQ)).astype(BF16)
        parts = []
        for c in range(nch):
            x = scb_ref[c, 0:rows_of(c), :]
            parts.append(tree_sum([jnp.where(x[r:r + BF16_ROWS] >= tb, one_b, zero_b)
                                   for r in range(0, x.shape[0], BF16_ROWS)]))
        return jnp.sum(tree_sum(parts).astype(F32), axis=0, keepdims=True)

    if n_keys <= TOPK:
        thr = jnp.full((1, TQ), -jnp.inf, F32)
        need = jnp.zeros((1, TQ), F32)
    else:
        thr = _kth_largest_two_step(count_ge_rounded, lambda t: count_pass(lambda x: x >= t), (1, TQ))
        need = float(TOPK) - count_pass(lambda x: x > thr)

    tri = jnp.where(lax.broadcasted_iota(I32, (KEY_CHUNK, KEY_CHUNK), 1)
                    <= lax.broadcasted_iota(I32, (KEY_CHUNK, KEY_CHUNK), 0), 1.0, 0.0).astype(BF16)

    m_part = [jnp.full((SUBLANES, TQ), -jnp.inf, F32) for _ in range(N_HEADS)]
    eq_before = jnp.zeros((1, TQ), F32)
    for c in range(nch):
        rows = rows_of(c)
        sc = sc_ref[c, 0:rows, :]
        eq = sc == thr
        rank = _dot(tri[0:rows, 0:rows], jnp.where(eq, 1.0, 0.0).astype(BF16)) + eq_before
        sel = (sc > thr) | (eq & (rank <= need))
        if c == nch - 1:
            sel = sel & (sc > -jnp.inf)
        eq_before = rank[rows - 1:rows, :]
        for hp in range(n_pairs):
            for par in range(2):
                cols = slice(par * TQ, (par + 1) * TQ)
                lg = jnp.where(sel, lg_ref[hp, c, 0:rows, cols], -jnp.inf)
                lg_ref[hp, c, 0:rows, cols] = lg
                m_part[2 * hp + par] = jnp.maximum(m_part[2 * hp + par], _fold_rows(lg, jnp.maximum))
    m_rows = [jnp.max(m, axis=0, keepdims=True) for m in m_part]

    for hp in range(n_pairs):
        m2 = jnp.concatenate([m_rows[2 * hp], m_rows[2 * hp + 1]], axis=1)
        acc = jnp.zeros((VT_ROWS, 2 * TQ), F32)
        for c in range(nch):
            rows = rows_of(c)
            p2 = jnp.exp(lg_ref[hp, c, 0:rows, :] - m2).astype(BF16)
            acc = acc + _dot(vtx_ref[0, group_of(hp), c, :, 0:rows], p2)
        acc_ref[hp] = acc


def _attn_prompt_kernel(qi_ref, wit_ref, qs_ref, sga_ref, kik_ref, kk_ref, vtx_ref, ag_ref,
                        sc_ref, scb_ref, lg_ref, wq_ref, acc_ref):
    i = pl.program_id(1)
    n_pairs = N_HEADS // 2
    lo = _lane_iota((TQ, LANES)) < HEAD_DIM

    for kind, ref in enumerate((qi_ref, qs_ref)):
        for hp in range(n_pairs):
            pair = ref[:, hp * LANES:(hp + 1) * LANES].astype(F32)
            wq_ref[kind, hp, 0:TQ, :] = jnp.where(lo, pair, 0.0).astype(BF16)
            wq_ref[kind, hp, TQ:2 * TQ, :] = jnp.where(lo, 0.0, pair).astype(BF16)

    nch = (i + 2) // 2
    for n_static in range(1, sc_ref.shape[0] + 1):
        @pl.when(nch == n_static)
        def _(n_static=n_static):
            _attn_tile(n_static, i * TQ, wit_ref, kik_ref, kk_ref, vtx_ref, sc_ref, scb_ref, lg_ref, wq_ref,
                       acc_ref)

    for hp in range(n_pairs):
        a = acc_ref[hp]
        o0 = a[0:HEAD_DIM, 0:TQ] / a[HEAD_DIM:HEAD_DIM + 1, 0:TQ]
        o1 = a[0:HEAD_DIM, TQ:2 * TQ] / a[HEAD_DIM:HEAD_DIM + 1, TQ:2 * TQ]
        pair = jnp.concatenate([o0, o1], axis=0).T
        cols = slice(hp * LANES, (hp + 1) * LANES)
        ag_ref[:, cols] = (pair * sga_ref[:, cols].astype(F32)).astype(BF16)


def _attn_prompt(qi, wit, qs, sga, kik, kk, vtx, b, s):
    n = b * s
    nq = s // TQ
    nkc = s // KEY_CHUNK
    row = lambda bi, qi_: (bi * nq + qi_, 0)
    return pl.pallas_call(
        _attn_prompt_kernel,
        out_shape=jax.ShapeDtypeStruct((n, ATTN_W), BF16),
        grid=(b, nq),
        in_specs=[
            pl.BlockSpec((TQ, ATTN_W), row),
            pl.BlockSpec((IDX_HEADS, TQ), lambda bi, qi_: (0, bi * nq + qi_)),
            pl.BlockSpec((TQ, ATTN_W), row),
            pl.BlockSpec((TQ, ATTN_W), row),
            pl.BlockSpec((s, LANES), lambda bi, qi_: (bi, 0)),
            pl.BlockSpec((KV_HEADS, s, LANES), lambda bi, qi_: (0, bi, 0)),
            pl.BlockSpec((1, KV_HEADS, nkc, VT_ROWS, KEY_CHUNK), lambda bi, qi_: (bi, 0, 0, 0, 0)),
        ],
        out_specs=pl.BlockSpec((TQ, ATTN_W), row),
        scratch_shapes=[
            pltpu.VMEM((nkc, KEY_CHUNK, TQ), F32),
            pltpu.VMEM((nkc, KEY_CHUNK, TQ), BF16),
            pltpu.VMEM((N_HEADS // 2, nkc, KEY_CHUNK, 2 * TQ), F32),
            pltpu.VMEM((2, N_HEADS // 2, 2 * TQ, LANES), BF16),
            pltpu.VMEM((N_HEADS // 2, VT_ROWS, 2 * TQ), F32),
        ],
        compiler_params=pltpu.CompilerParams(
            dimension_semantics=("arbitrary", "arbitrary"), vmem_limit_bytes=VMEM_LIMIT),
        name="attn_prompt",
    )(qi, wit, qs, sga, kik, kk, vtx)


def _out_proj_kernel(x_ref, a_ref, p_ref, gate_ref, w_ref, o_ref):
    y = _dot(a_ref[...], w_ref[0:ATTN_W, :]) + _dot(p_ref[...], w_ref[ATTN_W:ATTN_W + POOL_W, :])
    o_ref[0] = x_ref[0] + gate_ref[0] * y


def _out_proj_prompt(x, ag, pg, ada_p, w_out_b):
    b, s, _ = x.shape
    nt = s // TM
    row = lambda bi, ti: (bi * nt + ti, 0)
    return pl.pallas_call(
        _out_proj_kernel,
        out_shape=jax.ShapeDtypeStruct(x.shape, F32),
        grid=(b, nt),
        in_specs=[
            pl.BlockSpec((1, TM, D_MODEL), lambda bi, ti: (bi, ti, 0)),
            pl.BlockSpec((TM, ATTN_W), row),
            pl.BlockSpec((TM, POOL_W), row),
            pl.BlockSpec((1, 1, D_MODEL), lambda bi, ti: (bi, 0, 2)),
            pl.BlockSpec((D_MODEL, D_MODEL), lambda bi, ti: (0, 0)),
        ],
        out_specs=pl.BlockSpec((1, TM, D_MODEL), lambda bi, ti: (bi, ti, 0)),
        compiler_params=pltpu.CompilerParams(
            dimension_semantics=("arbitrary", "arbitrary"), vmem_limit_bytes=VMEM_LIMIT),
        name="out_proj_prompt",
    )(x, ag, pg, ada_p, w_out_b)


def _proj_sample_kernel(x_ref, ada_ref, nw_ref, w_ref, qnw_ref, knw_ref, wpool_ref, ps_ref, hist_ref,
                        qpad_ref, k_ref, v_ref, ki_ref, qi_ref, wi_ref, snew_ref, lnew_ref,
                        sga_ref, pg_ref, pool_ref):
    nb = x_ref.shape[0]
    x = x_ref[...]
    shift = ada_ref[:, 0:D_MODEL]
    scale = ada_ref[:, D_MODEL:2 * D_MODEL]
    hb = _modulated_norm(x, nw_ref[...], scale, shift).astype(BF16)

    lane = _lane_iota((nb, LANES))
    lo = lane < HEAD_DIM
    seg256 = _seg_ones(2 * LANES, HEAD_DIM)
    seg128 = _seg_ones(LANES, HEAD_DIM)
    head_sel = jnp.where(lax.broadcasted_iota(I32, (ATTN_W, LANES), 0) // HEAD_DIM
                         == lax.broadcasted_iota(I32, (ATTN_W, LANES), 1), 1.0, 0.0).astype(BF16)

    def head_sums(prod):
        hi = prod.astype(BF16)
        rest = (prod - hi.astype(F32)).astype(BF16)
        return _dot(hi, head_sel) + _dot(rest, head_sel)

    q = _head_rms(_dot(hb, w_ref[:, C_Q:C_Q + ATTN_W]), seg256, qnw_ref[...])
    qb = q.astype(BF16)
    k = _head_rms(_dot(hb, w_ref[:, C_K:C_K + LANES]), seg128, knw_ref[...])
    k_ref[...] = k
    v = _dot(hb, w_ref[:, C_V:C_V + LANES])
    v_ref[...] = v

    for hp in range(N_HEADS // 2):
        pair = q[:, hp * LANES:(hp + 1) * LANES]
        pair_sw = pltpu.roll(pair, HEAD_DIM, axis=1)
        g = (2 * hp) // (N_HEADS // KV_HEADS)
        if g == 0:
            h_even, h_odd = jnp.where(lo, pair, 0.0), jnp.where(lo, pair_sw, 0.0)
        else:
            h_even, h_odd = jnp.where(lo, 0.0, pair_sw), jnp.where(lo, 0.0, pair)
        qpad_ref[:, (2 * hp) * LANES:(2 * hp + 1) * LANES] = h_even.astype(BF16)
        qpad_ref[:, (2 * hp + 1) * LANES:(2 * hp + 2) * LANES] = h_odd.astype(BF16)

    kq = k.astype(BF16).astype(F32)
    kq_sw = pltpu.roll(kq, HEAD_DIM, axis=1)
    k0t = jnp.where(lo, kq, kq_sw)
    k1t = jnp.where(lo, kq_sw, kq)
    qf = qb.astype(F32)
    prod = jnp.concatenate([qf[:, 0:LANES] * k0t, qf[:, LANES:2 * LANES] * k0t,
                            qf[:, 2 * LANES:3 * LANES] * k1t, qf[:, 3 * LANES:4 * LANES] * k1t], axis=1)
    lnew_ref[...] = head_sums(prod)

    qi = _dot(hb, w_ref[:, C_QI:C_QI + ATTN_W])
    qib = qi.astype(BF16)
    qi_ref[...] = qib
    kw = _dot(hb, w_ref[:, C_KW:C_KW + LANES])
    kw_sw = pltpu.roll(kw, HEAD_DIM, axis=1)
    ki_ref[...] = kw[:, 0:IDX_DIM]
    wi_full = jnp.where(lane < IDX_HEADS, kw_sw, 0.0) * ((IDX_HEADS * IDX_DIM) ** -0.5)
    wi_ref[...] = wi_full[:, 0:IDX_HEADS]

    kib = kw.astype(BF16).astype(F32)
    kit = jnp.where(lo, kib, pltpu.roll(kib, HEAD_DIM, axis=1))
    qif = qib.astype(F32)
    prod_i = jnp.concatenate([qif[:, j * LANES:(j + 1) * LANES] * kit for j in range(4)], axis=1)
    s_new = jnp.maximum(head_sums(prod_i), 0.0) * wi_full
    snew_ref[...] = jnp.broadcast_to(jnp.sum(s_new, axis=1, keepdims=True), (nb, LANES))

    sga_ref[...] = _silu(_dot(hb, w_ref[:, C_GA:C_GA + ATTN_W])).astype(BF16)

    u = _dot(hb, w_ref[:, C_U:C_U + POOL_W])
    gp = _dot(hb, w_ref[:, C_GP:C_GP + POOL_W])
    for j in range(POOL_HIST - 1):
        pool_ref[j] = hist_ref[j + 1]
    pool_ref[POOL_HIST - 1] = u
    ds = []
    for g, w in enumerate(POOL_WINDOWS):
        cs = slice(g * POOL_GC, (g + 1) * POOL_GC)
        s = u[:, cs]
        for j in range(1, w):
            s = s + hist_ref[POOL_HIST - j, :, cs]
        ds.append(s / float(w) - u[:, cs])
    d = jnp.concatenate(ds, axis=1)
    pg_ref[...] = _pool_mix(d, wpool_ref, ps_ref[...], gp).astype(BF16)


def _proj_sample(x, ada_s, norm_w, w_in_b, qnw, knw, wpool_b, pscale, hist_t):
    nb = x.shape[0]
    out_shape = (
        jax.ShapeDtypeStruct((nb, N_HEADS * LANES), BF16),
        jax.ShapeDtypeStruct((nb, LANES), F32),
        jax.ShapeDtypeStruct((nb, LANES), F32),
        jax.ShapeDtypeStruct((nb, IDX_DIM), F32),
        jax.ShapeDtypeStruct((nb, ATTN_W), BF16),
        jax.ShapeDtypeStruct((nb, IDX_HEADS), F32),
        jax.ShapeDtypeStruct((nb, LANES), F32),
        jax.ShapeDtypeStruct((nb, LANES), F32),
        jax.ShapeDtypeStruct((nb, ATTN_W), BF16),
        jax.ShapeDtypeStruct((nb, POOL_W), BF16),
        jax.ShapeDtypeStruct((POOL_HIST, nb, POOL_W), F32),
    )
    return pl.pallas_call(
        _proj_sample_kernel,
        out_shape=out_shape,
        compiler_params=pltpu.CompilerParams(vmem_limit_bytes=VMEM_LIMIT),
        name="proj_sample",
    )(x, ada_s, norm_w, w_in_b, qnw, knw, wpool_b, pscale, hist_t)


def _page_copies(pt_ref, pages_hbm, buf_ref, sem_ref, step, slot, group, n_pages):
    copies = []
    for j in range(group):
        for p in range(n_pages):
            page = pt_ref[step * group + j, p]
            copies.append(pltpu.make_async_copy(
                pages_hbm.at[page], buf_ref.at[slot, j, :, pl.ds(p * PAGE, PAGE)], sem_ref.at[slot]))
    return copies


def _score_sample_kernel(pt_ref, qi_ref, wi_ref, kidx_hbm, o_ref, buf_ref, sem_ref, *, group, n_pages):
    step = pl.program_id(0)
    slot = lax.rem(step, 2)

    def copies(st, sl):
        return _page_copies(pt_ref, kidx_hbm, buf_ref, sem_ref, st, sl, group, n_pages)

    @pl.when(step == 0)
    def _():
        for cp in copies(step, slot):
            cp.start()

    @pl.when(step + 1 < pl.num_programs(0))
    def _():
        for cp in copies(step + 1, 1 - slot):
            cp.start()

    for cp in copies(step, slot):
        cp.wait()

    for j in range(group):
        ki_t = buf_ref[slot, j].astype(BF16)
        s = _dot(qi_ref[j], ki_t)
        o_ref[j:j + 1, :] = jnp.sum(jnp.maximum(s, 0.0) * wi_ref[j], axis=0, keepdims=True)


def _score_sample(page_table, qi3, wi3, kidx_t):
    nb, n_pages = page_table.shape
    n_keys = n_pages * PAGE
    g = SCORE_GROUP
    return pl.pallas_call(
        functools.partial(_score_sample_kernel, group=g, n_pages=n_pages),
        out_shape=jax.ShapeDtypeStruct((nb, n_keys), F32),
        grid_spec=pltpu.PrefetchScalarGridSpec(
            num_scalar_prefetch=1,
            grid=(nb // g,),
            in_specs=[
                pl.BlockSpec((g, IDX_HEADS, IDX_DIM), lambda s, pt: (s, 0, 0)),
                pl.BlockSpec((g, IDX_HEADS, 1), lambda s, pt: (s, 0, 0)),
                pl.BlockSpec(memory_space=pl.ANY),
            ],
            out_specs=pl.BlockSpec((g, n_keys), lambda s, pt: (s, 0)),
            scratch_shapes=[
                pltpu.VMEM((2, g, IDX_DIM, n_keys), F32),
                pltpu.SemaphoreType.DMA((2,)),
            ],
        ),
        compiler_params=pltpu.CompilerParams(
            dimension_semantics=("arbitrary",), vmem_limit_bytes=VMEM_LIMIT),
        name="score_sample",
    )(page_table, qi3, wi3, kidx_t)


def _select_sample_kernel(sc_ref, snew_ref, mask_ref, mnew_ref):
    nb, n_keys = sc_ref.shape
    nch = n_keys // KEY_CHUNK
    snew = snew_ref[...]
    ones_mat = jnp.ones((LANES, LANES), BF16)

    def count_pass(pred):
        acc = jnp.zeros((nb, LANES), F32)
        for c in range(n_keys // LANES):
            acc = acc + pred(sc_ref[:, c * LANES:(c + 1) * LANES]).astype(F32)
        return _dot(acc.astype(BF16), ones_mat) + pred(snew).astype(F32)

    thr = _kth_largest(lambda t: count_pass(lambda x: x >= t), (nb, LANES))
    need = float(TOPK) - count_pass(lambda x: x > thr)
    thr2 = jnp.concatenate([thr, thr], axis=1)
    need2 = jnp.concatenate([need, need], axis=1)
    tri = jnp.where(lax.broadcasted_iota(I32, (KEY_CHUNK, KEY_CHUNK), 0)
                    <= lax.broadcasted_iota(I32, (KEY_CHUNK, KEY_CHUNK), 1), 1.0, 0.0).astype(BF16)
    ones_cl = jnp.ones((KEY_CHUNK, LANES), BF16)

    eq_before = jnp.zeros((nb, LANES), F32)
    for c in range(nch):
        sc = sc_ref[:, c * KEY_CHUNK:(c + 1) * KEY_CHUNK]
        eq = sc == thr2
        eqb = jnp.where(eq, 1.0, 0.0).astype(BF16)
        rank = _dot(eqb, tri) + jnp.concatenate([eq_before, eq_before], axis=1)
        sel = (sc > thr2) | (eq & (rank <= need2))
        mask_ref[:, c * KEY_CHUNK:(c + 1) * KEY_CHUNK] = jnp.where(sel, 0.0, -jnp.inf)
        eq_before = eq_before + _dot(eqb, ones_cl)
    sel_new = (snew > thr) | ((snew == thr) & (eq_before + 1.0 <= need))
    mnew_ref[...] = jnp.where(sel_new, 0.0, -jnp.inf)


def _select_sample(scores, snew):
    nb, n_keys = scores.shape
    return pl.pallas_call(
        _select_sample_kernel,
        out_shape=(jax.ShapeDtypeStruct((nb, n_keys), F32), jax.ShapeDtypeStruct((nb, LANES), F32)),
        name="select_sample",
    )(scores, snew)


def _attn_sample_kernel(pt_ref, q_ref, mask_ref, lnew_ref, mnew_ref, vnew_ref, k_hbm, v_hbm,
                        o_ref, kbuf_ref, vbuf_ref, ksem_ref, vsem_ref, *, group, n_pages):
    step = pl.program_id(0)
    slot = lax.rem(step, 2)

    def copies(st, sl):
        return (_page_copies(pt_ref, k_hbm, kbuf_ref, ksem_ref, st, sl, group, n_pages)
                + _page_copies(pt_ref, v_hbm, vbuf_ref, vsem_ref, st, sl, group, n_pages))

    @pl.when(step == 0)
    def _():
        for cp in copies(step, slot):
            cp.start()

    @pl.when(step + 1 < pl.num_programs(0))
    def _():
        for cp in copies(step + 1, 1 - slot):
            cp.start()

    for cp in copies(step, slot):
        cp.wait()

    for j in range(group):
        k_t = kbuf_ref[slot, j].astype(BF16)
        lg = _dot(q_ref[j], k_t) + mask_ref[j]
        lg_n = lnew_ref[j] + mnew_ref[j]
        m = jnp.maximum(jnp.max(lg, axis=1, keepdims=True), lg_n)
        p = jnp.exp(lg - m)
        p_n = jnp.exp(lg_n - m)
        l = jnp.sum(p, axis=1, keepdims=True) + p_n
        acc = _nt_dot(p.astype(BF16), vbuf_ref[slot, j].astype(BF16)) + p_n * vnew_ref[j]
        o_ref[j] = acc / l


def _attn_sample(page_table, qpad3, mask3, lnew3, mnew3, vnew3, k_t, v_t):
    nb, n_pages = page_table.shape
    n_keys = n_pages * PAGE
    g = ATTN_GROUP
    per_s = lambda s, pt: (s, 0, 0)
    return pl.pallas_call(
        functools.partial(_attn_sample_kernel, group=g, n_pages=n_pages),
        out_shape=jax.ShapeDtypeStruct((nb, N_HEADS, LANES), F32),
        grid_spec=pltpu.PrefetchScalarGridSpec(
            num_scalar_prefetch=1,
            grid=(nb // g,),
            in_specs=[
                pl.BlockSpec((g, N_HEADS, LANES), per_s),
                pl.BlockSpec((g, 1, n_keys), per_s),
                pl.BlockSpec((g, N_HEADS, 1), per_s),
                pl.BlockSpec((g, N_HEADS, 1), per_s),
                pl.BlockSpec((g, 1, LANES), per_s),
                pl.BlockSpec(memory_space=pl.ANY),
                pl.BlockSpec(memory_space=pl.ANY),
            ],
            out_specs=pl.BlockSpec((g, N_HEADS, LANES), per_s),
            scratch_shapes=[
                pltpu.VMEM((2, g, LANES, n_keys), F32),
                pltpu.VMEM((2, g, LANES, n_keys), F32),
                pltpu.SemaphoreType.DMA((2,)),
                pltpu.SemaphoreType.DMA((2,)),
            ],
        ),
        compiler_params=pltpu.CompilerParams(
            dimension_semantics=("arbitrary",), vmem_limit_bytes=VMEM_LIMIT),
        name="attn_sample",
    )(page_table, qpad3, mask3, lnew3, mnew3, vnew3, k_t, v_t)


def _out_proj_sample_kernel(x_ref, a_ref, sga_ref, p_ref, ada_ref, w_ref, o_ref):
    lo = _lane_iota((x_ref.shape[0], LANES)) < HEAD_DIM
    pairs = []
    for hp in range(N_HEADS // 2):
        even = a_ref[:, (2 * hp) * LANES:(2 * hp + 1) * LANES]
        odd = a_ref[:, (2 * hp + 1) * LANES:(2 * hp + 2) * LANES]
        if (2 * hp) // (N_HEADS // KV_HEADS) == 0:
            pairs.append(jnp.where(lo, even, pltpu.roll(odd, HEAD_DIM, axis=1)))
        else:
            pairs.append(jnp.where(lo, pltpu.roll(even, HEAD_DIM, axis=1), odd))
    a = jnp.concatenate(pairs, axis=1)
    ag = (a * sga_ref[...].astype(F32)).astype(BF16)
    y = _dot(ag, w_ref[0:ATTN_W, :]) + _dot(p_ref[...], w_ref[ATTN_W:ATTN_W + POOL_W, :])
    o_ref[...] = x_ref[...] + ada_ref[:, 2 * D_MODEL:3 * D_MODEL] * y


def _out_proj_sample(x, a, sga, pg, ada_s, w_out_b):
    return pl.pallas_call(
        _out_proj_sample_kernel,
        out_shape=jax.ShapeDtypeStruct(x.shape, F32),
        compiler_params=pltpu.CompilerParams(vmem_limit_bytes=VMEM_LIMIT),
        name="out_proj_sample",
    )(x, a, sga, pg, ada_s, w_out_b)


def _permute_w_in(w_in):
    q, k, v, qi, ki, wi, ga, u, gp = jnp.split(
        w_in, np.cumsum([512, 128, 128, 512, 64, 8, 512, 512]).tolist(), axis=1)
    pad = jnp.zeros((w_in.shape[0], LANES - IDX_DIM - IDX_HEADS), w_in.dtype)
    return jnp.concatenate([q, k, v, qi, ki, wi, pad, ga, u, gp], axis=1).astype(BF16)


def kernel(x_prompt, x_sample, cache_k, cache_v, cache_kidx, state_pool, page_table, c_prompt, c_sample,
           norm_w, w_ada, b_ada, w_in, q_norm_w, k_norm_w, w_pool, pool_scale, w_out):
    bp, s, _ = x_prompt.shape
    bs = x_sample.shape[0]
    assert w_in.shape[0] == 1 and x_sample.shape[1] == 1, "single layer, single decode token"
    n_phys = cache_k.shape[1]

    w_in_b = _permute_w_in(w_in[0])
    w_out_b = w_out[0].astype(BF16)
    zero_blk = jnp.zeros((POOL_GC, POOL_GC), w_pool.dtype)
    wpool_b = jnp.stack([jnp.block([[w_pool[0, 2 * p], zero_blk], [zero_blk, w_pool[0, 2 * p + 1]]])
                         for p in range(len(POOL_WINDOWS) // 2)]).astype(BF16)
    qnw = jnp.tile(q_norm_w[0], N_HEADS)[None, :] * (HEAD_DIM ** -0.5)
    knw = jnp.tile(k_norm_w[0], KV_HEADS)[None, :]
    nw = norm_w[0][None, :]
    pscale = pool_scale[0][None, :]

    ada_p, ada_s = _ada(c_prompt, c_sample, w_ada, b_ada)
    ada_p = ada_p.reshape(bp, 1, 3 * D_MODEL)

    (qs, k_t_p, kk, v_t_p, vtx, qi, ki_t_p, kik, wit, sga, pg, ulast) = _proj_prompt(
        x_prompt, ada_p, nw, w_in_b, qnw, knw, wpool_b, pscale)
    ag = _attn_prompt(qi, wit, qs, sga, kik, kk, vtx, bp, s)
    y_prompt = _out_proj_prompt(x_prompt, ag, pg, ada_p, w_out_b)

    hist_t = jnp.transpose(state_pool[0], (1, 0, 2))
    (qpad, k_s, v_s, ki_s, qi_s, wi_s, snew, lnew, sga_s, pg_s, pool_s) = _proj_sample(
        x_sample[:, 0, :], ada_s, nw, w_in_b, qnw, knw, wpool_b, pscale, hist_t)
    n_pages = page_table.shape[1]
    kidx_t = jnp.transpose(cache_kidx[0], (0, 2, 1))
    k_t = jnp.transpose(cache_k[0], (0, 2, 3, 1)).reshape(n_phys, LANES, PAGE)
    v_t = jnp.transpose(cache_v[0], (0, 2, 3, 1)).reshape(n_phys, LANES, PAGE)
    scores = _score_sample(page_table, qi_s.reshape(bs, IDX_HEADS, IDX_DIM), wi_s.reshape(bs, IDX_HEADS, 1),
                           kidx_t)
    mask, mnew = _select_sample(scores, snew)
    o_s = _attn_sample(
        page_table, qpad.reshape(bs, N_HEADS, LANES),
        mask.reshape(bs, 1, n_pages * PAGE),
        lnew[:, :N_HEADS].reshape(bs, N_HEADS, 1),
        jnp.broadcast_to(mnew[:, :1], (bs, N_HEADS)).reshape(bs, N_HEADS, 1),
        v_s.reshape(bs, 1, LANES), k_t, v_t)
    y_sample = _out_proj_sample(x_sample[:, 0, :], o_s.reshape(bs, N_HEADS * LANES), sga_s, pg_s, ada_s, w_out_b)

    to_heads = lambda a: jnp.transpose(a.reshape(bp, KV_HEADS, HEAD_DIM, s), (0, 3, 1, 2))[None]
    return (
        y_prompt,
        y_sample[:, None, :],
        to_heads(k_t_p),
        to_heads(v_t_p),
        jnp.transpose(ki_t_p, (0, 2, 1))[None],
        ulast[:, 1:, :][None],
        k_s.reshape(1, bs, 1, KV_HEADS, HEAD_DIM),
        v_s.reshape(1, bs, 1, KV_HEADS, HEAD_DIM),
        ki_s.reshape(1, bs, 1, IDX_DIM),
        jnp.transpose(pool_s, (1, 0, 2))[None],
    )
```

```python
import functools

import jax
import jax.numpy as jnp
import numpy as np
from jax import lax
from jax.experimental import pallas as pl
from jax.experimental.pallas import tpu as pltpu

F32 = jnp.float32
BF16 = jnp.bfloat16
I32 = jnp.int32

D_MODEL = 1024
ATTN_W = 512
POOL_W = 512
HEAD_DIM = 64
N_HEADS = 8
KV_HEADS = 2
IDX_HEADS = 8
IDX_DIM = 64
TOPK = 256
POOL_WINDOWS = (2, 4, 8, 16)
POOL_GC = 128
POOL_HIST = 15
EPS = 1e-6
PAGE = 128

LANES = 128
SUBLANES = 8
BF16_ROWS = 16
KEY_CHUNK = 256
TQ = 128
TM = 512
TM_OUT = 1024
VT_ROWS = HEAD_DIM + BF16_ROWS
SCORE_GROUP = 8
ATTN_GROUP = 4
INT_MIN = np.int32(-2 ** 31)
VMEM_LIMIT = 56 * 1024 * 1024

C_Q, C_K, C_V, C_QI, C_KW, C_GA, C_U, C_GP, N_PROJ = 0, 512, 640, 768, 1280, 1408, 1920, 2432, 2944


def _nt_dot(a, b):
    return lax.dot_general(a, b, (((1,), (1,)), ((), ())), preferred_element_type=F32)


def _dot(a, b):
    return jnp.dot(a, b, preferred_element_type=F32)


def _silu(z):
    return z / (1.0 + jnp.exp(-z))


def _lane_iota(shape):
    return lax.broadcasted_iota(I32, shape, len(shape) - 1)


def _seg_ones(n, seg):
    r = lax.broadcasted_iota(I32, (n, n), 0) // seg
    c = lax.broadcasted_iota(I32, (n, n), 1) // seg
    return jnp.where(r == c, 1.0, 0.0).astype(BF16)


def _head_rms(z, seg_mat, w):
    n = seg_mat.shape[0]
    sq = (z * z).astype(BF16)
    ss = jnp.concatenate([_dot(sq[:, j:j + n], seg_mat) for j in range(0, z.shape[1], n)], axis=1)
    return z * lax.rsqrt(ss * (1.0 / HEAD_DIM) + EPS) * w


def _float_of_rank(u):
    key = u ^ INT_MIN
    bits = key ^ ((key >> 31) & np.int32(0x7FFFFFFF))
    return lax.bitcast_convert_type(bits, F32)


def _kth_largest(count_ge, shape):
    def bit_body(it, ans):
        cand = ans | jnp.left_shift(jnp.int32(1), 31 - it)
        return jnp.where(count_ge(_float_of_rank(cand)) >= float(TOPK), cand, ans)

    ans = lax.fori_loop(0, 32, bit_body, jnp.zeros(shape, I32))
    return jnp.where(ans == 0, -jnp.inf, _float_of_rank(ans))


def _fold_rows(x, op):
    parts = [x[r:r + SUBLANES] for r in range(0, x.shape[0], SUBLANES)]
    while len(parts) > 1:
        parts = [op(parts[i], parts[i + 1]) for i in range(0, len(parts), 2)]
    return parts[0]


def _ada_kernel(cp_ref, cs_ref, w_ref, b_ref, op_ref, os_ref):
    w = w_ref[0].astype(BF16)
    op_ref[...] = _dot(_silu(cp_ref[...]).astype(BF16), w) + b_ref[...]
    os_ref[...] = _dot(_silu(cs_ref[...]).astype(BF16), w) + b_ref[...]


def _ada(c_prompt, c_sample, w_ada, b_ada):
    bp, bs = c_prompt.shape[0], c_sample.shape[0]
    return pl.pallas_call(
        _ada_kernel,
        out_shape=(jax.ShapeDtypeStruct((bp, 3 * D_MODEL), F32), jax.ShapeDtypeStruct((bs, 3 * D_MODEL), F32)),
        grid=(3,),
        in_specs=[
            pl.BlockSpec((bp, D_MODEL), lambda j: (0, 0)),
            pl.BlockSpec((bs, D_MODEL), lambda j: (0, 0)),
            pl.BlockSpec((1, D_MODEL, D_MODEL), lambda j: (0, 0, j)),
            pl.BlockSpec((1, D_MODEL), lambda j: (0, j)),
        ],
        out_specs=(pl.BlockSpec((bp, D_MODEL), lambda j: (0, j)), pl.BlockSpec((bs, D_MODEL), lambda j: (0, j))),
        compiler_params=pltpu.CompilerParams(dimension_semantics=("arbitrary",)),
        name="ada_ln",
    )(c_prompt, c_sample, w_ada, b_ada)


def _modulated_norm(x, norm_w, scale, shift):
    ms = jnp.mean(x * x, axis=-1, keepdims=True)
    return (x * lax.rsqrt(ms + EPS)) * norm_w * (1.0 + scale) + shift


def _pool_mix(d, wpool_ref, pscale, gp):
    db = d.astype(BF16)
    wide = 2 * POOL_GC
    y = jnp.concatenate([_dot(db[:, p * wide:(p + 1) * wide], wpool_ref[p])
                         for p in range(len(POOL_WINDOWS) // 2)], axis=1)
    return y * pscale * _silu(gp)


def _proj_prompt_kernel(x_ref, ada_ref, nw_ref, w_ref, qnw_ref, knw_ref, wpool_ref, ps_ref,
                        qs_ref, kt_ref, kk_ref, vt_ref, vtx_ref, qi_ref, kit_ref, kik_ref,
                        wit_ref, sga_ref, pg_ref, ulast_ref,
                        ext_ref):
    t = pl.program_id(1)
    ada = ada_ref[0]
    shift = ada[:, 0:D_MODEL]
    scale = ada[:, D_MODEL:2 * D_MODEL]
    sub_rows = KEY_CHUNK
    lo = _lane_iota((sub_rows, LANES)) < HEAD_DIM
    seg256 = _seg_ones(2 * LANES, HEAD_DIM)
    seg128 = _seg_ones(LANES, HEAD_DIM)
    ones = jnp.ones((BF16_ROWS, KEY_CHUNK), BF16)

    @pl.when(t == 0)
    def _():
        ext_ref[0:16, :] = jnp.zeros((16, POOL_W), F32)

    for sub in range(TM // sub_rows):
        rows = slice(sub * sub_rows, (sub + 1) * sub_rows)
        hb = _modulated_norm(x_ref[0, rows, :], nw_ref[...], scale, shift).astype(BF16)

        q = _dot(hb, w_ref[:, C_Q:C_Q + ATTN_W])
        qs_ref[rows, :] = _head_rms(q, seg256, qnw_ref[...]).astype(BF16)

        kv = _dot(hb, w_ref[:, C_K:C_K + 2 * LANES])
        k = _head_rms(kv[:, 0:LANES], seg128, knw_ref[...])
        kt_ref[0, :, rows] = k.T
        k_sw = pltpu.roll(k, HEAD_DIM, axis=1)
        kk_ref[0, rows, :] = jnp.where(lo, k, k_sw).astype(BF16)
        kk_ref[1, rows, :] = jnp.where(lo, k_sw, k).astype(BF16)

        v_t = kv[:, LANES:2 * LANES].T
        vt_ref[0, :, rows] = v_t
        for g in range(KV_HEADS):
            vtx_ref[0, g, sub, 0:HEAD_DIM, :] = v_t[g * HEAD_DIM:(g + 1) * HEAD_DIM].astype(BF16)
            vtx_ref[0, g, sub, HEAD_DIM:VT_ROWS, :] = ones

        qi_ref[rows, :] = _dot(hb, w_ref[:, C_QI:C_QI + ATTN_W]).astype(BF16)
        kw = _dot(hb, w_ref[:, C_KW:C_KW + LANES])
        kw_t = kw.T
        kit_ref[0, :, rows] = kw_t[0:IDX_DIM]
        wit_ref[:, rows] = kw_t[IDX_DIM:IDX_DIM + IDX_HEADS] * ((IDX_HEADS * IDX_DIM) ** -0.5)
        kik_ref[rows, :] = jnp.where(lo, kw, pltpu.roll(kw, HEAD_DIM, axis=1)).astype(BF16)

        sga_ref[rows, :] = _silu(_dot(hb, w_ref[:, C_GA:C_GA + ATTN_W])).astype(BF16)

        u = _dot(hb, w_ref[:, C_U:C_U + POOL_W])
        gp = _dot(hb, w_ref[:, C_GP:C_GP + POOL_W])
        base = 16 + sub * sub_rows
        ext_ref[base:base + sub_rows, :] = u
        pos = t * TM + sub * sub_rows + lax.broadcasted_iota(I32, (sub_rows, POOL_GC), 0)
        ds = []
        for g, w in enumerate(POOL_WINDOWS):
            cs = slice(g * POOL_GC, (g + 1) * POOL_GC)
            s = u[:, cs]
            for j in range(1, w):
                s = s + ext_ref[base - j:base - j + sub_rows, cs]
            cnt = jnp.minimum(pos + 1, w).astype(F32)
            ds.append(s / cnt - u[:, cs])
        d = jnp.concatenate(ds, axis=1)
        pg_ref[rows, :] = _pool_mix(d, wpool_ref, ps_ref[...], gp).astype(BF16)

    tail = ext_ref[TM:TM + 16, :]
    ulast_ref[0] = tail
    ext_ref[0:16, :] = tail


def _proj_prompt(x, ada_p, norm_w, w_in_b, qnw, knw, wpool_b, pscale):
    b, s, _ = x.shape
    n = b * s
    nt = s // TM
    cpt = TM // KEY_CHUNK
    row = lambda bi, ti: (bi * nt + ti, 0)
    tok = lambda bi, ti: (bi, 0, ti)
    const2 = lambda bi, ti: (0, 0)
    const3 = lambda bi, ti: (0, 0, 0)
    out_shape = (
        jax.ShapeDtypeStruct((n, ATTN_W), BF16),
        jax.ShapeDtypeStruct((b, LANES, s), F32),
        jax.ShapeDtypeStruct((KV_HEADS, n, LANES), BF16),
        jax.ShapeDtypeStruct((b, LANES, s), F32),
        jax.ShapeDtypeStruct((b, KV_HEADS, s // KEY_CHUNK, VT_ROWS, KEY_CHUNK), BF16),
        jax.ShapeDtypeStruct((n, ATTN_W), BF16),
        jax.ShapeDtypeStruct((b, IDX_DIM, s), F32),
        jax.ShapeDtypeStruct((n, LANES), BF16),
        jax.ShapeDtypeStruct((IDX_HEADS, n), F32),
        jax.ShapeDtypeStruct((n, ATTN_W), BF16),
        jax.ShapeDtypeStruct((n, POOL_W), BF16),
        jax.ShapeDtypeStruct((b, 16, POOL_W), F32),
    )
    out_specs = (
        pl.BlockSpec((TM, ATTN_W), row),
        pl.BlockSpec((1, LANES, TM), tok),
        pl.BlockSpec((KV_HEADS, TM, LANES), lambda bi, ti: (0, bi * nt + ti, 0)),
        pl.BlockSpec((1, LANES, TM), tok),
        pl.BlockSpec((1, KV_HEADS, cpt, VT_ROWS, KEY_CHUNK), lambda bi, ti: (bi, 0, ti, 0, 0)),
        pl.BlockSpec((TM, ATTN_W), row),
        pl.BlockSpec((1, IDX_DIM, TM), tok),
        pl.BlockSpec((TM, LANES), row),
        pl.BlockSpec((IDX_HEADS, TM), lambda bi, ti: (0, bi * nt + ti)),
        pl.BlockSpec((TM, ATTN_W), row),
        pl.BlockSpec((TM, POOL_W), row),
        pl.BlockSpec((1, 16, POOL_W), lambda bi, ti: (bi, 0, 0)),
    )
    in_specs = [
        pl.BlockSpec((1, TM, D_MODEL), lambda bi, ti: (bi, ti, 0)),
        pl.BlockSpec((1, 1, 3 * D_MODEL), lambda bi, ti: (bi, 0, 0)),
        pl.BlockSpec((1, D_MODEL), const2),
        pl.BlockSpec((D_MODEL, N_PROJ), const2),
        pl.BlockSpec((1, ATTN_W), const2),
        pl.BlockSpec((1, LANES), const2),
        pl.BlockSpec((2, 2 * POOL_GC, 2 * POOL_GC), const3),
        pl.BlockSpec((1, POOL_W), const2),
    ]
    return pl.pallas_call(
        _proj_prompt_kernel,
        out_shape=out_shape,
        grid=(b, nt),
        in_specs=in_specs,
        out_specs=out_specs,
        scratch_shapes=[pltpu.VMEM((16 + TM, POOL_W), F32)],
        compiler_params=pltpu.CompilerParams(
            dimension_semantics=("arbitrary", "arbitrary"), vmem_limit_bytes=VMEM_LIMIT),
        name="proj_prompt",
    )(x, ada_p, norm_w, w_in_b, qnw, knw, wpool_b, pscale)


def _attn_tile(nch, row0, wit_ref, kik_ref, kk_ref, vtx_ref, sc_ref, lg_ref, wq_ref, acc_ref):
    shape2 = (KEY_CHUNK, TQ)
    n_pairs = N_HEADS // 2
    group_of = lambda hp: (2 * hp) // (N_HEADS // KV_HEADS)

    for c in range(nch):
        keys = slice(c * KEY_CHUNK, (c + 1) * KEY_CHUNK)
        kk = kik_ref[keys, :]
        acc = jnp.zeros(shape2, F32)
        for hp in range(n_pairs):
            s2 = _nt_dot(kk, wq_ref[0, hp])
            acc = acc + jnp.maximum(s2[:, 0:TQ], 0.0) * wit_ref[2 * hp:2 * hp + 1, :]
            acc = acc + jnp.maximum(s2[:, TQ:2 * TQ], 0.0) * wit_ref[2 * hp + 1:2 * hp + 2, :]
        if c == nch - 1:
            kpos = c * KEY_CHUNK + lax.broadcasted_iota(I32, shape2, 0)
            qpos = row0 + lax.broadcasted_iota(I32, shape2, 1)
            acc = jnp.where(kpos <= qpos, acc, -jnp.inf)
        sc_ref[c] = acc
        for hp in range(n_pairs):
            lg_ref[hp, c] = _nt_dot(kk_ref[group_of(hp), keys, :], wq_ref[1, hp])

    def count_pass(pred):
        parts = [_fold_rows(pred(sc_ref[c]).astype(F32), jnp.add) for c in range(nch)]
        while len(parts) > 1:
            parts = [sum(parts[j:j + 2]) for j in range(0, len(parts), 2)]
        return jnp.sum(parts[0], axis=0, keepdims=True)

    if nch * KEY_CHUNK <= TOPK:
        thr = jnp.full((1, TQ), -jnp.inf, F32)
        need = jnp.zeros((1, TQ), F32)
    else:
        thr = _kth_largest(lambda t: count_pass(lambda x: x >= t), (1, TQ))
        need = float(TOPK) - count_pass(lambda x: x > thr)

    tri = jnp.where(lax.broadcasted_iota(I32, (KEY_CHUNK, KEY_CHUNK), 1)
                    <= lax.broadcasted_iota(I32, (KEY_CHUNK, KEY_CHUNK), 0), 1.0, 0.0).astype(BF16)

    m_part = [jnp.full((SUBLANES, TQ), -jnp.inf, F32) for _ in range(N_HEADS)]
    eq_before = jnp.zeros((1, TQ), F32)
    for c in range(nch):
        sc = sc_ref[c]
        eq = sc == thr
        rank = _dot(tri, jnp.where(eq, 1.0, 0.0).astype(BF16)) + eq_before
        sel = (sc > thr) | (eq & (rank <= need))
        if c == nch - 1:
            sel = sel & (sc > -jnp.inf)
        eq_before = rank[KEY_CHUNK - 1:KEY_CHUNK, :]
        for hp in range(n_pairs):
            for par in range(2):
                cols = slice(par * TQ, (par + 1) * TQ)
                lg = jnp.where(sel, lg_ref[hp, c, :, cols], -jnp.inf)
                lg_ref[hp, c, :, cols] = lg
                m_part[2 * hp + par] = jnp.maximum(m_part[2 * hp + par], _fold_rows(lg, jnp.maximum))
    m_rows = [jnp.max(m, axis=0, keepdims=True) for m in m_part]

    for hp in range(n_pairs):
        m2 = jnp.concatenate([m_rows[2 * hp], m_rows[2 * hp + 1]], axis=1)
        acc = jnp.zeros((VT_ROWS, 2 * TQ), F32)
        for c in range(nch):
            p2 = jnp.exp(lg_ref[hp, c] - m2).astype(BF16)
            acc = acc + _dot(vtx_ref[0, group_of(hp), c], p2)
        acc_ref[hp] = acc


def _attn_prompt_kernel(qi_ref, wit_ref, qs_ref, sga_ref, kik_ref, kk_ref, vtx_ref, ag_ref,
                        sc_ref, lg_ref, wq_ref, acc_ref):
    i = pl.program_id(1)
    n_pairs = N_HEADS // 2
    lo = _lane_iota((TQ, LANES)) < HEAD_DIM

    for kind, ref in enumerate((qi_ref, qs_ref)):
        for hp in range(n_pairs):
            pair = ref[:, hp * LANES:(hp + 1) * LANES].astype(F32)
            wq_ref[kind, hp, 0:TQ, :] = jnp.where(lo, pair, 0.0).astype(BF16)
            wq_ref[kind, hp, TQ:2 * TQ, :] = jnp.where(lo, 0.0, pair).astype(BF16)

    nch = (i + 2) // 2
    for n_static in range(1, sc_ref.shape[0] + 1):
        @pl.when(nch == n_static)
        def _(n_static=n_static):
            _attn_tile(n_static, i * TQ, wit_ref, kik_ref, kk_ref, vtx_ref, sc_ref, lg_ref, wq_ref, acc_ref)

    for hp in range(n_pairs):
        a = acc_ref[hp]
        o0 = a[0:HEAD_DIM, 0:TQ] / a[HEAD_DIM:HEAD_DIM + 1, 0:TQ]
        o1 = a[0:HEAD_DIM, TQ:2 * TQ] / a[HEAD_DIM:HEAD_DIM + 1, TQ:2 * TQ]
        pair = jnp.concatenate([o0, o1], axis=0).T
        cols = slice(hp * LANES, (hp + 1) * LANES)
        ag_ref[:, cols] = (pair * sga_ref[:, cols].astype(F32)).astype(BF16)


def _attn_prompt(qi, wit, qs, sga, kik, kk, vtx, b, s):
    n = b * s
    nq = s // TQ
    nkc = s // KEY_CHUNK
    row = lambda bi, qi_: (bi * nq + qi_, 0)
    return pl.pallas_call(
        _attn_prompt_kernel,
        out_shape=jax.ShapeDtypeStruct((n, ATTN_W), BF16),
        grid=(b, nq),
        in_specs=[
            pl.BlockSpec((TQ, ATTN_W), row),
            pl.BlockSpec((IDX_HEADS, TQ), lambda bi, qi_: (0, bi * nq + qi_)),
            pl.BlockSpec((TQ, ATTN_W), row),
            pl.BlockSpec((TQ, ATTN_W), row),
            pl.BlockSpec((s, LANES), lambda bi, qi_: (bi, 0)),
            pl.BlockSpec((KV_HEADS, s, LANES), lambda bi, qi_: (0, bi, 0)),
            pl.BlockSpec((1, KV_HEADS, nkc, VT_ROWS, KEY_CHUNK), lambda bi, qi_: (bi, 0, 0, 0, 0)),
        ],
        out_specs=pl.BlockSpec((TQ, ATTN_W), row),
        scratch_shapes=[
            pltpu.VMEM((nkc, KEY_CHUNK, TQ), F32),
            pltpu.VMEM((N_HEADS // 2, nkc, KEY_CHUNK, 2 * TQ), F32),
            pltpu.VMEM((2, N_HEADS // 2, 2 * TQ, LANES), BF16),
            pltpu.VMEM((N_HEADS // 2, VT_ROWS, 2 * TQ), F32),
        ],
        compiler_params=pltpu.CompilerParams(
            dimension_semantics=("arbitrary", "arbitrary"), vmem_limit_bytes=VMEM_LIMIT),
        name="attn_prompt",
    )(qi, wit, qs, sga, kik, kk, vtx)


def _out_proj_kernel(x_ref, a_ref, p_ref, gate_ref, w_ref, o_ref):
    y = _dot(a_ref[...], w_ref[0:ATTN_W, :]) + _dot(p_ref[...], w_ref[ATTN_W:ATTN_W + POOL_W, :])
    o_ref[0] = x_ref[0] + gate_ref[0] * y


def _out_proj_prompt(x, ag, pg, ada_p, w_out_b):
    b, s, _ = x.shape
    nt = s // TM_OUT
    row = lambda bi, ti: (bi * nt + ti, 0)
    return pl.pallas_call(
        _out_proj_kernel,
        out_shape=jax.ShapeDtypeStruct(x.shape, F32),
        grid=(b, nt),
        in_specs=[
            pl.BlockSpec((1, TM_OUT, D_MODEL), lambda bi, ti: (bi, ti, 0)),
            pl.BlockSpec((TM_OUT, ATTN_W), row),
            pl.BlockSpec((TM_OUT, POOL_W), row),
            pl.BlockSpec((1, 1, D_MODEL), lambda bi, ti: (bi, 0, 2)),
            pl.BlockSpec((D_MODEL, D_MODEL), lambda bi, ti: (0, 0)),
        ],
        out_specs=pl.BlockSpec((1, TM_OUT, D_MODEL), lambda bi, ti: (bi, ti, 0)),
        compiler_params=pltpu.CompilerParams(
            dimension_semantics=("arbitrary", "arbitrary"), vmem_limit_bytes=VMEM_LIMIT),
        name="out_proj_prompt",
    )(x, ag, pg, ada_p, w_out_b)


def _proj_sample_kernel(x_ref, ada_ref, nw_ref, w_ref, qnw_ref, knw_ref, wpool_ref, ps_ref, hist_ref,
                        qpad_ref, k_ref, v_ref, ki_ref, qi_ref, wi_ref, snew_ref, lnew_ref,
                        sga_ref, pg_ref, pool_ref):
    nb = x_ref.shape[0]
    x = x_ref[...]
    shift = ada_ref[:, 0:D_MODEL]
    scale = ada_ref[:, D_MODEL:2 * D_MODEL]
    hb = _modulated_norm(x, nw_ref[...], scale, shift).astype(BF16)

    lane = _lane_iota((nb, LANES))
    lo = lane < HEAD_DIM
    seg256 = _seg_ones(2 * LANES, HEAD_DIM)
    seg128 = _seg_ones(LANES, HEAD_DIM)
    head_sel = jnp.where(lax.broadcasted_iota(I32, (ATTN_W, LANES), 0) // HEAD_DIM
                         == lax.broadcasted_iota(I32, (ATTN_W, LANES), 1), 1.0, 0.0).astype(BF16)

    def head_sums(prod):
        hi = prod.astype(BF16)
        rest = (prod - hi.astype(F32)).astype(BF16)
        return _dot(hi, head_sel) + _dot(rest, head_sel)

    q = _head_rms(_dot(hb, w_ref[:, C_Q:C_Q + ATTN_W]), seg256, qnw_ref[...])
    qb = q.astype(BF16)
    k = _head_rms(_dot(hb, w_ref[:, C_K:C_K + LANES]), seg128, knw_ref[...])
    k_ref[...] = k
    v = _dot(hb, w_ref[:, C_V:C_V + LANES])
    v_ref[...] = v

    for hp in range(N_HEADS // 2):
        pair = q[:, hp * LANES:(hp + 1) * LANES]
        pair_sw = pltpu.roll(pair, HEAD_DIM, axis=1)
        g = (2 * hp) // (N_HEADS // KV_HEADS)
        if g == 0:
            h_even, h_odd = jnp.where(lo, pair, 0.0), jnp.where(lo, pair_sw, 0.0)
        else:
            h_even, h_odd = jnp.where(lo, 0.0, pair_sw), jnp.where(lo, 0.0, pair)
        qpad_ref[:, (2 * hp) * LANES:(2 * hp + 1) * LANES] = h_even.astype(BF16)
        qpad_ref[:, (2 * hp + 1) * LANES:(2 * hp + 2) * LANES] = h_odd.astype(BF16)

    kq = k.astype(BF16).astype(F32)
    kq_sw = pltpu.roll(kq, HEAD_DIM, axis=1)
    k0t = jnp.where(lo, kq, kq_sw)
    k1t = jnp.where(lo, kq_sw, kq)
    qf = qb.astype(F32)
    prod = jnp.concatenate([qf[:, 0:LANES] * k0t, qf[:, LANES:2 * LANES] * k0t,
                            qf[:, 2 * LANES:3 * LANES] * k1t, qf[:, 3 * LANES:4 * LANES] * k1t], axis=1)
    lnew_ref[...] = head_sums(prod)

    qi = _dot(hb, w_ref[:, C_QI:C_QI + ATTN_W])
    qib = qi.astype(BF16)
    qi_ref[...] = qib
    kw = _dot(hb, w_ref[:, C_KW:C_KW + LANES])
    kw_sw = pltpu.roll(kw, HEAD_DIM, axis=1)
    ki_ref[...] = kw[:, 0:IDX_DIM]
    wi_full = jnp.where(lane < IDX_HEADS, kw_sw, 0.0) * ((IDX_HEADS * IDX_DIM) ** -0.5)
    wi_ref[...] = wi_full[:, 0:IDX_HEADS]

    kib = kw.astype(BF16).astype(F32)
    kit = jnp.where(lo, kib, pltpu.roll(kib, HEAD_DIM, axis=1))
    qif = qib.astype(F32)
    prod_i = jnp.concatenate([qif[:, j * LANES:(j + 1) * LANES] * kit for j in range(4)], axis=1)
    s_new = jnp.maximum(head_sums(prod_i), 0.0) * wi_full
    snew_ref[...] = jnp.broadcast_to(jnp.sum(s_new, axis=1, keepdims=True), (nb, LANES))

    sga_ref[...] = _silu(_dot(hb, w_ref[:, C_GA:C_GA + ATTN_W])).astype(BF16)

    u = _dot(hb, w_ref[:, C_U:C_U + POOL_W])
    gp = _dot(hb, w_ref[:, C_GP:C_GP + POOL_W])
    for j in range(POOL_HIST - 1):
        pool_ref[j] = hist_ref[j + 1]
    pool_ref[POOL_HIST - 1] = u
    ds = []
    for g, w in enumerate(POOL_WINDOWS):
        cs = slice(g * POOL_GC, (g + 1) * POOL_GC)
        s = u[:, cs]
        for j in range(1, w):
            s = s + hist_ref[POOL_HIST - j, :, cs]
        ds.append(s / float(w) - u[:, cs])
    d = jnp.concatenate(ds, axis=1)
    pg_ref[...] = _pool_mix(d, wpool_ref, ps_ref[...], gp).astype(BF16)


def _proj_sample(x, ada_s, norm_w, w_in_b, qnw, knw, wpool_b, pscale, hist_t):
    nb = x.shape[0]
    out_shape = (
        jax.ShapeDtypeStruct((nb, N_HEADS * LANES), BF16),
        jax.ShapeDtypeStruct((nb, LANES), F32),
        jax.ShapeDtypeStruct((nb, LANES), F32),
        jax.ShapeDtypeStruct((nb, IDX_DIM), F32),
        jax.ShapeDtypeStruct((nb, ATTN_W), BF16),
        jax.ShapeDtypeStruct((nb, IDX_HEADS), F32),
        jax.ShapeDtypeStruct((nb, LANES), F32),
        jax.ShapeDtypeStruct((nb, LANES), F32),
        jax.ShapeDtypeStruct((nb, ATTN_W), BF16),
        jax.ShapeDtypeStruct((nb, POOL_W), BF16),
        jax.ShapeDtypeStruct((POOL_HIST, nb, POOL_W), F32),
    )
    return pl.pallas_call(
        _proj_sample_kernel,
        out_shape=out_shape,
        compiler_params=pltpu.CompilerParams(vmem_limit_bytes=VMEM_LIMIT),
        name="proj_sample",
    )(x, ada_s, norm_w, w_in_b, qnw, knw, wpool_b, pscale, hist_t)


def _page_copies(pt_ref, pages_hbm, buf_ref, sem_ref, step, slot, group, n_pages):
    copies = []
    for j in range(group):
        for p in range(n_pages):
            page = pt_ref[step * group + j, p]
            copies.append(pltpu.make_async_copy(
                pages_hbm.at[page], buf_ref.at[slot, j, :, pl.ds(p * PAGE, PAGE)], sem_ref.at[slot]))
    return copies


def _score_sample_kernel(pt_ref, qi_ref, wi_ref, kidx_hbm, o_ref, buf_ref, sem_ref, *, group, n_pages):
    step = pl.program_id(0)
    slot = lax.rem(step, 2)

    def copies(st, sl):
        return _page_copies(pt_ref, kidx_hbm, buf_ref, sem_ref, st, sl, group, n_pages)

    @pl.when(step == 0)
    def _():
        for cp in copies(step, slot):
            cp.start()

    @pl.when(step + 1 < pl.num_programs(0))
    def _():
        for cp in copies(step + 1, 1 - slot):
            cp.start()

    for cp in copies(step, slot):
        cp.wait()

    for j in range(group):
        ki_t = buf_ref[slot, j].astype(BF16)
        s = _dot(qi_ref[j], ki_t)
        o_ref[j:j + 1, :] = jnp.sum(jnp.maximum(s, 0.0) * wi_ref[j], axis=0, keepdims=True)


def _score_sample(page_table, qi3, wi3, kidx_t):
    nb, n_pages = page_table.shape
    n_keys = n_pages * PAGE
    g = SCORE_GROUP
    return pl.pallas_call(
        functools.partial(_score_sample_kernel, group=g, n_pages=n_pages),
        out_shape=jax.ShapeDtypeStruct((nb, n_keys), F32),
        grid_spec=pltpu.PrefetchScalarGridSpec(
            num_scalar_prefetch=1,
            grid=(nb // g,),
            in_specs=[
                pl.BlockSpec((g, IDX_HEADS, IDX_DIM), lambda s, pt: (s, 0, 0)),
                pl.BlockSpec((g, IDX_HEADS, 1), lambda s, pt: (s, 0, 0)),
                pl.BlockSpec(memory_space=pl.ANY),
            ],
            out_specs=pl.BlockSpec((g, n_keys), lambda s, pt: (s, 0)),
            scratch_shapes=[
                pltpu.VMEM((2, g, IDX_DIM, n_keys), F32),
                pltpu.SemaphoreType.DMA((2,)),
            ],
        ),
        compiler_params=pltpu.CompilerParams(
            dimension_semantics=("arbitrary",), vmem_limit_bytes=VMEM_LIMIT),
        name="score_sample",
    )(page_table, qi3, wi3, kidx_t)


def _select_sample_kernel(sc_ref, snew_ref, mask_ref, mnew_ref):
    nb, n_keys = sc_ref.shape
    nch = n_keys // KEY_CHUNK
    snew = snew_ref[...]
    ones_mat = jnp.ones((LANES, LANES), BF16)

    def count_pass(pred):
        acc = jnp.zeros((nb, LANES), F32)
        for c in range(n_keys // LANES):
            acc = acc + pred(sc_ref[:, c * LANES:(c + 1) * LANES]).astype(F32)
        return _dot(acc.astype(BF16), ones_mat) + pred(snew).astype(F32)

    thr = _kth_largest(lambda t: count_pass(lambda x: x >= t), (nb, LANES))
    need = float(TOPK) - count_pass(lambda x: x > thr)
    thr2 = jnp.concatenate([thr, thr], axis=1)
    need2 = jnp.concatenate([need, need], axis=1)
    tri = jnp.where(lax.broadcasted_iota(I32, (KEY_CHUNK, KEY_CHUNK), 0)
                    <= lax.broadcasted_iota(I32, (KEY_CHUNK, KEY_CHUNK), 1), 1.0, 0.0).astype(BF16)
    ones_cl = jnp.ones((KEY_CHUNK, LANES), BF16)

    eq_before = jnp.zeros((nb, LANES), F32)
    for c in range(nch):
        sc = sc_ref[:, c * KEY_CHUNK:(c + 1) * KEY_CHUNK]
        eq = sc == thr2
        eqb = jnp.where(eq, 1.0, 0.0).astype(BF16)
        rank = _dot(eqb, tri) + jnp.concatenate([eq_before, eq_before], axis=1)
        sel = (sc > thr2) | (eq & (rank <= need2))
        mask_ref[:, c * KEY_CHUNK:(c + 1) * KEY_CHUNK] = jnp.where(sel, 0.0, -jnp.inf)
        eq_before = eq_before + _dot(eqb, ones_cl)
    sel_new = (snew > thr) | ((snew == thr) & (eq_before + 1.0 <= need))
    mnew_ref[...] = jnp.where(sel_new, 0.0, -jnp.inf)


def _select_sample(scores, snew):
    nb, n_keys = scores.shape
    return pl.pallas_call(
        _select_sample_kernel,
        out_shape=(jax.ShapeDtypeStruct((nb, n_keys), F32), jax.ShapeDtypeStruct((nb, LANES), F32)),
        name="select_sample",
    )(scores, snew)


def _attn_sample_kernel(pt_ref, q_ref, mask_ref, lnew_ref, mnew_ref, vnew_ref, k_hbm, v_hbm,
                        o_ref, kbuf_ref, vbuf_ref, ksem_ref, vsem_ref, *, group, n_pages):
    step = pl.program_id(0)
    slot = lax.rem(step, 2)

    def copies(st, sl):
        return (_page_copies(pt_ref, k_hbm, kbuf_ref, ksem_ref, st, sl, group, n_pages)
                + _page_copies(pt_ref, v_hbm, vbuf_ref, vsem_ref, st, sl, group, n_pages))

    @pl.when(step == 0)
    def _():
        for cp in copies(step, slot):
            cp.start()

    @pl.when(step + 1 < pl.num_programs(0))
    def _():
        for cp in copies(step + 1, 1 - slot):
            cp.start()

    for cp in copies(step, slot):
        cp.wait()

    for j in range(group):
        k_t = kbuf_ref[slot, j].astype(BF16)
        lg = _dot(q_ref[j], k_t) + mask_ref[j]
        lg_n = lnew_ref[j] + mnew_ref[j]
        m = jnp.maximum(jnp.max(lg, axis=1, keepdims=True), lg_n)
        p = jnp.exp(lg - m)
        p_n = jnp.exp(lg_n - m)
        l = jnp.sum(p, axis=1, keepdims=True) + p_n
        acc = _nt_dot(p.astype(BF16), vbuf_ref[slot, j].astype(BF16)) + p_n * vnew_ref[j]
        o_ref[j] = acc / l


def _attn_sample(page_table, qpad3, mask3, lnew3, mnew3, vnew3, k_t, v_t):
    nb, n_pages = page_table.shape
    n_keys = n_pages * PAGE
    g = ATTN_GROUP
    per_s = lambda s, pt: (s, 0, 0)
    return pl.pallas_call(
        functools.partial(_attn_sample_kernel, group=g, n_pages=n_pages),
        out_shape=jax.ShapeDtypeStruct((nb, N_HEADS, LANES), F32),
        grid_spec=pltpu.PrefetchScalarGridSpec(
            num_scalar_prefetch=1,
            grid=(nb // g,),
            in_specs=[
                pl.BlockSpec((g, N_HEADS, LANES), per_s),
                pl.BlockSpec((g, 1, n_keys), per_s),
                pl.BlockSpec((g, N_HEADS, 1), per_s),
                pl.BlockSpec((g, N_HEADS, 1), per_s),
                pl.BlockSpec((g, 1, LANES), per_s),
                pl.BlockSpec(memory_space=pl.ANY),
                pl.BlockSpec(memory_space=pl.ANY),
            ],
            out_specs=pl.BlockSpec((g, N_HEADS, LANES), per_s),
            scratch_shapes=[
                pltpu.VMEM((2, g, LANES, n_keys), F32),
                pltpu.VMEM((2, g, LANES, n_keys), F32),
                pltpu.SemaphoreType.DMA((2,)),
                pltpu.SemaphoreType.DMA((2,)),
            ],
        ),
        compiler_params=pltpu.CompilerParams(
            dimension_semantics=("arbitrary",), vmem_limit_bytes=VMEM_LIMIT),
        name="attn_sample",
    )(page_table, qpad3, mask3, lnew3, mnew3, vnew3, k_t, v_t)


def _out_proj_sample_kernel(x_ref, a_ref, sga_ref, p_ref, ada_ref, w_ref, o_ref):
    lo = _lane_iota((x_ref.shape[0], LANES)) < HEAD_DIM
    pairs = []
    for hp in range(N_HEADS // 2):
        even = a_ref[:, (2 * hp) * LANES:(2 * hp + 1) * LANES]
        odd = a_ref[:, (2 * hp + 1) * LANES:(2 * hp + 2) * LANES]
        if (2 * hp) // (N_HEADS // KV_HEADS) == 0:
            pairs.append(jnp.where(lo, even, pltpu.roll(odd, HEAD_DIM, axis=1)))
        else:
            pairs.append(jnp.where(lo, pltpu.roll(even, HEAD_DIM, axis=1), odd))
    a = jnp.concatenate(pairs, axis=1)
    ag = (a * sga_ref[...].astype(F32)).astype(BF16)
    y = _dot(ag, w_ref[0:ATTN_W, :]) + _dot(p_ref[...], w_ref[ATTN_W:ATTN_W + POOL_W, :])
    o_ref[...] = x_ref[...] + ada_ref[:, 2 * D_MODEL:3 * D_MODEL] * y


def _out_proj_sample(x, a, sga, pg, ada_s, w_out_b):
    return pl.pallas_call(
        _out_proj_sample_kernel,
        out_shape=jax.ShapeDtypeStruct(x.shape, F32),
        compiler_params=pltpu.CompilerParams(vmem_limit_bytes=VMEM_LIMIT),
        name="out_proj_sample",
    )(x, a, sga, pg, ada_s, w_out_b)


def _permute_w_in(w_in):
    q, k, v, qi, ki, wi, ga, u, gp = jnp.split(
        w_in, np.cumsum([512, 128, 128, 512, 64, 8, 512, 512]).tolist(), axis=1)
    pad = jnp.zeros((w_in.shape[0], LANES - IDX_DIM - IDX_HEADS), w_in.dtype)
    return jnp.concatenate([q, k, v, qi, ki, wi, pad, ga, u, gp], axis=1).astype(BF16)


def kernel(x_prompt, x_sample, cache_k, cache_v, cache_kidx, state_pool, page_table, c_prompt, c_sample,
           norm_w, w_ada, b_ada, w_in, q_norm_w, k_norm_w, w_pool, pool_scale, w_out):
    bp, s, _ = x_prompt.shape
    bs = x_sample.shape[0]
    assert w_in.shape[0] == 1 and x_sample.shape[1] == 1, "single layer, single decode token"
    n_phys = cache_k.shape[1]

    w_in_b = _permute_w_in(w_in[0])
    w_out_b = w_out[0].astype(BF16)
    zero_blk = jnp.zeros((POOL_GC, POOL_GC), w_pool.dtype)
    wpool_b = jnp.stack([jnp.block([[w_pool[0, 2 * p], zero_blk], [zero_blk, w_pool[0, 2 * p + 1]]])
                         for p in range(len(POOL_WINDOWS) // 2)]).astype(BF16)
    qnw = jnp.tile(q_norm_w[0], N_HEADS)[None, :] * (HEAD_DIM ** -0.5)
    knw = jnp.tile(k_norm_w[0], KV_HEADS)[None, :]
    nw = norm_w[0][None, :]
    pscale = pool_scale[0][None, :]

    ada_p, ada_s = _ada(c_prompt, c_sample, w_ada, b_ada)
    ada_p = ada_p.reshape(bp, 1, 3 * D_MODEL)

    (qs, k_t_p, kk, v_t_p, vtx, qi, ki_t_p, kik, wit, sga, pg, ulast) = _proj_prompt(
        x_prompt, ada_p, nw, w_in_b, qnw, knw, wpool_b, pscale)
    ag = _attn_prompt(qi, wit, qs, sga, kik, kk, vtx, bp, s)
    y_prompt = _out_proj_prompt(x_prompt, ag, pg, ada_p, w_out_b)

    hist_t = jnp.transpose(state_pool[0], (1, 0, 2))
    (qpad, k_s, v_s, ki_s, qi_s, wi_s, snew, lnew, sga_s, pg_s, pool_s) = _proj_sample(
        x_sample[:, 0, :], ada_s, nw, w_in_b, qnw, knw, wpool_b, pscale, hist_t)
    n_pages = page_table.shape[1]
    kidx_t = jnp.transpose(cache_kidx[0], (0, 2, 1))
    k_t = jnp.transpose(cache_k[0], (0, 2, 3, 1)).reshape(n_phys, LANES, PAGE)
    v_t = jnp.transpose(cache_v[0], (0, 2, 3, 1)).reshape(n_phys, LANES, PAGE)
    scores = _score_sample(page_table, qi_s.reshape(bs, IDX_HEADS, IDX_DIM), wi_s.reshape(bs, IDX_HEADS, 1),
                           kidx_t)
    mask, mnew = _select_sample(scores, snew)
    o_s = _attn_sample(
        page_table, qpad.reshape(bs, N_HEADS, LANES),
        mask.reshape(bs, 1, n_pages * PAGE),
        lnew[:, :N_HEADS].reshape(bs, N_HEADS, 1),
        jnp.broadcast_to(mnew[:, :1], (bs, N_HEADS)).reshape(bs, N_HEADS, 1),
        v_s.reshape(bs, 1, LANES), k_t, v_t)
    y_sample = _out_proj_sample(x_sample[:, 0, :], o_s.reshape(bs, N_HEADS * LANES), sga_s, pg_s, ada_s, w_out_b)

    to_heads = lambda a: jnp.transpose(a.reshape(bp, KV_HEADS, HEAD_DIM, s), (0, 3, 1, 2))[None]
    return (
        y_prompt,
        y_sample[:, None, :],
        to_heads(k_t_p),
        to_heads(v_t_p),
        jnp.transpose(ki_t_p, (0, 2, 1))[None],
        ulast[:, 1:, :][None],
        k_s.reshape(1, bs, 1, KV_HEADS, HEAD_DIM),
        v_s.reshape(1, bs, 1, KV_HEADS, HEAD_DIM),
        ki_s.reshape(1, bs, 1, IDX_DIM),
        jnp.transpose(pool_s, (1, 0, 2))[None],
    )
```

```python
import functools

import jax
import jax.numpy as jnp
import numpy as np
from jax import lax
from jax.experimental import pallas as pl
from jax.experimental.pallas import tpu as pltpu

F32 = jnp.float32
BF16 = jnp.bfloat16
I32 = jnp.int32

D_MODEL = 1024
ATTN_W = 512
POOL_W = 512
HEAD_DIM = 64
N_HEADS = 8
KV_HEADS = 2
IDX_HEADS = 8
IDX_DIM = 64
TOPK = 256
POOL_WINDOWS = (2, 4, 8, 16)
POOL_GC = 128
POOL_HIST = 15
EPS = 1e-6
PAGE = 128

LANES = 128
SUBLANES = 8
BF16_ROWS = 16
KEY_CHUNK = 256
TQ = 128
TM = 512
TM_OUT = 1024
VT_ROWS = HEAD_DIM + BF16_ROWS
SCORE_GROUP = 8
ATTN_GROUP = 4
INT_MIN = np.int32(-2 ** 31)
VMEM_LIMIT = 56 * 1024 * 1024

C_Q, C_K, C_V, C_QI, C_KW, C_GA, C_U, C_GP, N_PROJ = 0, 512, 640, 768, 1280, 1408, 1920, 2432, 2944


def _nt_dot(a, b):
    return lax.dot_general(a, b, (((1,), (1,)), ((), ())), preferred_element_type=F32)


def _dot(a, b):
    return jnp.dot(a, b, preferred_element_type=F32)


def _silu(z):
    return z / (1.0 + jnp.exp(-z))


def _lane_iota(shape):
    return lax.broadcasted_iota(I32, shape, len(shape) - 1)


def _seg_ones(n, seg):
    r = lax.broadcasted_iota(I32, (n, n), 0) // seg
    c = lax.broadcasted_iota(I32, (n, n), 1) // seg
    return jnp.where(r == c, 1.0, 0.0).astype(BF16)


def _head_rms(z, seg_mat, w):
    n = seg_mat.shape[0]
    sq = (z * z).astype(BF16)
    ss = jnp.concatenate([_dot(sq[:, j:j + n], seg_mat) for j in range(0, z.shape[1], n)], axis=1)
    return z * lax.rsqrt(ss * (1.0 / HEAD_DIM) + EPS) * w


def _float_of_rank(u):
    key = u ^ INT_MIN
    bits = key ^ ((key >> 31) & np.int32(0x7FFFFFFF))
    return lax.bitcast_convert_type(bits, F32)


def _kth_largest(count_ge, shape):
    def bit_body(it, ans):
        cand = ans | jnp.left_shift(jnp.int32(1), 31 - it)
        return jnp.where(count_ge(_float_of_rank(cand)) >= float(TOPK), cand, ans)

    ans = lax.fori_loop(0, 32, bit_body, jnp.zeros(shape, I32))
    return jnp.where(ans == 0, -jnp.inf, _float_of_rank(ans))


def _fold_rows(x, op):
    parts = [x[r:r + SUBLANES] for r in range(0, x.shape[0], SUBLANES)]
    while len(parts) > 1:
        parts = [op(parts[i], parts[i + 1]) for i in range(0, len(parts), 2)]
    return parts[0]


def _ada_kernel(cp_ref, cs_ref, w_ref, b_ref, op_ref, os_ref):
    w = w_ref[0].astype(BF16)
    op_ref[...] = _dot(_silu(cp_ref[...]).astype(BF16), w) + b_ref[...]
    os_ref[...] = _dot(_silu(cs_ref[...]).astype(BF16), w) + b_ref[...]


def _ada(c_prompt, c_sample, w_ada, b_ada):
    bp, bs = c_prompt.shape[0], c_sample.shape[0]
    return pl.pallas_call(
        _ada_kernel,
        out_shape=(jax.ShapeDtypeStruct((bp, 3 * D_MODEL), F32), jax.ShapeDtypeStruct((bs, 3 * D_MODEL), F32)),
        grid=(3,),
        in_specs=[
            pl.BlockSpec((bp, D_MODEL), lambda j: (0, 0)),
            pl.BlockSpec((bs, D_MODEL), lambda j: (0, 0)),
            pl.BlockSpec((1, D_MODEL, D_MODEL), lambda j: (0, 0, j)),
            pl.BlockSpec((1, D_MODEL), lambda j: (0, j)),
        ],
        out_specs=(pl.BlockSpec((bp, D_MODEL), lambda j: (0, j)), pl.BlockSpec((bs, D_MODEL), lambda j: (0, j))),
        compiler_params=pltpu.CompilerParams(dimension_semantics=("arbitrary",)),
        name="ada_ln",
    )(c_prompt, c_sample, w_ada, b_ada)


def _modulated_norm(x, norm_w, scale, shift):
    ms = jnp.mean(x * x, axis=-1, keepdims=True)
    return (x * lax.rsqrt(ms + EPS)) * norm_w * (1.0 + scale) + shift


def _pool_mix(d, wpool_ref, pscale, gp):
    db = d.astype(BF16)
    wide = 2 * POOL_GC
    y = jnp.concatenate([_dot(db[:, p * wide:(p + 1) * wide], wpool_ref[p])
                         for p in range(len(POOL_WINDOWS) // 2)], axis=1)
    return y * pscale * _silu(gp)


def _proj_prompt_kernel(x_ref, ada_ref, nw_ref, w_ref, qnw_ref, knw_ref, wpool_ref, ps_ref,
                        qs_ref, kt_ref, kk_ref, vt_ref, vtx_ref, qi_ref, kit_ref, kik_ref,
                        wit_ref, sga_ref, pg_ref, ulast_ref,
                        ext_ref):
    t = pl.program_id(1)
    ada = ada_ref[0]
    shift = ada[:, 0:D_MODEL]
    scale = ada[:, D_MODEL:2 * D_MODEL]
    sub_rows = KEY_CHUNK
    lo = _lane_iota((sub_rows, LANES)) < HEAD_DIM
    seg256 = _seg_ones(2 * LANES, HEAD_DIM)
    seg128 = _seg_ones(LANES, HEAD_DIM)
    ones = jnp.ones((BF16_ROWS, KEY_CHUNK), BF16)

    @pl.when(t == 0)
    def _():
        ext_ref[0:16, :] = jnp.zeros((16, POOL_W), F32)

    for sub in range(TM // sub_rows):
        rows = slice(sub * sub_rows, (sub + 1) * sub_rows)
        hb = _modulated_norm(x_ref[0, rows, :], nw_ref[...], scale, shift).astype(BF16)

        q = _dot(hb, w_ref[:, C_Q:C_Q + ATTN_W])
        qs_ref[rows, :] = _head_rms(q, seg256, qnw_ref[...]).astype(BF16)

        kv = _dot(hb, w_ref[:, C_K:C_K + 2 * LANES])
        k = _head_rms(kv[:, 0:LANES], seg128, knw_ref[...])
        kt_ref[0, :, rows] = k.T
        k_sw = pltpu.roll(k, HEAD_DIM, axis=1)
        kk_ref[0, rows, :] = jnp.where(lo, k, k_sw).astype(BF16)
        kk_ref[1, rows, :] = jnp.where(lo, k_sw, k).astype(BF16)

        v_t = kv[:, LANES:2 * LANES].T
        vt_ref[0, :, rows] = v_t
        for g in range(KV_HEADS):
            vtx_ref[0, g, sub, 0:HEAD_DIM, :] = v_t[g * HEAD_DIM:(g + 1) * HEAD_DIM].astype(BF16)
            vtx_ref[0, g, sub, HEAD_DIM:VT_ROWS, :] = ones

        qi_ref[rows, :] = _dot(hb, w_ref[:, C_QI:C_QI + ATTN_W]).astype(BF16)
        kw = _dot(hb, w_ref[:, C_KW:C_KW + LANES])
        kw_t = kw.T
        kit_ref[0, :, rows] = kw_t[0:IDX_DIM]
        wi_t = kw_t[IDX_DIM:IDX_DIM + IDX_HEADS] * ((IDX_HEADS * IDX_DIM) ** -0.5)
        for j in range(sub_rows // TQ):
            wit_ref[sub * (sub_rows // TQ) + j] = wi_t[:, j * TQ:(j + 1) * TQ]
        kik_ref[rows, :] = jnp.where(lo, kw, pltpu.roll(kw, HEAD_DIM, axis=1)).astype(BF16)

        sga_ref[rows, :] = _silu(_dot(hb, w_ref[:, C_GA:C_GA + ATTN_W])).astype(BF16)

        u = _dot(hb, w_ref[:, C_U:C_U + POOL_W])
        gp = _dot(hb, w_ref[:, C_GP:C_GP + POOL_W])
        base = 16 + sub * sub_rows
        ext_ref[base:base + sub_rows, :] = u
        pos = t * TM + sub * sub_rows + lax.broadcasted_iota(I32, (sub_rows, POOL_GC), 0)
        ds = []
        for g, w in enumerate(POOL_WINDOWS):
            cs = slice(g * POOL_GC, (g + 1) * POOL_GC)
            s = u[:, cs]
            for j in range(1, w):
                s = s + ext_ref[base - j:base - j + sub_rows, cs]
            cnt = jnp.minimum(pos + 1, w).astype(F32)
            ds.append(s / cnt - u[:, cs])
        d = jnp.concatenate(ds, axis=1)
        pg_ref[rows, :] = _pool_mix(d, wpool_ref, ps_ref[...], gp).astype(BF16)

    tail = ext_ref[TM:TM + 16, :]
    ulast_ref[0] = tail
    ext_ref[0:16, :] = tail


def _proj_prompt(x, ada_p, norm_w, w_in_b, qnw, knw, wpool_b, pscale):
    b, s, _ = x.shape
    n = b * s
    nt = s // TM
    cpt = TM // KEY_CHUNK
    row = lambda bi, ti: (bi * nt + ti, 0)
    tok = lambda bi, ti: (bi, 0, ti)
    const2 = lambda bi, ti: (0, 0)
    const3 = lambda bi, ti: (0, 0, 0)
    out_shape = (
        jax.ShapeDtypeStruct((n, ATTN_W), BF16),
        jax.ShapeDtypeStruct((b, LANES, s), F32),
        jax.ShapeDtypeStruct((KV_HEADS, n, LANES), BF16),
        jax.ShapeDtypeStruct((b, LANES, s), F32),
        jax.ShapeDtypeStruct((b, KV_HEADS, s // KEY_CHUNK, VT_ROWS, KEY_CHUNK), BF16),
        jax.ShapeDtypeStruct((n, ATTN_W), BF16),
        jax.ShapeDtypeStruct((b, IDX_DIM, s), F32),
        jax.ShapeDtypeStruct((n, LANES), BF16),
        jax.ShapeDtypeStruct((n // TQ, IDX_HEADS, TQ), F32),
        jax.ShapeDtypeStruct((n, ATTN_W), BF16),
        jax.ShapeDtypeStruct((n, POOL_W), BF16),
        jax.ShapeDtypeStruct((b, 16, POOL_W), F32),
    )
    out_specs = (
        pl.BlockSpec((TM, ATTN_W), row),
        pl.BlockSpec((1, LANES, TM), tok),
        pl.BlockSpec((KV_HEADS, TM, LANES), lambda bi, ti: (0, bi * nt + ti, 0)),
        pl.BlockSpec((1, LANES, TM), tok),
        pl.BlockSpec((1, KV_HEADS, cpt, VT_ROWS, KEY_CHUNK), lambda bi, ti: (bi, 0, ti, 0, 0)),
        pl.BlockSpec((TM, ATTN_W), row),
        pl.BlockSpec((1, IDX_DIM, TM), tok),
        pl.BlockSpec((TM, LANES), row),
        pl.BlockSpec((TM // TQ, IDX_HEADS, TQ), lambda bi, ti: (bi * nt + ti, 0, 0)),
        pl.BlockSpec((TM, ATTN_W), row),
        pl.BlockSpec((TM, POOL_W), row),
        pl.BlockSpec((1, 16, POOL_W), lambda bi, ti: (bi, 0, 0)),
    )
    in_specs = [
        pl.BlockSpec((1, TM, D_MODEL), lambda bi, ti: (bi, ti, 0)),
        pl.BlockSpec((1, 1, 3 * D_MODEL), lambda bi, ti: (bi, 0, 0)),
        pl.BlockSpec((1, D_MODEL), const2),
        pl.BlockSpec((D_MODEL, N_PROJ), const2),
        pl.BlockSpec((1, ATTN_W), const2),
        pl.BlockSpec((1, LANES), const2),
        pl.BlockSpec((2, 2 * POOL_GC, 2 * POOL_GC), const3),
        pl.BlockSpec((1, POOL_W), const2),
    ]
    return pl.pallas_call(
        _proj_prompt_kernel,
        out_shape=out_shape,
        grid=(b, nt),
        in_specs=in_specs,
        out_specs=out_specs,
        scratch_shapes=[pltpu.VMEM((16 + TM, POOL_W), F32)],
        compiler_params=pltpu.CompilerParams(
            dimension_semantics=("arbitrary", "arbitrary"), vmem_limit_bytes=VMEM_LIMIT),
        name="proj_prompt",
    )(x, ada_p, norm_w, w_in_b, qnw, knw, wpool_b, pscale)


def _attn_tile(nch, row0, wit, kik_ref, kk_ref, vtx_ref, sc_ref, lg_ref, wq_ref, acc_ref):
    shape2 = (KEY_CHUNK, TQ)
    n_pairs = N_HEADS // 2
    group_of = lambda hp: (2 * hp) // (N_HEADS // KV_HEADS)

    for c in range(nch):
        keys = slice(c * KEY_CHUNK, (c + 1) * KEY_CHUNK)
        kk = kik_ref[keys, :]
        acc = jnp.zeros(shape2, F32)
        for hp in range(n_pairs):
            s2 = _nt_dot(kk, wq_ref[0, hp])
            acc = acc + jnp.maximum(s2[:, 0:TQ], 0.0) * wit[2 * hp:2 * hp + 1, :]
            acc = acc + jnp.maximum(s2[:, TQ:2 * TQ], 0.0) * wit[2 * hp + 1:2 * hp + 2, :]
        if c == nch - 1:
            kpos = c * KEY_CHUNK + lax.broadcasted_iota(I32, shape2, 0)
            qpos = row0 + lax.broadcasted_iota(I32, shape2, 1)
            acc = jnp.where(kpos <= qpos, acc, -jnp.inf)
        sc_ref[c] = acc
        for hp in range(n_pairs):
            lg_ref[hp, c] = _nt_dot(kk_ref[group_of(hp), keys, :], wq_ref[1, hp])

    def count_pass(pred):
        parts = [_fold_rows(pred(sc_ref[c]).astype(F32), jnp.add) for c in range(nch)]
        while len(parts) > 1:
            parts = [sum(parts[j:j + 2]) for j in range(0, len(parts), 2)]
        return jnp.sum(parts[0], axis=0, keepdims=True)

    if nch * KEY_CHUNK <= TOPK:
        thr = jnp.full((1, TQ), -jnp.inf, F32)
        need = jnp.zeros((1, TQ), F32)
    else:
        thr = _kth_largest(lambda t: count_pass(lambda x: x >= t), (1, TQ))
        need = float(TOPK) - count_pass(lambda x: x > thr)

    tri = jnp.where(lax.broadcasted_iota(I32, (KEY_CHUNK, KEY_CHUNK), 1)
                    <= lax.broadcasted_iota(I32, (KEY_CHUNK, KEY_CHUNK), 0), 1.0, 0.0).astype(BF16)

    m_part = [jnp.full((SUBLANES, TQ), -jnp.inf, F32) for _ in range(N_HEADS)]
    eq_before = jnp.zeros((1, TQ), F32)
    for c in range(nch):
        sc = sc_ref[c]
        eq = sc == thr
        rank = _dot(tri, jnp.where(eq, 1.0, 0.0).astype(BF16)) + eq_before
        sel = (sc > thr) | (eq & (rank <= need))
        if c == nch - 1:
            sel = sel & (sc > -jnp.inf)
        eq_before = rank[KEY_CHUNK - 1:KEY_CHUNK, :]
        for hp in range(n_pairs):
            for par in range(2):
                cols = slice(par * TQ, (par + 1) * TQ)
                lg = jnp.where(sel, lg_ref[hp, c, :, cols], -jnp.inf)
                lg_ref[hp, c, :, cols] = lg
                m_part[2 * hp + par] = jnp.maximum(m_part[2 * hp + par], _fold_rows(lg, jnp.maximum))
    m_rows = [jnp.max(m, axis=0, keepdims=True) for m in m_part]

    for hp in range(n_pairs):
        m2 = jnp.concatenate([m_rows[2 * hp], m_rows[2 * hp + 1]], axis=1)
        acc = jnp.zeros((VT_ROWS, 2 * TQ), F32)
        for c in range(nch):
            p2 = jnp.exp(lg_ref[hp, c] - m2).astype(BF16)
            acc = acc + _dot(vtx_ref[0, group_of(hp), c], p2)
        acc_ref[hp] = acc


def _attn_prompt_kernel(qi_ref, wit_ref, qs_ref, sga_ref, kik_ref, kk_ref, vtx_ref, ag_ref,
                        sc_ref, lg_ref, wq_ref, acc_ref):
    step = pl.program_id(1)
    n_pairs = N_HEADS // 2
    tiles_per_step = KEY_CHUNK // TQ
    nch = step + 1
    lo = _lane_iota((TQ, LANES)) < HEAD_DIM

    def one_tile(sub, carry):
        rows = pl.ds(pl.multiple_of(sub * TQ, TQ), TQ)
        for kind, ref in enumerate((qi_ref, qs_ref)):
            for hp in range(n_pairs):
                pair = ref[rows, hp * LANES:(hp + 1) * LANES].astype(F32)
                wq_ref[kind, hp, 0:TQ, :] = jnp.where(lo, pair, 0.0).astype(BF16)
                wq_ref[kind, hp, TQ:2 * TQ, :] = jnp.where(lo, 0.0, pair).astype(BF16)

        wit = wit_ref[sub]
        row0 = (step * tiles_per_step + sub) * TQ
        for n_static in range(1, sc_ref.shape[0] + 1):
            @pl.when(nch == n_static)
            def _(n_static=n_static):
                _attn_tile(n_static, row0, wit, kik_ref, kk_ref, vtx_ref, sc_ref, lg_ref, wq_ref, acc_ref)

        for hp in range(n_pairs):
            a = acc_ref[hp]
            o0 = a[0:HEAD_DIM, 0:TQ] / a[HEAD_DIM:HEAD_DIM + 1, 0:TQ]
            o1 = a[0:HEAD_DIM, TQ:2 * TQ] / a[HEAD_DIM:HEAD_DIM + 1, TQ:2 * TQ]
            pair = jnp.concatenate([o0, o1], axis=0).T
            cols = slice(hp * LANES, (hp + 1) * LANES)
            ag_ref[rows, cols] = (pair * sga_ref[rows, cols].astype(F32)).astype(BF16)
        return carry

    lax.fori_loop(0, tiles_per_step, one_tile, 0)


def _attn_prompt(qi, wit, qs, sga, kik, kk, vtx, b, s):
    n = b * s
    nkc = s // KEY_CHUNK
    tps = KEY_CHUNK // TQ
    row = lambda bi, st: (bi * nkc + st, 0)
    return pl.pallas_call(
        _attn_prompt_kernel,
        out_shape=jax.ShapeDtypeStruct((n, ATTN_W), BF16),
        grid=(b, nkc),
        in_specs=[
            pl.BlockSpec((KEY_CHUNK, ATTN_W), row),
            pl.BlockSpec((tps, IDX_HEADS, TQ), lambda bi, st: (bi * nkc + st, 0, 0)),
            pl.BlockSpec((KEY_CHUNK, ATTN_W), row),
            pl.BlockSpec((KEY_CHUNK, ATTN_W), row),
            pl.BlockSpec((s, LANES), lambda bi, st: (bi, 0)),
            pl.BlockSpec((KV_HEADS, s, LANES), lambda bi, st: (0, bi, 0)),
            pl.BlockSpec((1, KV_HEADS, nkc, VT_ROWS, KEY_CHUNK), lambda bi, st: (bi, 0, 0, 0, 0)),
        ],
        out_specs=pl.BlockSpec((KEY_CHUNK, ATTN_W), row),
        scratch_shapes=[
            pltpu.VMEM((nkc, KEY_CHUNK, TQ), F32),
            pltpu.VMEM((N_HEADS // 2, nkc, KEY_CHUNK, 2 * TQ), F32),
            pltpu.VMEM((2, N_HEADS // 2, 2 * TQ, LANES), BF16),
            pltpu.VMEM((N_HEADS // 2, VT_ROWS, 2 * TQ), F32),
        ],
        compiler_params=pltpu.CompilerParams(
            dimension_semantics=("arbitrary", "arbitrary"), vmem_limit_bytes=VMEM_LIMIT),
        name="attn_prompt",
    )(qi, wit, qs, sga, kik, kk, vtx)


def _out_proj_kernel(x_ref, a_ref, p_ref, gate_ref, w_ref, o_ref):
    y = _dot(a_ref[...], w_ref[0:ATTN_W, :]) + _dot(p_ref[...], w_ref[ATTN_W:ATTN_W + POOL_W, :])
    o_ref[0] = x_ref[0] + gate_ref[0] * y


def _out_proj_prompt(x, ag, pg, ada_p, w_out_b):
    b, s, _ = x.shape
    nt = s // TM_OUT
    row = lambda bi, ti: (bi * nt + ti, 0)
    return pl.pallas_call(
        _out_proj_kernel,
        out_shape=jax.ShapeDtypeStruct(x.shape, F32),
        grid=(b, nt),
        in_specs=[
            pl.BlockSpec((1, TM_OUT, D_MODEL), lambda bi, ti: (bi, ti, 0)),
            pl.BlockSpec((TM_OUT, ATTN_W), row),
            pl.BlockSpec((TM_OUT, POOL_W), row),
            pl.BlockSpec((1, 1, D_MODEL), lambda bi, ti: (bi, 0, 2)),
            pl.BlockSpec((D_MODEL, D_MODEL), lambda bi, ti: (0, 0)),
        ],
        out_specs=pl.BlockSpec((1, TM_OUT, D_MODEL), lambda bi, ti: (bi, ti, 0)),
        compiler_params=pltpu.CompilerParams(
            dimension_semantics=("arbitrary", "arbitrary"), vmem_limit_bytes=VMEM_LIMIT),
        name="out_proj_prompt",
    )(x, ag, pg, ada_p, w_out_b)


def _proj_sample_kernel(x_ref, ada_ref, nw_ref, w_ref, qnw_ref, knw_ref, wpool_ref, ps_ref, hist_ref,
                        qpad_ref, k_ref, v_ref, ki_ref, qi_ref, wi_ref, snew_ref, lnew_ref,
                        sga_ref, pg_ref, pool_ref):
    nb = x_ref.shape[0]
    x = x_ref[...]
    shift = ada_ref[:, 0:D_MODEL]
    scale = ada_ref[:, D_MODEL:2 * D_MODEL]
    hb = _modulated_norm(x, nw_ref[...], scale, shift).astype(BF16)

    lane = _lane_iota((nb, LANES))
    lo = lane < HEAD_DIM
    seg256 = _seg_ones(2 * LANES, HEAD_DIM)
    seg128 = _seg_ones(LANES, HEAD_DIM)
    head_sel = jnp.where(lax.broadcasted_iota(I32, (ATTN_W, LANES), 0) // HEAD_DIM
                         == lax.broadcasted_iota(I32, (ATTN_W, LANES), 1), 1.0, 0.0).astype(BF16)

    def head_sums(prod):
        hi = prod.astype(BF16)
        rest = (prod - hi.astype(F32)).astype(BF16)
        return _dot(hi, head_sel) + _dot(rest, head_sel)

    q = _head_rms(_dot(hb, w_ref[:, C_Q:C_Q + ATTN_W]), seg256, qnw_ref[...])
    qb = q.astype(BF16)
    k = _head_rms(_dot(hb, w_ref[:, C_K:C_K + LANES]), seg128, knw_ref[...])
    k_ref[...] = k
    v = _dot(hb, w_ref[:, C_V:C_V + LANES])
    v_ref[...] = v

    for hp in range(N_HEADS // 2):
        pair = q[:, hp * LANES:(hp + 1) * LANES]
        pair_sw = pltpu.roll(pair, HEAD_DIM, axis=1)
        g = (2 * hp) // (N_HEADS // KV_HEADS)
        if g == 0:
            h_even, h_odd = jnp.where(lo, pair, 0.0), jnp.where(lo, pair_sw, 0.0)
        else:
            h_even, h_odd = jnp.where(lo, 0.0, pair_sw), jnp.where(lo, 0.0, pair)
        qpad_ref[:, (2 * hp) * LANES:(2 * hp + 1) * LANES] = h_even.astype(BF16)
        qpad_ref[:, (2 * hp + 1) * LANES:(2 * hp + 2) * LANES] = h_odd.astype(BF16)

    kq = k.astype(BF16).astype(F32)
    kq_sw = pltpu.roll(kq, HEAD_DIM, axis=1)
    k0t = jnp.where(lo, kq, kq_sw)
    k1t = jnp.where(lo, kq_sw, kq)
    qf = qb.astype(F32)
    prod = jnp.concatenate([qf[:, 0:LANES] * k0t, qf[:, LANES:2 * LANES] * k0t,
                            qf[:, 2 * LANES:3 * LANES] * k1t, qf[:, 3 * LANES:4 * LANES] * k1t], axis=1)
    lnew_ref[...] = head_sums(prod)

    qi = _dot(hb, w_ref[:, C_QI:C_QI + ATTN_W])
    qib = qi.astype(BF16)
    qi_ref[...] = qib
    kw = _dot(hb, w_ref[:, C_KW:C_KW + LANES])
    kw_sw = pltpu.roll(kw, HEAD_DIM, axis=1)
    ki_ref[...] = kw[:, 0:IDX_DIM]
    wi_full = jnp.where(lane < IDX_HEADS, kw_sw, 0.0) * ((IDX_HEADS * IDX_DIM) ** -0.5)
    wi_ref[...] = wi_full[:, 0:IDX_HEADS]

    kib = kw.astype(BF16).astype(F32)
    kit = jnp.where(lo, kib, pltpu.roll(kib, HEAD_DIM, axis=1))
    qif = qib.astype(F32)
    prod_i = jnp.concatenate([qif[:, j * LANES:(j + 1) * LANES] * kit for j in range(4)], axis=1)
    s_new = jnp.maximum(head_sums(prod_i), 0.0) * wi_full
    snew_ref[...] = jnp.broadcast_to(jnp.sum(s_new, axis=1, keepdims=True), (nb, LANES))

    sga_ref[...] = _silu(_dot(hb, w_ref[:, C_GA:C_GA + ATTN_W])).astype(BF16)

    u = _dot(hb, w_ref[:, C_U:C_U + POOL_W])
    gp = _dot(hb, w_ref[:, C_GP:C_GP + POOL_W])
    for j in range(POOL_HIST - 1):
        pool_ref[j] = hist_ref[j + 1]
    pool_ref[POOL_HIST - 1] = u
    ds = []
    for g, w in enumerate(POOL_WINDOWS):
        cs = slice(g * POOL_GC, (g + 1) * POOL_GC)
        s = u[:, cs]
        for j in range(1, w):
            s = s + hist_ref[POOL_HIST - j, :, cs]
        ds.append(s / float(w) - u[:, cs])
    d = jnp.concatenate(ds, axis=1)
    pg_ref[...] = _pool_mix(d, wpool_ref, ps_ref[...], gp).astype(BF16)


def _proj_sample(x, ada_s, norm_w, w_in_b, qnw, knw, wpool_b, pscale, hist_t):
    nb = x.shape[0]
    out_shape = (
        jax.ShapeDtypeStruct((nb, N_HEADS * LANES), BF16),
        jax.ShapeDtypeStruct((nb, LANES), F32),
        jax.ShapeDtypeStruct((nb, LANES), F32),
        jax.ShapeDtypeStruct((nb, IDX_DIM), F32),
        jax.ShapeDtypeStruct((nb, ATTN_W), BF16),
        jax.ShapeDtypeStruct((nb, IDX_HEADS), F32),
        jax.ShapeDtypeStruct((nb, LANES), F32),
        jax.ShapeDtypeStruct((nb, LANES), F32),
        jax.ShapeDtypeStruct((nb, ATTN_W), BF16),
        jax.ShapeDtypeStruct((nb, POOL_W), BF16),
        jax.ShapeDtypeStruct((POOL_HIST, nb, POOL_W), F32),
    )
    return pl.pallas_call(
        _proj_sample_kernel,
        out_shape=out_shape,
        compiler_params=pltpu.CompilerParams(vmem_limit_bytes=VMEM_LIMIT),
        name="proj_sample",
    )(x, ada_s, norm_w, w_in_b, qnw, knw, wpool_b, pscale, hist_t)


def _page_copies(pt_ref, pages_hbm, buf_ref, sem_ref, step, slot, group, n_pages):
    copies = []
    for j in range(group):
        for p in range(n_pages):
            page = pt_ref[step * group + j, p]
            copies.append(pltpu.make_async_copy(
                pages_hbm.at[page], buf_ref.at[slot, j, :, pl.ds(p * PAGE, PAGE)], sem_ref.at[slot]))
    return copies


def _score_sample_kernel(pt_ref, qi_ref, wi_ref, kidx_hbm, o_ref, buf_ref, sem_ref, *, group, n_pages):
    step = pl.program_id(0)
    slot = lax.rem(step, 2)

    def copies(st, sl):
        return _page_copies(pt_ref, kidx_hbm, buf_ref, sem_ref, st, sl, group, n_pages)

    @pl.when(step == 0)
    def _():
        for cp in copies(step, slot):
            cp.start()

    @pl.when(step + 1 < pl.num_programs(0))
    def _():
        for cp in copies(step + 1, 1 - slot):
            cp.start()

    for cp in copies(step, slot):
        cp.wait()

    for j in range(group):
        ki_t = buf_ref[slot, j].astype(BF16)
        s = _dot(qi_ref[j], ki_t)
        o_ref[j:j + 1, :] = jnp.sum(jnp.maximum(s, 0.0) * wi_ref[j], axis=0, keepdims=True)


def _score_sample(page_table, qi3, wi3, kidx_t):
    nb, n_pages = page_table.shape
    n_keys = n_pages * PAGE
    g = SCORE_GROUP
    return pl.pallas_call(
        functools.partial(_score_sample_kernel, group=g, n_pages=n_pages),
        out_shape=jax.ShapeDtypeStruct((nb, n_keys), F32),
        grid_spec=pltpu.PrefetchScalarGridSpec(
            num_scalar_prefetch=1,
            grid=(nb // g,),
            in_specs=[
                pl.BlockSpec((g, IDX_HEADS, IDX_DIM), lambda s, pt: (s, 0, 0)),
                pl.BlockSpec((g, IDX_HEADS, 1), lambda s, pt: (s, 0, 0)),
                pl.BlockSpec(memory_space=pl.ANY),
            ],
            out_specs=pl.BlockSpec((g, n_keys), lambda s, pt: (s, 0)),
            scratch_shapes=[
                pltpu.VMEM((2, g, IDX_DIM, n_keys), F32),
                pltpu.SemaphoreType.DMA((2,)),
            ],
        ),
        compiler_params=pltpu.CompilerParams(
            dimension_semantics=("arbitrary",), vmem_limit_bytes=VMEM_LIMIT),
        name="score_sample",
    )(page_table, qi3, wi3, kidx_t)


def _select_sample_kernel(sc_ref, snew_ref, mask_ref, mnew_ref):
    nb, n_keys = sc_ref.shape
    nch = n_keys // KEY_CHUNK
    snew = snew_ref[...]
    ones_mat = jnp.ones((LANES, LANES), BF16)

    def count_pass(pred):
        acc = jnp.zeros((nb, LANES), F32)
        for c in range(n_keys // LANES):
            acc = acc + pred(sc_ref[:, c * LANES:(c + 1) * LANES]).astype(F32)
        return _dot(acc.astype(BF16), ones_mat) + pred(snew).astype(F32)

    thr = _kth_largest(lambda t: count_pass(lambda x: x >= t), (nb, LANES))
    need = float(TOPK) - count_pass(lambda x: x > thr)
    thr2 = jnp.concatenate([thr, thr], axis=1)
    need2 = jnp.concatenate([need, need], axis=1)
    tri = jnp.where(lax.broadcasted_iota(I32, (KEY_CHUNK, KEY_CHUNK), 0)
                    <= lax.broadcasted_iota(I32, (KEY_CHUNK, KEY_CHUNK), 1), 1.0, 0.0).astype(BF16)
    ones_cl = jnp.ones((KEY_CHUNK, LANES), BF16)

    eq_before = jnp.zeros((nb, LANES), F32)
    for c in range(nch):
        sc = sc_ref[:, c * KEY_CHUNK:(c + 1) * KEY_CHUNK]
        eq = sc == thr2
        eqb = jnp.where(eq, 1.0, 0.0).astype(BF16)
        rank = _dot(eqb, tri) + jnp.concatenate([eq_before, eq_before], axis=1)
        sel = (sc > thr2) | (eq & (rank <= need2))
        mask_ref[:, c * KEY_CHUNK:(c + 1) * KEY_CHUNK] = jnp.where(sel, 0.0, -jnp.inf)
        eq_before = eq_before + _dot(eqb, ones_cl)
    sel_new = (snew > thr) | ((snew == thr) & (eq_before + 1.0 <= need))
    mnew_ref[...] = jnp.where(sel_new, 0.0, -jnp.inf)


def _select_sample(scores, snew):
    nb, n_keys = scores.shape
    return pl.pallas_call(
        _select_sample_kernel,
        out_shape=(jax.ShapeDtypeStruct((nb, n_keys), F32), jax.ShapeDtypeStruct((nb, LANES), F32)),
        name="select_sample",
    )(scores, snew)


def _attn_sample_kernel(pt_ref, q_ref, mask_ref, lnew_ref, mnew_ref, vnew_ref, k_hbm, v_hbm,
                        o_ref, kbuf_ref, vbuf_ref, ksem_ref, vsem_ref, *, group, n_pages):
    step = pl.program_id(0)
    slot = lax.rem(step, 2)

    def copies(st, sl):
        return (_page_copies(pt_ref, k_hbm, kbuf_ref, ksem_ref, st, sl, group, n_pages)
                + _page_copies(pt_ref, v_hbm, vbuf_ref, vsem_ref, st, sl, group, n_pages))

    @pl.when(step == 0)
    def _():
        for cp in copies(step, slot):
            cp.start()

    @pl.when(step + 1 < pl.num_programs(0))
    def _():
        for cp in copies(step + 1, 1 - slot):
            cp.start()

    for cp in copies(step, slot):
        cp.wait()

    for j in range(group):
        k_t = kbuf_ref[slot, j].astype(BF16)
        lg = _dot(q_ref[j], k_t) + mask_ref[j]
        lg_n = lnew_ref[j] + mnew_ref[j]
        m = jnp.maximum(jnp.max(lg, axis=1, keepdims=True), lg_n)
        p = jnp.exp(lg - m)
        p_n = jnp.exp(lg_n - m)
        l = jnp.sum(p, axis=1, keepdims=True) + p_n
        acc = _nt_dot(p.astype(BF16), vbuf_ref[slot, j].astype(BF16)) + p_n * vnew_ref[j]
        o_ref[j] = acc / l


def _attn_sample(page_table, qpad3, mask3, lnew3, mnew3, vnew3, k_t, v_t):
    nb, n_pages = page_table.shape
    n_keys = n_pages * PAGE
    g = ATTN_GROUP
    per_s = lambda s, pt: (s, 0, 0)
    return pl.pallas_call(
        functools.partial(_attn_sample_kernel, group=g, n_pages=n_pages),
        out_shape=jax.ShapeDtypeStruct((nb, N_HEADS, LANES), F32),
        grid_spec=pltpu.PrefetchScalarGridSpec(
            num_scalar_prefetch=1,
            grid=(nb // g,),
            in_specs=[
                pl.BlockSpec((g, N_HEADS, LANES), per_s),
                pl.BlockSpec((g, 1, n_keys), per_s),
                pl.BlockSpec((g, N_HEADS, 1), per_s),
                pl.BlockSpec((g, N_HEADS, 1), per_s),
                pl.BlockSpec((g, 1, LANES), per_s),
                pl.BlockSpec(memory_space=pl.ANY),
                pl.BlockSpec(memory_space=pl.ANY),
            ],
            out_specs=pl.BlockSpec((g, N_HEADS, LANES), per_s),
            scratch_shapes=[
                pltpu.VMEM((2, g, LANES, n_keys), F32),
                pltpu.VMEM((2, g, LANES, n_keys), F32),
                pltpu.SemaphoreType.DMA((2,)),
                pltpu.SemaphoreType.DMA((2,)),
            ],
        ),
        compiler_params=pltpu.CompilerParams(
            dimension_semantics=("arbitrary",), vmem_limit_bytes=VMEM_LIMIT),
        name="attn_sample",
    )(page_table, qpad3, mask3, lnew3, mnew3, vnew3, k_t, v_t)


def _out_proj_sample_kernel(x_ref, a_ref, sga_ref, p_ref, ada_ref, w_ref, o_ref):
    lo = _lane_iota((x_ref.shape[0], LANES)) < HEAD_DIM
    pairs = []
    for hp in range(N_HEADS // 2):
        even = a_ref[:, (2 * hp) * LANES:(2 * hp + 1) * LANES]
        odd = a_ref[:, (2 * hp + 1) * LANES:(2 * hp + 2) * LANES]
        if (2 * hp) // (N_HEADS // KV_HEADS) == 0:
            pairs.append(jnp.where(lo, even, pltpu.roll(odd, HEAD_DIM, axis=1)))
        else:
            pairs.append(jnp.where(lo, pltpu.roll(even, HEAD_DIM, axis=1), odd))
    a = jnp.concatenate(pairs, axis=1)
    ag = (a * sga_ref[...].astype(F32)).astype(BF16)
    y = _dot(ag, w_ref[0:ATTN_W, :]) + _dot(p_ref[...], w_ref[ATTN_W:ATTN_W + POOL_W, :])
    o_ref[...] = x_ref[...] + ada_ref[:, 2 * D_MODEL:3 * D_MODEL] * y


def _out_proj_sample(x, a, sga, pg, ada_s, w_out_b):
    return pl.pallas_call(
        _out_proj_sample_kernel,
        out_shape=jax.ShapeDtypeStruct(x.shape, F32),
        compiler_params=pltpu.CompilerParams(vmem_limit_bytes=VMEM_LIMIT),
        name="out_proj_sample",
    )(x, a, sga, pg, ada_s, w_out_b)


def _permute_w_in(w_in):
    q, k, v, qi, ki, wi, ga, u, gp = jnp.split(
        w_in, np.cumsum([512, 128, 128, 512, 64, 8, 512, 512]).tolist(), axis=1)
    pad = jnp.zeros((w_in.shape[0], LANES - IDX_DIM - IDX_HEADS), w_in.dtype)
    return jnp.concatenate([q, k, v, qi, ki, wi, pad, ga, u, gp], axis=1).astype(BF16)


def kernel(x_prompt, x_sample, cache_k, cache_v, cache_kidx, state_pool, page_table, c_prompt, c_sample,
           norm_w, w_ada, b_ada, w_in, q_norm_w, k_norm_w, w_pool, pool_scale, w_out):
    bp, s, _ = x_prompt.shape
    bs = x_sample.shape[0]
    assert w_in.shape[0] == 1 and x_sample.shape[1] == 1, "single layer, single decode token"
    n_phys = cache_k.shape[1]

    w_in_b = _permute_w_in(w_in[0])
    w_out_b = w_out[0].astype(BF16)
    zero_blk = jnp.zeros((POOL_GC, POOL_GC), w_pool.dtype)
    wpool_b = jnp.stack([jnp.block([[w_pool[0, 2 * p], zero_blk], [zero_blk, w_pool[0, 2 * p + 1]]])
                         for p in range(len(POOL_WINDOWS) // 2)]).astype(BF16)
    qnw = jnp.tile(q_norm_w[0], N_HEADS)[None, :] * (HEAD_DIM ** -0.5)
    knw = jnp.tile(k_norm_w[0], KV_HEADS)[None, :]
    nw = norm_w[0][None, :]
    pscale = pool_scale[0][None, :]

    ada_p, ada_s = _ada(c_prompt, c_sample, w_ada, b_ada)
    ada_p = ada_p.reshape(bp, 1, 3 * D_MODEL)

    (qs, k_t_p, kk, v_t_p, vtx, qi, ki_t_p, kik, wit, sga, pg, ulast) = _proj_prompt(
        x_prompt, ada_p, nw, w_in_b, qnw, knw, wpool_b, pscale)
    ag = _attn_prompt(qi, wit, qs, sga, kik, kk, vtx, bp, s)
    y_prompt = _out_proj_prompt(x_prompt, ag, pg, ada_p, w_out_b)

    hist_t = jnp.transpose(state_pool[0], (1, 0, 2))
    (qpad, k_s, v_s, ki_s, qi_s, wi_s, snew, lnew, sga_s, pg_s, pool_s) = _proj_sample(
        x_sample[:, 0, :], ada_s, nw, w_in_b, qnw, knw, wpool_b, pscale, hist_t)
    n_pages = page_table.shape[1]
    kidx_t = jnp.transpose(cache_kidx[0], (0, 2, 1))
    k_t = jnp.transpose(cache_k[0], (0, 2, 3, 1)).reshape(n_phys, LANES, PAGE)
    v_t = jnp.transpose(cache_v[0], (0, 2, 3, 1)).reshape(n_phys, LANES, PAGE)
    scores = _score_sample(page_table, qi_s.reshape(bs, IDX_HEADS, IDX_DIM), wi_s.reshape(bs, IDX_HEADS, 1),
                           kidx_t)
    mask, mnew = _select_sample(scores, snew)
    o_s = _attn_sample(
        page_table, qpad.reshape(bs, N_HEADS, LANES),
        mask.reshape(bs, 1, n_pages * PAGE),
        lnew[:, :N_HEADS].reshape(bs, N_HEADS, 1),
        jnp.broadcast_to(mnew[:, :1], (bs, N_HEADS)).reshape(bs, N_HEADS, 1),
        v_s.reshape(bs, 1, LANES), k_t, v_t)
    y_sample = _out_proj_sample(x_sample[:, 0, :], o_s.reshape(bs, N_HEADS * LANES), sga_s, pg_s, ada_s, w_out_b)

    to_heads = lambda a: jnp.transpose(a.reshape(bp, KV_HEADS, HEAD_DIM, s), (0, 3, 1, 2))[None]
    return (
        y_prompt,
        y_sample[:, None, :],
        to_heads(k_t_p),
        to_heads(v_t_p),
        jnp.transpose(ki_t_p, (0, 2, 1))[None],
        ulast[:, 1:, :][None],
        k_s.reshape(1, bs, 1, KV_HEADS, HEAD_DIM),
        v_s.reshape(1, bs, 1, KV_HEADS, HEAD_DIM),
        ki_s.reshape(1, bs, 1, IDX_DIM),
        jnp.transpose(pool_s, (1, 0, 2))[None],
    )
```

```python
import functools

import jax
import jax.numpy as jnp
import numpy as np
from jax import lax
from jax.experimental import pallas as pl
from jax.experimental.pallas import tpu as pltpu

F32 = jnp.float32
BF16 = jnp.bfloat16
I32 = jnp.int32

D_MODEL = 1024
ATTN_W = 512
POOL_W = 512
HEAD_DIM = 64
N_HEADS = 8
KV_HEADS = 2
IDX_HEADS = 8
IDX_DIM = 64
TOPK = 256
POOL_WINDOWS = (2, 4, 8, 16)
POOL_GC = 128
POOL_HIST = 15
EPS = 1e-6
PAGE = 128

LANES = 128
SUBLANES = 8
BF16_ROWS = 16
KEY_CHUNK = 256
TQ = 128
TM = 512
TM_OUT = 1024
VT_ROWS = HEAD_DIM + BF16_ROWS
SEARCH_PEEL = 6
SCORE_GROUP = 8
ATTN_GROUP = 4
INT_MIN = np.int32(-2 ** 31)
VMEM_LIMIT = 56 * 1024 * 1024

C_Q, C_K, C_V, C_QI, C_KW, C_GA, C_U, C_GP, N_PROJ = 0, 512, 640, 768, 1280, 1408, 1920, 2432, 2944


def _nt_dot(a, b):
    return lax.dot_general(a, b, (((1,), (1,)), ((), ())), preferred_element_type=F32)


def _dot(a, b):
    return jnp.dot(a, b, preferred_element_type=F32)


def _silu(z):
    return z / (1.0 + jnp.exp(-z))


def _lane_iota(shape):
    return lax.broadcasted_iota(I32, shape, len(shape) - 1)


def _seg_ones(n, seg):
    r = lax.broadcasted_iota(I32, (n, n), 0) // seg
    c = lax.broadcasted_iota(I32, (n, n), 1) // seg
    return jnp.where(r == c, 1.0, 0.0).astype(BF16)


def _head_rms(z, seg_mat, w):
    n = seg_mat.shape[0]
    sq = (z * z).astype(BF16)
    ss = jnp.concatenate([_dot(sq[:, j:j + n], seg_mat) for j in range(0, z.shape[1], n)], axis=1)
    return z * lax.rsqrt(ss * (1.0 / HEAD_DIM) + EPS) * w


def _float_of_rank(u):
    key = u ^ INT_MIN
    bits = key ^ ((key >> 31) & np.int32(0x7FFFFFFF))
    return lax.bitcast_convert_type(bits, F32)


def _kth_largest(count_ge, shape, peel=0):
    def bit_body(it, ans):
        cand = ans | jnp.left_shift(jnp.int32(1), 31 - it)
        return jnp.where(count_ge(_float_of_rank(cand)) >= float(TOPK), cand, ans)

    ans = jnp.zeros(shape, I32)
    for it in range(peel):
        ans = bit_body(it, ans)
    ans = lax.fori_loop(peel, 32, bit_body, ans)
    return jnp.where(ans == 0, -jnp.inf, _float_of_rank(ans))


def _fold_rows(x, op):
    parts = [x[r:r + SUBLANES] for r in range(0, x.shape[0], SUBLANES)]
    while len(parts) > 1:
        parts = [op(parts[i], parts[i + 1]) for i in range(0, len(parts), 2)]
    return parts[0]


def _ada_kernel(cp_ref, cs_ref, w_ref, b_ref, op_ref, os_ref):
    w = w_ref[0].astype(BF16)
    op_ref[...] = _dot(_silu(cp_ref[...]).astype(BF16), w) + b_ref[...]
    os_ref[...] = _dot(_silu(cs_ref[...]).astype(BF16), w) + b_ref[...]


def _ada(c_prompt, c_sample, w_ada, b_ada):
    bp, bs = c_prompt.shape[0], c_sample.shape[0]
    return pl.pallas_call(
        _ada_kernel,
        out_shape=(jax.ShapeDtypeStruct((bp, 3 * D_MODEL), F32), jax.ShapeDtypeStruct((bs, 3 * D_MODEL), F32)),
        grid=(3,),
        in_specs=[
            pl.BlockSpec((bp, D_MODEL), lambda j: (0, 0)),
            pl.BlockSpec((bs, D_MODEL), lambda j: (0, 0)),
            pl.BlockSpec((1, D_MODEL, D_MODEL), lambda j: (0, 0, j)),
            pl.BlockSpec((1, D_MODEL), lambda j: (0, j)),
        ],
        out_specs=(pl.BlockSpec((bp, D_MODEL), lambda j: (0, j)), pl.BlockSpec((bs, D_MODEL), lambda j: (0, j))),
        compiler_params=pltpu.CompilerParams(dimension_semantics=("arbitrary",)),
        name="ada_ln",
    )(c_prompt, c_sample, w_ada, b_ada)


def _modulated_norm(x, norm_w, scale, shift):
    ms = jnp.mean(x * x, axis=-1, keepdims=True)
    return (x * lax.rsqrt(ms + EPS)) * norm_w * (1.0 + scale) + shift


def _pool_mix(d, wpool_ref, pscale, gp):
    db = d.astype(BF16)
    wide = 2 * POOL_GC
    y = jnp.concatenate([_dot(db[:, p * wide:(p + 1) * wide], wpool_ref[p])
                         for p in range(len(POOL_WINDOWS) // 2)], axis=1)
    return y * pscale * _silu(gp)


def _proj_prompt_kernel(x_ref, ada_ref, nw_ref, w_ref, qnw_ref, knw_ref, wpool_ref, ps_ref,
                        qs_ref, kt_ref, kk_ref, vt_ref, vtx_ref, qi_ref, kit_ref, kik_ref,
                        wit_ref, sga_ref, pg_ref, ulast_ref,
                        ext_ref):
    t = pl.program_id(1)
    ada = ada_ref[0]
    shift = ada[:, 0:D_MODEL]
    scale = ada[:, D_MODEL:2 * D_MODEL]
    sub_rows = KEY_CHUNK
    lo = _lane_iota((sub_rows, LANES)) < HEAD_DIM
    seg256 = _seg_ones(2 * LANES, HEAD_DIM)
    seg128 = _seg_ones(LANES, HEAD_DIM)
    ones = jnp.ones((BF16_ROWS, KEY_CHUNK), BF16)

    @pl.when(t == 0)
    def _():
        ext_ref[0:16, :] = jnp.zeros((16, POOL_W), F32)

    for sub in range(TM // sub_rows):
        rows = slice(sub * sub_rows, (sub + 1) * sub_rows)
        hb = _modulated_norm(x_ref[0, rows, :], nw_ref[...], scale, shift).astype(BF16)

        q = _dot(hb, w_ref[:, C_Q:C_Q + ATTN_W])
        qs_ref[rows, :] = _head_rms(q, seg256, qnw_ref[...]).astype(BF16)

        kv = _dot(hb, w_ref[:, C_K:C_K + 2 * LANES])
        k = _head_rms(kv[:, 0:LANES], seg128, knw_ref[...])
        kt_ref[0, :, rows] = k.T
        k_sw = pltpu.roll(k, HEAD_DIM, axis=1)
        kk_ref[0, rows, :] = jnp.where(lo, k, k_sw).astype(BF16)
        kk_ref[1, rows, :] = jnp.where(lo, k_sw, k).astype(BF16)

        v_t = kv[:, LANES:2 * LANES].T
        vt_ref[0, :, rows] = v_t
        for g in range(KV_HEADS):
            vtx_ref[0, g, sub, 0:HEAD_DIM, :] = v_t[g * HEAD_DIM:(g + 1) * HEAD_DIM].astype(BF16)
            vtx_ref[0, g, sub, HEAD_DIM:VT_ROWS, :] = ones

        qi_ref[rows, :] = _dot(hb, w_ref[:, C_QI:C_QI + ATTN_W]).astype(BF16)
        kw = _dot(hb, w_ref[:, C_KW:C_KW + LANES])
        kw_t = kw.T
        kit_ref[0, :, rows] = kw_t[0:IDX_DIM]
        wi_t = kw_t[IDX_DIM:IDX_DIM + IDX_HEADS] * ((IDX_HEADS * IDX_DIM) ** -0.5)
        for j in range(sub_rows // TQ):
            wit_ref[sub * (sub_rows // TQ) + j] = wi_t[:, j * TQ:(j + 1) * TQ]
        kik_ref[rows, :] = jnp.where(lo, kw, pltpu.roll(kw, HEAD_DIM, axis=1)).astype(BF16)

        sga_ref[rows, :] = _silu(_dot(hb, w_ref[:, C_GA:C_GA + ATTN_W])).astype(BF16)

        u = _dot(hb, w_ref[:, C_U:C_U + POOL_W])
        gp = _dot(hb, w_ref[:, C_GP:C_GP + POOL_W])
        base = 16 + sub * sub_rows
        ext_ref[base:base + sub_rows, :] = u
        pos = t * TM + sub * sub_rows + lax.broadcasted_iota(I32, (sub_rows, POOL_GC), 0)
        ds = []
        for g, w in enumerate(POOL_WINDOWS):
            cs = slice(g * POOL_GC, (g + 1) * POOL_GC)
            s = u[:, cs]
            for j in range(1, w):
                s = s + ext_ref[base - j:base - j + sub_rows, cs]
            cnt = jnp.minimum(pos + 1, w).astype(F32)
            ds.append(s / cnt - u[:, cs])
        d = jnp.concatenate(ds, axis=1)
        pg_ref[rows, :] = _pool_mix(d, wpool_ref, ps_ref[...], gp).astype(BF16)

    tail = ext_ref[TM:TM + 16, :]
    ulast_ref[0] = tail
    ext_ref[0:16, :] = tail


def _proj_prompt(x, ada_p, norm_w, w_in_b, qnw, knw, wpool_b, pscale):
    b, s, _ = x.shape
    n = b * s
    nt = s // TM
    cpt = TM // KEY_CHUNK
    row = lambda bi, ti: (bi * nt + ti, 0)
    tok = lambda bi, ti: (bi, 0, ti)
    const2 = lambda bi, ti: (0, 0)
    const3 = lambda bi, ti: (0, 0, 0)
    out_shape = (
        jax.ShapeDtypeStruct((n, ATTN_W), BF16),
        jax.ShapeDtypeStruct((b, LANES, s), F32),
        jax.ShapeDtypeStruct((KV_HEADS, n, LANES), BF16),
        jax.ShapeDtypeStruct((b, LANES, s), F32),
        jax.ShapeDtypeStruct((b, KV_HEADS, s // KEY_CHUNK, VT_ROWS, KEY_CHUNK), BF16),
        jax.ShapeDtypeStruct((n, ATTN_W), BF16),
        jax.ShapeDtypeStruct((b, IDX_DIM, s), F32),
        jax.ShapeDtypeStruct((n, LANES), BF16),
        jax.ShapeDtypeStruct((n // TQ, IDX_HEADS, TQ), F32),
        jax.ShapeDtypeStruct((n, ATTN_W), BF16),
        jax.ShapeDtypeStruct((n, POOL_W), BF16),
        jax.ShapeDtypeStruct((b, 16, POOL_W), F32),
    )
    out_specs = (
        pl.BlockSpec((TM, ATTN_W), row),
        pl.BlockSpec((1, LANES, TM), tok),
        pl.BlockSpec((KV_HEADS, TM, LANES), lambda bi, ti: (0, bi * nt + ti, 0)),
        pl.BlockSpec((1, LANES, TM), tok),
        pl.BlockSpec((1, KV_HEADS, cpt, VT_ROWS, KEY_CHUNK), lambda bi, ti: (bi, 0, ti, 0, 0)),
        pl.BlockSpec((TM, ATTN_W), row),
        pl.BlockSpec((1, IDX_DIM, TM), tok),
        pl.BlockSpec((TM, LANES), row),
        pl.BlockSpec((TM // TQ, IDX_HEADS, TQ), lambda bi, ti: (bi * nt + ti, 0, 0)),
        pl.BlockSpec((TM, ATTN_W), row),
        pl.BlockSpec((TM, POOL_W), row),
        pl.BlockSpec((1, 16, POOL_W), lambda bi, ti: (bi, 0, 0)),
    )
    in_specs = [
        pl.BlockSpec((1, TM, D_MODEL), lambda bi, ti: (bi, ti, 0)),
        pl.BlockSpec((1, 1, 3 * D_MODEL), lambda bi, ti: (bi, 0, 0)),
        pl.BlockSpec((1, D_MODEL), const2),
        pl.BlockSpec((D_MODEL, N_PROJ), const2),
        pl.BlockSpec((1, ATTN_W), const2),
        pl.BlockSpec((1, LANES), const2),
        pl.BlockSpec((2, 2 * POOL_GC, 2 * POOL_GC), const3),
        pl.BlockSpec((1, POOL_W), const2),
    ]
    return pl.pallas_call(
        _proj_prompt_kernel,
        out_shape=out_shape,
        grid=(b, nt),
        in_specs=in_specs,
        out_specs=out_specs,
        scratch_shapes=[pltpu.VMEM((16 + TM, POOL_W), F32)],
        compiler_params=pltpu.CompilerParams(
            dimension_semantics=("arbitrary", "arbitrary"), vmem_limit_bytes=VMEM_LIMIT),
        name="proj_prompt",
    )(x, ada_p, norm_w, w_in_b, qnw, knw, wpool_b, pscale)


def _attn_tile(nch, row0, wit, kik_ref, kk_ref, vtx_ref, sc_ref, lg_ref, wq_ref, acc_ref):
    shape2 = (KEY_CHUNK, TQ)
    n_pairs = N_HEADS // 2
    group_of = lambda hp: (2 * hp) // (N_HEADS // KV_HEADS)

    for c in range(nch):
        keys = slice(c * KEY_CHUNK, (c + 1) * KEY_CHUNK)
        kk = kik_ref[keys, :]
        acc = jnp.zeros(shape2, F32)
        for hp in range(n_pairs):
            s2 = _nt_dot(kk, wq_ref[0, hp])
            acc = acc + jnp.maximum(s2[:, 0:TQ], 0.0) * wit[2 * hp:2 * hp + 1, :]
            acc = acc + jnp.maximum(s2[:, TQ:2 * TQ], 0.0) * wit[2 * hp + 1:2 * hp + 2, :]
        if c == nch - 1:
            kpos = c * KEY_CHUNK + lax.broadcasted_iota(I32, shape2, 0)
            qpos = row0 + lax.broadcasted_iota(I32, shape2, 1)
            acc = jnp.where(kpos <= qpos, acc, -jnp.inf)
        sc_ref[c] = acc

    for c in range(nch):
        keys = slice(c * KEY_CHUNK, (c + 1) * KEY_CHUNK)
        for hp in range(n_pairs):
            lg_ref[hp, c] = _nt_dot(kk_ref[group_of(hp), keys, :], wq_ref[1, hp])

    def count_pass(pred):
        parts = [_fold_rows(pred(sc_ref[c]).astype(F32), jnp.add) for c in range(nch)]
        while len(parts) > 1:
            parts = [sum(parts[j:j + 2]) for j in range(0, len(parts), 2)]
        return jnp.sum(parts[0], axis=0, keepdims=True)

    if nch * KEY_CHUNK <= TOPK:
        thr = jnp.full((1, TQ), -jnp.inf, F32)
        need = jnp.zeros((1, TQ), F32)
    else:
        thr = _kth_largest(lambda t: count_pass(lambda x: x >= t), (1, TQ), peel=SEARCH_PEEL)
        need = float(TOPK) - count_pass(lambda x: x > thr)

    tri = jnp.where(lax.broadcasted_iota(I32, (KEY_CHUNK, KEY_CHUNK), 1)
                    <= lax.broadcasted_iota(I32, (KEY_CHUNK, KEY_CHUNK), 0), 1.0, 0.0).astype(BF16)

    m_part = [jnp.full((SUBLANES, TQ), -jnp.inf, F32) for _ in range(N_HEADS)]
    eq_before = jnp.zeros((1, TQ), F32)
    for c in range(nch):
        sc = sc_ref[c]
        eq = sc == thr
        rank = _dot(tri, jnp.where(eq, 1.0, 0.0).astype(BF16)) + eq_before
        sel = (sc > thr) | (eq & (rank <= need))
        if c == nch - 1:
            sel = sel & (sc > -jnp.inf)
        eq_before = rank[KEY_CHUNK - 1:KEY_CHUNK, :]
        for hp in range(n_pairs):
            for par in range(2):
                cols = slice(par * TQ, (par + 1) * TQ)
                lg = jnp.where(sel, lg_ref[hp, c, :, cols], -jnp.inf)
                lg_ref[hp, c, :, cols] = lg
                m_part[2 * hp + par] = jnp.maximum(m_part[2 * hp + par], _fold_rows(lg, jnp.maximum))
    m_rows = [jnp.max(m, axis=0, keepdims=True) for m in m_part]

    for hp in range(n_pairs):
        m2 = jnp.concatenate([m_rows[2 * hp], m_rows[2 * hp + 1]], axis=1)
        acc = jnp.zeros((VT_ROWS, 2 * TQ), F32)
        for c in range(nch):
            p2 = jnp.exp(lg_ref[hp, c] - m2).astype(BF16)
            acc = acc + _dot(vtx_ref[0, group_of(hp), c], p2)
        acc_ref[hp] = acc


def _attn_prompt_kernel(qi_ref, wit_ref, qs_ref, sga_ref, kik_ref, kk_ref, vtx_ref, ag_ref,
                        sc_ref, lg_ref, wq_ref, acc_ref):
    step = pl.program_id(1)
    n_pairs = N_HEADS // 2
    tiles_per_step = KEY_CHUNK // TQ
    nch = step + 1
    lo = _lane_iota((TQ, LANES)) < HEAD_DIM

    def one_tile(sub, carry):
        rows = pl.ds(pl.multiple_of(sub * TQ, TQ), TQ)
        for kind, ref in enumerate((qi_ref, qs_ref)):
            for hp in range(n_pairs):
                pair = ref[rows, hp * LANES:(hp + 1) * LANES].astype(F32)
                wq_ref[kind, hp, 0:TQ, :] = jnp.where(lo, pair, 0.0).astype(BF16)
                wq_ref[kind, hp, TQ:2 * TQ, :] = jnp.where(lo, 0.0, pair).astype(BF16)

        wit = wit_ref[sub]
        row0 = (step * tiles_per_step + sub) * TQ
        for n_static in range(1, sc_ref.shape[0] + 1):
            @pl.when(nch == n_static)
            def _(n_static=n_static):
                _attn_tile(n_static, row0, wit, kik_ref, kk_ref, vtx_ref, sc_ref, lg_ref, wq_ref, acc_ref)

        for hp in range(n_pairs):
            a = acc_ref[hp]
            o0 = a[0:HEAD_DIM, 0:TQ] / a[HEAD_DIM:HEAD_DIM + 1, 0:TQ]
            o1 = a[0:HEAD_DIM, TQ:2 * TQ] / a[HEAD_DIM:HEAD_DIM + 1, TQ:2 * TQ]
            pair = jnp.concatenate([o0, o1], axis=0).T
            cols = slice(hp * LANES, (hp + 1) * LANES)
            ag_ref[rows, cols] = (pair * sga_ref[rows, cols].astype(F32)).astype(BF16)
        return carry

    lax.fori_loop(0, tiles_per_step, one_tile, 0)


def _attn_prompt(qi, wit, qs, sga, kik, kk, vtx, b, s):
    n = b * s
    nkc = s // KEY_CHUNK
    tps = KEY_CHUNK // TQ
    row = lambda bi, st: (bi * nkc + st, 0)
    return pl.pallas_call(
        _attn_prompt_kernel,
        out_shape=jax.ShapeDtypeStruct((n, ATTN_W), BF16),
        grid=(b, nkc),
        in_specs=[
            pl.BlockSpec((KEY_CHUNK, ATTN_W), row),
            pl.BlockSpec((tps, IDX_HEADS, TQ), lambda bi, st: (bi * nkc + st, 0, 0)),
            pl.BlockSpec((KEY_CHUNK, ATTN_W), row),
            pl.BlockSpec((KEY_CHUNK, ATTN_W), row),
            pl.BlockSpec((s, LANES), lambda bi, st: (bi, 0)),
            pl.BlockSpec((KV_HEADS, s, LANES), lambda bi, st: (0, bi, 0)),
            pl.BlockSpec((1, KV_HEADS, nkc, VT_ROWS, KEY_CHUNK), lambda bi, st: (bi, 0, 0, 0, 0)),
        ],
        out_specs=pl.BlockSpec((KEY_CHUNK, ATTN_W), row),
        scratch_shapes=[
            pltpu.VMEM((nkc, KEY_CHUNK, TQ), F32),
            pltpu.VMEM((N_HEADS // 2, nkc, KEY_CHUNK, 2 * TQ), F32),
            pltpu.VMEM((2, N_HEADS // 2, 2 * TQ, LANES), BF16),
            pltpu.VMEM((N_HEADS // 2, VT_ROWS, 2 * TQ), F32),
        ],
        compiler_params=pltpu.CompilerParams(
            dimension_semantics=("arbitrary", "arbitrary"), vmem_limit_bytes=VMEM_LIMIT),
        name="attn_prompt",
    )(qi, wit, qs, sga, kik, kk, vtx)


def _out_proj_kernel(x_ref, a_ref, p_ref, gate_ref, w_ref, o_ref):
    y = _dot(a_ref[...], w_ref[0:ATTN_W, :]) + _dot(p_ref[...], w_ref[ATTN_W:ATTN_W + POOL_W, :])
    o_ref[0] = x_ref[0] + gate_ref[0] * y


def _out_proj_prompt(x, ag, pg, ada_p, w_out_b):
    b, s, _ = x.shape
    nt = s // TM_OUT
    row = lambda bi, ti: (bi * nt + ti, 0)
    return pl.pallas_call(
        _out_proj_kernel,
        out_shape=jax.ShapeDtypeStruct(x.shape, F32),
        grid=(b, nt),
        in_specs=[
            pl.BlockSpec((1, TM_OUT, D_MODEL), lambda bi, ti: (bi, ti, 0)),
            pl.BlockSpec((TM_OUT, ATTN_W), row),
            pl.BlockSpec((TM_OUT, POOL_W), row),
            pl.BlockSpec((1, 1, D_MODEL), lambda bi, ti: (bi, 0, 2)),
            pl.BlockSpec((D_MODEL, D_MODEL), lambda bi, ti: (0, 0)),
        ],
        out_specs=pl.BlockSpec((1, TM_OUT, D_MODEL), lambda bi, ti: (bi, ti, 0)),
        compiler_params=pltpu.CompilerParams(
            dimension_semantics=("arbitrary", "arbitrary"), vmem_limit_bytes=VMEM_LIMIT),
        name="out_proj_prompt",
    )(x, ag, pg, ada_p, w_out_b)


def _proj_sample_kernel(x_ref, ada_ref, nw_ref, w_ref, qnw_ref, knw_ref, wpool_ref, ps_ref, hist_ref,
                        qpad_ref, k_ref, v_ref, ki_ref, qi_ref, wi_ref, snew_ref, lnew_ref,
                        sga_ref, pg_ref, pool_ref):
    nb = x_ref.shape[0]
    x = x_ref[...]
    shift = ada_ref[:, 0:D_MODEL]
    scale = ada_ref[:, D_MODEL:2 * D_MODEL]
    hb = _modulated_norm(x, nw_ref[...], scale, shift).astype(BF16)

    lane = _lane_iota((nb, LANES))
    lo = lane < HEAD_DIM
    seg256 = _seg_ones(2 * LANES, HEAD_DIM)
    seg128 = _seg_ones(LANES, HEAD_DIM)
    head_sel = jnp.where(lax.broadcasted_iota(I32, (ATTN_W, LANES), 0) // HEAD_DIM
                         == lax.broadcasted_iota(I32, (ATTN_W, LANES), 1), 1.0, 0.0).astype(BF16)

    def head_sums(prod):
        hi = prod.astype(BF16)
        rest = (prod - hi.astype(F32)).astype(BF16)
        return _dot(hi, head_sel) + _dot(rest, head_sel)

    q = _head_rms(_dot(hb, w_ref[:, C_Q:C_Q + ATTN_W]), seg256, qnw_ref[...])
    qb = q.astype(BF16)
    k = _head_rms(_dot(hb, w_ref[:, C_K:C_K + LANES]), seg128, knw_ref[...])
    k_ref[...] = k
    v = _dot(hb, w_ref[:, C_V:C_V + LANES])
    v_ref[...] = v

    for hp in range(N_HEADS // 2):
        pair = q[:, hp * LANES:(hp + 1) * LANES]
        pair_sw = pltpu.roll(pair, HEAD_DIM, axis=1)
        g = (2 * hp) // (N_HEADS // KV_HEADS)
        if g == 0:
            h_even, h_odd = jnp.where(lo, pair, 0.0), jnp.where(lo, pair_sw, 0.0)
        else:
            h_even, h_odd = jnp.where(lo, 0.0, pair_sw), jnp.where(lo, 0.0, pair)
        qpad_ref[:, (2 * hp) * LANES:(2 * hp + 1) * LANES] = h_even.astype(BF16)
        qpad_ref[:, (2 * hp + 1) * LANES:(2 * hp + 2) * LANES] = h_odd.astype(BF16)

    kq = k.astype(BF16).astype(F32)
    kq_sw = pltpu.roll(kq, HEAD_DIM, axis=1)
    k0t = jnp.where(lo, kq, kq_sw)
    k1t = jnp.where(lo, kq_sw, kq)
    qf = qb.astype(F32)
    prod = jnp.concatenate([qf[:, 0:LANES] * k0t, qf[:, LANES:2 * LANES] * k0t,
                            qf[:, 2 * LANES:3 * LANES] * k1t, qf[:, 3 * LANES:4 * LANES] * k1t], axis=1)
    lnew_ref[...] = head_sums(prod)

    qi = _dot(hb, w_ref[:, C_QI:C_QI + ATTN_W])
    qib = qi.astype(BF16)
    qi_ref[...] = qib
    kw = _dot(hb, w_ref[:, C_KW:C_KW + LANES])
    kw_sw = pltpu.roll(kw, HEAD_DIM, axis=1)
    ki_ref[...] = kw[:, 0:IDX_DIM]
    wi_full = jnp.where(lane < IDX_HEADS, kw_sw, 0.0) * ((IDX_HEADS * IDX_DIM) ** -0.5)
    wi_ref[...] = wi_full[:, 0:IDX_HEADS]

    kib = kw.astype(BF16).astype(F32)
    kit = jnp.where(lo, kib, pltpu.roll(kib, HEAD_DIM, axis=1))
    qif = qib.astype(F32)
    prod_i = jnp.concatenate([qif[:, j * LANES:(j + 1) * LANES] * kit for j in range(4)], axis=1)
    s_new = jnp.maximum(head_sums(prod_i), 0.0) * wi_full
    snew_ref[...] = jnp.broadcast_to(jnp.sum(s_new, axis=1, keepdims=True), (nb, LANES))

    sga_ref[...] = _silu(_dot(hb, w_ref[:, C_GA:C_GA + ATTN_W])).astype(BF16)

    u = _dot(hb, w_ref[:, C_U:C_U + POOL_W])
    gp = _dot(hb, w_ref[:, C_GP:C_GP + POOL_W])
    for j in range(POOL_HIST - 1):
        pool_ref[j] = hist_ref[j + 1]
    pool_ref[POOL_HIST - 1] = u
    ds = []
    for g, w in enumerate(POOL_WINDOWS):
        cs = slice(g * POOL_GC, (g + 1) * POOL_GC)
        s = u[:, cs]
        for j in range(1, w):
            s = s + hist_ref[POOL_HIST - j, :, cs]
        ds.append(s / float(w) - u[:, cs])
    d = jnp.concatenate(ds, axis=1)
    pg_ref[...] = _pool_mix(d, wpool_ref, ps_ref[...], gp).astype(BF16)


def _proj_sample(x, ada_s, norm_w, w_in_b, qnw, knw, wpool_b, pscale, hist_t):
    nb = x.shape[0]
    out_shape = (
        jax.ShapeDtypeStruct((nb, N_HEADS * LANES), BF16),
        jax.ShapeDtypeStruct((nb, LANES), F32),
        jax.ShapeDtypeStruct((nb, LANES), F32),
        jax.ShapeDtypeStruct((nb, IDX_DIM), F32),
        jax.ShapeDtypeStruct((nb, ATTN_W), BF16),
        jax.ShapeDtypeStruct((nb, IDX_HEADS), F32),
        jax.ShapeDtypeStruct((nb, LANES), F32),
        jax.ShapeDtypeStruct((nb, LANES), F32),
        jax.ShapeDtypeStruct((nb, ATTN_W), BF16),
        jax.ShapeDtypeStruct((nb, POOL_W), BF16),
        jax.ShapeDtypeStruct((POOL_HIST, nb, POOL_W), F32),
    )
    return pl.pallas_call(
        _proj_sample_kernel,
        out_shape=out_shape,
        compiler_params=pltpu.CompilerParams(vmem_limit_bytes=VMEM_LIMIT),
        name="proj_sample",
    )(x, ada_s, norm_w, w_in_b, qnw, knw, wpool_b, pscale, hist_t)


def _page_copies(pt_ref, pages_hbm, buf_ref, sem_ref, step, slot, group, n_pages):
    copies = []
    for j in range(group):
        for p in range(n_pages):
            page = pt_ref[step * group + j, p]
            copies.append(pltpu.make_async_copy(
                pages_hbm.at[page], buf_ref.at[slot, j, :, pl.ds(p * PAGE, PAGE)], sem_ref.at[slot]))
    return copies


def _score_sample_kernel(pt_ref, qi_ref, wi_ref, kidx_hbm, o_ref, buf_ref, sem_ref, *, group, n_pages):
    step = pl.program_id(0)
    slot = lax.rem(step, 2)

    def copies(st, sl):
        return _page_copies(pt_ref, kidx_hbm, buf_ref, sem_ref, st, sl, group, n_pages)

    @pl.when(step == 0)
    def _():
        for cp in copies(step, slot):
            cp.start()

    @pl.when(step + 1 < pl.num_programs(0))
    def _():
        for cp in copies(step + 1, 1 - slot):
            cp.start()

    for cp in copies(step, slot):
        cp.wait()

    for j in range(group):
        ki_t = buf_ref[slot, j].astype(BF16)
        s = _dot(qi_ref[j], ki_t)
        o_ref[j:j + 1, :] = jnp.sum(jnp.maximum(s, 0.0) * wi_ref[j], axis=0, keepdims=True)


def _score_sample(page_table, qi3, wi3, kidx_t):
    nb, n_pages = page_table.shape
    n_keys = n_pages * PAGE
    g = SCORE_GROUP
    return pl.pallas_call(
        functools.partial(_score_sample_kernel, group=g, n_pages=n_pages),
        out_shape=jax.ShapeDtypeStruct((nb, n_keys), F32),
        grid_spec=pltpu.PrefetchScalarGridSpec(
            num_scalar_prefetch=1,
            grid=(nb // g,),
            in_specs=[
                pl.BlockSpec((g, IDX_HEADS, IDX_DIM), lambda s, pt: (s, 0, 0)),
                pl.BlockSpec((g, IDX_HEADS, 1), lambda s, pt: (s, 0, 0)),
                pl.BlockSpec(memory_space=pl.ANY),
            ],
            out_specs=pl.BlockSpec((g, n_keys), lambda s, pt: (s, 0)),
            scratch_shapes=[
                pltpu.VMEM((2, g, IDX_DIM, n_keys), F32),
                pltpu.SemaphoreType.DMA((2,)),
            ],
        ),
        compiler_params=pltpu.CompilerParams(
            dimension_semantics=("arbitrary",), vmem_limit_bytes=VMEM_LIMIT),
        name="score_sample",
    )(page_table, qi3, wi3, kidx_t)


def _select_sample_kernel(sc_ref, snew_ref, mask_ref, mnew_ref):
    nb, n_keys = sc_ref.shape
    nch = n_keys // KEY_CHUNK
    snew = snew_ref[...]
    ones_mat = jnp.ones((LANES, LANES), BF16)

    def count_pass(pred):
        acc = jnp.zeros((nb, LANES), F32)
        for c in range(n_keys // LANES):
            acc = acc + pred(sc_ref[:, c * LANES:(c + 1) * LANES]).astype(F32)
        return _dot(acc.astype(BF16), ones_mat) + pred(snew).astype(F32)

    thr = _kth_largest(lambda t: count_pass(lambda x: x >= t), (nb, LANES))
    need = float(TOPK) - count_pass(lambda x: x > thr)
    thr2 = jnp.concatenate([thr, thr], axis=1)
    need2 = jnp.concatenate([need, need], axis=1)
    tri = jnp.where(lax.broadcasted_iota(I32, (KEY_CHUNK, KEY_CHUNK), 0)
                    <= lax.broadcasted_iota(I32, (KEY_CHUNK, KEY_CHUNK), 1), 1.0, 0.0).astype(BF16)
    ones_cl = jnp.ones((KEY_CHUNK, LANES), BF16)

    eq_before = jnp.zeros((nb, LANES), F32)
    for c in range(nch):
        sc = sc_ref[:, c * KEY_CHUNK:(c + 1) * KEY_CHUNK]
        eq = sc == thr2
        eqb = jnp.where(eq, 1.0, 0.0).astype(BF16)
        rank = _dot(eqb, tri) + jnp.concatenate([eq_before, eq_before], axis=1)
        sel = (sc > thr2) | (eq & (rank <= need2))
        mask_ref[:, c * KEY_CHUNK:(c + 1) * KEY_CHUNK] = jnp.where(sel, 0.0, -jnp.inf)
        eq_before = eq_before + _dot(eqb, ones_cl)
    sel_new = (snew > thr) | ((snew == thr) & (eq_before + 1.0 <= need))
    mnew_ref[...] = jnp.where(sel_new, 0.0, -jnp.inf)


def _select_sample(scores, snew):
    nb, n_keys = scores.shape
    return pl.pallas_call(
        _select_sample_kernel,
        out_shape=(jax.ShapeDtypeStruct((nb, n_keys), F32), jax.ShapeDtypeStruct((nb, LANES), F32)),
        name="select_sample",
    )(scores, snew)


def _attn_sample_kernel(pt_ref, q_ref, mask_ref, lnew_ref, mnew_ref, vnew_ref, k_hbm, v_hbm,
                        o_ref, kbuf_ref, vbuf_ref, ksem_ref, vsem_ref, *, group, n_pages):
    step = pl.program_id(0)
    slot = lax.rem(step, 2)

    def copies(st, sl):
        return (_page_copies(pt_ref, k_hbm, kbuf_ref, ksem_ref, st, sl, group, n_pages)
                + _page_copies(pt_ref, v_hbm, vbuf_ref, vsem_ref, st, sl, group, n_pages))

    @pl.when(step == 0)
    def _():
        for cp in copies(step, slot):
            cp.start()

    @pl.when(step + 1 < pl.num_programs(0))
    def _():
        for cp in copies(step + 1, 1 - slot):
            cp.start()

    for cp in copies(step, slot):
        cp.wait()

    for j in range(group):
        k_t = kbuf_ref[slot, j].astype(BF16)
        lg = _dot(q_ref[j], k_t) + mask_ref[j]
        lg_n = lnew_ref[j] + mnew_ref[j]
        m = jnp.maximum(jnp.max(lg, axis=1, keepdims=True), lg_n)
        p = jnp.exp(lg - m)
        p_n = jnp.exp(lg_n - m)
        l = jnp.sum(p, axis=1, keepdims=True) + p_n
        acc = _nt_dot(p.astype(BF16), vbuf_ref[slot, j].astype(BF16)) + p_n * vnew_ref[j]
        o_ref[j] = acc / l


def _attn_sample(page_table, qpad3, mask3, lnew3, mnew3, vnew3, k_t, v_t):
    nb, n_pages = page_table.shape
    n_keys = n_pages * PAGE
    g = ATTN_GROUP
    per_s = lambda s, pt: (s, 0, 0)
    return pl.pallas_call(
        functools.partial(_attn_sample_kernel, group=g, n_pages=n_pages),
        out_shape=jax.ShapeDtypeStruct((nb, N_HEADS, LANES), F32),
        grid_spec=pltpu.PrefetchScalarGridSpec(
            num_scalar_prefetch=1,
            grid=(nb // g,),
            in_specs=[
                pl.BlockSpec((g, N_HEADS, LANES), per_s),
                pl.BlockSpec((g, 1, n_keys), per_s),
                pl.BlockSpec((g, N_HEADS, 1), per_s),
                pl.BlockSpec((g, N_HEADS, 1), per_s),
                pl.BlockSpec((g, 1, LANES), per_s),
                pl.BlockSpec(memory_space=pl.ANY),
                pl.BlockSpec(memory_space=pl.ANY),
            ],
            out_specs=pl.BlockSpec((g, N_HEADS, LANES), per_s),
            scratch_shapes=[
                pltpu.VMEM((2, g, LANES, n_keys), F32),
                pltpu.VMEM((2, g, LANES, n_keys), F32),
                pltpu.SemaphoreType.DMA((2,)),
                pltpu.SemaphoreType.DMA((2,)),
            ],
        ),
        compiler_params=pltpu.CompilerParams(
            dimension_semantics=("arbitrary",), vmem_limit_bytes=VMEM_LIMIT),
        name="attn_sample",
    )(page_table, qpad3, mask3, lnew3, mnew3, vnew3, k_t, v_t)


def _out_proj_sample_kernel(x_ref, a_ref, sga_ref, p_ref, ada_ref, w_ref, o_ref):
    lo = _lane_iota((x_ref.shape[0], LANES)) < HEAD_DIM
    pairs = []
    for hp in range(N_HEADS // 2):
        even = a_ref[:, (2 * hp) * LANES:(2 * hp + 1) * LANES]
        odd = a_ref[:, (2 * hp + 1) * LANES:(2 * hp + 2) * LANES]
        if (2 * hp) // (N_HEADS // KV_HEADS) == 0:
            pairs.append(jnp.where(lo, even, pltpu.roll(odd, HEAD_DIM, axis=1)))
        else:
            pairs.append(jnp.where(lo, pltpu.roll(even, HEAD_DIM, axis=1), odd))
    a = jnp.concatenate(pairs, axis=1)
    ag = (a * sga_ref[...].astype(F32)).astype(BF16)
    y = _dot(ag, w_ref[0:ATTN_W, :]) + _dot(p_ref[...], w_ref[ATTN_W:ATTN_W + POOL_W, :])
    o_ref[...] = x_ref[...] + ada_ref[:, 2 * D_MODEL:3 * D_MODEL] * y


def _out_proj_sample(x, a, sga, pg, ada_s, w_out_b):
    return pl.pallas_call(
        _out_proj_sample_kernel,
        out_shape=jax.ShapeDtypeStruct(x.shape, F32),
        compiler_params=pltpu.CompilerParams(vmem_limit_bytes=VMEM_LIMIT),
        name="out_proj_sample",
    )(x, a, sga, pg, ada_s, w_out_b)


def _permute_w_in(w_in):
    q, k, v, qi, ki, wi, ga, u, gp = jnp.split(
        w_in, np.cumsum([512, 128, 128, 512, 64, 8, 512, 512]).tolist(), axis=1)
    pad = jnp.zeros((w_in.shape[0], LANES - IDX_DIM - IDX_HEADS), w_in.dtype)
    return jnp.concatenate([q, k, v, qi, ki, wi, pad, ga, u, gp], axis=1).astype(BF16)


def kernel(x_prompt, x_sample, cache_k, cache_v, cache_kidx, state_pool, page_table, c_prompt, c_sample,
           norm_w, w_ada, b_ada, w_in, q_norm_w, k_norm_w, w_pool, pool_scale, w_out):
    bp, s, _ = x_prompt.shape
    bs = x_sample.shape[0]
    assert w_in.shape[0] == 1 and x_sample.shape[1] == 1, "single layer, single decode token"
    n_phys = cache_k.shape[1]

    w_in_b = _permute_w_in(w_in[0])
    w_out_b = w_out[0].astype(BF16)
    zero_blk = jnp.zeros((POOL_GC, POOL_GC), w_pool.dtype)
    wpool_b = jnp.stack([jnp.block([[w_pool[0, 2 * p], zero_blk], [zero_blk, w_pool[0, 2 * p + 1]]])
                         for p in range(len(POOL_WINDOWS) // 2)]).astype(BF16)
    qnw = jnp.tile(q_norm_w[0], N_HEADS)[None, :] * (HEAD_DIM ** -0.5)
    knw = jnp.tile(k_norm_w[0], KV_HEADS)[None, :]
    nw = norm_w[0][None, :]
    pscale = pool_scale[0][None, :]

    ada_p, ada_s = _ada(c_prompt, c_sample, w_ada, b_ada)
    ada_p = ada_p.reshape(bp, 1, 3 * D_MODEL)

    (qs, k_t_p, kk, v_t_p, vtx, qi, ki_t_p, kik, wit, sga, pg, ulast) = _proj_prompt(
        x_prompt, ada_p, nw, w_in_b, qnw, knw, wpool_b, pscale)
    ag = _attn_prompt(qi, wit, qs, sga, kik, kk, vtx, bp, s)
    y_prompt = _out_proj_prompt(x_prompt, ag, pg, ada_p, w_out_b)

    hist_t = jnp.transpose(state_pool[0], (1, 0, 2))
    (qpad, k_s, v_s, ki_s, qi_s, wi_s, snew, lnew, sga_s, pg_s, pool_s) = _proj_sample(
        x_sample[:, 0, :], ada_s, nw, w_in_b, qnw, knw, wpool_b, pscale, hist_t)
    n_pages = page_table.shape[1]
    kidx_t = jnp.transpose(cache_kidx[0], (0, 2, 1))
    k_t = jnp.transpose(cache_k[0], (0, 2, 3, 1)).reshape(n_phys, LANES, PAGE)
    v_t = jnp.transpose(cache_v[0], (0, 2, 3, 1)).reshape(n_phys, LANES, PAGE)
    scores = _score_sample(page_table, qi_s.reshape(bs, IDX_HEADS, IDX_DIM), wi_s.reshape(bs, IDX_HEADS, 1),
                           kidx_t)
    mask, mnew = _select_sample(scores, snew)
    o_s = _attn_sample(
        page_table, qpad.reshape(bs, N_HEADS, LANES),
        mask.reshape(bs, 1, n_pages * PAGE),
        lnew[:, :N_HEADS].reshape(bs, N_HEADS, 1),
        jnp.broadcast_to(mnew[:, :1], (bs, N_HEADS)).reshape(bs, N_HEADS, 1),
        v_s.reshape(bs, 1, LANES), k_t, v_t)
    y_sample = _out_proj_sample(x_sample[:, 0, :], o_s.reshape(bs, N_HEADS * LANES), sga_s, pg_s, ada_s, w_out_b)

    to_heads = lambda a: jnp.transpose(a.reshape(bp, KV_HEADS, HEAD_DIM, s), (0, 3, 1, 2))[None]
    return (
        y_prompt,
        y_sample[:, None, :],
        to_heads(k_t_p),
        to_heads(v_t_p),
        jnp.transpose(ki_t_p, (0, 2, 1))[None],
        ulast[:, 1:, :][None],
        k_s.reshape(1, bs, 1, KV_HEADS, HEAD_DIM),
        v_s.reshape(1, bs, 1, KV_HEADS, HEAD_DIM),
        ki_s.reshape(1, bs, 1, IDX_DIM),
        jnp.transpose(pool_s, (1, 0, 2))[None],
    )
```

```python
import functools

import jax
import jax.numpy as jnp
import numpy as np
from jax import lax
from jax.experimental import pallas as pl
from jax.experimental.pallas import tpu as pltpu

F32 = jnp.float32
BF16 = jnp.bfloat16
I32 = jnp.int32

D_MODEL = 1024
ATTN_W = 512
POOL_W = 512
HEAD_DIM = 64
N_HEADS = 8
KV_HEADS = 2
IDX_HEADS = 8
IDX_DIM = 64
TOPK = 256
POOL_WINDOWS = (2, 4, 8, 16)
POOL_GC = 128
POOL_HIST = 15
EPS = 1e-6
PAGE = 128

LANES = 128
SUBLANES = 8
BF16_ROWS = 16
KEY_CHUNK = 256
TQ = 128
TM = 512
TM_OUT = 1024
VT_ROWS = HEAD_DIM + BF16_ROWS
SEARCH_PEEL = 10
SCORE_GROUP = 8
ATTN_GROUP = 4
INT_MIN = np.int32(-2 ** 31)
VMEM_LIMIT = 56 * 1024 * 1024

C_Q, C_K, C_V, C_QI, C_KW, C_GA, C_U, C_GP, N_PROJ = 0, 512, 640, 768, 1280, 1408, 1920, 2432, 2944


def _nt_dot(a, b):
    return lax.dot_general(a, b, (((1,), (1,)), ((), ())), preferred_element_type=F32)


def _dot(a, b):
    return jnp.dot(a, b, preferred_element_type=F32)


def _silu(z):
    return z / (1.0 + jnp.exp(-z))


def _lane_iota(shape):
    return lax.broadcasted_iota(I32, shape, len(shape) - 1)


def _seg_ones(n, seg):
    r = lax.broadcasted_iota(I32, (n, n), 0) // seg
    c = lax.broadcasted_iota(I32, (n, n), 1) // seg
    return jnp.where(r == c, 1.0, 0.0).astype(BF16)


def _head_rms(z, seg_mat, w):
    n = seg_mat.shape[0]
    sq = (z * z).astype(BF16)
    ss = jnp.concatenate([_dot(sq[:, j:j + n], seg_mat) for j in range(0, z.shape[1], n)], axis=1)
    return z * lax.rsqrt(ss * (1.0 / HEAD_DIM) + EPS) * w


def _float_of_rank(u):
    key = u ^ INT_MIN
    bits = key ^ ((key >> 31) & np.int32(0x7FFFFFFF))
    return lax.bitcast_convert_type(bits, F32)


def _kth_largest(count_ge, shape, peel=0):
    def bit_body(it, ans):
        cand = ans | jnp.left_shift(jnp.int32(1), 31 - it)
        return jnp.where(count_ge(_float_of_rank(cand)) >= float(TOPK), cand, ans)

    ans = jnp.zeros(shape, I32)
    for it in range(peel):
        ans = bit_body(it, ans)
    ans = lax.fori_loop(peel, 32, bit_body, ans)
    return jnp.where(ans == 0, -jnp.inf, _float_of_rank(ans))


def _fold_rows(x, op):
    parts = [x[r:r + SUBLANES] for r in range(0, x.shape[0], SUBLANES)]
    while len(parts) > 1:
        parts = [op(parts[i], parts[i + 1]) for i in range(0, len(parts), 2)]
    return parts[0]


def _ada_kernel(cp_ref, cs_ref, w_ref, b_ref, op_ref, os_ref):
    w = w_ref[0].astype(BF16)
    op_ref[...] = _dot(_silu(cp_ref[...]).astype(BF16), w) + b_ref[...]
    os_ref[...] = _dot(_silu(cs_ref[...]).astype(BF16), w) + b_ref[...]


def _ada(c_prompt, c_sample, w_ada, b_ada):
    bp, bs = c_prompt.shape[0], c_sample.shape[0]
    return pl.pallas_call(
        _ada_kernel,
        out_shape=(jax.ShapeDtypeStruct((bp, 3 * D_MODEL), F32), jax.ShapeDtypeStruct((bs, 3 * D_MODEL), F32)),
        grid=(3,),
        in_specs=[
            pl.BlockSpec((bp, D_MODEL), lambda j: (0, 0)),
            pl.BlockSpec((bs, D_MODEL), lambda j: (0, 0)),
            pl.BlockSpec((1, D_MODEL, D_MODEL), lambda j: (0, 0, j)),
            pl.BlockSpec((1, D_MODEL), lambda j: (0, j)),
        ],
        out_specs=(pl.BlockSpec((bp, D_MODEL), lambda j: (0, j)), pl.BlockSpec((bs, D_MODEL), lambda j: (0, j))),
        compiler_params=pltpu.CompilerParams(dimension_semantics=("arbitrary",)),
        name="ada_ln",
    )(c_prompt, c_sample, w_ada, b_ada)


def _modulated_norm(x, norm_w, scale, shift):
    ms = jnp.mean(x * x, axis=-1, keepdims=True)
    return (x * lax.rsqrt(ms + EPS)) * norm_w * (1.0 + scale) + shift


def _pool_mix(d, wpool_ref, pscale, gp):
    db = d.astype(BF16)
    wide = 2 * POOL_GC
    y = jnp.concatenate([_dot(db[:, p * wide:(p + 1) * wide], wpool_ref[p])
                         for p in range(len(POOL_WINDOWS) // 2)], axis=1)
    return y * pscale * _silu(gp)


def _proj_prompt_kernel(x_ref, ada_ref, nw_ref, w_ref, qnw_ref, knw_ref, wpool_ref, ps_ref,
                        qs_ref, kt_ref, kk_ref, vt_ref, vtx_ref, qi_ref, kit_ref, kik_ref,
                        wit_ref, sga_ref, pg_ref, ulast_ref,
                        ext_ref):
    t = pl.program_id(1)
    ada = ada_ref[0]
    shift = ada[:, 0:D_MODEL]
    scale = ada[:, D_MODEL:2 * D_MODEL]
    sub_rows = KEY_CHUNK
    lo = _lane_iota((sub_rows, LANES)) < HEAD_DIM
    seg256 = _seg_ones(2 * LANES, HEAD_DIM)
    seg128 = _seg_ones(LANES, HEAD_DIM)
    ones = jnp.ones((BF16_ROWS, KEY_CHUNK), BF16)

    @pl.when(t == 0)
    def _():
        ext_ref[0:16, :] = jnp.zeros((16, POOL_W), F32)

    for sub in range(TM // sub_rows):
        rows = slice(sub * sub_rows, (sub + 1) * sub_rows)
        hb = _modulated_norm(x_ref[0, rows, :], nw_ref[...], scale, shift).astype(BF16)

        q = _dot(hb, w_ref[:, C_Q:C_Q + ATTN_W])
        qs_ref[rows, :] = _head_rms(q, seg256, qnw_ref[...]).astype(BF16)

        kv = _dot(hb, w_ref[:, C_K:C_K + 2 * LANES])
        k = _head_rms(kv[:, 0:LANES], seg128, knw_ref[...])
        kt_ref[0, :, rows] = k.T
        k_sw = pltpu.roll(k, HEAD_DIM, axis=1)
        kk_ref[0, rows, :] = jnp.where(lo, k, k_sw).astype(BF16)
        kk_ref[1, rows, :] = jnp.where(lo, k_sw, k).astype(BF16)

        v_t = kv[:, LANES:2 * LANES].T
        vt_ref[0, :, rows] = v_t
        for g in range(KV_HEADS):
            vtx_ref[0, g, sub, 0:HEAD_DIM, :] = v_t[g * HEAD_DIM:(g + 1) * HEAD_DIM].astype(BF16)
            vtx_ref[0, g, sub, HEAD_DIM:VT_ROWS, :] = ones

        qi_ref[rows, :] = _dot(hb, w_ref[:, C_QI:C_QI + ATTN_W]).astype(BF16)
        kw = _dot(hb, w_ref[:, C_KW:C_KW + LANES])
        kw_t = kw.T
        kit_ref[0, :, rows] = kw_t[0:IDX_DIM]
        wi_t = kw_t[IDX_DIM:IDX_DIM + IDX_HEADS] * ((IDX_HEADS * IDX_DIM) ** -0.5)
        for j in range(sub_rows // TQ):
            wit_ref[sub * (sub_rows // TQ) + j] = wi_t[:, j * TQ:(j + 1) * TQ]
        kik_ref[rows, :] = jnp.where(lo, kw, pltpu.roll(kw, HEAD_DIM, axis=1)).astype(BF16)

        sga_ref[rows, :] = _silu(_dot(hb, w_ref[:, C_GA:C_GA + ATTN_W])).astype(BF16)

        u = _dot(hb, w_ref[:, C_U:C_U + POOL_W])
        gp = _dot(hb, w_ref[:, C_GP:C_GP + POOL_W])
        base = 16 + sub * sub_rows
        ext_ref[base:base + sub_rows, :] = u
        pos = t * TM + sub * sub_rows + lax.broadcasted_iota(I32, (sub_rows, POOL_GC), 0)
        ds = []
        for g, w in enumerate(POOL_WINDOWS):
            cs = slice(g * POOL_GC, (g + 1) * POOL_GC)
            s = u[:, cs]
            for j in range(1, w):
                s = s + ext_ref[base - j:base - j + sub_rows, cs]
            cnt = jnp.minimum(pos + 1, w).astype(F32)
            ds.append(s / cnt - u[:, cs])
        d = jnp.concatenate(ds, axis=1)
        pg_ref[rows, :] = _pool_mix(d, wpool_ref, ps_ref[...], gp).astype(BF16)

    tail = ext_ref[TM:TM + 16, :]
    ulast_ref[0] = tail
    ext_ref[0:16, :] = tail


def _proj_prompt(x, ada_p, norm_w, w_in_b, qnw, knw, wpool_b, pscale):
    b, s, _ = x.shape
    n = b * s
    nt = s // TM
    cpt = TM // KEY_CHUNK
    row = lambda bi, ti: (bi * nt + ti, 0)
    tok = lambda bi, ti: (bi, 0, ti)
    const2 = lambda bi, ti: (0, 0)
    const3 = lambda bi, ti: (0, 0, 0)
    out_shape = (
        jax.ShapeDtypeStruct((n, ATTN_W), BF16),
        jax.ShapeDtypeStruct((b, LANES, s), F32),
        jax.ShapeDtypeStruct((KV_HEADS, n, LANES), BF16),
        jax.ShapeDtypeStruct((b, LANES, s), F32),
        jax.ShapeDtypeStruct((b, KV_HEADS, s // KEY_CHUNK, VT_ROWS, KEY_CHUNK), BF16),
        jax.ShapeDtypeStruct((n, ATTN_W), BF16),
        jax.ShapeDtypeStruct((b, IDX_DIM, s), F32),
        jax.ShapeDtypeStruct((n, LANES), BF16),
        jax.ShapeDtypeStruct((n // TQ, IDX_HEADS, TQ), F32),
        jax.ShapeDtypeStruct((n, ATTN_W), BF16),
        jax.ShapeDtypeStruct((n, POOL_W), BF16),
        jax.ShapeDtypeStruct((b, 16, POOL_W), F32),
    )
    out_specs = (
        pl.BlockSpec((TM, ATTN_W), row),
        pl.BlockSpec((1, LANES, TM), tok),
        pl.BlockSpec((KV_HEADS, TM, LANES), lambda bi, ti: (0, bi * nt + ti, 0)),
        pl.BlockSpec((1, LANES, TM), tok),
        pl.BlockSpec((1, KV_HEADS, cpt, VT_ROWS, KEY_CHUNK), lambda bi, ti: (bi, 0, ti, 0, 0)),
        pl.BlockSpec((TM, ATTN_W), row),
        pl.BlockSpec((1, IDX_DIM, TM), tok),
        pl.BlockSpec((TM, LANES), row),
        pl.BlockSpec((TM // TQ, IDX_HEADS, TQ), lambda bi, ti: (bi * nt + ti, 0, 0)),
        pl.BlockSpec((TM, ATTN_W), row),
        pl.BlockSpec((TM, POOL_W), row),
        pl.BlockSpec((1, 16, POOL_W), lambda bi, ti: (bi, 0, 0)),
    )
    in_specs = [
        pl.BlockSpec((1, TM, D_MODEL), lambda bi, ti: (bi, ti, 0)),
        pl.BlockSpec((1, 1, 3 * D_MODEL), lambda bi, ti: (bi, 0, 0)),
        pl.BlockSpec((1, D_MODEL), const2),
        pl.BlockSpec((D_MODEL, N_PROJ), const2),
        pl.BlockSpec((1, ATTN_W), const2),
        pl.BlockSpec((1, LANES), const2),
        pl.BlockSpec((2, 2 * POOL_GC, 2 * POOL_GC), const3),
        pl.BlockSpec((1, POOL_W), const2),
    ]
    return pl.pallas_call(
        _proj_prompt_kernel,
        out_shape=out_shape,
        grid=(b, nt),
        in_specs=in_specs,
        out_specs=out_specs,
        scratch_shapes=[pltpu.VMEM((16 + TM, POOL_W), F32)],
        compiler_params=pltpu.CompilerParams(
            dimension_semantics=("arbitrary", "arbitrary"), vmem_limit_bytes=VMEM_LIMIT),
        name="proj_prompt",
    )(x, ada_p, norm_w, w_in_b, qnw, knw, wpool_b, pscale)


def _attn_tile(nch, row0, wit, kik_ref, kk_ref, vtx_ref, sc_ref, lg_ref, wq_ref, acc_ref):
    shape2 = (KEY_CHUNK, TQ)
    n_pairs = N_HEADS // 2
    group_of = lambda hp: (2 * hp) // (N_HEADS // KV_HEADS)

    for c in range(nch):
        keys = slice(c * KEY_CHUNK, (c + 1) * KEY_CHUNK)
        kk = kik_ref[keys, :]
        acc = jnp.zeros(shape2, F32)
        for hp in range(n_pairs):
            s2 = _nt_dot(kk, wq_ref[0, hp])
            acc = acc + jnp.maximum(s2[:, 0:TQ], 0.0) * wit[2 * hp:2 * hp + 1, :]
            acc = acc + jnp.maximum(s2[:, TQ:2 * TQ], 0.0) * wit[2 * hp + 1:2 * hp + 2, :]
        if c == nch - 1:
            kpos = c * KEY_CHUNK + lax.broadcasted_iota(I32, shape2, 0)
            qpos = row0 + lax.broadcasted_iota(I32, shape2, 1)
            acc = jnp.where(kpos <= qpos, acc, -jnp.inf)
        sc_ref[c] = acc

    for c in range(nch):
        keys = slice(c * KEY_CHUNK, (c + 1) * KEY_CHUNK)
        for hp in range(n_pairs):
            lg_ref[hp, c] = _nt_dot(kk_ref[group_of(hp), keys, :], wq_ref[1, hp])

    def count_pass(pred):
        parts = [_fold_rows(pred(sc_ref[c]).astype(F32), jnp.add) for c in range(nch)]
        while len(parts) > 1:
            parts = [sum(parts[j:j + 2]) for j in range(0, len(parts), 2)]
        return jnp.sum(parts[0], axis=0, keepdims=True)

    if nch * KEY_CHUNK <= TOPK:
        thr = jnp.full((1, TQ), -jnp.inf, F32)
        need = jnp.zeros((1, TQ), F32)
    else:
        thr = _kth_largest(lambda t: count_pass(lambda x: x >= t), (1, TQ), peel=SEARCH_PEEL)
        need = float(TOPK) - count_pass(lambda x: x > thr)

    tri = jnp.where(lax.broadcasted_iota(I32, (KEY_CHUNK, KEY_CHUNK), 1)
                    <= lax.broadcasted_iota(I32, (KEY_CHUNK, KEY_CHUNK), 0), 1.0, 0.0).astype(BF16)

    m_part = [jnp.full((SUBLANES, TQ), -jnp.inf, F32) for _ in range(N_HEADS)]
    eq_before = jnp.zeros((1, TQ), F32)
    for c in range(nch):
        sc = sc_ref[c]
        eq = sc == thr
        rank = _dot(tri, jnp.where(eq, 1.0, 0.0).astype(BF16)) + eq_before
        sel = (sc > thr) | (eq & (rank <= need))
        if c == nch - 1:
            sel = sel & (sc > -jnp.inf)
        eq_before = rank[KEY_CHUNK - 1:KEY_CHUNK, :]
        for hp in range(n_pairs):
            for par in range(2):
                cols = slice(par * TQ, (par + 1) * TQ)
                lg = jnp.where(sel, lg_ref[hp, c, :, cols], -jnp.inf)
                lg_ref[hp, c, :, cols] = lg
                m_part[2 * hp + par] = jnp.maximum(m_part[2 * hp + par], _fold_rows(lg, jnp.maximum))
    m_rows = [jnp.max(m, axis=0, keepdims=True) for m in m_part]

    for hp in range(n_pairs):
        m2 = jnp.concatenate([m_rows[2 * hp], m_rows[2 * hp + 1]], axis=1)
        acc = jnp.zeros((VT_ROWS, 2 * TQ), F32)
        for c in range(nch):
            p2 = jnp.exp(lg_ref[hp, c] - m2).astype(BF16)
            acc = acc + _dot(vtx_ref[0, group_of(hp), c], p2)
        acc_ref[hp] = acc


def _attn_prompt_kernel(qi_ref, wit_ref, qs_ref, sga_ref, kik_ref, kk_ref, vtx_ref, ag_ref,
                        sc_ref, lg_ref, wq_ref, acc_ref):
    step = pl.program_id(1)
    n_pairs = N_HEADS // 2
    tiles_per_step = KEY_CHUNK // TQ
    nch = step + 1
    lo = _lane_iota((TQ, LANES)) < HEAD_DIM

    def one_tile(sub, carry):
        rows = pl.ds(pl.multiple_of(sub * TQ, TQ), TQ)
        for kind, ref in enumerate((qi_ref, qs_ref)):
            for hp in range(n_pairs):
                pair = ref[rows, hp * LANES:(hp + 1) * LANES].astype(F32)
                wq_ref[kind, hp, 0:TQ, :] = jnp.where(lo, pair, 0.0).astype(BF16)
                wq_ref[kind, hp, TQ:2 * TQ, :] = jnp.where(lo, 0.0, pair).astype(BF16)

        wit = wit_ref[sub]
        row0 = (step * tiles_per_step + sub) * TQ
        for n_static in range(1, sc_ref.shape[0] + 1):
            @pl.when(nch == n_static)
            def _(n_static=n_static):
                _attn_tile(n_static, row0, wit, kik_ref, kk_ref, vtx_ref, sc_ref, lg_ref, wq_ref, acc_ref)

        for hp in range(n_pairs):
            a = acc_ref[hp]
            o0 = a[0:HEAD_DIM, 0:TQ] / a[HEAD_DIM:HEAD_DIM + 1, 0:TQ]
            o1 = a[0:HEAD_DIM, TQ:2 * TQ] / a[HEAD_DIM:HEAD_DIM + 1, TQ:2 * TQ]
            pair = jnp.concatenate([o0, o1], axis=0).T
            cols = slice(hp * LANES, (hp + 1) * LANES)
            ag_ref[rows, cols] = (pair * sga_ref[rows, cols].astype(F32)).astype(BF16)
        return carry

    lax.fori_loop(0, tiles_per_step, one_tile, 0)


def _attn_prompt(qi, wit, qs, sga, kik, kk, vtx, b, s):
    n = b * s
    nkc = s // KEY_CHUNK
    tps = KEY_CHUNK // TQ
    row = lambda bi, st: (bi * nkc + st, 0)
    return pl.pallas_call(
        _attn_prompt_kernel,
        out_shape=jax.ShapeDtypeStruct((n, ATTN_W), BF16),
        grid=(b, nkc),
        in_specs=[
            pl.BlockSpec((KEY_CHUNK, ATTN_W), row),
            pl.BlockSpec((tps, IDX_HEADS, TQ), lambda bi, st: (bi * nkc + st, 0, 0)),
            pl.BlockSpec((KEY_CHUNK, ATTN_W), row),
            pl.BlockSpec((KEY_CHUNK, ATTN_W), row),
            pl.BlockSpec((s, LANES), lambda bi, st: (bi, 0)),
            pl.BlockSpec((KV_HEADS, s, LANES), lambda bi, st: (0, bi, 0)),
            pl.BlockSpec((1, KV_HEADS, nkc, VT_ROWS, KEY_CHUNK), lambda bi, st: (bi, 0, 0, 0, 0)),
        ],
        out_specs=pl.BlockSpec((KEY_CHUNK, ATTN_W), row),
        scratch_shapes=[
            pltpu.VMEM((nkc, KEY_CHUNK, TQ), F32),
            pltpu.VMEM((N_HEADS // 2, nkc, KEY_CHUNK, 2 * TQ), F32),
            pltpu.VMEM((2, N_HEADS // 2, 2 * TQ, LANES), BF16),
            pltpu.VMEM((N_HEADS // 2, VT_ROWS, 2 * TQ), F32),
        ],
        compiler_params=pltpu.CompilerParams(
            dimension_semantics=("arbitrary", "arbitrary"), vmem_limit_bytes=VMEM_LIMIT),
        name="attn_prompt",
    )(qi, wit, qs, sga, kik, kk, vtx)


def _out_proj_kernel(x_ref, a_ref, p_ref, gate_ref, w_ref, o_ref):
    y = _dot(a_ref[...], w_ref[0:ATTN_W, :]) + _dot(p_ref[...], w_ref[ATTN_W:ATTN_W + POOL_W, :])
    o_ref[0] = x_ref[0] + gate_ref[0] * y


def _out_proj_prompt(x, ag, pg, ada_p, w_out_b):
    b, s, _ = x.shape
    nt = s // TM_OUT
    row = lambda bi, ti: (bi * nt + ti, 0)
    return pl.pallas_call(
        _out_proj_kernel,
        out_shape=jax.ShapeDtypeStruct(x.shape, F32),
        grid=(b, nt),
        in_specs=[
            pl.BlockSpec((1, TM_OUT, D_MODEL), lambda bi, ti: (bi, ti, 0)),
            pl.BlockSpec((TM_OUT, ATTN_W), row),
            pl.BlockSpec((TM_OUT, POOL_W), row),
            pl.BlockSpec((1, 1, D_MODEL), lambda bi, ti: (bi, 0, 2)),
            pl.BlockSpec((D_MODEL, D_MODEL), lambda bi, ti: (0, 0)),
        ],
        out_specs=pl.BlockSpec((1, TM_OUT, D_MODEL), lambda bi, ti: (bi, ti, 0)),
        compiler_params=pltpu.CompilerParams(
            dimension_semantics=("arbitrary", "arbitrary"), vmem_limit_bytes=VMEM_LIMIT),
        name="out_proj_prompt",
    )(x, ag, pg, ada_p, w_out_b)


def _proj_sample_kernel(x_ref, ada_ref, nw_ref, w_ref, qnw_ref, knw_ref, wpool_ref, ps_ref, hist_ref,
                        qpad_ref, k_ref, v_ref, ki_ref, qi_ref, wi_ref, snew_ref, lnew_ref,
                        sga_ref, pg_ref, pool_ref):
    nb = x_ref.shape[0]
    x = x_ref[...]
    shift = ada_ref[:, 0:D_MODEL]
    scale = ada_ref[:, D_MODEL:2 * D_MODEL]
    hb = _modulated_norm(x, nw_ref[...], scale, shift).astype(BF16)

    lane = _lane_iota((nb, LANES))
    lo = lane < HEAD_DIM
    seg256 = _seg_ones(2 * LANES, HEAD_DIM)
    seg128 = _seg_ones(LANES, HEAD_DIM)
    head_sel = jnp.where(lax.broadcasted_iota(I32, (ATTN_W, LANES), 0) // HEAD_DIM
                         == lax.broadcasted_iota(I32, (ATTN_W, LANES), 1), 1.0, 0.0).astype(BF16)

    def head_sums(prod):
        hi = prod.astype(BF16)
        rest = (prod - hi.astype(F32)).astype(BF16)
        return _dot(hi, head_sel) + _dot(rest, head_sel)

    q = _head_rms(_dot(hb, w_ref[:, C_Q:C_Q + ATTN_W]), seg256, qnw_ref[...])
    qb = q.astype(BF16)
    k = _head_rms(_dot(hb, w_ref[:, C_K:C_K + LANES]), seg128, knw_ref[...])
    k_ref[...] = k
    v = _dot(hb, w_ref[:, C_V:C_V + LANES])
    v_ref[...] = v

    for hp in range(N_HEADS // 2):
        pair = q[:, hp * LANES:(hp + 1) * LANES]
        pair_sw = pltpu.roll(pair, HEAD_DIM, axis=1)
        g = (2 * hp) // (N_HEADS // KV_HEADS)
        if g == 0:
            h_even, h_odd = jnp.where(lo, pair, 0.0), jnp.where(lo, pair_sw, 0.0)
        else:
            h_even, h_odd = jnp.where(lo, 0.0, pair_sw), jnp.where(lo, 0.0, pair)
        qpad_ref[:, (2 * hp) * LANES:(2 * hp + 1) * LANES] = h_even.astype(BF16)
        qpad_ref[:, (2 * hp + 1) * LANES:(2 * hp + 2) * LANES] = h_odd.astype(BF16)

    kq = k.astype(BF16).astype(F32)
    kq_sw = pltpu.roll(kq, HEAD_DIM, axis=1)
    k0t = jnp.where(lo, kq, kq_sw)
    k1t = jnp.where(lo, kq_sw, kq)
    qf = qb.astype(F32)
    prod = jnp.concatenate([qf[:, 0:LANES] * k0t, qf[:, LANES:2 * LANES] * k0t,
                            qf[:, 2 * LANES:3 * LANES] * k1t, qf[:, 3 * LANES:4 * LANES] * k1t], axis=1)
    lnew_ref[...] = head_sums(prod)

    qi = _dot(hb, w_ref[:, C_QI:C_QI + ATTN_W])
    qib = qi.astype(BF16)
    qi_ref[...] = qib
    kw = _dot(hb, w_ref[:, C_KW:C_KW + LANES])
    kw_sw = pltpu.roll(kw, HEAD_DIM, axis=1)
    ki_ref[...] = kw[:, 0:IDX_DIM]
    wi_full = jnp.where(lane < IDX_HEADS, kw_sw, 0.0) * ((IDX_HEADS * IDX_DIM) ** -0.5)
    wi_ref[...] = wi_full[:, 0:IDX_HEADS]

    kib = kw.astype(BF16).astype(F32)
    kit = jnp.where(lo, kib, pltpu.roll(kib, HEAD_DIM, axis=1))
    qif = qib.astype(F32)
    prod_i = jnp.concatenate([qif[:, j * LANES:(j + 1) * LANES] * kit for j in range(4)], axis=1)
    s_new = jnp.maximum(head_sums(prod_i), 0.0) * wi_full
    snew_ref[...] = jnp.broadcast_to(jnp.sum(s_new, axis=1, keepdims=True), (nb, LANES))

    sga_ref[...] = _silu(_dot(hb, w_ref[:, C_GA:C_GA + ATTN_W])).astype(BF16)

    u = _dot(hb, w_ref[:, C_U:C_U + POOL_W])
    gp = _dot(hb, w_ref[:, C_GP:C_GP + POOL_W])
    for j in range(POOL_HIST - 1):
        pool_ref[j] = hist_ref[j + 1]
    pool_ref[POOL_HIST - 1] = u
    ds = []
    for g, w in enumerate(POOL_WINDOWS):
        cs = slice(g * POOL_GC, (g + 1) * POOL_GC)
        s = u[:, cs]
        for j in range(1, w):
            s = s + hist_ref[POOL_HIST - j, :, cs]
        ds.append(s / float(w) - u[:, cs])
    d = jnp.concatenate(ds, axis=1)
    pg_ref[...] = _pool_mix(d, wpool_ref, ps_ref[...], gp).astype(BF16)


def _proj_sample(x, ada_s, norm_w, w_in_b, qnw, knw, wpool_b, pscale, hist_t):
    nb = x.shape[0]
    out_shape = (
        jax.ShapeDtypeStruct((nb, N_HEADS * LANES), BF16),
        jax.ShapeDtypeStruct((nb, LANES), F32),
        jax.ShapeDtypeStruct((nb, LANES), F32),
        jax.ShapeDtypeStruct((nb, IDX_DIM), F32),
        jax.ShapeDtypeStruct((nb, ATTN_W), BF16),
        jax.ShapeDtypeStruct((nb, IDX_HEADS), F32),
        jax.ShapeDtypeStruct((nb, LANES), F32),
        jax.ShapeDtypeStruct((nb, LANES), F32),
        jax.ShapeDtypeStruct((nb, ATTN_W), BF16),
        jax.ShapeDtypeStruct((nb, POOL_W), BF16),
        jax.ShapeDtypeStruct((POOL_HIST, nb, POOL_W), F32),
    )
    return pl.pallas_call(
        _proj_sample_kernel,
        out_shape=out_shape,
        compiler_params=pltpu.CompilerParams(vmem_limit_bytes=VMEM_LIMIT),
        name="proj_sample",
    )(x, ada_s, norm_w, w_in_b, qnw, knw, wpool_b, pscale, hist_t)


def _page_copies(pt_ref, pages_hbm, buf_ref, sem_ref, step, slot, group, n_pages):
    copies = []
    for j in range(group):
        for p in range(n_pages):
            page = pt_ref[step * group + j, p]
            copies.append(pltpu.make_async_copy(
                pages_hbm.at[page], buf_ref.at[slot, j, :, pl.ds(p * PAGE, PAGE)], sem_ref.at[slot]))
    return copies


def _score_sample_kernel(pt_ref, qi_ref, wi_ref, kidx_hbm, o_ref, buf_ref, sem_ref, *, group, n_pages):
    step = pl.program_id(0)
    slot = lax.rem(step, 2)

    def copies(st, sl):
        return _page_copies(pt_ref, kidx_hbm, buf_ref, sem_ref, st, sl, group, n_pages)

    @pl.when(step == 0)
    def _():
        for cp in copies(step, slot):
            cp.start()

    @pl.when(step + 1 < pl.num_programs(0))
    def _():
        for cp in copies(step + 1, 1 - slot):
            cp.start()

    for cp in copies(step, slot):
        cp.wait()

    for j in range(group):
        ki_t = buf_ref[slot, j].astype(BF16)
        s = _dot(qi_ref[j], ki_t)
        o_ref[j:j + 1, :] = jnp.sum(jnp.maximum(s, 0.0) * wi_ref[j], axis=0, keepdims=True)


def _score_sample(page_table, qi3, wi3, kidx_t):
    nb, n_pages = page_table.shape
    n_keys = n_pages * PAGE
    g = SCORE_GROUP
    return pl.pallas_call(
        functools.partial(_score_sample_kernel, group=g, n_pages=n_pages),
        out_shape=jax.ShapeDtypeStruct((nb, n_keys), F32),
        grid_spec=pltpu.PrefetchScalarGridSpec(
            num_scalar_prefetch=1,
            grid=(nb // g,),
            in_specs=[
                pl.BlockSpec((g, IDX_HEADS, IDX_DIM), lambda s, pt: (s, 0, 0)),
                pl.BlockSpec((g, IDX_HEADS, 1), lambda s, pt: (s, 0, 0)),
                pl.BlockSpec(memory_space=pl.ANY),
            ],
            out_specs=pl.BlockSpec((g, n_keys), lambda s, pt: (s, 0)),
            scratch_shapes=[
                pltpu.VMEM((2, g, IDX_DIM, n_keys), F32),
                pltpu.SemaphoreType.DMA((2,)),
            ],
        ),
        compiler_params=pltpu.CompilerParams(
            dimension_semantics=("arbitrary",), vmem_limit_bytes=VMEM_LIMIT),
        name="score_sample",
    )(page_table, qi3, wi3, kidx_t)


def _select_sample_kernel(sc_ref, snew_ref, mask_ref, mnew_ref):
    nb, n_keys = sc_ref.shape
    nch = n_keys // KEY_CHUNK
    snew = snew_ref[...]
    ones_mat = jnp.ones((LANES, LANES), BF16)

    def count_pass(pred):
        acc = jnp.zeros((nb, LANES), F32)
        for c in range(n_keys // LANES):
            acc = acc + pred(sc_ref[:, c * LANES:(c + 1) * LANES]).astype(F32)
        return _dot(acc.astype(BF16), ones_mat) + pred(snew).astype(F32)

    thr = _kth_largest(lambda t: count_pass(lambda x: x >= t), (nb, LANES))
    need = float(TOPK) - count_pass(lambda x: x > thr)
    thr2 = jnp.concatenate([thr, thr], axis=1)
    need2 = jnp.concatenate([need, need], axis=1)
    tri = jnp.where(lax.broadcasted_iota(I32, (KEY_CHUNK, KEY_CHUNK), 0)
                    <= lax.broadcasted_iota(I32, (KEY_CHUNK, KEY_CHUNK), 1), 1.0, 0.0).astype(BF16)
    ones_cl = jnp.ones((KEY_CHUNK, LANES), BF16)

    eq_before = jnp.zeros((nb, LANES), F32)
    for c in range(nch):
        sc = sc_ref[:, c * KEY_CHUNK:(c + 1) * KEY_CHUNK]
        eq = sc == thr2
        eqb = jnp.where(eq, 1.0, 0.0).astype(BF16)
        rank = _dot(eqb, tri) + jnp.concatenate([eq_before, eq_before], axis=1)
        sel = (sc > thr2) | (eq & (rank <= need2))
        mask_ref[:, c * KEY_CHUNK:(c + 1) * KEY_CHUNK] = jnp.where(sel, 0.0, -jnp.inf)
        eq_before = eq_before + _dot(eqb, ones_cl)
    sel_new = (snew > thr) | ((snew == thr) & (eq_before + 1.0 <= need))
    mnew_ref[...] = jnp.where(sel_new, 0.0, -jnp.inf)


def _select_sample(scores, snew):
    nb, n_keys = scores.shape
    return pl.pallas_call(
        _select_sample_kernel,
        out_shape=(jax.ShapeDtypeStruct((nb, n_keys), F32), jax.ShapeDtypeStruct((nb, LANES), F32)),
        name="select_sample",
    )(scores, snew)


def _attn_sample_kernel(pt_ref, q_ref, mask_ref, lnew_ref, mnew_ref, vnew_ref, k_hbm, v_hbm,
                        o_ref, kbuf_ref, vbuf_ref, ksem_ref, vsem_ref, *, group, n_pages):
    step = pl.program_id(0)
    slot = lax.rem(step, 2)

    def copies(st, sl):
        return (_page_copies(pt_ref, k_hbm, kbuf_ref, ksem_ref, st, sl, group, n_pages)
                + _page_copies(pt_ref, v_hbm, vbuf_ref, vsem_ref, st, sl, group, n_pages))

    @pl.when(step == 0)
    def _():
        for cp in copies(step, slot):
            cp.start()

    @pl.when(step + 1 < pl.num_programs(0))
    def _():
        for cp in copies(step + 1, 1 - slot):
            cp.start()

    for cp in copies(step, slot):
        cp.wait()

    for j in range(group):
        k_t = kbuf_ref[slot, j].astype(BF16)
        lg = _dot(q_ref[j], k_t) + mask_ref[j]
        lg_n = lnew_ref[j] + mnew_ref[j]
        m = jnp.maximum(jnp.max(lg, axis=1, keepdims=True), lg_n)
        p = jnp.exp(lg - m)
        p_n = jnp.exp(lg_n - m)
        l = jnp.sum(p, axis=1, keepdims=True) + p_n
        acc = _nt_dot(p.astype(BF16), vbuf_ref[slot, j].astype(BF16)) + p_n * vnew_ref[j]
        o_ref[j] = acc / l


def _attn_sample(page_table, qpad3, mask3, lnew3, mnew3, vnew3, k_t, v_t):
    nb, n_pages = page_table.shape
    n_keys = n_pages * PAGE
    g = ATTN_GROUP
    per_s = lambda s, pt: (s, 0, 0)
    return pl.pallas_call(
        functools.partial(_attn_sample_kernel, group=g, n_pages=n_pages),
        out_shape=jax.ShapeDtypeStruct((nb, N_HEADS, LANES), F32),
        grid_spec=pltpu.PrefetchScalarGridSpec(
            num_scalar_prefetch=1,
            grid=(nb // g,),
            in_specs=[
                pl.BlockSpec((g, N_HEADS, LANES), per_s),
                pl.BlockSpec((g, 1, n_keys), per_s),
                pl.BlockSpec((g, N_HEADS, 1), per_s),
                pl.BlockSpec((g, N_HEADS, 1), per_s),
                pl.BlockSpec((g, 1, LANES), per_s),
                pl.BlockSpec(memory_space=pl.ANY),
                pl.BlockSpec(memory_space=pl.ANY),
            ],
            out_specs=pl.BlockSpec((g, N_HEADS, LANES), per_s),
            scratch_shapes=[
                pltpu.VMEM((2, g, LANES, n_keys), F32),
                pltpu.VMEM((2, g, LANES, n_keys), F32),
                pltpu.SemaphoreType.DMA((2,)),
                pltpu.SemaphoreType.DMA((2,)),
            ],
        ),
        compiler_params=pltpu.CompilerParams(
            dimension_semantics=("arbitrary",), vmem_limit_bytes=VMEM_LIMIT),
        name="attn_sample",
    )(page_table, qpad3, mask3, lnew3, mnew3, vnew3, k_t, v_t)


def _out_proj_sample_kernel(x_ref, a_ref, sga_ref, p_ref, ada_ref, w_ref, o_ref):
    lo = _lane_iota((x_ref.shape[0], LANES)) < HEAD_DIM
    pairs = []
    for hp in range(N_HEADS // 2):
        even = a_ref[:, (2 * hp) * LANES:(2 * hp + 1) * LANES]
        odd = a_ref[:, (2 * hp + 1) * LANES:(2 * hp + 2) * LANES]
        if (2 * hp) // (N_HEADS // KV_HEADS) == 0:
            pairs.append(jnp.where(lo, even, pltpu.roll(odd, HEAD_DIM, axis=1)))
        else:
            pairs.append(jnp.where(lo, pltpu.roll(even, HEAD_DIM, axis=1), odd))
    a = jnp.concatenate(pairs, axis=1)
    ag = (a * sga_ref[...].astype(F32)).astype(BF16)
    y = _dot(ag, w_ref[0:ATTN_W, :]) + _dot(p_ref[...], w_ref[ATTN_W:ATTN_W + POOL_W, :])
    o_ref[...] = x_ref[...] + ada_ref[:, 2 * D_MODEL:3 * D_MODEL] * y


def _out_proj_sample(x, a, sga, pg, ada_s, w_out_b):
    return pl.pallas_call(
        _out_proj_sample_kernel,
        out_shape=jax.ShapeDtypeStruct(x.shape, F32),
        compiler_params=pltpu.CompilerParams(vmem_limit_bytes=VMEM_LIMIT),
        name="out_proj_sample",
    )(x, a, sga, pg, ada_s, w_out_b)


def _permute_w_in(w_in):
    q, k, v, qi, ki, wi, ga, u, gp = jnp.split(
        w_in, np.cumsum([512, 128, 128, 512, 64, 8, 512, 512]).tolist(), axis=1)
    pad = jnp.zeros((w_in.shape[0], LANES - IDX_DIM - IDX_HEADS), w_in.dtype)
    return jnp.concatenate([q, k, v, qi, ki, wi, pad, ga, u, gp], axis=1).astype(BF16)


def kernel(x_prompt, x_sample, cache_k, cache_v, cache_kidx, state_pool, page_table, c_prompt, c_sample,
           norm_w, w_ada, b_ada, w_in, q_norm_w, k_norm_w, w_pool, pool_scale, w_out):
    bp, s, _ = x_prompt.shape
    bs = x_sample.shape[0]
    assert w_in.shape[0] == 1 and x_sample.shape[1] == 1, "single layer, single decode token"
    n_phys = cache_k.shape[1]

    w_in_b = _permute_w_in(w_in[0])
    w_out_b = w_out[0].astype(BF16)
    zero_blk = jnp.zeros((POOL_GC, POOL_GC), w_pool.dtype)
    wpool_b = jnp.stack([jnp.block([[w_pool[0, 2 * p], zero_blk], [zero_blk, w_pool[0, 2 * p + 1]]])
                         for p in range(len(POOL_WINDOWS) // 2)]).astype(BF16)
    qnw = jnp.tile(q_norm_w[0], N_HEADS)[None, :] * (HEAD_DIM ** -0.5)
    knw = jnp.tile(k_norm_w[0], KV_HEADS)[None, :]
    nw = norm_w[0][None, :]
    pscale = pool_scale[0][None, :]

    ada_p, ada_s = _ada(c_prompt, c_sample, w_ada, b_ada)
    ada_p = ada_p.reshape(bp, 1, 3 * D_MODEL)

    (qs, k_t_p, kk, v_t_p, vtx, qi, ki_t_p, kik, wit, sga, pg, ulast) = _proj_prompt(
        x_prompt, ada_p, nw, w_in_b, qnw, knw, wpool_b, pscale)
    ag = _attn_prompt(qi, wit, qs, sga, kik, kk, vtx, bp, s)
    y_prompt = _out_proj_prompt(x_prompt, ag, pg, ada_p, w_out_b)

    hist_t = jnp.transpose(state_pool[0], (1, 0, 2))
    (qpad, k_s, v_s, ki_s, qi_s, wi_s, snew, lnew, sga_s, pg_s, pool_s) = _proj_sample(
        x_sample[:, 0, :], ada_s, nw, w_in_b, qnw, knw, wpool_b, pscale, hist_t)
    n_pages = page_table.shape[1]
    kidx_t = jnp.transpose(cache_kidx[0], (0, 2, 1))
    k_t = jnp.transpose(cache_k[0], (0, 2, 3, 1)).reshape(n_phys, LANES, PAGE)
    v_t = jnp.transpose(cache_v[0], (0, 2, 3, 1)).reshape(n_phys, LANES, PAGE)
    scores = _score_sample(page_table, qi_s.reshape(bs, IDX_HEADS, IDX_DIM), wi_s.reshape(bs, IDX_HEADS, 1),
                           kidx_t)
    mask, mnew = _select_sample(scores, snew)
    o_s = _attn_sample(
        page_table, qpad.reshape(bs, N_HEADS, LANES),
        mask.reshape(bs, 1, n_pages * PAGE),
        lnew[:, :N_HEADS].reshape(bs, N_HEADS, 1),
        jnp.broadcast_to(mnew[:, :1], (bs, N_HEADS)).reshape(bs, N_HEADS, 1),
        v_s.reshape(bs, 1, LANES), k_t, v_t)
    y_sample = _out_proj_sample(x_sample[:, 0, :], o_s.reshape(bs, N_HEADS * LANES), sga_s, pg_s, ada_s, w_out_b)

    to_heads = lambda a: jnp.transpose(a.reshape(bp, KV_HEADS, HEAD_DIM, s), (0, 3, 1, 2))[None]
    return (
        y_prompt,
        y_sample[:, None, :],
        to_heads(k_t_p),
        to_heads(v_t_p),
        jnp.transpose(ki_t_p, (0, 2, 1))[None],
        ulast[:, 1:, :][None],
        k_s.reshape(1, bs, 1, KV_HEADS, HEAD_DIM),
        v_s.reshape(1, bs, 1, KV_HEADS, HEAD_DIM),
        ki_s.reshape(1, bs, 1, IDX_DIM),
        jnp.transpose(pool_s, (1, 0, 2))[None],
    )
```

```python
import functools

import jax
import jax.numpy as jnp
import numpy as np
from jax import lax
from jax.experimental import pallas as pl
from jax.experimental.pallas import tpu as pltpu

F32 = jnp.float32
BF16 = jnp.bfloat16
I32 = jnp.int32

D_MODEL = 1024
ATTN_W = 512
POOL_W = 512
HEAD_DIM = 64
N_HEADS = 8
KV_HEADS = 2
IDX_HEADS = 8
IDX_DIM = 64
TOPK = 256
POOL_WINDOWS = (2, 4, 8, 16)
POOL_GC = 128
POOL_HIST = 15
EPS = 1e-6
PAGE = 128

LANES = 128
SUBLANES = 8
BF16_ROWS = 16
KEY_CHUNK = 256
TQ = 128
TM = 512
TM_OUT = 1024
VT_ROWS = HEAD_DIM + BF16_ROWS
SEARCH_PEEL = 10
SCORE_GROUP = 8
ATTN_GROUP = 4
INT_MIN = np.int32(-2 ** 31)
VMEM_LIMIT = 56 * 1024 * 1024

C_Q, C_K, C_V, C_QI, C_KW, C_GA, C_U, C_GP, N_PROJ = 0, 512, 640, 768, 1280, 1408, 1920, 2432, 2944


def _nt_dot(a, b):
    return lax.dot_general(a, b, (((1,), (1,)), ((), ())), preferred_element_type=F32)


def _dot(a, b):
    return jnp.dot(a, b, preferred_element_type=F32)


def _silu(z):
    return z / (1.0 + jnp.exp(-z))


def _lane_iota(shape):
    return lax.broadcasted_iota(I32, shape, len(shape) - 1)


def _seg_ones(n, seg):
    r = lax.broadcasted_iota(I32, (n, n), 0) // seg
    c = lax.broadcasted_iota(I32, (n, n), 1) // seg
    return jnp.where(r == c, 1.0, 0.0).astype(BF16)


def _head_rms(z, seg_mat, w):
    n = seg_mat.shape[0]
    sq = (z * z).astype(BF16)
    ss = jnp.concatenate([_dot(sq[:, j:j + n], seg_mat) for j in range(0, z.shape[1], n)], axis=1)
    return z * lax.rsqrt(ss * (1.0 / HEAD_DIM) + EPS) * w


def _float_of_rank(u):
    key = u ^ INT_MIN
    bits = key ^ ((key >> 31) & np.int32(0x7FFFFFFF))
    return lax.bitcast_convert_type(bits, F32)


def _kth_largest(count_ge, shape, peel=0):
    def bit_body(it, ans):
        cand = ans | jnp.left_shift(jnp.int32(1), 31 - it)
        return jnp.where(count_ge(_float_of_rank(cand)) >= float(TOPK), cand, ans)

    ans = jnp.zeros(shape, I32)
    for it in range(peel):
        ans = bit_body(it, ans)
    ans = lax.fori_loop(peel, 32, bit_body, ans)
    return jnp.where(ans == 0, -jnp.inf, _float_of_rank(ans))


def _fold_rows(x, op):
    parts = [x[r:r + SUBLANES] for r in range(0, x.shape[0], SUBLANES)]
    while len(parts) > 1:
        parts = [op(parts[i], parts[i + 1]) for i in range(0, len(parts), 2)]
    return parts[0]


def _ada_kernel(cp_ref, cs_ref, w_ref, b_ref, op_ref, os_ref):
    w = w_ref[0].astype(BF16)
    op_ref[...] = _dot(_silu(cp_ref[...]).astype(BF16), w) + b_ref[...]
    os_ref[...] = _dot(_silu(cs_ref[...]).astype(BF16), w) + b_ref[...]


def _ada(c_prompt, c_sample, w_ada, b_ada):
    bp, bs = c_prompt.shape[0], c_sample.shape[0]
    return pl.pallas_call(
        _ada_kernel,
        out_shape=(jax.ShapeDtypeStruct((bp, 3 * D_MODEL), F32), jax.ShapeDtypeStruct((bs, 3 * D_MODEL), F32)),
        grid=(3,),
        in_specs=[
            pl.BlockSpec((bp, D_MODEL), lambda j: (0, 0)),
            pl.BlockSpec((bs, D_MODEL), lambda j: (0, 0)),
            pl.BlockSpec((1, D_MODEL, D_MODEL), lambda j: (0, 0, j)),
            pl.BlockSpec((1, D_MODEL), lambda j: (0, j)),
        ],
        out_specs=(pl.BlockSpec((bp, D_MODEL), lambda j: (0, j)), pl.BlockSpec((bs, D_MODEL), lambda j: (0, j))),
        compiler_params=pltpu.CompilerParams(dimension_semantics=("arbitrary",)),
        name="ada_ln",
    )(c_prompt, c_sample, w_ada, b_ada)


def _modulated_norm(x, norm_w, scale, shift):
    ms = jnp.mean(x * x, axis=-1, keepdims=True)
    return (x * lax.rsqrt(ms + EPS)) * norm_w * (1.0 + scale) + shift


def _pool_mix(d, wpool_ref, pscale, gp):
    db = d.astype(BF16)
    wide = 2 * POOL_GC
    y = jnp.concatenate([_dot(db[:, p * wide:(p + 1) * wide], wpool_ref[p])
                         for p in range(len(POOL_WINDOWS) // 2)], axis=1)
    return y * pscale * _silu(gp)


def _permuted_weight_block(wt_ref, g):
    cut = C_KW + IDX_DIM + IDX_HEADS
    c0 = g * LANES
    if c0 + LANES <= cut:
        blk = wt_ref[c0:c0 + LANES, :]
    elif c0 < cut:
        blk = jnp.concatenate([wt_ref[c0:cut, :], jnp.zeros((c0 + LANES - cut, D_MODEL), F32)], axis=0)
    else:
        r0 = c0 - (C_GA - cut)
        blk = wt_ref[r0:r0 + LANES, :]
    return blk.T.astype(BF16)


def _proj_prompt_kernel(x_ref, ada_ref, nw_ref, wt_ref, qnw_ref, knw_ref, wpool_ref, ps_ref,
                        qs_ref, kt_ref, kk_ref, vt_ref, vtx_ref, qi_ref, kit_ref, kik_ref,
                        wit_ref, sga_ref, pg_ref, ulast_ref, w_ref,
                        ext_ref):
    t = pl.program_id(1)

    @pl.when((pl.program_id(0) == 0) & (t == 0))
    def _():
        for g in range(N_PROJ // LANES):
            w_ref[:, g * LANES:(g + 1) * LANES] = _permuted_weight_block(wt_ref, g)

    ada = ada_ref[0]
    shift = ada[:, 0:D_MODEL]
    scale = ada[:, D_MODEL:2 * D_MODEL]
    sub_rows = KEY_CHUNK
    lo = _lane_iota((sub_rows, LANES)) < HEAD_DIM
    seg256 = _seg_ones(2 * LANES, HEAD_DIM)
    seg128 = _seg_ones(LANES, HEAD_DIM)
    ones = jnp.ones((BF16_ROWS, KEY_CHUNK), BF16)

    @pl.when(t == 0)
    def _():
        ext_ref[0:16, :] = jnp.zeros((16, POOL_W), F32)

    for sub in range(TM // sub_rows):
        rows = slice(sub * sub_rows, (sub + 1) * sub_rows)
        hb = _modulated_norm(x_ref[0, rows, :], nw_ref[...], scale, shift).astype(BF16)

        q = _dot(hb, w_ref[:, C_Q:C_Q + ATTN_W])
        qs_ref[rows, :] = _head_rms(q, seg256, qnw_ref[...]).astype(BF16)

        kv = _dot(hb, w_ref[:, C_K:C_K + 2 * LANES])
        k = _head_rms(kv[:, 0:LANES], seg128, knw_ref[...])
        kt_ref[0, :, rows] = k.T
        k_sw = pltpu.roll(k, HEAD_DIM, axis=1)
        kk_ref[0, rows, :] = jnp.where(lo, k, k_sw).astype(BF16)
        kk_ref[1, rows, :] = jnp.where(lo, k_sw, k).astype(BF16)

        v_t = kv[:, LANES:2 * LANES].T
        vt_ref[0, :, rows] = v_t
        for g in range(KV_HEADS):
            vtx_ref[0, g, sub, 0:HEAD_DIM, :] = v_t[g * HEAD_DIM:(g + 1) * HEAD_DIM].astype(BF16)
            vtx_ref[0, g, sub, HEAD_DIM:VT_ROWS, :] = ones

        qi_ref[rows, :] = _dot(hb, w_ref[:, C_QI:C_QI + ATTN_W]).astype(BF16)
        kw = _dot(hb, w_ref[:, C_KW:C_KW + LANES])
        kw_t = kw.T
        kit_ref[0, :, rows] = kw_t[0:IDX_DIM]
        wi_t = kw_t[IDX_DIM:IDX_DIM + IDX_HEADS] * ((IDX_HEADS * IDX_DIM) ** -0.5)
        for j in range(sub_rows // TQ):
            wit_ref[sub * (sub_rows // TQ) + j] = wi_t[:, j * TQ:(j + 1) * TQ]
        kik_ref[rows, :] = jnp.where(lo, kw, pltpu.roll(kw, HEAD_DIM, axis=1)).astype(BF16)

        sga_ref[rows, :] = _silu(_dot(hb, w_ref[:, C_GA:C_GA + ATTN_W])).astype(BF16)

        u = _dot(hb, w_ref[:, C_U:C_U + POOL_W])
        gp = _dot(hb, w_ref[:, C_GP:C_GP + POOL_W])
        base = 16 + sub * sub_rows
        ext_ref[base:base + sub_rows, :] = u
        pos = t * TM + sub * sub_rows + lax.broadcasted_iota(I32, (sub_rows, POOL_GC), 0)
        ds = []
        for g, w in enumerate(POOL_WINDOWS):
            cs = slice(g * POOL_GC, (g + 1) * POOL_GC)
            s = u[:, cs]
            for j in range(1, w):
                s = s + ext_ref[base - j:base - j + sub_rows, cs]
            cnt = jnp.minimum(pos + 1, w).astype(F32)
            ds.append(s / cnt - u[:, cs])
        d = jnp.concatenate(ds, axis=1)
        pg_ref[rows, :] = _pool_mix(d, wpool_ref, ps_ref[...], gp).astype(BF16)

    tail = ext_ref[TM:TM + 16, :]
    ulast_ref[0] = tail
    ext_ref[0:16, :] = tail


def _proj_prompt(x, ada_p, norm_w, w_in_t, qnw, knw, wpool_b, pscale):
    b, s, _ = x.shape
    n = b * s
    nt = s // TM
    cpt = TM // KEY_CHUNK
    row = lambda bi, ti: (bi * nt + ti, 0)
    tok = lambda bi, ti: (bi, 0, ti)
    const2 = lambda bi, ti: (0, 0)
    const3 = lambda bi, ti: (0, 0, 0)
    out_shape = (
        jax.ShapeDtypeStruct((n, ATTN_W), BF16),
        jax.ShapeDtypeStruct((b, LANES, s), F32),
        jax.ShapeDtypeStruct((KV_HEADS, n, LANES), BF16),
        jax.ShapeDtypeStruct((b, LANES, s), F32),
        jax.ShapeDtypeStruct((b, KV_HEADS, s // KEY_CHUNK, VT_ROWS, KEY_CHUNK), BF16),
        jax.ShapeDtypeStruct((n, ATTN_W), BF16),
        jax.ShapeDtypeStruct((b, IDX_DIM, s), F32),
        jax.ShapeDtypeStruct((n, LANES), BF16),
        jax.ShapeDtypeStruct((n // TQ, IDX_HEADS, TQ), F32),
        jax.ShapeDtypeStruct((n, ATTN_W), BF16),
        jax.ShapeDtypeStruct((n, POOL_W), BF16),
        jax.ShapeDtypeStruct((b, 16, POOL_W), F32),
        jax.ShapeDtypeStruct((D_MODEL, N_PROJ), BF16),
    )
    out_specs = (
        pl.BlockSpec((TM, ATTN_W), row),
        pl.BlockSpec((1, LANES, TM), tok),
        pl.BlockSpec((KV_HEADS, TM, LANES), lambda bi, ti: (0, bi * nt + ti, 0)),
        pl.BlockSpec((1, LANES, TM), tok),
        pl.BlockSpec((1, KV_HEADS, cpt, VT_ROWS, KEY_CHUNK), lambda bi, ti: (bi, 0, ti, 0, 0)),
        pl.BlockSpec((TM, ATTN_W), row),
        pl.BlockSpec((1, IDX_DIM, TM), tok),
        pl.BlockSpec((TM, LANES), row),
        pl.BlockSpec((TM // TQ, IDX_HEADS, TQ), lambda bi, ti: (bi * nt + ti, 0, 0)),
        pl.BlockSpec((TM, ATTN_W), row),
        pl.BlockSpec((TM, POOL_W), row),
        pl.BlockSpec((1, 16, POOL_W), lambda bi, ti: (bi, 0, 0)),
        pl.BlockSpec((D_MODEL, N_PROJ), const2),
    )
    in_specs = [
        pl.BlockSpec((1, TM, D_MODEL), lambda bi, ti: (bi, ti, 0)),
        pl.BlockSpec((1, 1, 3 * D_MODEL), lambda bi, ti: (bi, 0, 0)),
        pl.BlockSpec((1, D_MODEL), const2),
        pl.BlockSpec(w_in_t.shape, const2),
        pl.BlockSpec((1, ATTN_W), const2),
        pl.BlockSpec((1, LANES), const2),
        pl.BlockSpec((2, 2 * POOL_GC, 2 * POOL_GC), const3),
        pl.BlockSpec((1, POOL_W), const2),
    ]
    return pl.pallas_call(
        _proj_prompt_kernel,
        out_shape=out_shape,
        grid=(b, nt),
        in_specs=in_specs,
        out_specs=out_specs,
        scratch_shapes=[pltpu.VMEM((16 + TM, POOL_W), F32)],
        compiler_params=pltpu.CompilerParams(
            dimension_semantics=("arbitrary", "arbitrary"), vmem_limit_bytes=VMEM_LIMIT),
        name="proj_prompt",
    )(x, ada_p, norm_w, w_in_t, qnw, knw, wpool_b, pscale)


def _attn_tile(nch, row0, wit, kik_ref, kk_ref, vtx_ref, sc_ref, lg_ref, wq_ref, acc_ref):
    shape2 = (KEY_CHUNK, TQ)
    n_pairs = N_HEADS // 2
    group_of = lambda hp: (2 * hp) // (N_HEADS // KV_HEADS)

    for c in range(nch):
        keys = slice(c * KEY_CHUNK, (c + 1) * KEY_CHUNK)
        kk = kik_ref[keys, :]
        acc = jnp.zeros(shape2, F32)
        for hp in range(n_pairs):
            s2 = _nt_dot(kk, wq_ref[0, hp])
            acc = acc + jnp.maximum(s2[:, 0:TQ], 0.0) * wit[2 * hp:2 * hp + 1, :]
            acc = acc + jnp.maximum(s2[:, TQ:2 * TQ], 0.0) * wit[2 * hp + 1:2 * hp + 2, :]
        if c == nch - 1:
            kpos = c * KEY_CHUNK + lax.broadcasted_iota(I32, shape2, 0)
            qpos = row0 + lax.broadcasted_iota(I32, shape2, 1)
            acc = jnp.where(kpos <= qpos, acc, -jnp.inf)
        sc_ref[c] = acc

    for c in range(nch):
        keys = slice(c * KEY_CHUNK, (c + 1) * KEY_CHUNK)
        for hp in range(n_pairs):
            lg_ref[hp, c] = _nt_dot(kk_ref[group_of(hp), keys, :], wq_ref[1, hp])

    def count_pass(pred):
        parts = [_fold_rows(pred(sc_ref[c]).astype(F32), jnp.add) for c in range(nch)]
        while len(parts) > 1:
            parts = [sum(parts[j:j + 2]) for j in range(0, len(parts), 2)]
        return jnp.sum(parts[0], axis=0, keepdims=True)

    if nch * KEY_CHUNK <= TOPK:
        thr = jnp.full((1, TQ), -jnp.inf, F32)
        need = jnp.zeros((1, TQ), F32)
    else:
        thr = _kth_largest(lambda t: count_pass(lambda x: x >= t), (1, TQ), peel=SEARCH_PEEL)
        need = float(TOPK) - count_pass(lambda x: x > thr)

    tri = jnp.where(lax.broadcasted_iota(I32, (KEY_CHUNK, KEY_CHUNK), 1)
                    <= lax.broadcasted_iota(I32, (KEY_CHUNK, KEY_CHUNK), 0), 1.0, 0.0).astype(BF16)

    m_part = [jnp.full((SUBLANES, TQ), -jnp.inf, F32) for _ in range(N_HEADS)]
    eq_before = jnp.zeros((1, TQ), F32)
    for c in range(nch):
        sc = sc_ref[c]
        eq = sc == thr
        rank = _dot(tri, jnp.where(eq, 1.0, 0.0).astype(BF16)) + eq_before
        sel = (sc > thr) | (eq & (rank <= need))
        if c == nch - 1:
            sel = sel & (sc > -jnp.inf)
        eq_before = rank[KEY_CHUNK - 1:KEY_CHUNK, :]
        for hp in range(n_pairs):
            for par in range(2):
                cols = slice(par * TQ, (par + 1) * TQ)
                lg = jnp.where(sel, lg_ref[hp, c, :, cols], -jnp.inf)
                lg_ref[hp, c, :, cols] = lg
                m_part[2 * hp + par] = jnp.maximum(m_part[2 * hp + par], _fold_rows(lg, jnp.maximum))
    m_rows = [jnp.max(m, axis=0, keepdims=True) for m in m_part]

    for hp in range(n_pairs):
        m2 = jnp.concatenate([m_rows[2 * hp], m_rows[2 * hp + 1]], axis=1)
        acc = jnp.zeros((VT_ROWS, 2 * TQ), F32)
        for c in range(nch):
            p2 = jnp.exp(lg_ref[hp, c] - m2).astype(BF16)
            acc = acc + _dot(vtx_ref[0, group_of(hp), c], p2)
        acc_ref[hp] = acc


def _attn_prompt_kernel(qi_ref, wit_ref, qs_ref, sga_ref, kik_ref, kk_ref, vtx_ref, ag_ref,
                        sc_ref, lg_ref, wq_ref, acc_ref):
    step = pl.program_id(1)
    n_pairs = N_HEADS // 2
    tiles_per_step = KEY_CHUNK // TQ
    nch = step + 1
    lo = _lane_iota((TQ, LANES)) < HEAD_DIM

    def one_tile(sub, carry):
        rows = pl.ds(pl.multiple_of(sub * TQ, TQ), TQ)
        for kind, ref in enumerate((qi_ref, qs_ref)):
            for hp in range(n_pairs):
                pair = ref[rows, hp * LANES:(hp + 1) * LANES].astype(F32)
                wq_ref[kind, hp, 0:TQ, :] = jnp.where(lo, pair, 0.0).astype(BF16)
                wq_ref[kind, hp, TQ:2 * TQ, :] = jnp.where(lo, 0.0, pair).astype(BF16)

        wit = wit_ref[sub]
        row0 = (step * tiles_per_step + sub) * TQ
        for n_static in range(1, sc_ref.shape[0] + 1):
            @pl.when(nch == n_static)
            def _(n_static=n_static):
                _attn_tile(n_static, row0, wit, kik_ref, kk_ref, vtx_ref, sc_ref, lg_ref, wq_ref, acc_ref)

        for hp in range(n_pairs):
            a = acc_ref[hp]
            o0 = a[0:HEAD_DIM, 0:TQ] / a[HEAD_DIM:HEAD_DIM + 1, 0:TQ]
            o1 = a[0:HEAD_DIM, TQ:2 * TQ] / a[HEAD_DIM:HEAD_DIM + 1, TQ:2 * TQ]
            pair = jnp.concatenate([o0, o1], axis=0).T
            cols = slice(hp * LANES, (hp + 1) * LANES)
            ag_ref[rows, cols] = (pair * sga_ref[rows, cols].astype(F32)).astype(BF16)
        return carry

    lax.fori_loop(0, tiles_per_step, one_tile, 0)


def _attn_prompt(qi, wit, qs, sga, kik, kk, vtx, b, s):
    n = b * s
    nkc = s // KEY_CHUNK
    tps = KEY_CHUNK // TQ
    row = lambda bi, st: (bi * nkc + st, 0)
    return pl.pallas_call(
        _attn_prompt_kernel,
        out_shape=jax.ShapeDtypeStruct((n, ATTN_W), BF16),
        grid=(b, nkc),
        in_specs=[
            pl.BlockSpec((KEY_CHUNK, ATTN_W), row),
            pl.BlockSpec((tps, IDX_HEADS, TQ), lambda bi, st: (bi * nkc + st, 0, 0)),
            pl.BlockSpec((KEY_CHUNK, ATTN_W), row),
            pl.BlockSpec((KEY_CHUNK, ATTN_W), row),
            pl.BlockSpec((s, LANES), lambda bi, st: (bi, 0)),
            pl.BlockSpec((KV_HEADS, s, LANES), lambda bi, st: (0, bi, 0)),
            pl.BlockSpec((1, KV_HEADS, nkc, VT_ROWS, KEY_CHUNK), lambda bi, st: (bi, 0, 0, 0, 0)),
        ],
        out_specs=pl.BlockSpec((KEY_CHUNK, ATTN_W), row),
        scratch_shapes=[
            pltpu.VMEM((nkc, KEY_CHUNK, TQ), F32),
            pltpu.VMEM((N_HEADS // 2, nkc, KEY_CHUNK, 2 * TQ), F32),
            pltpu.VMEM((2, N_HEADS // 2, 2 * TQ, LANES), BF16),
            pltpu.VMEM((N_HEADS // 2, VT_ROWS, 2 * TQ), F32),
        ],
        compiler_params=pltpu.CompilerParams(
            dimension_semantics=("arbitrary", "arbitrary"), vmem_limit_bytes=VMEM_LIMIT),
        name="attn_prompt",
    )(qi, wit, qs, sga, kik, kk, vtx)


def _out_proj_kernel(x_ref, a_ref, p_ref, gate_ref, w_ref, o_ref):
    y = _dot(a_ref[...], w_ref[0:ATTN_W, :]) + _dot(p_ref[...], w_ref[ATTN_W:ATTN_W + POOL_W, :])
    o_ref[0] = x_ref[0] + gate_ref[0] * y


def _out_proj_prompt(x, ag, pg, ada_p, w_out_b):
    b, s, _ = x.shape
    nt = s // TM_OUT
    row = lambda bi, ti: (bi * nt + ti, 0)
    return pl.pallas_call(
        _out_proj_kernel,
        out_shape=jax.ShapeDtypeStruct(x.shape, F32),
        grid=(b, nt),
        in_specs=[
            pl.BlockSpec((1, TM_OUT, D_MODEL), lambda bi, ti: (bi, ti, 0)),
            pl.BlockSpec((TM_OUT, ATTN_W), row),
            pl.BlockSpec((TM_OUT, POOL_W), row),
            pl.BlockSpec((1, 1, D_MODEL), lambda bi, ti: (bi, 0, 2)),
            pl.BlockSpec((D_MODEL, D_MODEL), lambda bi, ti: (0, 0)),
        ],
        out_specs=pl.BlockSpec((1, TM_OUT, D_MODEL), lambda bi, ti: (bi, ti, 0)),
        compiler_params=pltpu.CompilerParams(
            dimension_semantics=("arbitrary", "arbitrary"), vmem_limit_bytes=VMEM_LIMIT),
        name="out_proj_prompt",
    )(x, ag, pg, ada_p, w_out_b)


def _proj_sample_kernel(x_ref, ada_ref, nw_ref, w_ref, qnw_ref, knw_ref, wpool_ref, ps_ref, hist_ref,
                        qpad_ref, k_ref, v_ref, ki_ref, qi_ref, wi_ref, snew_ref, lnew_ref,
                        sga_ref, pg_ref, pool_ref):
    nb = x_ref.shape[0]
    x = x_ref[...]
    shift = ada_ref[:, 0:D_MODEL]
    scale = ada_ref[:, D_MODEL:2 * D_MODEL]
    hb = _modulated_norm(x, nw_ref[...], scale, shift).astype(BF16)

    lane = _lane_iota((nb, LANES))
    lo = lane < HEAD_DIM
    seg256 = _seg_ones(2 * LANES, HEAD_DIM)
    seg128 = _seg_ones(LANES, HEAD_DIM)
    head_sel = jnp.where(lax.broadcasted_iota(I32, (ATTN_W, LANES), 0) // HEAD_DIM
                         == lax.broadcasted_iota(I32, (ATTN_W, LANES), 1), 1.0, 0.0).astype(BF16)

    def head_sums(prod):
        hi = prod.astype(BF16)
        rest = (prod - hi.astype(F32)).astype(BF16)
        return _dot(hi, head_sel) + _dot(rest, head_sel)

    q = _head_rms(_dot(hb, w_ref[:, C_Q:C_Q + ATTN_W]), seg256, qnw_ref[...])
    qb = q.astype(BF16)
    k = _head_rms(_dot(hb, w_ref[:, C_K:C_K + LANES]), seg128, knw_ref[...])
    k_ref[...] = k
    v = _dot(hb, w_ref[:, C_V:C_V + LANES])
    v_ref[...] = v

    for hp in range(N_HEADS // 2):
        pair = q[:, hp * LANES:(hp + 1) * LANES]
        pair_sw = pltpu.roll(pair, HEAD_DIM, axis=1)
        g = (2 * hp) // (N_HEADS // KV_HEADS)
        if g == 0:
            h_even, h_odd = jnp.where(lo, pair, 0.0), jnp.where(lo, pair_sw, 0.0)
        else:
            h_even, h_odd = jnp.where(lo, 0.0, pair_sw), jnp.where(lo, 0.0, pair)
        qpad_ref[:, (2 * hp) * LANES:(2 * hp + 1) * LANES] = h_even.astype(BF16)
        qpad_ref[:, (2 * hp + 1) * LANES:(2 * hp + 2) * LANES] = h_odd.astype(BF16)

    kq = k.astype(BF16).astype(F32)
    kq_sw = pltpu.roll(kq, HEAD_DIM, axis=1)
    k0t = jnp.where(lo, kq, kq_sw)
    k1t = jnp.where(lo, kq_sw, kq)
    qf = qb.astype(F32)
    prod = jnp.concatenate([qf[:, 0:LANES] * k0t, qf[:, LANES:2 * LANES] * k0t,
                            qf[:, 2 * LANES:3 * LANES] * k1t, qf[:, 3 * LANES:4 * LANES] * k1t], axis=1)
    lnew_ref[...] = head_sums(prod)

    qi = _dot(hb, w_ref[:, C_QI:C_QI + ATTN_W])
    qib = qi.astype(BF16)
    qi_ref[...] = qib
    kw = _dot(hb, w_ref[:, C_KW:C_KW + LANES])
    kw_sw = pltpu.roll(kw, HEAD_DIM, axis=1)
    ki_ref[...] = kw[:, 0:IDX_DIM]
    wi_full = jnp.where(lane < IDX_HEADS, kw_sw, 0.0) * ((IDX_HEADS * IDX_DIM) ** -0.5)
    wi_ref[...] = wi_full[:, 0:IDX_HEADS]

    kib = kw.astype(BF16).astype(F32)
    kit = jnp.where(lo, kib, pltpu.roll(kib, HEAD_DIM, axis=1))
    qif = qib.astype(F32)
    prod_i = jnp.concatenate([qif[:, j * LANES:(j + 1) * LANES] * kit for j in range(4)], axis=1)
    s_new = jnp.maximum(head_sums(prod_i), 0.0) * wi_full
    snew_ref[...] = jnp.broadcast_to(jnp.sum(s_new, axis=1, keepdims=True), (nb, LANES))

    sga_ref[...] = _silu(_dot(hb, w_ref[:, C_GA:C_GA + ATTN_W])).astype(BF16)

    u = _dot(hb, w_ref[:, C_U:C_U + POOL_W])
    gp = _dot(hb, w_ref[:, C_GP:C_GP + POOL_W])
    for j in range(POOL_HIST - 1):
        pool_ref[j] = hist_ref[j + 1]
    pool_ref[POOL_HIST - 1] = u
    ds = []
    for g, w in enumerate(POOL_WINDOWS):
        cs = slice(g * POOL_GC, (g + 1) * POOL_GC)
        s = u[:, cs]
        for j in range(1, w):
            s = s + hist_ref[POOL_HIST - j, :, cs]
        ds.append(s / float(w) - u[:, cs])
    d = jnp.concatenate(ds, axis=1)
    pg_ref[...] = _pool_mix(d, wpool_ref, ps_ref[...], gp).astype(BF16)


def _proj_sample(x, ada_s, norm_w, w_in_b, qnw, knw, wpool_b, pscale, hist_t):
    nb = x.shape[0]
    out_shape = (
        jax.ShapeDtypeStruct((nb, N_HEADS * LANES), BF16),
        jax.ShapeDtypeStruct((nb, LANES), F32),
        jax.ShapeDtypeStruct((nb, LANES), F32),
        jax.ShapeDtypeStruct((nb, IDX_DIM), F32),
        jax.ShapeDtypeStruct((nb, ATTN_W), BF16),
        jax.ShapeDtypeStruct((nb, IDX_HEADS), F32),
        jax.ShapeDtypeStruct((nb, LANES), F32),
        jax.ShapeDtypeStruct((nb, LANES), F32),
        jax.ShapeDtypeStruct((nb, ATTN_W), BF16),
        jax.ShapeDtypeStruct((nb, POOL_W), BF16),
        jax.ShapeDtypeStruct((POOL_HIST, nb, POOL_W), F32),
    )
    return pl.pallas_call(
        _proj_sample_kernel,
        out_shape=out_shape,
        compiler_params=pltpu.CompilerParams(vmem_limit_bytes=VMEM_LIMIT),
        name="proj_sample",
    )(x, ada_s, norm_w, w_in_b, qnw, knw, wpool_b, pscale, hist_t)


def _page_copies(pt_ref, pages_hbm, buf_ref, sem_ref, step, slot, group, n_pages):
    copies = []
    for j in range(group):
        for p in range(n_pages):
            page = pt_ref[step * group + j, p]
            copies.append(pltpu.make_async_copy(
                pages_hbm.at[page], buf_ref.at[slot, j, :, pl.ds(p * PAGE, PAGE)], sem_ref.at[slot]))
    return copies


def _score_sample_kernel(pt_ref, qi_ref, wi_ref, kidx_hbm, o_ref, buf_ref, sem_ref, *, group, n_pages):
    step = pl.program_id(0)
    slot = lax.rem(step, 2)

    def copies(st, sl):
        return _page_copies(pt_ref, kidx_hbm, buf_ref, sem_ref, st, sl, group, n_pages)

    @pl.when(step == 0)
    def _():
        for cp in copies(step, slot):
            cp.start()

    @pl.when(step + 1 < pl.num_programs(0))
    def _():
        for cp in copies(step + 1, 1 - slot):
            cp.start()

    for cp in copies(step, slot):
        cp.wait()

    for j in range(group):
        ki_t = buf_ref[slot, j].astype(BF16)
        s = _dot(qi_ref[j], ki_t)
        o_ref[j:j + 1, :] = jnp.sum(jnp.maximum(s, 0.0) * wi_ref[j], axis=0, keepdims=True)


def _score_sample(page_table, qi3, wi3, kidx_t):
    nb, n_pages = page_table.shape
    n_keys = n_pages * PAGE
    g = SCORE_GROUP
    return pl.pallas_call(
        functools.partial(_score_sample_kernel, group=g, n_pages=n_pages),
        out_shape=jax.ShapeDtypeStruct((nb, n_keys), F32),
        grid_spec=pltpu.PrefetchScalarGridSpec(
            num_scalar_prefetch=1,
            grid=(nb // g,),
            in_specs=[
                pl.BlockSpec((g, IDX_HEADS, IDX_DIM), lambda s, pt: (s, 0, 0)),
                pl.BlockSpec((g, IDX_HEADS, 1), lambda s, pt: (s, 0, 0)),
                pl.BlockSpec(memory_space=pl.ANY),
            ],
            out_specs=pl.BlockSpec((g, n_keys), lambda s, pt: (s, 0)),
            scratch_shapes=[
                pltpu.VMEM((2, g, IDX_DIM, n_keys), F32),
                pltpu.SemaphoreType.DMA((2,)),
            ],
        ),
        compiler_params=pltpu.CompilerParams(
            dimension_semantics=("arbitrary",), vmem_limit_bytes=VMEM_LIMIT),
        name="score_sample",
    )(page_table, qi3, wi3, kidx_t)


def _select_sample_kernel(sc_ref, snew_ref, mask_ref, mnew_ref):
    nb, n_keys = sc_ref.shape
    nch = n_keys // KEY_CHUNK
    snew = snew_ref[...]
    ones_mat = jnp.ones((LANES, LANES), BF16)

    def count_pass(pred):
        acc = jnp.zeros((nb, LANES), F32)
        for c in range(n_keys // LANES):
            acc = acc + pred(sc_ref[:, c * LANES:(c + 1) * LANES]).astype(F32)
        return _dot(acc.astype(BF16), ones_mat) + pred(snew).astype(F32)

    thr = _kth_largest(lambda t: count_pass(lambda x: x >= t), (nb, LANES))
    need = float(TOPK) - count_pass(lambda x: x > thr)
    thr2 = jnp.concatenate([thr, thr], axis=1)
    need2 = jnp.concatenate([need, need], axis=1)
    tri = jnp.where(lax.broadcasted_iota(I32, (KEY_CHUNK, KEY_CHUNK), 0)
                    <= lax.broadcasted_iota(I32, (KEY_CHUNK, KEY_CHUNK), 1), 1.0, 0.0).astype(BF16)
    ones_cl = jnp.ones((KEY_CHUNK, LANES), BF16)

    eq_before = jnp.zeros((nb, LANES), F32)
    for c in range(nch):
        sc = sc_ref[:, c * KEY_CHUNK:(c + 1) * KEY_CHUNK]
        eq = sc == thr2
        eqb = jnp.where(eq, 1.0, 0.0).astype(BF16)
        rank = _dot(eqb, tri) + jnp.concatenate([eq_before, eq_before], axis=1)
        sel = (sc > thr2) | (eq & (rank <= need2))
        mask_ref[:, c * KEY_CHUNK:(c + 1) * KEY_CHUNK] = jnp.where(sel, 0.0, -jnp.inf)
        eq_before = eq_before + _dot(eqb, ones_cl)
    sel_new = (snew > thr) | ((snew == thr) & (eq_before + 1.0 <= need))
    mnew_ref[...] = jnp.where(sel_new, 0.0, -jnp.inf)


def _select_sample(scores, snew):
    nb, n_keys = scores.shape
    return pl.pallas_call(
        _select_sample_kernel,
        out_shape=(jax.ShapeDtypeStruct((nb, n_keys), F32), jax.ShapeDtypeStruct((nb, LANES), F32)),
        name="select_sample",
    )(scores, snew)


def _attn_sample_kernel(pt_ref, q_ref, mask_ref, lnew_ref, mnew_ref, vnew_ref, k_hbm, v_hbm,
                        o_ref, kbuf_ref, vbuf_ref, ksem_ref, vsem_ref, *, group, n_pages):
    step = pl.program_id(0)
    slot = lax.rem(step, 2)

    def copies(st, sl):
        return (_page_copies(pt_ref, k_hbm, kbuf_ref, ksem_ref, st, sl, group, n_pages)
                + _page_copies(pt_ref, v_hbm, vbuf_ref, vsem_ref, st, sl, group, n_pages))

    @pl.when(step == 0)
    def _():
        for cp in copies(step, slot):
            cp.start()

    @pl.when(step + 1 < pl.num_programs(0))
    def _():
        for cp in copies(step + 1, 1 - slot):
            cp.start()

    for cp in copies(step, slot):
        cp.wait()

    for j in range(group):
        k_t = kbuf_ref[slot, j].astype(BF16)
        lg = _dot(q_ref[j], k_t) + mask_ref[j]
        lg_n = lnew_ref[j] + mnew_ref[j]
        m = jnp.maximum(jnp.max(lg, axis=1, keepdims=True), lg_n)
        p = jnp.exp(lg - m)
        p_n = jnp.exp(lg_n - m)
        l = jnp.sum(p, axis=1, keepdims=True) + p_n
        acc = _nt_dot(p.astype(BF16), vbuf_ref[slot, j].astype(BF16)) + p_n * vnew_ref[j]
        o_ref[j] = acc / l


def _attn_sample(page_table, qpad3, mask3, lnew3, mnew3, vnew3, k_t, v_t):
    nb, n_pages = page_table.shape
    n_keys = n_pages * PAGE
    g = ATTN_GROUP
    per_s = lambda s, pt: (s, 0, 0)
    return pl.pallas_call(
        functools.partial(_attn_sample_kernel, group=g, n_pages=n_pages),
        out_shape=jax.ShapeDtypeStruct((nb, N_HEADS, LANES), F32),
        grid_spec=pltpu.PrefetchScalarGridSpec(
            num_scalar_prefetch=1,
            grid=(nb // g,),
            in_specs=[
                pl.BlockSpec((g, N_HEADS, LANES), per_s),
                pl.BlockSpec((g, 1, n_keys), per_s),
                pl.BlockSpec((g, N_HEADS, 1), per_s),
                pl.BlockSpec((g, N_HEADS, 1), per_s),
                pl.BlockSpec((g, 1, LANES), per_s),
                pl.BlockSpec(memory_space=pl.ANY),
                pl.BlockSpec(memory_space=pl.ANY),
            ],
            out_specs=pl.BlockSpec((g, N_HEADS, LANES), per_s),
            scratch_shapes=[
                pltpu.VMEM((2, g, LANES, n_keys), F32),
                pltpu.VMEM((2, g, LANES, n_keys), F32),
                pltpu.SemaphoreType.DMA((2,)),
                pltpu.SemaphoreType.DMA((2,)),
            ],
        ),
        compiler_params=pltpu.CompilerParams(
            dimension_semantics=("arbitrary",), vmem_limit_bytes=VMEM_LIMIT),
        name="attn_sample",
    )(page_table, qpad3, mask3, lnew3, mnew3, vnew3, k_t, v_t)


def _out_proj_sample_kernel(x_ref, a_ref, sga_ref, p_ref, ada_ref, w_ref, o_ref):
    lo = _lane_iota((x_ref.shape[0], LANES)) < HEAD_DIM
    pairs = []
    for hp in range(N_HEADS // 2):
        even = a_ref[:, (2 * hp) * LANES:(2 * hp + 1) * LANES]
        odd = a_ref[:, (2 * hp + 1) * LANES:(2 * hp + 2) * LANES]
        if (2 * hp) // (N_HEADS // KV_HEADS) == 0:
            pairs.append(jnp.where(lo, even, pltpu.roll(odd, HEAD_DIM, axis=1)))
        else:
            pairs.append(jnp.where(lo, pltpu.roll(even, HEAD_DIM, axis=1), odd))
    a = jnp.concatenate(pairs, axis=1)
    ag = (a * sga_ref[...].astype(F32)).astype(BF16)
    y = _dot(ag, w_ref[0:ATTN_W, :]) + _dot(p_ref[...], w_ref[ATTN_W:ATTN_W + POOL_W, :])
    o_ref[...] = x_ref[...] + ada_ref[:, 2 * D_MODEL:3 * D_MODEL] * y


def _out_proj_sample(x, a, sga, pg, ada_s, w_out_b):
    return pl.pallas_call(
        _out_proj_sample_kernel,
        out_shape=jax.ShapeDtypeStruct(x.shape, F32),
        compiler_params=pltpu.CompilerParams(vmem_limit_bytes=VMEM_LIMIT),
        name="out_proj_sample",
    )(x, a, sga, pg, ada_s, w_out_b)


def kernel(x_prompt, x_sample, cache_k, cache_v, cache_kidx, state_pool, page_table, c_prompt, c_sample,
           norm_w, w_ada, b_ada, w_in, q_norm_w, k_norm_w, w_pool, pool_scale, w_out):
    bp, s, _ = x_prompt.shape
    bs = x_sample.shape[0]
    assert w_in.shape[0] == 1 and x_sample.shape[1] == 1, "single layer, single decode token"
    n_phys = cache_k.shape[1]

    w_in_t = jnp.transpose(w_in[0])
    w_out_b = w_out[0].astype(BF16)
    zero_blk = jnp.zeros((POOL_GC, POOL_GC), w_pool.dtype)
    wpool_b = jnp.stack([jnp.block([[w_pool[0, 2 * p], zero_blk], [zero_blk, w_pool[0, 2 * p + 1]]])
                         for p in range(len(POOL_WINDOWS) // 2)]).astype(BF16)
    qnw = jnp.tile(q_norm_w[0], N_HEADS)[None, :] * (HEAD_DIM ** -0.5)
    knw = jnp.tile(k_norm_w[0], KV_HEADS)[None, :]
    nw = norm_w[0][None, :]
    pscale = pool_scale[0][None, :]

    ada_p, ada_s = _ada(c_prompt, c_sample, w_ada, b_ada)
    ada_p = ada_p.reshape(bp, 1, 3 * D_MODEL)

    (qs, k_t_p, kk, v_t_p, vtx, qi, ki_t_p, kik, wit, sga, pg, ulast, w_in_b) = _proj_prompt(
        x_prompt, ada_p, nw, w_in_t, qnw, knw, wpool_b, pscale)
    ag = _attn_prompt(qi, wit, qs, sga, kik, kk, vtx, bp, s)
    y_prompt = _out_proj_prompt(x_prompt, ag, pg, ada_p, w_out_b)

    hist_t = jnp.transpose(state_pool[0], (1, 0, 2))
    (qpad, k_s, v_s, ki_s, qi_s, wi_s, snew, lnew, sga_s, pg_s, pool_s) = _proj_sample(
        x_sample[:, 0, :], ada_s, nw, w_in_b, qnw, knw, wpool_b, pscale, hist_t)
    n_pages = page_table.shape[1]
    kidx_t = jnp.transpose(cache_kidx[0], (0, 2, 1))
    k_t = jnp.transpose(cache_k[0], (0, 2, 3, 1)).reshape(n_phys, LANES, PAGE)
    v_t = jnp.transpose(cache_v[0], (0, 2, 3, 1)).reshape(n_phys, LANES, PAGE)
    scores = _score_sample(page_table, qi_s.reshape(bs, IDX_HEADS, IDX_DIM), wi_s.reshape(bs, IDX_HEADS, 1),
                           kidx_t)
    mask, mnew = _select_sample(scores, snew)
    o_s = _attn_sample(
        page_table, qpad.reshape(bs, N_HEADS, LANES),
        mask.reshape(bs, 1, n_pages * PAGE),
        lnew[:, :N_HEADS].reshape(bs, N_HEADS, 1),
        jnp.broadcast_to(mnew[:, :1], (bs, N_HEADS)).reshape(bs, N_HEADS, 1),
        v_s.reshape(bs, 1, LANES), k_t, v_t)
    y_sample = _out_proj_sample(x_sample[:, 0, :], o_s.reshape(bs, N_HEADS * LANES), sga_s, pg_s, ada_s, w_out_b)

    to_heads = lambda a: jnp.transpose(a.reshape(bp, KV_HEADS, HEAD_DIM, s), (0, 3, 1, 2))[None]
    return (
        y_prompt,
        y_sample[:, None, :],
        to_heads(k_t_p),
        to_heads(v_t_p),
        jnp.transpose(ki_t_p, (0, 2, 1))[None],
        ulast[:, 1:, :][None],
        k_s.reshape(1, bs, 1, KV_HEADS, HEAD_DIM),
        v_s.reshape(1, bs, 1, KV_HEADS, HEAD_DIM),
        ki_s.reshape(1, bs, 1, IDX_DIM),
        jnp.transpose(pool_s, (1, 0, 2))[None],
    )
```

```python
import functools

import jax
import jax.numpy as jnp
import numpy as np
from jax import lax
from jax.experimental import pallas as pl
from jax.experimental.pallas import tpu as pltpu

F32 = jnp.float32
BF16 = jnp.bfloat16
I32 = jnp.int32

D_MODEL = 1024
ATTN_W = 512
POOL_W = 512
HEAD_DIM = 64
N_HEADS = 8
KV_HEADS = 2
IDX_HEADS = 8
IDX_DIM = 64
TOPK = 256
POOL_WINDOWS = (2, 4, 8, 16)
POOL_GC = 128
POOL_HIST = 15
EPS = 1e-6
PAGE = 128

LANES = 128
SUBLANES = 8
BF16_ROWS = 16
KEY_CHUNK = 256
TQ = 128
TM = 512
TM_OUT = 1024
VT_ROWS = HEAD_DIM + BF16_ROWS
SEARCH_PEEL = 10
SCORE_GROUP = 8
ATTN_GROUP = 8
INT_MIN = np.int32(-2 ** 31)
VMEM_LIMIT = 56 * 1024 * 1024

C_Q, C_K, C_V, C_QI, C_KW, C_GA, C_U, C_GP, N_PROJ = 0, 512, 640, 768, 1280, 1408, 1920, 2432, 2944


def _nt_dot(a, b):
    return lax.dot_general(a, b, (((1,), (1,)), ((), ())), preferred_element_type=F32)


def _dot(a, b):
    return jnp.dot(a, b, preferred_element_type=F32)


def _silu(z):
    return z / (1.0 + jnp.exp(-z))


def _lane_iota(shape):
    return lax.broadcasted_iota(I32, shape, len(shape) - 1)


def _seg_ones(n, seg):
    r = lax.broadcasted_iota(I32, (n, n), 0) // seg
    c = lax.broadcasted_iota(I32, (n, n), 1) // seg
    return jnp.where(r == c, 1.0, 0.0).astype(BF16)


def _head_rms(z, seg_mat, w):
    n = seg_mat.shape[0]
    sq = (z * z).astype(BF16)
    ss = jnp.concatenate([_dot(sq[:, j:j + n], seg_mat) for j in range(0, z.shape[1], n)], axis=1)
    return z * lax.rsqrt(ss * (1.0 / HEAD_DIM) + EPS) * w


def _float_of_rank(u):
    key = u ^ INT_MIN
    bits = key ^ ((key >> 31) & np.int32(0x7FFFFFFF))
    return lax.bitcast_convert_type(bits, F32)


def _kth_largest(count_ge, shape, peel=0):
    def bit_body(it, ans):
        cand = ans | jnp.left_shift(jnp.int32(1), 31 - it)
        return jnp.where(count_ge(_float_of_rank(cand)) >= float(TOPK), cand, ans)

    ans = jnp.zeros(shape, I32)
    for it in range(peel):
        ans = bit_body(it, ans)
    ans = lax.fori_loop(peel, 32, bit_body, ans)
    return jnp.where(ans == 0, -jnp.inf, _float_of_rank(ans))


def _fold_rows(x, op):
    parts = [x[r:r + SUBLANES] for r in range(0, x.shape[0], SUBLANES)]
    while len(parts) > 1:
        parts = [op(parts[i], parts[i + 1]) for i in range(0, len(parts), 2)]
    return parts[0]


def _ada_kernel(cp_ref, cs_ref, w_ref, b_ref, op_ref, os_ref):
    w = w_ref[0].astype(BF16)
    op_ref[...] = _dot(_silu(cp_ref[...]).astype(BF16), w) + b_ref[...]
    os_ref[...] = _dot(_silu(cs_ref[...]).astype(BF16), w) + b_ref[...]


def _ada(c_prompt, c_sample, w_ada, b_ada):
    bp, bs = c_prompt.shape[0], c_sample.shape[0]
    return pl.pallas_call(
        _ada_kernel,
        out_shape=(jax.ShapeDtypeStruct((bp, 3 * D_MODEL), F32), jax.ShapeDtypeStruct((bs, 3 * D_MODEL), F32)),
        grid=(3,),
        in_specs=[
            pl.BlockSpec((bp, D_MODEL), lambda j: (0, 0)),
            pl.BlockSpec((bs, D_MODEL), lambda j: (0, 0)),
            pl.BlockSpec((1, D_MODEL, D_MODEL), lambda j: (0, 0, j)),
            pl.BlockSpec((1, D_MODEL), lambda j: (0, j)),
        ],
        out_specs=(pl.BlockSpec((bp, D_MODEL), lambda j: (0, j)), pl.BlockSpec((bs, D_MODEL), lambda j: (0, j))),
        compiler_params=pltpu.CompilerParams(dimension_semantics=("arbitrary",)),
        name="ada_ln",
    )(c_prompt, c_sample, w_ada, b_ada)


def _modulated_norm(x, norm_w, scale, shift):
    ms = jnp.mean(x * x, axis=-1, keepdims=True)
    return (x * lax.rsqrt(ms + EPS)) * norm_w * (1.0 + scale) + shift


def _pool_mix(d, wpool_ref, pscale, gp):
    db = d.astype(BF16)
    wide = 2 * POOL_GC
    y = jnp.concatenate([_dot(db[:, p * wide:(p + 1) * wide], wpool_ref[p])
                         for p in range(len(POOL_WINDOWS) // 2)], axis=1)
    return y * pscale * _silu(gp)


def _permuted_weight_block(wt_ref, g):
    cut = C_KW + IDX_DIM + IDX_HEADS
    c0 = g * LANES
    if c0 + LANES <= cut:
        blk = wt_ref[c0:c0 + LANES, :]
    elif c0 < cut:
        blk = jnp.concatenate([wt_ref[c0:cut, :], jnp.zeros((c0 + LANES - cut, D_MODEL), F32)], axis=0)
    else:
        r0 = c0 - (C_GA - cut)
        blk = wt_ref[r0:r0 + LANES, :]
    return blk.T.astype(BF16)


def _proj_prompt_kernel(x_ref, ada_ref, nw_ref, wt_ref, qnw_ref, knw_ref, wpool_ref, ps_ref,
                        qs_ref, kt_ref, kk_ref, vt_ref, vtx_ref, qi_ref, kit_ref, kik_ref,
                        wit_ref, sga_ref, pg_ref, ulast_ref, w_ref,
                        ext_ref):
    t = pl.program_id(1)

    @pl.when((pl.program_id(0) == 0) & (t == 0))
    def _():
        for g in range(N_PROJ // LANES):
            w_ref[:, g * LANES:(g + 1) * LANES] = _permuted_weight_block(wt_ref, g)

    ada = ada_ref[0]
    shift = ada[:, 0:D_MODEL]
    scale = ada[:, D_MODEL:2 * D_MODEL]
    sub_rows = KEY_CHUNK
    lo = _lane_iota((sub_rows, LANES)) < HEAD_DIM
    seg256 = _seg_ones(2 * LANES, HEAD_DIM)
    seg128 = _seg_ones(LANES, HEAD_DIM)
    ones = jnp.ones((BF16_ROWS, KEY_CHUNK), BF16)

    @pl.when(t == 0)
    def _():
        ext_ref[0:16, :] = jnp.zeros((16, POOL_W), F32)

    for sub in range(TM // sub_rows):
        rows = slice(sub * sub_rows, (sub + 1) * sub_rows)
        hb = _modulated_norm(x_ref[0, rows, :], nw_ref[...], scale, shift).astype(BF16)

        q = _dot(hb, w_ref[:, C_Q:C_Q + ATTN_W])
        qs_ref[rows, :] = _head_rms(q, seg256, qnw_ref[...]).astype(BF16)

        kv = _dot(hb, w_ref[:, C_K:C_K + 2 * LANES])
        k = _head_rms(kv[:, 0:LANES], seg128, knw_ref[...])
        kt_ref[0, :, rows] = k.T
        k_sw = pltpu.roll(k, HEAD_DIM, axis=1)
        kk_ref[0, rows, :] = jnp.where(lo, k, k_sw).astype(BF16)
        kk_ref[1, rows, :] = jnp.where(lo, k_sw, k).astype(BF16)

        v_t = kv[:, LANES:2 * LANES].T
        vt_ref[0, :, rows] = v_t
        for g in range(KV_HEADS):
            vtx_ref[0, g, sub, 0:HEAD_DIM, :] = v_t[g * HEAD_DIM:(g + 1) * HEAD_DIM].astype(BF16)
            vtx_ref[0, g, sub, HEAD_DIM:VT_ROWS, :] = ones

        qi_ref[rows, :] = _dot(hb, w_ref[:, C_QI:C_QI + ATTN_W]).astype(BF16)
        kw = _dot(hb, w_ref[:, C_KW:C_KW + LANES])
        kw_t = kw.T
        kit_ref[0, :, rows] = kw_t[0:IDX_DIM]
        wi_t = kw_t[IDX_DIM:IDX_DIM + IDX_HEADS] * ((IDX_HEADS * IDX_DIM) ** -0.5)
        for j in range(sub_rows // TQ):
            wit_ref[sub * (sub_rows // TQ) + j] = wi_t[:, j * TQ:(j + 1) * TQ]
        kik_ref[rows, :] = jnp.where(lo, kw, pltpu.roll(kw, HEAD_DIM, axis=1)).astype(BF16)

        sga_ref[rows, :] = _silu(_dot(hb, w_ref[:, C_GA:C_GA + ATTN_W])).astype(BF16)

        u = _dot(hb, w_ref[:, C_U:C_U + POOL_W])
        gp = _dot(hb, w_ref[:, C_GP:C_GP + POOL_W])
        base = 16 + sub * sub_rows
        ext_ref[base:base + sub_rows, :] = u
        pos = t * TM + sub * sub_rows + lax.broadcasted_iota(I32, (sub_rows, POOL_GC), 0)
        ds = []
        for g, w in enumerate(POOL_WINDOWS):
            cs = slice(g * POOL_GC, (g + 1) * POOL_GC)
            s = ext_ref[base - 16:base + sub_rows, cs]
            k = 1
            while k < w:
                s = s + pltpu.roll(s, k, axis=0)
                k *= 2
            cnt = jnp.minimum(pos + 1, w).astype(F32)
            ds.append(s[16:16 + sub_rows] / cnt - u[:, cs])
        d = jnp.concatenate(ds, axis=1)
        pg_ref[rows, :] = _pool_mix(d, wpool_ref, ps_ref[...], gp).astype(BF16)

    tail = ext_ref[TM:TM + 16, :]
    ulast_ref[0] = tail
    ext_ref[0:16, :] = tail


def _proj_prompt(x, ada_p, norm_w, w_in_t, qnw, knw, wpool_b, pscale):
    b, s, _ = x.shape
    n = b * s
    nt = s // TM
    cpt = TM // KEY_CHUNK
    row = lambda bi, ti: (bi * nt + ti, 0)
    tok = lambda bi, ti: (bi, 0, ti)
    const2 = lambda bi, ti: (0, 0)
    const3 = lambda bi, ti: (0, 0, 0)
    out_shape = (
        jax.ShapeDtypeStruct((n, ATTN_W), BF16),
        jax.ShapeDtypeStruct((b, LANES, s), F32),
        jax.ShapeDtypeStruct((KV_HEADS, n, LANES), BF16),
        jax.ShapeDtypeStruct((b, LANES, s), F32),
        jax.ShapeDtypeStruct((b, KV_HEADS, s // KEY_CHUNK, VT_ROWS, KEY_CHUNK), BF16),
        jax.ShapeDtypeStruct((n, ATTN_W), BF16),
        jax.ShapeDtypeStruct((b, IDX_DIM, s), F32),
        jax.ShapeDtypeStruct((n, LANES), BF16),
        jax.ShapeDtypeStruct((n // TQ, IDX_HEADS, TQ), F32),
        jax.ShapeDtypeStruct((n, ATTN_W), BF16),
        jax.ShapeDtypeStruct((n, POOL_W), BF16),
        jax.ShapeDtypeStruct((b, 16, POOL_W), F32),
        jax.ShapeDtypeStruct((D_MODEL, N_PROJ), BF16),
    )
    out_specs = (
        pl.BlockSpec((TM, ATTN_W), row),
        pl.BlockSpec((1, LANES, TM), tok),
        pl.BlockSpec((KV_HEADS, TM, LANES), lambda bi, ti: (0, bi * nt + ti, 0)),
        pl.BlockSpec((1, LANES, TM), tok),
        pl.BlockSpec((1, KV_HEADS, cpt, VT_ROWS, KEY_CHUNK), lambda bi, ti: (bi, 0, ti, 0, 0)),
        pl.BlockSpec((TM, ATTN_W), row),
        pl.BlockSpec((1, IDX_DIM, TM), tok),
        pl.BlockSpec((TM, LANES), row),
        pl.BlockSpec((TM // TQ, IDX_HEADS, TQ), lambda bi, ti: (bi * nt + ti, 0, 0)),
        pl.BlockSpec((TM, ATTN_W), row),
        pl.BlockSpec((TM, POOL_W), row),
        pl.BlockSpec((1, 16, POOL_W), lambda bi, ti: (bi, 0, 0)),
        pl.BlockSpec((D_MODEL, N_PROJ), const2),
    )
    in_specs = [
        pl.BlockSpec((1, TM, D_MODEL), lambda bi, ti: (bi, ti, 0)),
        pl.BlockSpec((1, 1, 3 * D_MODEL), lambda bi, ti: (bi, 0, 0)),
        pl.BlockSpec((1, D_MODEL), const2),
        pl.BlockSpec(w_in_t.shape, const2),
        pl.BlockSpec((1, ATTN_W), const2),
        pl.BlockSpec((1, LANES), const2),
        pl.BlockSpec((2, 2 * POOL_GC, 2 * POOL_GC), const3),
        pl.BlockSpec((1, POOL_W), const2),
    ]
    return pl.pallas_call(
        _proj_prompt_kernel,
        out_shape=out_shape,
        grid=(b, nt),
        in_specs=in_specs,
        out_specs=out_specs,
        scratch_shapes=[pltpu.VMEM((16 + TM, POOL_W), F32)],
        compiler_params=pltpu.CompilerParams(
            dimension_semantics=("arbitrary", "arbitrary"), vmem_limit_bytes=VMEM_LIMIT),
        name="proj_prompt",
    )(x, ada_p, norm_w, w_in_t, qnw, knw, wpool_b, pscale)


def _attn_tile(nch, row0, wit, kik_ref, kk_ref, vtx_ref, sc_ref, lg_ref, wq_ref, acc_ref):
    shape2 = (KEY_CHUNK, TQ)
    n_pairs = N_HEADS // 2
    group_of = lambda hp: (2 * hp) // (N_HEADS // KV_HEADS)

    for c in range(nch):
        keys = slice(c * KEY_CHUNK, (c + 1) * KEY_CHUNK)
        kk = kik_ref[keys, :]
        acc = jnp.zeros(shape2, F32)
        for hp in range(n_pairs):
            s2 = _nt_dot(kk, wq_ref[0, hp])
            acc = acc + jnp.maximum(s2[:, 0:TQ], 0.0) * wit[2 * hp:2 * hp + 1, :]
            acc = acc + jnp.maximum(s2[:, TQ:2 * TQ], 0.0) * wit[2 * hp + 1:2 * hp + 2, :]
        if c == nch - 1:
            kpos = c * KEY_CHUNK + lax.broadcasted_iota(I32, shape2, 0)
            qpos = row0 + lax.broadcasted_iota(I32, shape2, 1)
            acc = jnp.where(kpos <= qpos, acc, -jnp.inf)
        sc_ref[c] = acc

    for c in range(nch):
        keys = slice(c * KEY_CHUNK, (c + 1) * KEY_CHUNK)
        for hp in range(n_pairs):
            lg_ref[hp, c] = _nt_dot(kk_ref[group_of(hp), keys, :], wq_ref[1, hp])

    def count_pass(pred):
        parts = [_fold_rows(pred(sc_ref[c]).astype(F32), jnp.add) for c in range(nch)]
        while len(parts) > 1:
            parts = [sum(parts[j:j + 2]) for j in range(0, len(parts), 2)]
        return jnp.sum(parts[0], axis=0, keepdims=True)

    if nch * KEY_CHUNK <= TOPK:
        thr = jnp.full((1, TQ), -jnp.inf, F32)
        need = jnp.zeros((1, TQ), F32)
    else:
        thr = _kth_largest(lambda t: count_pass(lambda x: x >= t), (1, TQ), peel=SEARCH_PEEL)
        need = float(TOPK) - count_pass(lambda x: x > thr)

    tri = jnp.where(lax.broadcasted_iota(I32, (KEY_CHUNK, KEY_CHUNK), 1)
                    <= lax.broadcasted_iota(I32, (KEY_CHUNK, KEY_CHUNK), 0), 1.0, 0.0).astype(BF16)

    m_part = [jnp.full((SUBLANES, TQ), -jnp.inf, F32) for _ in range(N_HEADS)]
    eq_before = jnp.zeros((1, TQ), F32)
    for c in range(nch):
        sc = sc_ref[c]
        eq = sc == thr
        rank = _dot(tri, jnp.where(eq, 1.0, 0.0).astype(BF16)) + eq_before
        sel = (sc > thr) | (eq & (rank <= need))
        if c == nch - 1:
            sel = sel & (sc > -jnp.inf)
        eq_before = rank[KEY_CHUNK - 1:KEY_CHUNK, :]
        for hp in range(n_pairs):
            for par in range(2):
                cols = slice(par * TQ, (par + 1) * TQ)
                lg = jnp.where(sel, lg_ref[hp, c, :, cols], -jnp.inf)
                lg_ref[hp, c, :, cols] = lg
                m_part[2 * hp + par] = jnp.maximum(m_part[2 * hp + par], _fold_rows(lg, jnp.maximum))
    m_rows = [jnp.max(m, axis=0, keepdims=True) for m in m_part]

    for hp in range(n_pairs):
        m2 = jnp.concatenate([m_rows[2 * hp], m_rows[2 * hp + 1]], axis=1)
        acc = jnp.zeros((VT_ROWS, 2 * TQ), F32)
        for c in range(nch):
            p2 = jnp.exp(lg_ref[hp, c] - m2).astype(BF16)
            acc = acc + _dot(vtx_ref[0, group_of(hp), c], p2)
        acc_ref[hp] = acc


def _attn_prompt_kernel(qi_ref, wit_ref, qs_ref, sga_ref, kik_ref, kk_ref, vtx_ref, ag_ref,
                        sc_ref, lg_ref, wq_ref, acc_ref):
    step = pl.program_id(1)
    n_pairs = N_HEADS // 2
    tiles_per_step = KEY_CHUNK // TQ
    nch = step + 1
    lo = _lane_iota((TQ, LANES)) < HEAD_DIM

    def one_tile(sub, carry):
        rows = pl.ds(pl.multiple_of(sub * TQ, TQ), TQ)
        for kind, ref in enumerate((qi_ref, qs_ref)):
            for hp in range(n_pairs):
                pair = ref[rows, hp * LANES:(hp + 1) * LANES].astype(F32)
                wq_ref[kind, hp, 0:TQ, :] = jnp.where(lo, pair, 0.0).astype(BF16)
                wq_ref[kind, hp, TQ:2 * TQ, :] = jnp.where(lo, 0.0, pair).astype(BF16)

        wit = wit_ref[sub]
        row0 = (step * tiles_per_step + sub) * TQ
        for n_static in range(1, sc_ref.shape[0] + 1):
            @pl.when(nch == n_static)
            def _(n_static=n_static):
                _attn_tile(n_static, row0, wit, kik_ref, kk_ref, vtx_ref, sc_ref, lg_ref, wq_ref, acc_ref)

        for hp in range(n_pairs):
            a = acc_ref[hp]
            o0 = a[0:HEAD_DIM, 0:TQ] / a[HEAD_DIM:HEAD_DIM + 1, 0:TQ]
            o1 = a[0:HEAD_DIM, TQ:2 * TQ] / a[HEAD_DIM:HEAD_DIM + 1, TQ:2 * TQ]
            pair = jnp.concatenate([o0, o1], axis=0).T
            cols = slice(hp * LANES, (hp + 1) * LANES)
            ag_ref[rows, cols] = (pair * sga_ref[rows, cols].astype(F32)).astype(BF16)
        return carry

    lax.fori_loop(0, tiles_per_step, one_tile, 0)


def _attn_prompt(qi, wit, qs, sga, kik, kk, vtx, b, s):
    n = b * s
    nkc = s // KEY_CHUNK
    tps = KEY_CHUNK // TQ
    row = lambda bi, st: (bi * nkc + st, 0)
    return pl.pallas_call(
        _attn_prompt_kernel,
        out_shape=jax.ShapeDtypeStruct((n, ATTN_W), BF16),
        grid=(b, nkc),
        in_specs=[
            pl.BlockSpec((KEY_CHUNK, ATTN_W), row),
            pl.BlockSpec((tps, IDX_HEADS, TQ), lambda bi, st: (bi * nkc + st, 0, 0)),
            pl.BlockSpec((KEY_CHUNK, ATTN_W), row),
            pl.BlockSpec((KEY_CHUNK, ATTN_W), row),
            pl.BlockSpec((s, LANES), lambda bi, st: (bi, 0)),
            pl.BlockSpec((KV_HEADS, s, LANES), lambda bi, st: (0, bi, 0)),
            pl.BlockSpec((1, KV_HEADS, nkc, VT_ROWS, KEY_CHUNK), lambda bi, st: (bi, 0, 0, 0, 0)),
        ],
        out_specs=pl.BlockSpec((KEY_CHUNK, ATTN_W), row),
        scratch_shapes=[
            pltpu.VMEM((nkc, KEY_CHUNK, TQ), F32),
            pltpu.VMEM((N_HEADS // 2, nkc, KEY_CHUNK, 2 * TQ), F32),
            pltpu.VMEM((2, N_HEADS // 2, 2 * TQ, LANES), BF16),
            pltpu.VMEM((N_HEADS // 2, VT_ROWS, 2 * TQ), F32),
        ],
        compiler_params=pltpu.CompilerParams(
            dimension_semantics=("arbitrary", "arbitrary"), vmem_limit_bytes=VMEM_LIMIT),
        name="attn_prompt",
    )(qi, wit, qs, sga, kik, kk, vtx)


def _out_proj_kernel(x_ref, a_ref, p_ref, gate_ref, w_ref, o_ref):
    y = _dot(a_ref[...], w_ref[0:ATTN_W, :]) + _dot(p_ref[...], w_ref[ATTN_W:ATTN_W + POOL_W, :])
    o_ref[0] = x_ref[0] + gate_ref[0] * y


def _out_proj_prompt(x, ag, pg, ada_p, w_out_b):
    b, s, _ = x.shape
    nt = s // TM_OUT
    row = lambda bi, ti: (bi * nt + ti, 0)
    return pl.pallas_call(
        _out_proj_kernel,
        out_shape=jax.ShapeDtypeStruct(x.shape, F32),
        grid=(b, nt),
        in_specs=[
            pl.BlockSpec((1, TM_OUT, D_MODEL), lambda bi, ti: (bi, ti, 0)),
            pl.BlockSpec((TM_OUT, ATTN_W), row),
            pl.BlockSpec((TM_OUT, POOL_W), row),
            pl.BlockSpec((1, 1, D_MODEL), lambda bi, ti: (bi, 0, 2)),
            pl.BlockSpec((D_MODEL, D_MODEL), lambda bi, ti: (0, 0)),
        ],
        out_specs=pl.BlockSpec((1, TM_OUT, D_MODEL), lambda bi, ti: (bi, ti, 0)),
        compiler_params=pltpu.CompilerParams(
            dimension_semantics=("arbitrary", "arbitrary"), vmem_limit_bytes=VMEM_LIMIT),
        name="out_proj_prompt",
    )(x, ag, pg, ada_p, w_out_b)


def _proj_sample_kernel(x_ref, ada_ref, nw_ref, w_ref, qnw_ref, knw_ref, wpool_ref, ps_ref, hist_ref,
                        qpad_ref, k_ref, v_ref, ki_ref, qi_ref, wi_ref, snew_ref, lnew_ref,
                        sga_ref, pg_ref, pool_ref):
    nb = x_ref.shape[0]
    x = x_ref[...]
    shift = ada_ref[:, 0:D_MODEL]
    scale = ada_ref[:, D_MODEL:2 * D_MODEL]
    hb = _modulated_norm(x, nw_ref[...], scale, shift).astype(BF16)

    lane = _lane_iota((nb, LANES))
    lo = lane < HEAD_DIM
    seg256 = _seg_ones(2 * LANES, HEAD_DIM)
    seg128 = _seg_ones(LANES, HEAD_DIM)
    head_sel = jnp.where(lax.broadcasted_iota(I32, (ATTN_W, LANES), 0) // HEAD_DIM
                         == lax.broadcasted_iota(I32, (ATTN_W, LANES), 1), 1.0, 0.0).astype(BF16)

    def head_sums(prod):
        hi = prod.astype(BF16)
        rest = (prod - hi.astype(F32)).astype(BF16)
        return _dot(hi, head_sel) + _dot(rest, head_sel)

    q = _head_rms(_dot(hb, w_ref[:, C_Q:C_Q + ATTN_W]), seg256, qnw_ref[...])
    qb = q.astype(BF16)
    k = _head_rms(_dot(hb, w_ref[:, C_K:C_K + LANES]), seg128, knw_ref[...])
    k_ref[...] = k
    v = _dot(hb, w_ref[:, C_V:C_V + LANES])
    v_ref[...] = v

    for hp in range(N_HEADS // 2):
        pair = q[:, hp * LANES:(hp + 1) * LANES]
        pair_sw = pltpu.roll(pair, HEAD_DIM, axis=1)
        g = (2 * hp) // (N_HEADS // KV_HEADS)
        if g == 0:
            h_even, h_odd = jnp.where(lo, pair, 0.0), jnp.where(lo, pair_sw, 0.0)
        else:
            h_even, h_odd = jnp.where(lo, 0.0, pair_sw), jnp.where(lo, 0.0, pair)
        qpad_ref[:, (2 * hp) * LANES:(2 * hp + 1) * LANES] = h_even.astype(BF16)
        qpad_ref[:, (2 * hp + 1) * LANES:(2 * hp + 2) * LANES] = h_odd.astype(BF16)

    kq = k.astype(BF16).astype(F32)
    kq_sw = pltpu.roll(kq, HEAD_DIM, axis=1)
    k0t = jnp.where(lo, kq, kq_sw)
    k1t = jnp.where(lo, kq_sw, kq)
    qf = qb.astype(F32)
    prod = jnp.concatenate([qf[:, 0:LANES] * k0t, qf[:, LANES:2 * LANES] * k0t,
                            qf[:, 2 * LANES:3 * LANES] * k1t, qf[:, 3 * LANES:4 * LANES] * k1t], axis=1)
    lnew_ref[...] = head_sums(prod)

    qi = _dot(hb, w_ref[:, C_QI:C_QI + ATTN_W])
    qib = qi.astype(BF16)
    qi_ref[...] = qib
    kw = _dot(hb, w_ref[:, C_KW:C_KW + LANES])
    kw_sw = pltpu.roll(kw, HEAD_DIM, axis=1)
    ki_ref[...] = kw[:, 0:IDX_DIM]
    wi_full = jnp.where(lane < IDX_HEADS, kw_sw, 0.0) * ((IDX_HEADS * IDX_DIM) ** -0.5)
    wi_ref[...] = wi_full[:, 0:IDX_HEADS]

    kib = kw.astype(BF16).astype(F32)
    kit = jnp.where(lo, kib, pltpu.roll(kib, HEAD_DIM, axis=1))
    qif = qib.astype(F32)
    prod_i = jnp.concatenate([qif[:, j * LANES:(j + 1) * LANES] * kit for j in range(4)], axis=1)
    s_new = jnp.maximum(head_sums(prod_i), 0.0) * wi_full
    snew_ref[...] = jnp.broadcast_to(jnp.sum(s_new, axis=1, keepdims=True), (nb, LANES))

    sga_ref[...] = _silu(_dot(hb, w_ref[:, C_GA:C_GA + ATTN_W])).astype(BF16)

    u = _dot(hb, w_ref[:, C_U:C_U + POOL_W])
    gp = _dot(hb, w_ref[:, C_GP:C_GP + POOL_W])
    for j in range(POOL_HIST - 1):
        pool_ref[j] = hist_ref[j + 1]
    pool_ref[POOL_HIST - 1] = u
    ds = []
    for g, w in enumerate(POOL_WINDOWS):
        cs = slice(g * POOL_GC, (g + 1) * POOL_GC)
        s = u[:, cs]
        for j in range(1, w):
            s = s + hist_ref[POOL_HIST - j, :, cs]
        ds.append(s / float(w) - u[:, cs])
    d = jnp.concatenate(ds, axis=1)
    pg_ref[...] = _pool_mix(d, wpool_ref, ps_ref[...], gp).astype(BF16)


def _proj_sample(x, ada_s, norm_w, w_in_b, qnw, knw, wpool_b, pscale, hist_t):
    nb = x.shape[0]
    out_shape = (
        jax.ShapeDtypeStruct((nb, N_HEADS * LANES), BF16),
        jax.ShapeDtypeStruct((nb, LANES), F32),
        jax.ShapeDtypeStruct((nb, LANES), F32),
        jax.ShapeDtypeStruct((nb, IDX_DIM), F32),
        jax.ShapeDtypeStruct((nb, ATTN_W), BF16),
        jax.ShapeDtypeStruct((nb, IDX_HEADS), F32),
        jax.ShapeDtypeStruct((nb, LANES), F32),
        jax.ShapeDtypeStruct((nb, LANES), F32),
        jax.ShapeDtypeStruct((nb, ATTN_W), BF16),
        jax.ShapeDtypeStruct((nb, POOL_W), BF16),
        jax.ShapeDtypeStruct((POOL_HIST, nb, POOL_W), F32),
    )
    return pl.pallas_call(
        _proj_sample_kernel,
        out_shape=out_shape,
        compiler_params=pltpu.CompilerParams(vmem_limit_bytes=VMEM_LIMIT),
        name="proj_sample",
    )(x, ada_s, norm_w, w_in_b, qnw, knw, wpool_b, pscale, hist_t)


def _page_copies(pt_ref, pages_hbm, buf_ref, sem_ref, step, slot, group, n_pages):
    copies = []
    for j in range(group):
        for p in range(n_pages):
            page = pt_ref[step * group + j, p]
            copies.append(pltpu.make_async_copy(
                pages_hbm.at[page], buf_ref.at[slot, j, :, pl.ds(p * PAGE, PAGE)], sem_ref.at[slot]))
    return copies


def _score_sample_kernel(pt_ref, qi_ref, wi_ref, kidx_hbm, o_ref, buf_ref, sem_ref, *, group, n_pages):
    step = pl.program_id(0)
    slot = lax.rem(step, 2)

    def copies(st, sl):
        return _page_copies(pt_ref, kidx_hbm, buf_ref, sem_ref, st, sl, group, n_pages)

    @pl.when(step == 0)
    def _():
        for cp in copies(step, slot):
            cp.start()

    @pl.when(step + 1 < pl.num_programs(0))
    def _():
        for cp in copies(step + 1, 1 - slot):
            cp.start()

    for cp in copies(step, slot):
        cp.wait()

    for j in range(group):
        ki_t = buf_ref[slot, j].astype(BF16)
        s = _dot(qi_ref[j], ki_t)
        o_ref[j:j + 1, :] = jnp.sum(jnp.maximum(s, 0.0) * wi_ref[j], axis=0, keepdims=True)


def _score_sample(page_table, qi3, wi3, kidx_t):
    nb, n_pages = page_table.shape
    n_keys = n_pages * PAGE
    g = SCORE_GROUP
    return pl.pallas_call(
        functools.partial(_score_sample_kernel, group=g, n_pages=n_pages),
        out_shape=jax.ShapeDtypeStruct((nb, n_keys), F32),
        grid_spec=pltpu.PrefetchScalarGridSpec(
            num_scalar_prefetch=1,
            grid=(nb // g,),
            in_specs=[
                pl.BlockSpec((g, IDX_HEADS, IDX_DIM), lambda s, pt: (s, 0, 0)),
                pl.BlockSpec((g, IDX_HEADS, 1), lambda s, pt: (s, 0, 0)),
                pl.BlockSpec(memory_space=pl.ANY),
            ],
            out_specs=pl.BlockSpec((g, n_keys), lambda s, pt: (s, 0)),
            scratch_shapes=[
                pltpu.VMEM((2, g, IDX_DIM, n_keys), F32),
                pltpu.SemaphoreType.DMA((2,)),
            ],
        ),
        compiler_params=pltpu.CompilerParams(
            dimension_semantics=("arbitrary",), vmem_limit_bytes=VMEM_LIMIT),
        name="score_sample",
    )(page_table, qi3, wi3, kidx_t)


def _select_sample_kernel(sc_ref, snew_ref, mask_ref, mnew_ref):
    nb, n_keys = sc_ref.shape
    nch = n_keys // KEY_CHUNK
    snew = snew_ref[...]
    ones_mat = jnp.ones((LANES, LANES), BF16)

    def count_pass(pred):
        acc = jnp.zeros((nb, LANES), F32)
        for c in range(n_keys // LANES):
            acc = acc + pred(sc_ref[:, c * LANES:(c + 1) * LANES]).astype(F32)
        return _dot(acc.astype(BF16), ones_mat) + pred(snew).astype(F32)

    thr = _kth_largest(lambda t: count_pass(lambda x: x >= t), (nb, LANES))
    need = float(TOPK) - count_pass(lambda x: x > thr)
    thr2 = jnp.concatenate([thr, thr], axis=1)
    need2 = jnp.concatenate([need, need], axis=1)
    tri = jnp.where(lax.broadcasted_iota(I32, (KEY_CHUNK, KEY_CHUNK), 0)
                    <= lax.broadcasted_iota(I32, (KEY_CHUNK, KEY_CHUNK), 1), 1.0, 0.0).astype(BF16)
    ones_cl = jnp.ones((KEY_CHUNK, LANES), BF16)

    eq_before = jnp.zeros((nb, LANES), F32)
    for c in range(nch):
        sc = sc_ref[:, c * KEY_CHUNK:(c + 1) * KEY_CHUNK]
        eq = sc == thr2
        eqb = jnp.where(eq, 1.0, 0.0).astype(BF16)
        rank = _dot(eqb, tri) + jnp.concatenate([eq_before, eq_before], axis=1)
        sel = (sc > thr2) | (eq & (rank <= need2))
        mask_ref[:, c * KEY_CHUNK:(c + 1) * KEY_CHUNK] = jnp.where(sel, 0.0, -jnp.inf)
        eq_before = eq_before + _dot(eqb, ones_cl)
    sel_new = (snew > thr) | ((snew == thr) & (eq_before + 1.0 <= need))
    mnew_ref[...] = jnp.where(sel_new, 0.0, -jnp.inf)


def _select_sample(scores, snew):
    nb, n_keys = scores.shape
    return pl.pallas_call(
        _select_sample_kernel,
        out_shape=(jax.ShapeDtypeStruct((nb, n_keys), F32), jax.ShapeDtypeStruct((nb, LANES), F32)),
        name="select_sample",
    )(scores, snew)


def _attn_sample_kernel(pt_ref, q_ref, mask_ref, lnew_ref, mnew_ref, vnew_ref, k_hbm, v_hbm,
                        o_ref, kbuf_ref, vbuf_ref, ksem_ref, vsem_ref, *, group, n_pages):
    step = pl.program_id(0)
    slot = lax.rem(step, 2)

    def copies(st, sl):
        return (_page_copies(pt_ref, k_hbm, kbuf_ref, ksem_ref, st, sl, group, n_pages)
                + _page_copies(pt_ref, v_hbm, vbuf_ref, vsem_ref, st, sl, group, n_pages))

    @pl.when(step == 0)
    def _():
        for cp in copies(step, slot):
            cp.start()

    @pl.when(step + 1 < pl.num_programs(0))
    def _():
        for cp in copies(step + 1, 1 - slot):
            cp.start()

    for cp in copies(step, slot):
        cp.wait()

    for j in range(group):
        k_t = kbuf_ref[slot, j].astype(BF16)
        lg = _dot(q_ref[j], k_t) + mask_ref[j:j + 1, :]
        lg_n = lnew_ref[j] + mnew_ref[j]
        m = jnp.maximum(jnp.max(lg, axis=1, keepdims=True), lg_n)
        p = jnp.exp(lg - m)
        p_n = jnp.exp(lg_n - m)
        l = jnp.sum(p, axis=1, keepdims=True) + p_n
        acc = _nt_dot(p.astype(BF16), vbuf_ref[slot, j].astype(BF16)) + p_n * vnew_ref[j]
        o_ref[j] = acc / l


def _attn_sample(page_table, qpad3, mask3, lnew3, mnew3, vnew3, k_t, v_t):
    nb, n_pages = page_table.shape
    n_keys = n_pages * PAGE
    g = ATTN_GROUP
    per_s = lambda s, pt: (s, 0, 0)
    return pl.pallas_call(
        functools.partial(_attn_sample_kernel, group=g, n_pages=n_pages),
        out_shape=jax.ShapeDtypeStruct((nb, N_HEADS, LANES), F32),
        grid_spec=pltpu.PrefetchScalarGridSpec(
            num_scalar_prefetch=1,
            grid=(nb // g,),
            in_specs=[
                pl.BlockSpec((g, N_HEADS, LANES), per_s),
                pl.BlockSpec((g, n_keys), lambda s, pt: (s, 0)),
                pl.BlockSpec((g, N_HEADS, 1), per_s),
                pl.BlockSpec((g, N_HEADS, 1), per_s),
                pl.BlockSpec((g, 1, LANES), per_s),
                pl.BlockSpec(memory_space=pl.ANY),
                pl.BlockSpec(memory_space=pl.ANY),
            ],
            out_specs=pl.BlockSpec((g, N_HEADS, LANES), per_s),
            scratch_shapes=[
                pltpu.VMEM((2, g, LANES, n_keys), F32),
                pltpu.VMEM((2, g, LANES, n_keys), F32),
                pltpu.SemaphoreType.DMA((2,)),
                pltpu.SemaphoreType.DMA((2,)),
            ],
        ),
        compiler_params=pltpu.CompilerParams(
            dimension_semantics=("arbitrary",), vmem_limit_bytes=VMEM_LIMIT),
        name="attn_sample",
    )(page_table, qpad3, mask3, lnew3, mnew3, vnew3, k_t, v_t)


def _out_proj_sample_kernel(x_ref, a_ref, sga_ref, p_ref, ada_ref, w_ref, o_ref):
    lo = _lane_iota((x_ref.shape[0], LANES)) < HEAD_DIM
    pairs = []
    for hp in range(N_HEADS // 2):
        even = a_ref[:, (2 * hp) * LANES:(2 * hp + 1) * LANES]
        odd = a_ref[:, (2 * hp + 1) * LANES:(2 * hp + 2) * LANES]
        if (2 * hp) // (N_HEADS // KV_HEADS) == 0:
            pairs.append(jnp.where(lo, even, pltpu.roll(odd, HEAD_DIM, axis=1)))
        else:
            pairs.append(jnp.where(lo, pltpu.roll(even, HEAD_DIM, axis=1), odd))
    a = jnp.concatenate(pairs, axis=1)
    ag = (a * sga_ref[...].astype(F32)).astype(BF16)
    y = _dot(ag, w_ref[0:ATTN_W, :]) + _dot(p_ref[...], w_ref[ATTN_W:ATTN_W + POOL_W, :])
    o_ref[...] = x_ref[...] + ada_ref[:, 2 * D_MODEL:3 * D_MODEL] * y


def _out_proj_sample(x, a, sga, pg, ada_s, w_out_b):
    return pl.pallas_call(
        _out_proj_sample_kernel,
        out_shape=jax.ShapeDtypeStruct(x.shape, F32),
        compiler_params=pltpu.CompilerParams(vmem_limit_bytes=VMEM_LIMIT),
        name="out_proj_sample",
    )(x, a, sga, pg, ada_s, w_out_b)


def kernel(x_prompt, x_sample, cache_k, cache_v, cache_kidx, state_pool, page_table, c_prompt, c_sample,
           norm_w, w_ada, b_ada, w_in, q_norm_w, k_norm_w, w_pool, pool_scale, w_out):
    bp, s, _ = x_prompt.shape
    bs = x_sample.shape[0]
    assert w_in.shape[0] == 1 and x_sample.shape[1] == 1, "single layer, single decode token"
    n_phys = cache_k.shape[1]

    w_in_t = jnp.transpose(w_in[0])
    w_out_b = w_out[0].astype(BF16)
    zero_blk = jnp.zeros((POOL_GC, POOL_GC), w_pool.dtype)
    wpool_b = jnp.stack([jnp.block([[w_pool[0, 2 * p], zero_blk], [zero_blk, w_pool[0, 2 * p + 1]]])
                         for p in range(len(POOL_WINDOWS) // 2)]).astype(BF16)
    qnw = jnp.tile(q_norm_w[0], N_HEADS)[None, :] * (HEAD_DIM ** -0.5)
    knw = jnp.tile(k_norm_w[0], KV_HEADS)[None, :]
    nw = norm_w[0][None, :]
    pscale = pool_scale[0][None, :]

    ada_p, ada_s = _ada(c_prompt, c_sample, w_ada, b_ada)
    ada_p = ada_p.reshape(bp, 1, 3 * D_MODEL)

    (qs, k_t_p, kk, v_t_p, vtx, qi, ki_t_p, kik, wit, sga, pg, ulast, w_in_b) = _proj_prompt(
        x_prompt, ada_p, nw, w_in_t, qnw, knw, wpool_b, pscale)
    ag = _attn_prompt(qi, wit, qs, sga, kik, kk, vtx, bp, s)
    y_prompt = _out_proj_prompt(x_prompt, ag, pg, ada_p, w_out_b)

    hist_t = jnp.transpose(state_pool[0], (1, 0, 2))
    (qpad, k_s, v_s, ki_s, qi_s, wi_s, snew, lnew, sga_s, pg_s, pool_s) = _proj_sample(
        x_sample[:, 0, :], ada_s, nw, w_in_b, qnw, knw, wpool_b, pscale, hist_t)
    kidx_t = jnp.transpose(cache_kidx[0], (0, 2, 1))
    k_t = jnp.transpose(cache_k[0], (0, 2, 3, 1)).reshape(n_phys, LANES, PAGE)
    v_t = jnp.transpose(cache_v[0], (0, 2, 3, 1)).reshape(n_phys, LANES, PAGE)
    scores = _score_sample(page_table, qi_s.reshape(bs, IDX_HEADS, IDX_DIM), wi_s.reshape(bs, IDX_HEADS, 1),
                           kidx_t)
    mask, mnew = _select_sample(scores, snew)
    o_s = _attn_sample(
        page_table, qpad.reshape(bs, N_HEADS, LANES), mask,
        lnew[:, :N_HEADS].reshape(bs, N_HEADS, 1),
        jnp.broadcast_to(mnew[:, :1], (bs, N_HEADS)).reshape(bs, N_HEADS, 1),
        v_s.reshape(bs, 1, LANES), k_t, v_t)
    y_sample = _out_proj_sample(x_sample[:, 0, :], o_s.reshape(bs, N_HEADS * LANES), sga_s, pg_s, ada_s, w_out_b)

    to_heads = lambda a: jnp.transpose(a.reshape(bp, KV_HEADS, HEAD_DIM, s), (0, 3, 1, 2))[None]
    return (
        y_prompt,
        y_sample[:, None, :],
        to_heads(k_t_p),
        to_heads(v_t_p),
        jnp.transpose(ki_t_p, (0, 2, 1))[None],
        ulast[:, 1:, :][None],
        k_s.reshape(1, bs, 1, KV_HEADS, HEAD_DIM),
        v_s.reshape(1, bs, 1, KV_HEADS, HEAD_DIM),
        ki_s.reshape(1, bs, 1, IDX_DIM),
        jnp.transpose(pool_s, (1, 0, 2))[None],
    )
```

```python
import functools

import jax
import jax.numpy as jnp
import numpy as np
from jax import lax
from jax.experimental import pallas as pl
from jax.experimental.pallas import tpu as pltpu

F32 = jnp.float32
BF16 = jnp.bfloat16
I32 = jnp.int32

D_MODEL = 1024
ATTN_W = 512
POOL_W = 512
HEAD_DIM = 64
N_HEADS = 8
KV_HEADS = 2
IDX_HEADS = 8
IDX_DIM = 64
TOPK = 256
POOL_WINDOWS = (2, 4, 8, 16)
POOL_GC = 128
POOL_HIST = 15
EPS = 1e-6
PAGE = 128

LANES = 128
SUBLANES = 8
BF16_ROWS = 16
KEY_CHUNK = 256
TQ = 128
TM = 512
TM_OUT = 1024
VT_ROWS = HEAD_DIM + BF16_ROWS
SEARCH_PEEL = 10
SCORE_GROUP = 8
ATTN_GROUP = 8
INT_MIN = np.int32(-2 ** 31)
VMEM_LIMIT = 56 * 1024 * 1024

C_Q, C_K, C_V, C_QI, C_KW, C_GA, C_U, C_GP, N_PROJ = 0, 512, 640, 768, 1280, 1408, 1920, 2432, 2944


def _nt_dot(a, b):
    return lax.dot_general(a, b, (((1,), (1,)), ((), ())), preferred_element_type=F32)


def _dot(a, b):
    return jnp.dot(a, b, preferred_element_type=F32)


def _silu(z):
    return z / (1.0 + jnp.exp(-z))


def _lane_iota(shape):
    return lax.broadcasted_iota(I32, shape, len(shape) - 1)


def _seg_ones(n, seg):
    r = lax.broadcasted_iota(I32, (n, n), 0) // seg
    c = lax.broadcasted_iota(I32, (n, n), 1) // seg
    return jnp.where(r == c, 1.0, 0.0).astype(BF16)


def _head_rms(z, seg_mat, w):
    n = seg_mat.shape[0]
    sq = (z * z).astype(BF16)
    ss = jnp.concatenate([_dot(sq[:, j:j + n], seg_mat) for j in range(0, z.shape[1], n)], axis=1)
    return z * lax.rsqrt(ss * (1.0 / HEAD_DIM) + EPS) * w


def _float_of_rank(u):
    key = u ^ INT_MIN
    bits = key ^ ((key >> 31) & np.int32(0x7FFFFFFF))
    return lax.bitcast_convert_type(bits, F32)


def _kth_largest(count_ge, shape, peel=0):
    def bit_body(it, carry):
        ans, t = carry
        bit = jnp.left_shift(jnp.int32(1), 31 - it)
        nxt = lax.shift_right_logical(bit, jnp.int32(1))
        cand = ans | bit
        t_if_kept, t_if_dropped = _float_of_rank(cand | nxt), _float_of_rank(ans | nxt)
        keep = count_ge(t) >= float(TOPK)
        return jnp.where(keep, cand, ans), jnp.where(keep, t_if_kept, t_if_dropped)

    carry = (jnp.zeros(shape, I32), jnp.zeros(shape, F32))
    for it in range(peel):
        carry = bit_body(jnp.int32(it), carry)
    ans, _ = lax.fori_loop(peel, 32, bit_body, carry)
    return jnp.where(ans == 0, -jnp.inf, _float_of_rank(ans))


def _fold_rows(x, op):
    parts = [x[r:r + SUBLANES] for r in range(0, x.shape[0], SUBLANES)]
    while len(parts) > 1:
        parts = [op(parts[i], parts[i + 1]) for i in range(0, len(parts), 2)]
    return parts[0]


def _ada_kernel(cp_ref, cs_ref, w_ref, b_ref, op_ref, os_ref):
    w = w_ref[0].astype(BF16)
    op_ref[...] = _dot(_silu(cp_ref[...]).astype(BF16), w) + b_ref[...]
    os_ref[...] = _dot(_silu(cs_ref[...]).astype(BF16), w) + b_ref[...]


def _ada(c_prompt, c_sample, w_ada, b_ada):
    bp, bs = c_prompt.shape[0], c_sample.shape[0]
    return pl.pallas_call(
        _ada_kernel,
        out_shape=(jax.ShapeDtypeStruct((bp, 3 * D_MODEL), F32), jax.ShapeDtypeStruct((bs, 3 * D_MODEL), F32)),
        grid=(3,),
        in_specs=[
            pl.BlockSpec((bp, D_MODEL), lambda j: (0, 0)),
            pl.BlockSpec((bs, D_MODEL), lambda j: (0, 0)),
            pl.BlockSpec((1, D_MODEL, D_MODEL), lambda j: (0, 0, j)),
            pl.BlockSpec((1, D_MODEL), lambda j: (0, j)),
        ],
        out_specs=(pl.BlockSpec((bp, D_MODEL), lambda j: (0, j)), pl.BlockSpec((bs, D_MODEL), lambda j: (0, j))),
        compiler_params=pltpu.CompilerParams(dimension_semantics=("arbitrary",)),
        name="ada_ln",
    )(c_prompt, c_sample, w_ada, b_ada)


def _modulated_norm(x, norm_w, scale, shift):
    ms = jnp.mean(x * x, axis=-1, keepdims=True)
    return (x * lax.rsqrt(ms + EPS)) * norm_w * (1.0 + scale) + shift


def _pool_mix(d, wpool_ref, pscale, gp):
    db = d.astype(BF16)
    wide = 2 * POOL_GC
    y = jnp.concatenate([_dot(db[:, p * wide:(p + 1) * wide], wpool_ref[p])
                         for p in range(len(POOL_WINDOWS) // 2)], axis=1)
    return y * pscale * _silu(gp)


def _permuted_weight_block(wt_ref, g):
    cut = C_KW + IDX_DIM + IDX_HEADS
    c0 = g * LANES
    if c0 + LANES <= cut:
        blk = wt_ref[c0:c0 + LANES, :]
    elif c0 < cut:
        blk = jnp.concatenate([wt_ref[c0:cut, :], jnp.zeros((c0 + LANES - cut, D_MODEL), F32)], axis=0)
    else:
        r0 = c0 - (C_GA - cut)
        blk = wt_ref[r0:r0 + LANES, :]
    return blk.T.astype(BF16)


def _proj_prompt_kernel(x_ref, ada_ref, nw_ref, wt_ref, qnw_ref, knw_ref, wpool_ref, ps_ref,
                        qs_ref, kt_ref, kk_ref, vt_ref, vtx_ref, qi_ref, kit_ref, kik_ref,
                        wit_ref, sga_ref, pg_ref, ulast_ref, w_ref,
                        ext_ref):
    t = pl.program_id(1)

    @pl.when((pl.program_id(0) == 0) & (t == 0))
    def _():
        for g in range(N_PROJ // LANES):
            w_ref[:, g * LANES:(g + 1) * LANES] = _permuted_weight_block(wt_ref, g)

    ada = ada_ref[0]
    shift = ada[:, 0:D_MODEL]
    scale = ada[:, D_MODEL:2 * D_MODEL]
    sub_rows = KEY_CHUNK
    lo = _lane_iota((sub_rows, LANES)) < HEAD_DIM
    seg256 = _seg_ones(2 * LANES, HEAD_DIM)
    seg128 = _seg_ones(LANES, HEAD_DIM)
    ones = jnp.ones((BF16_ROWS, KEY_CHUNK), BF16)

    @pl.when(t == 0)
    def _():
        ext_ref[0:16, :] = jnp.zeros((16, POOL_W), F32)

    for sub in range(TM // sub_rows):
        rows = slice(sub * sub_rows, (sub + 1) * sub_rows)
        hb = _modulated_norm(x_ref[0, rows, :], nw_ref[...], scale, shift).astype(BF16)

        q = _dot(hb, w_ref[:, C_Q:C_Q + ATTN_W])
        qs_ref[rows, :] = _head_rms(q, seg256, qnw_ref[...]).astype(BF16)

        kv = _dot(hb, w_ref[:, C_K:C_K + 2 * LANES])
        k = _head_rms(kv[:, 0:LANES], seg128, knw_ref[...])
        kt_ref[0, :, rows] = k.T
        k_sw = pltpu.roll(k, HEAD_DIM, axis=1)
        kk_ref[0, rows, :] = jnp.where(lo, k, k_sw).astype(BF16)
        kk_ref[1, rows, :] = jnp.where(lo, k_sw, k).astype(BF16)

        v_t = kv[:, LANES:2 * LANES].T
        vt_ref[0, :, rows] = v_t
        for g in range(KV_HEADS):
            vtx_ref[0, g, sub, 0:HEAD_DIM, :] = v_t[g * HEAD_DIM:(g + 1) * HEAD_DIM].astype(BF16)
            vtx_ref[0, g, sub, HEAD_DIM:VT_ROWS, :] = ones

        qi_ref[rows, :] = _dot(hb, w_ref[:, C_QI:C_QI + ATTN_W]).astype(BF16)
        kw = _dot(hb, w_ref[:, C_KW:C_KW + LANES])
        kw_t = kw.T
        kit_ref[0, :, rows] = kw_t[0:IDX_DIM]
        wi_t = kw_t[IDX_DIM:IDX_DIM + IDX_HEADS] * ((IDX_HEADS * IDX_DIM) ** -0.5)
        for j in range(sub_rows // TQ):
            wit_ref[sub * (sub_rows // TQ) + j] = wi_t[:, j * TQ:(j + 1) * TQ]
        kik_ref[rows, :] = jnp.where(lo, kw, pltpu.roll(kw, HEAD_DIM, axis=1)).astype(BF16)

        sga_ref[rows, :] = _silu(_dot(hb, w_ref[:, C_GA:C_GA + ATTN_W])).astype(BF16)

        u = _dot(hb, w_ref[:, C_U:C_U + POOL_W])
        gp = _dot(hb, w_ref[:, C_GP:C_GP + POOL_W])
        base = 16 + sub * sub_rows
        ext_ref[base:base + sub_rows, :] = u
        pos = t * TM + sub * sub_rows + lax.broadcasted_iota(I32, (sub_rows, POOL_GC), 0)
        ds = []
        for g, w in enumerate(POOL_WINDOWS):
            cs = slice(g * POOL_GC, (g + 1) * POOL_GC)
            s = ext_ref[base - 16:base + sub_rows, cs]
            k = 1
            while k < w:
                s = s + pltpu.roll(s, k, axis=0)
                k *= 2
            cnt = jnp.minimum(pos + 1, w).astype(F32)
            ds.append(s[16:16 + sub_rows] / cnt - u[:, cs])
        d = jnp.concatenate(ds, axis=1)
        pg_ref[rows, :] = _pool_mix(d, wpool_ref, ps_ref[...], gp).astype(BF16)

    tail = ext_ref[TM:TM + 16, :]
    ulast_ref[0] = tail
    ext_ref[0:16, :] = tail


def _proj_prompt(x, ada_p, norm_w, w_in_t, qnw, knw, wpool_b, pscale):
    b, s, _ = x.shape
    n = b * s
    nt = s // TM
    cpt = TM // KEY_CHUNK
    row = lambda bi, ti: (bi * nt + ti, 0)
    tok = lambda bi, ti: (bi, 0, ti)
    const2 = lambda bi, ti: (0, 0)
    const3 = lambda bi, ti: (0, 0, 0)
    out_shape = (
        jax.ShapeDtypeStruct((n, ATTN_W), BF16),
        jax.ShapeDtypeStruct((b, LANES, s), F32),
        jax.ShapeDtypeStruct((KV_HEADS, n, LANES), BF16),
        jax.ShapeDtypeStruct((b, LANES, s), F32),
        jax.ShapeDtypeStruct((b, KV_HEADS, s // KEY_CHUNK, VT_ROWS, KEY_CHUNK), BF16),
        jax.ShapeDtypeStruct((n, ATTN_W), BF16),
        jax.ShapeDtypeStruct((b, IDX_DIM, s), F32),
        jax.ShapeDtypeStruct((n, LANES), BF16),
        jax.ShapeDtypeStruct((n // TQ, IDX_HEADS, TQ), F32),
        jax.ShapeDtypeStruct((n, ATTN_W), BF16),
        jax.ShapeDtypeStruct((n, POOL_W), BF16),
        jax.ShapeDtypeStruct((b, 16, POOL_W), F32),
        jax.ShapeDtypeStruct((D_MODEL, N_PROJ), BF16),
    )
    out_specs = (
        pl.BlockSpec((TM, ATTN_W), row),
        pl.BlockSpec((1, LANES, TM), tok),
        pl.BlockSpec((KV_HEADS, TM, LANES), lambda bi, ti: (0, bi * nt + ti, 0)),
        pl.BlockSpec((1, LANES, TM), tok),
        pl.BlockSpec((1, KV_HEADS, cpt, VT_ROWS, KEY_CHUNK), lambda bi, ti: (bi, 0, ti, 0, 0)),
        pl.BlockSpec((TM, ATTN_W), row),
        pl.BlockSpec((1, IDX_DIM, TM), tok),
        pl.BlockSpec((TM, LANES), row),
        pl.BlockSpec((TM // TQ, IDX_HEADS, TQ), lambda bi, ti: (bi * nt + ti, 0, 0)),
        pl.BlockSpec((TM, ATTN_W), row),
        pl.BlockSpec((TM, POOL_W), row),
        pl.BlockSpec((1, 16, POOL_W), lambda bi, ti: (bi, 0, 0)),
        pl.BlockSpec((D_MODEL, N_PROJ), const2),
    )
    in_specs = [
        pl.BlockSpec((1, TM, D_MODEL), lambda bi, ti: (bi, ti, 0)),
        pl.BlockSpec((1, 1, 3 * D_MODEL), lambda bi, ti: (bi, 0, 0)),
        pl.BlockSpec((1, D_MODEL), const2),
        pl.BlockSpec(w_in_t.shape, const2),
        pl.BlockSpec((1, ATTN_W), const2),
        pl.BlockSpec((1, LANES), const2),
        pl.BlockSpec((2, 2 * POOL_GC, 2 * POOL_GC), const3),
        pl.BlockSpec((1, POOL_W), const2),
    ]
    return pl.pallas_call(
        _proj_prompt_kernel,
        out_shape=out_shape,
        grid=(b, nt),
        in_specs=in_specs,
        out_specs=out_specs,
        scratch_shapes=[pltpu.VMEM((16 + TM, POOL_W), F32)],
        compiler_params=pltpu.CompilerParams(
            dimension_semantics=("arbitrary", "arbitrary"), vmem_limit_bytes=VMEM_LIMIT),
        name="proj_prompt",
    )(x, ada_p, norm_w, w_in_t, qnw, knw, wpool_b, pscale)


def _attn_tile(nch, row0, wit, kik_ref, kk_ref, vtx_ref, sc_ref, lg_ref, wq_ref, acc_ref):
    shape2 = (KEY_CHUNK, TQ)
    n_pairs = N_HEADS // 2
    group_of = lambda hp: (2 * hp) // (N_HEADS // KV_HEADS)

    for c in range(nch):
        keys = slice(c * KEY_CHUNK, (c + 1) * KEY_CHUNK)
        kk = kik_ref[keys, :]
        acc = jnp.zeros(shape2, F32)
        for hp in range(n_pairs):
            s2 = _nt_dot(kk, wq_ref[0, hp])
            acc = acc + jnp.maximum(s2[:, 0:TQ], 0.0) * wit[2 * hp:2 * hp + 1, :]
            acc = acc + jnp.maximum(s2[:, TQ:2 * TQ], 0.0) * wit[2 * hp + 1:2 * hp + 2, :]
        if c == nch - 1:
            kpos = c * KEY_CHUNK + lax.broadcasted_iota(I32, shape2, 0)
            qpos = row0 + lax.broadcasted_iota(I32, shape2, 1)
            acc = jnp.where(kpos <= qpos, acc, -jnp.inf)
        sc_ref[c] = acc

    for c in range(nch):
        keys = slice(c * KEY_CHUNK, (c + 1) * KEY_CHUNK)
        for hp in range(n_pairs):
            lg_ref[hp, c] = _nt_dot(kk_ref[group_of(hp), keys, :], wq_ref[1, hp])

    def count_pass(pred):
        parts = [_fold_rows(pred(sc_ref[c]).astype(F32), jnp.add) for c in range(nch)]
        while len(parts) > 1:
            parts = [sum(parts[j:j + 2]) for j in range(0, len(parts), 2)]
        return jnp.sum(parts[0], axis=0, keepdims=True)

    if nch * KEY_CHUNK <= TOPK:
        thr = jnp.full((1, TQ), -jnp.inf, F32)
        need = jnp.zeros((1, TQ), F32)
    else:
        thr = _kth_largest(lambda t: count_pass(lambda x: x >= t), (1, TQ), peel=SEARCH_PEEL)
        need = float(TOPK) - count_pass(lambda x: x > thr)

    tri = jnp.where(lax.broadcasted_iota(I32, (KEY_CHUNK, KEY_CHUNK), 1)
                    <= lax.broadcasted_iota(I32, (KEY_CHUNK, KEY_CHUNK), 0), 1.0, 0.0).astype(BF16)

    m_part = [jnp.full((SUBLANES, TQ), -jnp.inf, F32) for _ in range(N_HEADS)]
    eq_before = jnp.zeros((1, TQ), F32)
    for c in range(nch):
        sc = sc_ref[c]
        eq = sc == thr
        rank = _dot(tri, jnp.where(eq, 1.0, 0.0).astype(BF16)) + eq_before
        sel = (sc > thr) | (eq & (rank <= need))
        if c == nch - 1:
            sel = sel & (sc > -jnp.inf)
        eq_before = rank[KEY_CHUNK - 1:KEY_CHUNK, :]
        for hp in range(n_pairs):
            for par in range(2):
                cols = slice(par * TQ, (par + 1) * TQ)
                lg = jnp.where(sel, lg_ref[hp, c, :, cols], -jnp.inf)
                lg_ref[hp, c, :, cols] = lg
                m_part[2 * hp + par] = jnp.maximum(m_part[2 * hp + par], _fold_rows(lg, jnp.maximum))
    m_rows = [jnp.max(m, axis=0, keepdims=True) for m in m_part]

    for hp in range(n_pairs):
        m2 = jnp.concatenate([m_rows[2 * hp], m_rows[2 * hp + 1]], axis=1)
        acc = jnp.zeros((VT_ROWS, 2 * TQ), F32)
        for c in range(nch):
            p2 = jnp.exp(lg_ref[hp, c] - m2).astype(BF16)
            acc = acc + _dot(vtx_ref[0, group_of(hp), c], p2)
        acc_ref[hp] = acc


def _attn_prompt_kernel(qi_ref, wit_ref, qs_ref, sga_ref, kik_ref, kk_ref, vtx_ref, ag_ref,
                        sc_ref, lg_ref, wq_ref, acc_ref):
    step = pl.program_id(1)
    n_pairs = N_HEADS // 2
    tiles_per_step = KEY_CHUNK // TQ
    nch = step + 1
    lo = _lane_iota((TQ, LANES)) < HEAD_DIM

    def one_tile(sub, carry):
        rows = pl.ds(pl.multiple_of(sub * TQ, TQ), TQ)
        for kind, ref in enumerate((qi_ref, qs_ref)):
            for hp in range(n_pairs):
                pair = ref[rows, hp * LANES:(hp + 1) * LANES].astype(F32)
                wq_ref[kind, hp, 0:TQ, :] = jnp.where(lo, pair, 0.0).astype(BF16)
                wq_ref[kind, hp, TQ:2 * TQ, :] = jnp.where(lo, 0.0, pair).astype(BF16)

        wit = wit_ref[sub]
        row0 = (step * tiles_per_step + sub) * TQ
        for n_static in range(1, sc_ref.shape[0] + 1):
            @pl.when(nch == n_static)
            def _(n_static=n_static):
                _attn_tile(n_static, row0, wit, kik_ref, kk_ref, vtx_ref, sc_ref, lg_ref, wq_ref, acc_ref)

        for hp in range(n_pairs):
            a = acc_ref[hp]
            o0 = a[0:HEAD_DIM, 0:TQ] / a[HEAD_DIM:HEAD_DIM + 1, 0:TQ]
            o1 = a[0:HEAD_DIM, TQ:2 * TQ] / a[HEAD_DIM:HEAD_DIM + 1, TQ:2 * TQ]
            pair = jnp.concatenate([o0, o1], axis=0).T
            cols = slice(hp * LANES, (hp + 1) * LANES)
            ag_ref[rows, cols] = (pair * sga_ref[rows, cols].astype(F32)).astype(BF16)
        return carry

    lax.fori_loop(0, tiles_per_step, one_tile, 0)


def _attn_prompt(qi, wit, qs, sga, kik, kk, vtx, b, s):
    n = b * s
    nkc = s // KEY_CHUNK
    tps = KEY_CHUNK // TQ
    row = lambda bi, st: (bi * nkc + st, 0)
    return pl.pallas_call(
        _attn_prompt_kernel,
        out_shape=jax.ShapeDtypeStruct((n, ATTN_W), BF16),
        grid=(b, nkc),
        in_specs=[
            pl.BlockSpec((KEY_CHUNK, ATTN_W), row),
            pl.BlockSpec((tps, IDX_HEADS, TQ), lambda bi, st: (bi * nkc + st, 0, 0)),
            pl.BlockSpec((KEY_CHUNK, ATTN_W), row),
            pl.BlockSpec((KEY_CHUNK, ATTN_W), row),
            pl.BlockSpec((s, LANES), lambda bi, st: (bi, 0)),
            pl.BlockSpec((KV_HEADS, s, LANES), lambda bi, st: (0, bi, 0)),
            pl.BlockSpec((1, KV_HEADS, nkc, VT_ROWS, KEY_CHUNK), lambda bi, st: (bi, 0, 0, 0, 0)),
        ],
        out_specs=pl.BlockSpec((KEY_CHUNK, ATTN_W), row),
        scratch_shapes=[
            pltpu.VMEM((nkc, KEY_CHUNK, TQ), F32),
            pltpu.VMEM((N_HEADS // 2, nkc, KEY_CHUNK, 2 * TQ), F32),
            pltpu.VMEM((2, N_HEADS // 2, 2 * TQ, LANES), BF16),
            pltpu.VMEM((N_HEADS // 2, VT_ROWS, 2 * TQ), F32),
        ],
        compiler_params=pltpu.CompilerParams(
            dimension_semantics=("arbitrary", "arbitrary"), vmem_limit_bytes=VMEM_LIMIT),
        name="attn_prompt",
    )(qi, wit, qs, sga, kik, kk, vtx)


def _out_proj_kernel(x_ref, a_ref, p_ref, gate_ref, w_ref, o_ref):
    y = _dot(a_ref[...], w_ref[0:ATTN_W, :]) + _dot(p_ref[...], w_ref[ATTN_W:ATTN_W + POOL_W, :])
    o_ref[0] = x_ref[0] + gate_ref[0] * y


def _out_proj_prompt(x, ag, pg, ada_p, w_out_b):
    b, s, _ = x.shape
    nt = s // TM_OUT
    row = lambda bi, ti: (bi * nt + ti, 0)
    return pl.pallas_call(
        _out_proj_kernel,
        out_shape=jax.ShapeDtypeStruct(x.shape, F32),
        grid=(b, nt),
        in_specs=[
            pl.BlockSpec((1, TM_OUT, D_MODEL), lambda bi, ti: (bi, ti, 0)),
            pl.BlockSpec((TM_OUT, ATTN_W), row),
            pl.BlockSpec((TM_OUT, POOL_W), row),
            pl.BlockSpec((1, 1, D_MODEL), lambda bi, ti: (bi, 0, 2)),
            pl.BlockSpec((D_MODEL, D_MODEL), lambda bi, ti: (0, 0)),
        ],
        out_specs=pl.BlockSpec((1, TM_OUT, D_MODEL), lambda bi, ti: (bi, ti, 0)),
        compiler_params=pltpu.CompilerParams(
            dimension_semantics=("arbitrary", "arbitrary"), vmem_limit_bytes=VMEM_LIMIT),
        name="out_proj_prompt",
    )(x, ag, pg, ada_p, w_out_b)


def _proj_sample_kernel(x_ref, ada_ref, nw_ref, w_ref, qnw_ref, knw_ref, wpool_ref, ps_ref, hist_ref,
                        qpad_ref, k_ref, v_ref, ki_ref, qi_ref, wi_ref, snew_ref, lnew_ref,
                        sga_ref, pg_ref, pool_ref):
    nb = x_ref.shape[0]
    x = x_ref[...]
    shift = ada_ref[:, 0:D_MODEL]
    scale = ada_ref[:, D_MODEL:2 * D_MODEL]
    hb = _modulated_norm(x, nw_ref[...], scale, shift).astype(BF16)

    lane = _lane_iota((nb, LANES))
    lo = lane < HEAD_DIM
    seg256 = _seg_ones(2 * LANES, HEAD_DIM)
    seg128 = _seg_ones(LANES, HEAD_DIM)
    head_sel = jnp.where(lax.broadcasted_iota(I32, (ATTN_W, LANES), 0) // HEAD_DIM
                         == lax.broadcasted_iota(I32, (ATTN_W, LANES), 1), 1.0, 0.0).astype(BF16)

    def head_sums(prod):
        hi = prod.astype(BF16)
        rest = (prod - hi.astype(F32)).astype(BF16)
        return _dot(hi, head_sel) + _dot(rest, head_sel)

    q = _head_rms(_dot(hb, w_ref[:, C_Q:C_Q + ATTN_W]), seg256, qnw_ref[...])
    qb = q.astype(BF16)
    k = _head_rms(_dot(hb, w_ref[:, C_K:C_K + LANES]), seg128, knw_ref[...])
    k_ref[...] = k
    v = _dot(hb, w_ref[:, C_V:C_V + LANES])
    v_ref[...] = v

    for hp in range(N_HEADS // 2):
        pair = q[:, hp * LANES:(hp + 1) * LANES]
        pair_sw = pltpu.roll(pair, HEAD_DIM, axis=1)
        g = (2 * hp) // (N_HEADS // KV_HEADS)
        if g == 0:
            h_even, h_odd = jnp.where(lo, pair, 0.0), jnp.where(lo, pair_sw, 0.0)
        else:
            h_even, h_odd = jnp.where(lo, 0.0, pair_sw), jnp.where(lo, 0.0, pair)
        qpad_ref[:, (2 * hp) * LANES:(2 * hp + 1) * LANES] = h_even.astype(BF16)
        qpad_ref[:, (2 * hp + 1) * LANES:(2 * hp + 2) * LANES] = h_odd.astype(BF16)

    kq = k.astype(BF16).astype(F32)
    kq_sw = pltpu.roll(kq, HEAD_DIM, axis=1)
    k0t = jnp.where(lo, kq, kq_sw)
    k1t = jnp.where(lo, kq_sw, kq)
    qf = qb.astype(F32)
    prod = jnp.concatenate([qf[:, 0:LANES] * k0t, qf[:, LANES:2 * LANES] * k0t,
                            qf[:, 2 * LANES:3 * LANES] * k1t, qf[:, 3 * LANES:4 * LANES] * k1t], axis=1)
    lnew_ref[...] = head_sums(prod)

    qi = _dot(hb, w_ref[:, C_QI:C_QI + ATTN_W])
    qib = qi.astype(BF16)
    qi_ref[...] = qib
    kw = _dot(hb, w_ref[:, C_KW:C_KW + LANES])
    kw_sw = pltpu.roll(kw, HEAD_DIM, axis=1)
    ki_ref[...] = kw[:, 0:IDX_DIM]
    wi_full = jnp.where(lane < IDX_HEADS, kw_sw, 0.0) * ((IDX_HEADS * IDX_DIM) ** -0.5)
    wi_ref[...] = wi_full[:, 0:IDX_HEADS]

    kib = kw.astype(BF16).astype(F32)
    kit = jnp.where(lo, kib, pltpu.roll(kib, HEAD_DIM, axis=1))
    qif = qib.astype(F32)
    prod_i = jnp.concatenate([qif[:, j * LANES:(j + 1) * LANES] * kit for j in range(4)], axis=1)
    s_new = jnp.maximum(head_sums(prod_i), 0.0) * wi_full
    snew_ref[...] = jnp.broadcast_to(jnp.sum(s_new, axis=1, keepdims=True), (nb, LANES))

    sga_ref[...] = _silu(_dot(hb, w_ref[:, C_GA:C_GA + ATTN_W])).astype(BF16)

    u = _dot(hb, w_ref[:, C_U:C_U + POOL_W])
    gp = _dot(hb, w_ref[:, C_GP:C_GP + POOL_W])
    for j in range(POOL_HIST - 1):
        pool_ref[j] = hist_ref[j + 1]
    pool_ref[POOL_HIST - 1] = u
    ds = []
    for g, w in enumerate(POOL_WINDOWS):
        cs = slice(g * POOL_GC, (g + 1) * POOL_GC)
        s = u[:, cs]
        for j in range(1, w):
            s = s + hist_ref[POOL_HIST - j, :, cs]
        ds.append(s / float(w) - u[:, cs])
    d = jnp.concatenate(ds, axis=1)
    pg_ref[...] = _pool_mix(d, wpool_ref, ps_ref[...], gp).astype(BF16)


def _proj_sample(x, ada_s, norm_w, w_in_b, qnw, knw, wpool_b, pscale, hist_t):
    nb = x.shape[0]
    out_shape = (
        jax.ShapeDtypeStruct((nb, N_HEADS * LANES), BF16),
        jax.ShapeDtypeStruct((nb, LANES), F32),
        jax.ShapeDtypeStruct((nb, LANES), F32),
        jax.ShapeDtypeStruct((nb, IDX_DIM), F32),
        jax.ShapeDtypeStruct((nb, ATTN_W), BF16),
        jax.ShapeDtypeStruct((nb, IDX_HEADS), F32),
        jax.ShapeDtypeStruct((nb, LANES), F32),
        jax.ShapeDtypeStruct((nb, LANES), F32),
        jax.ShapeDtypeStruct((nb, ATTN_W), BF16),
        jax.ShapeDtypeStruct((nb, POOL_W), BF16),
        jax.ShapeDtypeStruct((POOL_HIST, nb, POOL_W), F32),
    )
    return pl.pallas_call(
        _proj_sample_kernel,
        out_shape=out_shape,
        compiler_params=pltpu.CompilerParams(vmem_limit_bytes=VMEM_LIMIT),
        name="proj_sample",
    )(x, ada_s, norm_w, w_in_b, qnw, knw, wpool_b, pscale, hist_t)


def _page_copies(pt_ref, pages_hbm, buf_ref, sem_ref, step, slot, group, n_pages):
    copies = []
    for j in range(group):
        for p in range(n_pages):
            page = pt_ref[step * group + j, p]
            copies.append(pltpu.make_async_copy(pages_hbm.at[page], buf_ref.at[slot, j, p], sem_ref.at[slot]))
    return copies


def _row_pages(buf_ref, slot, j):
    return jnp.concatenate([buf_ref[slot, j, p].astype(BF16) for p in range(buf_ref.shape[2])], axis=1)


def _score_sample_kernel(pt_ref, qi_ref, wi_ref, kidx_hbm, o_ref, buf_ref, sem_ref, *, group, n_pages):
    step = pl.program_id(0)
    slot = lax.rem(step, 2)

    def copies(st, sl):
        return _page_copies(pt_ref, kidx_hbm, buf_ref, sem_ref, st, sl, group, n_pages)

    @pl.when(step == 0)
    def _():
        for cp in copies(step, slot):
            cp.start()

    @pl.when(step + 1 < pl.num_programs(0))
    def _():
        for cp in copies(step + 1, 1 - slot):
            cp.start()

    for cp in copies(step, slot):
        cp.wait()

    for j in range(group):
        ki_t = _row_pages(buf_ref, slot, j)
        s = _dot(qi_ref[j], ki_t)
        o_ref[j:j + 1, :] = jnp.sum(jnp.maximum(s, 0.0) * wi_ref[j], axis=0, keepdims=True)


def _score_sample(page_table, qi3, wi3, kidx_t):
    nb, n_pages = page_table.shape
    n_keys = n_pages * PAGE
    g = SCORE_GROUP
    return pl.pallas_call(
        functools.partial(_score_sample_kernel, group=g, n_pages=n_pages),
        out_shape=jax.ShapeDtypeStruct((nb, n_keys), F32),
        grid_spec=pltpu.PrefetchScalarGridSpec(
            num_scalar_prefetch=1,
            grid=(nb // g,),
            in_specs=[
                pl.BlockSpec((g, IDX_HEADS, IDX_DIM), lambda s, pt: (s, 0, 0)),
                pl.BlockSpec((g, IDX_HEADS, 1), lambda s, pt: (s, 0, 0)),
                pl.BlockSpec(memory_space=pl.ANY),
            ],
            out_specs=pl.BlockSpec((g, n_keys), lambda s, pt: (s, 0)),
            scratch_shapes=[
                pltpu.VMEM((2, g, n_pages, IDX_DIM, PAGE), F32),
                pltpu.SemaphoreType.DMA((2,)),
            ],
        ),
        compiler_params=pltpu.CompilerParams(
            dimension_semantics=("arbitrary",), vmem_limit_bytes=VMEM_LIMIT),
        name="score_sample",
    )(page_table, qi3, wi3, kidx_t)


def _select_sample_kernel(sc_ref, snew_ref, mask_ref, mnew_ref):
    nb, n_keys = sc_ref.shape
    nch = n_keys // KEY_CHUNK
    snew = snew_ref[...]
    ones_mat = jnp.ones((LANES, LANES), BF16)

    def count_pass(pred):
        acc = jnp.zeros((nb, LANES), F32)
        for c in range(n_keys // LANES):
            acc = acc + pred(sc_ref[:, c * LANES:(c + 1) * LANES]).astype(F32)
        return _dot(acc.astype(BF16), ones_mat) + pred(snew).astype(F32)

    thr = _kth_largest(lambda t: count_pass(lambda x: x >= t), (nb, LANES))
    need = float(TOPK) - count_pass(lambda x: x > thr)
    thr2 = jnp.concatenate([thr, thr], axis=1)
    need2 = jnp.concatenate([need, need], axis=1)
    tri = jnp.where(lax.broadcasted_iota(I32, (KEY_CHUNK, KEY_CHUNK), 0)
                    <= lax.broadcasted_iota(I32, (KEY_CHUNK, KEY_CHUNK), 1), 1.0, 0.0).astype(BF16)
    ones_cl = jnp.ones((KEY_CHUNK, LANES), BF16)

    eq_before = jnp.zeros((nb, LANES), F32)
    for c in range(nch):
        sc = sc_ref[:, c * KEY_CHUNK:(c + 1) * KEY_CHUNK]
        eq = sc == thr2
        eqb = jnp.where(eq, 1.0, 0.0).astype(BF16)
        rank = _dot(eqb, tri) + jnp.concatenate([eq_before, eq_before], axis=1)
        sel = (sc > thr2) | (eq & (rank <= need2))
        mask_ref[:, c * KEY_CHUNK:(c + 1) * KEY_CHUNK] = jnp.where(sel, 0.0, -jnp.inf)
        eq_before = eq_before + _dot(eqb, ones_cl)
    sel_new = (snew > thr) | ((snew == thr) & (eq_before + 1.0 <= need))
    mnew_ref[...] = jnp.where(sel_new, 0.0, -jnp.inf)


def _select_sample(scores, snew):
    nb, n_keys = scores.shape
    return pl.pallas_call(
        _select_sample_kernel,
        out_shape=(jax.ShapeDtypeStruct((nb, n_keys), F32), jax.ShapeDtypeStruct((nb, LANES), F32)),
        name="select_sample",
    )(scores, snew)


def _attn_sample_kernel(pt_ref, q_ref, mask_ref, lnew_ref, mnew_ref, vnew_ref, k_hbm, v_hbm,
                        o_ref, kbuf_ref, vbuf_ref, ksem_ref, vsem_ref, *, group, n_pages):
    step = pl.program_id(0)
    slot = lax.rem(step, 2)

    def copies(st, sl):
        return (_page_copies(pt_ref, k_hbm, kbuf_ref, ksem_ref, st, sl, group, n_pages)
                + _page_copies(pt_ref, v_hbm, vbuf_ref, vsem_ref, st, sl, group, n_pages))

    @pl.when(step == 0)
    def _():
        for cp in copies(step, slot):
            cp.start()

    @pl.when(step + 1 < pl.num_programs(0))
    def _():
        for cp in copies(step + 1, 1 - slot):
            cp.start()

    for cp in copies(step, slot):
        cp.wait()

    for j in range(group):
        k_t = _row_pages(kbuf_ref, slot, j)
        lg = _dot(q_ref[j], k_t) + mask_ref[j:j + 1, :]
        lg_n = lnew_ref[j] + mnew_ref[j]
        m = jnp.maximum(jnp.max(lg, axis=1, keepdims=True), lg_n)
        p = jnp.exp(lg - m)
        p_n = jnp.exp(lg_n - m)
        l = jnp.sum(p, axis=1, keepdims=True) + p_n
        acc = _nt_dot(p.astype(BF16), _row_pages(vbuf_ref, slot, j)) + p_n * vnew_ref[j]
        o_ref[j] = acc / l


def _attn_sample(page_table, qpad3, mask3, lnew3, mnew3, vnew3, k_t, v_t):
    nb, n_pages = page_table.shape
    n_keys = n_pages * PAGE
    g = ATTN_GROUP
    per_s = lambda s, pt: (s, 0, 0)
    return pl.pallas_call(
        functools.partial(_attn_sample_kernel, group=g, n_pages=n_pages),
        out_shape=jax.ShapeDtypeStruct((nb, N_HEADS, LANES), F32),
        grid_spec=pltpu.PrefetchScalarGridSpec(
            num_scalar_prefetch=1,
            grid=(nb // g,),
            in_specs=[
                pl.BlockSpec((g, N_HEADS, LANES), per_s),
                pl.BlockSpec((g, n_keys), lambda s, pt: (s, 0)),
                pl.BlockSpec((g, N_HEADS, 1), per_s),
                pl.BlockSpec((g, N_HEADS, 1), per_s),
                pl.BlockSpec((g, 1, LANES), per_s),
                pl.BlockSpec(memory_space=pl.ANY),
                pl.BlockSpec(memory_space=pl.ANY),
            ],
            out_specs=pl.BlockSpec((g, N_HEADS, LANES), per_s),
            scratch_shapes=[
                pltpu.VMEM((2, g, n_pages, LANES, PAGE), F32),
                pltpu.VMEM((2, g, n_pages, LANES, PAGE), F32),
                pltpu.SemaphoreType.DMA((2,)),
                pltpu.SemaphoreType.DMA((2,)),
            ],
        ),
        compiler_params=pltpu.CompilerParams(
            dimension_semantics=("arbitrary",), vmem_limit_bytes=VMEM_LIMIT),
        name="attn_sample",
    )(page_table, qpad3, mask3, lnew3, mnew3, vnew3, k_t, v_t)


def _out_proj_sample_kernel(x_ref, a_ref, sga_ref, p_ref, ada_ref, w_ref, o_ref):
    lo = _lane_iota((x_ref.shape[0], LANES)) < HEAD_DIM
    pairs = []
    for hp in range(N_HEADS // 2):
        even = a_ref[:, (2 * hp) * LANES:(2 * hp + 1) * LANES]
        odd = a_ref[:, (2 * hp + 1) * LANES:(2 * hp + 2) * LANES]
        if (2 * hp) // (N_HEADS // KV_HEADS) == 0:
            pairs.append(jnp.where(lo, even, pltpu.roll(odd, HEAD_DIM, axis=1)))
        else:
            pairs.append(jnp.where(lo, pltpu.roll(even, HEAD_DIM, axis=1), odd))
    a = jnp.concatenate(pairs, axis=1)
    ag = (a * sga_ref[...].astype(F32)).astype(BF16)
    y = _dot(ag, w_ref[0:ATTN_W, :]) + _dot(p_ref[...], w_ref[ATTN_W:ATTN_W + POOL_W, :])
    o_ref[...] = x_ref[...] + ada_ref[:, 2 * D_MODEL:3 * D_MODEL] * y


def _out_proj_sample(x, a, sga, pg, ada_s, w_out_b):
    return pl.pallas_call(
        _out_proj_sample_kernel,
        out_shape=jax.ShapeDtypeStruct(x.shape, F32),
        compiler_params=pltpu.CompilerParams(vmem_limit_bytes=VMEM_LIMIT),
        name="out_proj_sample",
    )(x, a, sga, pg, ada_s, w_out_b)


def kernel(x_prompt, x_sample, cache_k, cache_v, cache_kidx, state_pool, page_table, c_prompt, c_sample,
           norm_w, w_ada, b_ada, w_in, q_norm_w, k_norm_w, w_pool, pool_scale, w_out):
    bp, s, _ = x_prompt.shape
    bs = x_sample.shape[0]
    assert w_in.shape[0] == 1 and x_sample.shape[1] == 1, "single layer, single decode token"
    n_phys = cache_k.shape[1]

    w_in_t = jnp.transpose(w_in[0])
    w_out_b = w_out[0].astype(BF16)
    zero_blk = jnp.zeros((POOL_GC, POOL_GC), w_pool.dtype)
    wpool_b = jnp.stack([jnp.block([[w_pool[0, 2 * p], zero_blk], [zero_blk, w_pool[0, 2 * p + 1]]])
                         for p in range(len(POOL_WINDOWS) // 2)]).astype(BF16)
    qnw = jnp.tile(q_norm_w[0], N_HEADS)[None, :] * (HEAD_DIM ** -0.5)
    knw = jnp.tile(k_norm_w[0], KV_HEADS)[None, :]
    nw = norm_w[0][None, :]
    pscale = pool_scale[0][None, :]

    ada_p, ada_s = _ada(c_prompt, c_sample, w_ada, b_ada)
    ada_p = ada_p.reshape(bp, 1, 3 * D_MODEL)

    (qs, k_t_p, kk, v_t_p, vtx, qi, ki_t_p, kik, wit, sga, pg, ulast, w_in_b) = _proj_prompt(
        x_prompt, ada_p, nw, w_in_t, qnw, knw, wpool_b, pscale)
    ag = _attn_prompt(qi, wit, qs, sga, kik, kk, vtx, bp, s)
    y_prompt = _out_proj_prompt(x_prompt, ag, pg, ada_p, w_out_b)

    hist_t = jnp.transpose(state_pool[0], (1, 0, 2))
    (qpad, k_s, v_s, ki_s, qi_s, wi_s, snew, lnew, sga_s, pg_s, pool_s) = _proj_sample(
        x_sample[:, 0, :], ada_s, nw, w_in_b, qnw, knw, wpool_b, pscale, hist_t)
    kidx_t = jnp.transpose(cache_kidx[0], (0, 2, 1))
    k_t = jnp.transpose(cache_k[0], (0, 2, 3, 1)).reshape(n_phys, LANES, PAGE)
    v_t = jnp.transpose(cache_v[0], (0, 2, 3, 1)).reshape(n_phys, LANES, PAGE)
    scores = _score_sample(page_table, qi_s.reshape(bs, IDX_HEADS, IDX_DIM), wi_s.reshape(bs, IDX_HEADS, 1),
                           kidx_t)
    mask, mnew = _select_sample(scores, snew)
    o_s = _attn_sample(
        page_table, qpad.reshape(bs, N_HEADS, LANES), mask,
        lnew[:, :N_HEADS].reshape(bs, N_HEADS, 1),
        jnp.broadcast_to(mnew[:, :1], (bs, N_HEADS)).reshape(bs, N_HEADS, 1),
        v_s.reshape(bs, 1, LANES), k_t, v_t)
    y_sample = _out_proj_sample(x_sample[:, 0, :], o_s.reshape(bs, N_HEADS * LANES), sga_s, pg_s, ada_s, w_out_b)

    to_heads = lambda a: jnp.transpose(a.reshape(bp, KV_HEADS, HEAD_DIM, s), (0, 3, 1, 2))[None]
    return (
        y_prompt,
        y_sample[:, None, :],
        to_heads(k_t_p),
        to_heads(v_t_p),
        jnp.transpose(ki_t_p, (0, 2, 1))[None],
        ulast[:, 1:, :][None],
        k_s.reshape(1, bs, 1, KV_HEADS, HEAD_DIM),
        v_s.reshape(1, bs, 1, KV_HEADS, HEAD_DIM),
        ki_s.reshape(1, bs, 1, IDX_DIM),
        jnp.transpose(pool_s, (1, 0, 2))[None],
    )
```

```python
import functools

import jax
import jax.numpy as jnp
import numpy as np
from jax import lax
from jax.experimental import pallas as pl
from jax.experimental.pallas import tpu as pltpu

F32 = jnp.float32
BF16 = jnp.bfloat16
I32 = jnp.int32

D_MODEL = 1024
ATTN_W = 512
POOL_W = 512
HEAD_DIM = 64
N_HEADS = 8
KV_HEADS = 2
IDX_HEADS = 8
IDX_DIM = 64
TOPK = 256
POOL_WINDOWS = (2, 4, 8, 16)
POOL_GC = 128
POOL_HIST = 15
EPS = 1e-6
PAGE = 128

LANES = 128
SUBLANES = 8
BF16_ROWS = 16
KEY_CHUNK = 256
TQ = 128
TM = 512
TM_OUT = 1024
VT_ROWS = HEAD_DIM + BF16_ROWS
SEARCH_PEEL = 10
SCORE_GROUP = 8
ATTN_GROUP = 8
INT_MIN = np.int32(-2 ** 31)
VMEM_LIMIT = 56 * 1024 * 1024

C_Q, C_K, C_V, C_QI, C_KW, C_GA, C_U, C_GP, N_PROJ = 0, 512, 640, 768, 1280, 1408, 1920, 2432, 2944


def _nt_dot(a, b):
    return lax.dot_general(a, b, (((1,), (1,)), ((), ())), preferred_element_type=F32)


def _dot(a, b):
    return jnp.dot(a, b, preferred_element_type=F32)


def _silu(z):
    return z / (1.0 + jnp.exp(-z))


def _lane_iota(shape):
    return lax.broadcasted_iota(I32, shape, len(shape) - 1)


def _seg_ones(n, seg):
    r = lax.broadcasted_iota(I32, (n, n), 0) // seg
    c = lax.broadcasted_iota(I32, (n, n), 1) // seg
    return jnp.where(r == c, 1.0, 0.0).astype(BF16)


def _head_rms(z, seg_mat, w):
    n = seg_mat.shape[0]
    sq = (z * z).astype(BF16)
    ss = jnp.concatenate([_dot(sq[:, j:j + n], seg_mat) for j in range(0, z.shape[1], n)], axis=1)
    return z * lax.rsqrt(ss * (1.0 / HEAD_DIM) + EPS) * w


def _float_of_rank(u):
    key = u ^ INT_MIN
    bits = key ^ ((key >> 31) & np.int32(0x7FFFFFFF))
    return lax.bitcast_convert_type(bits, F32)


def _kth_largest(count_ge, shape, peel=0):
    def bit_body(it, carry):
        ans, t, above = carry
        bit = jnp.left_shift(jnp.int32(1), 31 - it)
        nxt = lax.shift_right_logical(bit, jnp.int32(1))
        cand = ans | bit
        t_if_kept, t_if_dropped = _float_of_rank(cand | nxt), _float_of_rank(ans | nxt)
        cnt = count_ge(t)
        keep = cnt >= float(TOPK)
        return jnp.where(keep, cand, ans), jnp.where(keep, t_if_kept, t_if_dropped), jnp.where(keep, above, cnt)

    carry = (jnp.zeros(shape, I32), jnp.zeros(shape, F32), jnp.zeros(shape, F32))
    for it in range(peel):
        carry = bit_body(jnp.int32(it), carry)
    ans, _, above = lax.fori_loop(peel, 32, bit_body, carry)
    return jnp.where(ans == 0, -jnp.inf, _float_of_rank(ans)), above


def _fold_rows(x, op):
    parts = [x[r:r + SUBLANES] for r in range(0, x.shape[0], SUBLANES)]
    while len(parts) > 1:
        parts = [op(parts[i], parts[i + 1]) for i in range(0, len(parts), 2)]
    return parts[0]


def _ada_kernel(cp_ref, cs_ref, w_ref, b_ref, op_ref, os_ref):
    w = w_ref[0].astype(BF16)
    op_ref[...] = _dot(_silu(cp_ref[...]).astype(BF16), w) + b_ref[...]
    os_ref[...] = _dot(_silu(cs_ref[...]).astype(BF16), w) + b_ref[...]


def _ada(c_prompt, c_sample, w_ada, b_ada):
    bp, bs = c_prompt.shape[0], c_sample.shape[0]
    return pl.pallas_call(
        _ada_kernel,
        out_shape=(jax.ShapeDtypeStruct((bp, 3 * D_MODEL), F32), jax.ShapeDtypeStruct((bs, 3 * D_MODEL), F32)),
        grid=(3,),
        in_specs=[
            pl.BlockSpec((bp, D_MODEL), lambda j: (0, 0)),
            pl.BlockSpec((bs, D_MODEL), lambda j: (0, 0)),
            pl.BlockSpec((1, D_MODEL, D_MODEL), lambda j: (0, 0, j)),
            pl.BlockSpec((1, D_MODEL), lambda j: (0, j)),
        ],
        out_specs=(pl.BlockSpec((bp, D_MODEL), lambda j: (0, j)), pl.BlockSpec((bs, D_MODEL), lambda j: (0, j))),
        compiler_params=pltpu.CompilerParams(dimension_semantics=("arbitrary",)),
        name="ada_ln",
    )(c_prompt, c_sample, w_ada, b_ada)


def _modulated_norm(x, norm_w, scale, shift):
    ms = jnp.mean(x * x, axis=-1, keepdims=True)
    return (x * lax.rsqrt(ms + EPS)) * norm_w * (1.0 + scale) + shift


def _pool_mix(d, wpool_ref, pscale, gp):
    db = d.astype(BF16)
    wide = 2 * POOL_GC
    y = jnp.concatenate([_dot(db[:, p * wide:(p + 1) * wide], wpool_ref[p])
                         for p in range(len(POOL_WINDOWS) // 2)], axis=1)
    return y * pscale * _silu(gp)


def _permuted_weight_block(wt_ref, g):
    cut = C_KW + IDX_DIM + IDX_HEADS
    c0 = g * LANES
    if c0 + LANES <= cut:
        blk = wt_ref[c0:c0 + LANES, :]
    elif c0 < cut:
        blk = jnp.concatenate([wt_ref[c0:cut, :], jnp.zeros((c0 + LANES - cut, D_MODEL), F32)], axis=0)
    else:
        r0 = c0 - (C_GA - cut)
        blk = wt_ref[r0:r0 + LANES, :]
    return blk.T.astype(BF16)


def _proj_prompt_kernel(x_ref, ada_ref, nw_ref, wt_ref, qnw_ref, knw_ref, wpool_ref, ps_ref,
                        qs_ref, kt_ref, kk_ref, vt_ref, vtx_ref, qi_ref, kit_ref, kik_ref,
                        wit_ref, sga_ref, pg_ref, ulast_ref, w_ref,
                        ext_ref):
    t = pl.program_id(1)

    @pl.when((pl.program_id(0) == 0) & (t == 0))
    def _():
        for g in range(N_PROJ // LANES):
            w_ref[:, g * LANES:(g + 1) * LANES] = _permuted_weight_block(wt_ref, g)

    ada = ada_ref[0]
    shift = ada[:, 0:D_MODEL]
    scale = ada[:, D_MODEL:2 * D_MODEL]
    sub_rows = KEY_CHUNK
    lo = _lane_iota((sub_rows, LANES)) < HEAD_DIM
    seg256 = _seg_ones(2 * LANES, HEAD_DIM)
    seg128 = _seg_ones(LANES, HEAD_DIM)
    ones = jnp.ones((BF16_ROWS, KEY_CHUNK), BF16)

    @pl.when(t == 0)
    def _():
        ext_ref[0:16, :] = jnp.zeros((16, POOL_W), F32)

    for sub in range(TM // sub_rows):
        rows = slice(sub * sub_rows, (sub + 1) * sub_rows)
        hb = _modulated_norm(x_ref[0, rows, :], nw_ref[...], scale, shift).astype(BF16)

        q = _dot(hb, w_ref[:, C_Q:C_Q + ATTN_W])
        qs_ref[rows, :] = _head_rms(q, seg256, qnw_ref[...]).astype(BF16)

        kv = _dot(hb, w_ref[:, C_K:C_K + 2 * LANES])
        k = _head_rms(kv[:, 0:LANES], seg128, knw_ref[...])
        kt_ref[0, :, rows] = k.T
        k_sw = pltpu.roll(k, HEAD_DIM, axis=1)
        kk_ref[0, rows, :] = jnp.where(lo, k, k_sw).astype(BF16)
        kk_ref[1, rows, :] = jnp.where(lo, k_sw, k).astype(BF16)

        v_t = kv[:, LANES:2 * LANES].T
        vt_ref[0, :, rows] = v_t
        for g in range(KV_HEADS):
            vtx_ref[0, g, sub, 0:HEAD_DIM, :] = v_t[g * HEAD_DIM:(g + 1) * HEAD_DIM].astype(BF16)
            vtx_ref[0, g, sub, HEAD_DIM:VT_ROWS, :] = ones

        qi_ref[rows, :] = _dot(hb, w_ref[:, C_QI:C_QI + ATTN_W]).astype(BF16)
        kw = _dot(hb, w_ref[:, C_KW:C_KW + LANES])
        kw_t = kw.T
        kit_ref[0, :, rows] = kw_t[0:IDX_DIM]
        wi_t = kw_t[IDX_DIM:IDX_DIM + IDX_HEADS] * ((IDX_HEADS * IDX_DIM) ** -0.5)
        for j in range(sub_rows // TQ):
            wit_ref[sub * (sub_rows // TQ) + j] = wi_t[:, j * TQ:(j + 1) * TQ]
        kik_ref[rows, :] = jnp.where(lo, kw, pltpu.roll(kw, HEAD_DIM, axis=1)).astype(BF16)

        sga_ref[rows, :] = _silu(_dot(hb, w_ref[:, C_GA:C_GA + ATTN_W])).astype(BF16)

        u = _dot(hb, w_ref[:, C_U:C_U + POOL_W])
        gp = _dot(hb, w_ref[:, C_GP:C_GP + POOL_W])
        base = 16 + sub * sub_rows
        ext_ref[base:base + sub_rows, :] = u
        pos = t * TM + sub * sub_rows + lax.broadcasted_iota(I32, (sub_rows, POOL_GC), 0)
        ds = []
        for g, w in enumerate(POOL_WINDOWS):
            cs = slice(g * POOL_GC, (g + 1) * POOL_GC)
            s = ext_ref[base - 16:base + sub_rows, cs]
            k = 1
            while k < w:
                s = s + pltpu.roll(s, k, axis=0)
                k *= 2
            cnt = jnp.minimum(pos + 1, w).astype(F32)
            ds.append(s[16:16 + sub_rows] / cnt - u[:, cs])
        d = jnp.concatenate(ds, axis=1)
        pg_ref[rows, :] = _pool_mix(d, wpool_ref, ps_ref[...], gp).astype(BF16)

    tail = ext_ref[TM:TM + 16, :]
    ulast_ref[0] = tail
    ext_ref[0:16, :] = tail


def _proj_prompt(x, ada_p, norm_w, w_in_t, qnw, knw, wpool_b, pscale):
    b, s, _ = x.shape
    n = b * s
    nt = s // TM
    cpt = TM // KEY_CHUNK
    row = lambda bi, ti: (bi * nt + ti, 0)
    tok = lambda bi, ti: (bi, 0, ti)
    const2 = lambda bi, ti: (0, 0)
    const3 = lambda bi, ti: (0, 0, 0)
    out_shape = (
        jax.ShapeDtypeStruct((n, ATTN_W), BF16),
        jax.ShapeDtypeStruct((b, LANES, s), F32),
        jax.ShapeDtypeStruct((KV_HEADS, n, LANES), BF16),
        jax.ShapeDtypeStruct((b, LANES, s), F32),
        jax.ShapeDtypeStruct((b, KV_HEADS, s // KEY_CHUNK, VT_ROWS, KEY_CHUNK), BF16),
        jax.ShapeDtypeStruct((n, ATTN_W), BF16),
        jax.ShapeDtypeStruct((b, IDX_DIM, s), F32),
        jax.ShapeDtypeStruct((n, LANES), BF16),
        jax.ShapeDtypeStruct((n // TQ, IDX_HEADS, TQ), F32),
        jax.ShapeDtypeStruct((n, ATTN_W), BF16),
        jax.ShapeDtypeStruct((n, POOL_W), BF16),
        jax.ShapeDtypeStruct((b, 16, POOL_W), F32),
        jax.ShapeDtypeStruct((D_MODEL, N_PROJ), BF16),
    )
    out_specs = (
        pl.BlockSpec((TM, ATTN_W), row),
        pl.BlockSpec((1, LANES, TM), tok),
        pl.BlockSpec((KV_HEADS, TM, LANES), lambda bi, ti: (0, bi * nt + ti, 0)),
        pl.BlockSpec((1, LANES, TM), tok),
        pl.BlockSpec((1, KV_HEADS, cpt, VT_ROWS, KEY_CHUNK), lambda bi, ti: (bi, 0, ti, 0, 0)),
        pl.BlockSpec((TM, ATTN_W), row),
        pl.BlockSpec((1, IDX_DIM, TM), tok),
        pl.BlockSpec((TM, LANES), row),
        pl.BlockSpec((TM // TQ, IDX_HEADS, TQ), lambda bi, ti: (bi * nt + ti, 0, 0)),
        pl.BlockSpec((TM, ATTN_W), row),
        pl.BlockSpec((TM, POOL_W), row),
        pl.BlockSpec((1, 16, POOL_W), lambda bi, ti: (bi, 0, 0)),
        pl.BlockSpec((D_MODEL, N_PROJ), const2),
    )
    in_specs = [
        pl.BlockSpec((1, TM, D_MODEL), lambda bi, ti: (bi, ti, 0)),
        pl.BlockSpec((1, 1, 3 * D_MODEL), lambda bi, ti: (bi, 0, 0)),
        pl.BlockSpec((1, D_MODEL), const2),
        pl.BlockSpec(w_in_t.shape, const2),
        pl.BlockSpec((1, ATTN_W), const2),
        pl.BlockSpec((1, LANES), const2),
        pl.BlockSpec((2, 2 * POOL_GC, 2 * POOL_GC), const3),
        pl.BlockSpec((1, POOL_W), const2),
    ]
    return pl.pallas_call(
        _proj_prompt_kernel,
        out_shape=out_shape,
        grid=(b, nt),
        in_specs=in_specs,
        out_specs=out_specs,
        scratch_shapes=[pltpu.VMEM((16 + TM, POOL_W), F32)],
        compiler_params=pltpu.CompilerParams(
            dimension_semantics=("arbitrary", "arbitrary"), vmem_limit_bytes=VMEM_LIMIT),
        name="proj_prompt",
    )(x, ada_p, norm_w, w_in_t, qnw, knw, wpool_b, pscale)


def _attn_tile(nch, row0, wit, kik_ref, kk_ref, vtx_ref, sc_ref, lg_ref, wq_ref, acc_ref):
    shape2 = (KEY_CHUNK, TQ)
    n_pairs = N_HEADS // 2
    group_of = lambda hp: (2 * hp) // (N_HEADS // KV_HEADS)

    for c in range(nch):
        keys = slice(c * KEY_CHUNK, (c + 1) * KEY_CHUNK)
        kk = kik_ref[keys, :]
        acc = jnp.zeros(shape2, F32)
        for hp in range(n_pairs):
            s2 = _nt_dot(kk, wq_ref[0, hp])
            acc = acc + jnp.maximum(s2[:, 0:TQ], 0.0) * wit[2 * hp:2 * hp + 1, :]
            acc = acc + jnp.maximum(s2[:, TQ:2 * TQ], 0.0) * wit[2 * hp + 1:2 * hp + 2, :]
        if c == nch - 1:
            kpos = c * KEY_CHUNK + lax.broadcasted_iota(I32, shape2, 0)
            qpos = row0 + lax.broadcasted_iota(I32, shape2, 1)
            acc = jnp.where(kpos <= qpos, acc, -jnp.inf)
        sc_ref[c] = acc

    for c in range(nch):
        keys = slice(c * KEY_CHUNK, (c + 1) * KEY_CHUNK)
        for hp in range(n_pairs):
            lg_ref[hp, c] = _nt_dot(kk_ref[group_of(hp), keys, :], wq_ref[1, hp])

    def count_pass(pred):
        parts = [_fold_rows(pred(sc_ref[c]).astype(F32), jnp.add) for c in range(nch)]
        while len(parts) > 1:
            parts = [sum(parts[j:j + 2]) for j in range(0, len(parts), 2)]
        return jnp.sum(parts[0], axis=0, keepdims=True)

    if nch * KEY_CHUNK <= TOPK:
        thr = jnp.full((1, TQ), -jnp.inf, F32)
        need = jnp.zeros((1, TQ), F32)
    else:
        thr, above = _kth_largest(lambda t: count_pass(lambda x: x >= t), (1, TQ), peel=SEARCH_PEEL)
        need = float(TOPK) - above

    tri = jnp.where(lax.broadcasted_iota(I32, (KEY_CHUNK, KEY_CHUNK), 1)
                    <= lax.broadcasted_iota(I32, (KEY_CHUNK, KEY_CHUNK), 0), 1.0, 0.0).astype(BF16)

    m_part = [jnp.full((SUBLANES, TQ), -jnp.inf, F32) for _ in range(N_HEADS)]
    eq_before = jnp.zeros((1, TQ), F32)
    for c in range(nch):
        sc = sc_ref[c]
        eq = sc == thr
        rank = _dot(tri, jnp.where(eq, 1.0, 0.0).astype(BF16)) + eq_before
        sel = (sc > thr) | (eq & (rank <= need))
        if c == nch - 1:
            sel = sel & (sc > -jnp.inf)
        eq_before = rank[KEY_CHUNK - 1:KEY_CHUNK, :]
        for hp in range(n_pairs):
            for par in range(2):
                cols = slice(par * TQ, (par + 1) * TQ)
                lg = jnp.where(sel, lg_ref[hp, c, :, cols], -jnp.inf)
                lg_ref[hp, c, :, cols] = lg
                m_part[2 * hp + par] = jnp.maximum(m_part[2 * hp + par], _fold_rows(lg, jnp.maximum))
    m_rows = [jnp.max(m, axis=0, keepdims=True) for m in m_part]

    for hp in range(n_pairs):
        m2 = jnp.concatenate([m_rows[2 * hp], m_rows[2 * hp + 1]], axis=1)
        acc = jnp.zeros((VT_ROWS, 2 * TQ), F32)
        for c in range(nch):
            p2 = jnp.exp(lg_ref[hp, c] - m2).astype(BF16)
            acc = acc + _dot(vtx_ref[0, group_of(hp), c], p2)
        acc_ref[hp] = acc


def _attn_prompt_kernel(qi_ref, wit_ref, qs_ref, sga_ref, kik_ref, kk_ref, vtx_ref, ag_ref,
                        sc_ref, lg_ref, wq_ref, acc_ref):
    step = pl.program_id(1)
    n_pairs = N_HEADS // 2
    tiles_per_step = KEY_CHUNK // TQ
    nch = step + 1
    lo = _lane_iota((TQ, LANES)) < HEAD_DIM

    def one_tile(sub, carry):
        rows = pl.ds(pl.multiple_of(sub * TQ, TQ), TQ)
        for kind, ref in enumerate((qi_ref, qs_ref)):
            for hp in range(n_pairs):
                pair = ref[rows, hp * LANES:(hp + 1) * LANES].astype(F32)
                wq_ref[kind, hp, 0:TQ, :] = jnp.where(lo, pair, 0.0).astype(BF16)
                wq_ref[kind, hp, TQ:2 * TQ, :] = jnp.where(lo, 0.0, pair).astype(BF16)

        wit = wit_ref[sub]
        row0 = (step * tiles_per_step + sub) * TQ
        for n_static in range(1, sc_ref.shape[0] + 1):
            @pl.when(nch == n_static)
            def _(n_static=n_static):
                _attn_tile(n_static, row0, wit, kik_ref, kk_ref, vtx_ref, sc_ref, lg_ref, wq_ref, acc_ref)

        for hp in range(n_pairs):
            a = acc_ref[hp]
            o0 = a[0:HEAD_DIM, 0:TQ] / a[HEAD_DIM:HEAD_DIM + 1, 0:TQ]
            o1 = a[0:HEAD_DIM, TQ:2 * TQ] / a[HEAD_DIM:HEAD_DIM + 1, TQ:2 * TQ]
            pair = jnp.concatenate([o0, o1], axis=0).T
            cols = slice(hp * LANES, (hp + 1) * LANES)
            ag_ref[rows, cols] = (pair * sga_ref[rows, cols].astype(F32)).astype(BF16)
        return carry

    lax.fori_loop(0, tiles_per_step, one_tile, 0)


def _attn_prompt(qi, wit, qs, sga, kik, kk, vtx, b, s):
    n = b * s
    nkc = s // KEY_CHUNK
    tps = KEY_CHUNK // TQ
    row = lambda bi, st: (bi * nkc + st, 0)
    return pl.pallas_call(
        _attn_prompt_kernel,
        out_shape=jax.ShapeDtypeStruct((n, ATTN_W), BF16),
        grid=(b, nkc),
        in_specs=[
            pl.BlockSpec((KEY_CHUNK, ATTN_W), row),
            pl.BlockSpec((tps, IDX_HEADS, TQ), lambda bi, st: (bi * nkc + st, 0, 0)),
            pl.BlockSpec((KEY_CHUNK, ATTN_W), row),
            pl.BlockSpec((KEY_CHUNK, ATTN_W), row),
            pl.BlockSpec((s, LANES), lambda bi, st: (bi, 0)),
            pl.BlockSpec((KV_HEADS, s, LANES), lambda bi, st: (0, bi, 0)),
            pl.BlockSpec((1, KV_HEADS, nkc, VT_ROWS, KEY_CHUNK), lambda bi, st: (bi, 0, 0, 0, 0)),
        ],
        out_specs=pl.BlockSpec((KEY_CHUNK, ATTN_W), row),
        scratch_shapes=[
            pltpu.VMEM((nkc, KEY_CHUNK, TQ), F32),
            pltpu.VMEM((N_HEADS // 2, nkc, KEY_CHUNK, 2 * TQ), F32),
            pltpu.VMEM((2, N_HEADS // 2, 2 * TQ, LANES), BF16),
            pltpu.VMEM((N_HEADS // 2, VT_ROWS, 2 * TQ), F32),
        ],
        compiler_params=pltpu.CompilerParams(
            dimension_semantics=("arbitrary", "arbitrary"), vmem_limit_bytes=VMEM_LIMIT),
        name="attn_prompt",
    )(qi, wit, qs, sga, kik, kk, vtx)


def _out_proj_kernel(x_ref, a_ref, p_ref, gate_ref, w_ref, o_ref):
    y = _dot(a_ref[...], w_ref[0:ATTN_W, :]) + _dot(p_ref[...], w_ref[ATTN_W:ATTN_W + POOL_W, :])
    o_ref[0] = x_ref[0] + gate_ref[0] * y


def _out_proj_prompt(x, ag, pg, ada_p, w_out_b):
    b, s, _ = x.shape
    nt = s // TM_OUT
    row = lambda bi, ti: (bi * nt + ti, 0)
    return pl.pallas_call(
        _out_proj_kernel,
        out_shape=jax.ShapeDtypeStruct(x.shape, F32),
        grid=(b, nt),
        in_specs=[
            pl.BlockSpec((1, TM_OUT, D_MODEL), lambda bi, ti: (bi, ti, 0)),
            pl.BlockSpec((TM_OUT, ATTN_W), row),
            pl.BlockSpec((TM_OUT, POOL_W), row),
            pl.BlockSpec((1, 1, D_MODEL), lambda bi, ti: (bi, 0, 2)),
            pl.BlockSpec((D_MODEL, D_MODEL), lambda bi, ti: (0, 0)),
        ],
        out_specs=pl.BlockSpec((1, TM_OUT, D_MODEL), lambda bi, ti: (bi, ti, 0)),
        compiler_params=pltpu.CompilerParams(
            dimension_semantics=("arbitrary", "arbitrary"), vmem_limit_bytes=VMEM_LIMIT),
        name="out_proj_prompt",
    )(x, ag, pg, ada_p, w_out_b)


def _proj_sample_kernel(x_ref, ada_ref, nw_ref, w_ref, qnw_ref, knw_ref, wpool_ref, ps_ref, hist_ref,
                        qpad_ref, k_ref, v_ref, ki_ref, qi_ref, wi_ref, snew_ref, lnew_ref,
                        sga_ref, pg_ref, pool_ref):
    nb = x_ref.shape[0]
    x = x_ref[...]
    shift = ada_ref[:, 0:D_MODEL]
    scale = ada_ref[:, D_MODEL:2 * D_MODEL]
    hb = _modulated_norm(x, nw_ref[...], scale, shift).astype(BF16)

    lane = _lane_iota((nb, LANES))
    lo = lane < HEAD_DIM
    seg256 = _seg_ones(2 * LANES, HEAD_DIM)
    seg128 = _seg_ones(LANES, HEAD_DIM)
    head_sel = jnp.where(lax.broadcasted_iota(I32, (ATTN_W, LANES), 0) // HEAD_DIM
                         == lax.broadcasted_iota(I32, (ATTN_W, LANES), 1), 1.0, 0.0).astype(BF16)

    def head_sums(prod):
        hi = prod.astype(BF16)
        rest = (prod - hi.astype(F32)).astype(BF16)
        return _dot(hi, head_sel) + _dot(rest, head_sel)

    q = _head_rms(_dot(hb, w_ref[:, C_Q:C_Q + ATTN_W]), seg256, qnw_ref[...])
    qb = q.astype(BF16)
    k = _head_rms(_dot(hb, w_ref[:, C_K:C_K + LANES]), seg128, knw_ref[...])
    k_ref[...] = k
    v = _dot(hb, w_ref[:, C_V:C_V + LANES])
    v_ref[...] = v

    for hp in range(N_HEADS // 2):
        pair = q[:, hp * LANES:(hp + 1) * LANES]
        pair_sw = pltpu.roll(pair, HEAD_DIM, axis=1)
        g = (2 * hp) // (N_HEADS // KV_HEADS)
        if g == 0:
            h_even, h_odd = jnp.where(lo, pair, 0.0), jnp.where(lo, pair_sw, 0.0)
        else:
            h_even, h_odd = jnp.where(lo, 0.0, pair_sw), jnp.where(lo, 0.0, pair)
        qpad_ref[:, (2 * hp) * LANES:(2 * hp + 1) * LANES] = h_even.astype(BF16)
        qpad_ref[:, (2 * hp + 1) * LANES:(2 * hp + 2) * LANES] = h_odd.astype(BF16)

    kq = k.astype(BF16).astype(F32)
    kq_sw = pltpu.roll(kq, HEAD_DIM, axis=1)
    k0t = jnp.where(lo, kq, kq_sw)
    k1t = jnp.where(lo, kq_sw, kq)
    qf = qb.astype(F32)
    prod = jnp.concatenate([qf[:, 0:LANES] * k0t, qf[:, LANES:2 * LANES] * k0t,
                            qf[:, 2 * LANES:3 * LANES] * k1t, qf[:, 3 * LANES:4 * LANES] * k1t], axis=1)
    lnew_ref[...] = head_sums(prod)

    qi = _dot(hb, w_ref[:, C_QI:C_QI + ATTN_W])
    qib = qi.astype(BF16)
    qi_ref[...] = qib
    kw = _dot(hb, w_ref[:, C_KW:C_KW + LANES])
    kw_sw = pltpu.roll(kw, HEAD_DIM, axis=1)
    ki_ref[...] = kw[:, 0:IDX_DIM]
    wi_full = jnp.where(lane < IDX_HEADS, kw_sw, 0.0) * ((IDX_HEADS * IDX_DIM) ** -0.5)
    wi_ref[...] = wi_full[:, 0:IDX_HEADS]

    kib = kw.astype(BF16).astype(F32)
    kit = jnp.where(lo, kib, pltpu.roll(kib, HEAD_DIM, axis=1))
    qif = qib.astype(F32)
    prod_i = jnp.concatenate([qif[:, j * LANES:(j + 1) * LANES] * kit for j in range(4)], axis=1)
    s_new = jnp.maximum(head_sums(prod_i), 0.0) * wi_full
    snew_ref[...] = jnp.broadcast_to(jnp.sum(s_new, axis=1, keepdims=True), (nb, LANES))

    sga_ref[...] = _silu(_dot(hb, w_ref[:, C_GA:C_GA + ATTN_W])).astype(BF16)

    u = _dot(hb, w_ref[:, C_U:C_U + POOL_W])
    gp = _dot(hb, w_ref[:, C_GP:C_GP + POOL_W])
    for j in range(POOL_HIST - 1):
        pool_ref[j] = hist_ref[j + 1]
    pool_ref[POOL_HIST - 1] = u
    ds = []
    for g, w in enumerate(POOL_WINDOWS):
        cs = slice(g * POOL_GC, (g + 1) * POOL_GC)
        s = u[:, cs]
        for j in range(1, w):
            s = s + hist_ref[POOL_HIST - j, :, cs]
        ds.append(s / float(w) - u[:, cs])
    d = jnp.concatenate(ds, axis=1)
    pg_ref[...] = _pool_mix(d, wpool_ref, ps_ref[...], gp).astype(BF16)


def _proj_sample(x, ada_s, norm_w, w_in_b, qnw, knw, wpool_b, pscale, hist_t):
    nb = x.shape[0]
    out_shape = (
        jax.ShapeDtypeStruct((nb, N_HEADS * LANES), BF16),
        jax.ShapeDtypeStruct((nb, LANES), F32),
        jax.ShapeDtypeStruct((nb, LANES), F32),
        jax.ShapeDtypeStruct((nb, IDX_DIM), F32),
        jax.ShapeDtypeStruct((nb, ATTN_W), BF16),
        jax.ShapeDtypeStruct((nb, IDX_HEADS), F32),
        jax.ShapeDtypeStruct((nb, LANES), F32),
        jax.ShapeDtypeStruct((nb, LANES), F32),
        jax.ShapeDtypeStruct((nb, ATTN_W), BF16),
        jax.ShapeDtypeStruct((nb, POOL_W), BF16),
        jax.ShapeDtypeStruct((POOL_HIST, nb, POOL_W), F32),
    )
    return pl.pallas_call(
        _proj_sample_kernel,
        out_shape=out_shape,
        compiler_params=pltpu.CompilerParams(vmem_limit_bytes=VMEM_LIMIT),
        name="proj_sample",
    )(x, ada_s, norm_w, w_in_b, qnw, knw, wpool_b, pscale, hist_t)


def _page_copies(pt_ref, pages_hbm, buf_ref, sem_ref, step, slot, group, n_pages):
    copies = []
    for j in range(group):
        for p in range(n_pages):
            page = pt_ref[step * group + j, p]
            copies.append(pltpu.make_async_copy(pages_hbm.at[page], buf_ref.at[slot, j, p], sem_ref.at[slot]))
    return copies


def _row_pages(buf_ref, slot, j):
    return jnp.concatenate([buf_ref[slot, j, p].astype(BF16) for p in range(buf_ref.shape[2])], axis=1)


def _score_sample_kernel(pt_ref, qi_ref, wi_ref, kidx_hbm, o_ref, buf_ref, sem_ref, *, group, n_pages):
    step = pl.program_id(0)
    slot = lax.rem(step, 2)

    def copies(st, sl):
        return _page_copies(pt_ref, kidx_hbm, buf_ref, sem_ref, st, sl, group, n_pages)

    @pl.when(step == 0)
    def _():
        for cp in copies(step, slot):
            cp.start()

    @pl.when(step + 1 < pl.num_programs(0))
    def _():
        for cp in copies(step + 1, 1 - slot):
            cp.start()

    for cp in copies(step, slot):
        cp.wait()

    for j in range(group):
        ki_t = _row_pages(buf_ref, slot, j)
        s = _dot(qi_ref[j], ki_t)
        o_ref[j:j + 1, :] = jnp.sum(jnp.maximum(s, 0.0) * wi_ref[j], axis=0, keepdims=True)


def _score_sample(page_table, qi3, wi3, kidx_t):
    nb, n_pages = page_table.shape
    n_keys = n_pages * PAGE
    g = SCORE_GROUP
    return pl.pallas_call(
        functools.partial(_score_sample_kernel, group=g, n_pages=n_pages),
        out_shape=jax.ShapeDtypeStruct((nb, n_keys), F32),
        grid_spec=pltpu.PrefetchScalarGridSpec(
            num_scalar_prefetch=1,
            grid=(nb // g,),
            in_specs=[
                pl.BlockSpec((g, IDX_HEADS, IDX_DIM), lambda s, pt: (s, 0, 0)),
                pl.BlockSpec((g, IDX_HEADS, 1), lambda s, pt: (s, 0, 0)),
                pl.BlockSpec(memory_space=pl.ANY),
            ],
            out_specs=pl.BlockSpec((g, n_keys), lambda s, pt: (s, 0)),
            scratch_shapes=[
                pltpu.VMEM((2, g, n_pages, IDX_DIM, PAGE), F32),
                pltpu.SemaphoreType.DMA((2,)),
            ],
        ),
        compiler_params=pltpu.CompilerParams(
            dimension_semantics=("arbitrary",), vmem_limit_bytes=VMEM_LIMIT),
        name="score_sample",
    )(page_table, qi3, wi3, kidx_t)


def _select_sample_kernel(sc_ref, snew_ref, mask_ref, mnew_ref):
    nb, n_keys = sc_ref.shape
    nch = n_keys // KEY_CHUNK
    snew = snew_ref[...]
    ones_mat = jnp.ones((LANES, LANES), BF16)

    def count_pass(pred):
        acc = jnp.zeros((nb, LANES), F32)
        for c in range(n_keys // LANES):
            acc = acc + pred(sc_ref[:, c * LANES:(c + 1) * LANES]).astype(F32)
        return _dot(acc.astype(BF16), ones_mat) + pred(snew).astype(F32)

    thr, above = _kth_largest(lambda t: count_pass(lambda x: x >= t), (nb, LANES))
    need = float(TOPK) - above
    thr2 = jnp.concatenate([thr, thr], axis=1)
    need2 = jnp.concatenate([need, need], axis=1)
    tri = jnp.where(lax.broadcasted_iota(I32, (KEY_CHUNK, KEY_CHUNK), 0)
                    <= lax.broadcasted_iota(I32, (KEY_CHUNK, KEY_CHUNK), 1), 1.0, 0.0).astype(BF16)
    ones_cl = jnp.ones((KEY_CHUNK, LANES), BF16)

    eq_before = jnp.zeros((nb, LANES), F32)
    for c in range(nch):
        sc = sc_ref[:, c * KEY_CHUNK:(c + 1) * KEY_CHUNK]
        eq = sc == thr2
        eqb = jnp.where(eq, 1.0, 0.0).astype(BF16)
        rank = _dot(eqb, tri) + jnp.concatenate([eq_before, eq_before], axis=1)
        sel = (sc > thr2) | (eq & (rank <= need2))
        mask_ref[:, c * KEY_CHUNK:(c + 1) * KEY_CHUNK] = jnp.where(sel, 0.0, -jnp.inf)
        eq_before = eq_before + _dot(eqb, ones_cl)
    sel_new = (snew > thr) | ((snew == thr) & (eq_before + 1.0 <= need))
    mnew_ref[...] = jnp.where(sel_new, 0.0, -jnp.inf)


def _select_sample(scores, snew):
    nb, n_keys = scores.shape
    return pl.pallas_call(
        _select_sample_kernel,
        out_shape=(jax.ShapeDtypeStruct((nb, n_keys), F32), jax.ShapeDtypeStruct((nb, LANES), F32)),
        name="select_sample",
    )(scores, snew)


def _attn_sample_kernel(pt_ref, q_ref, mask_ref, lnew_ref, mnew_ref, vnew_ref, k_hbm, v_hbm,
                        o_ref, kbuf_ref, vbuf_ref, ksem_ref, vsem_ref, *, group, n_pages):
    step = pl.program_id(0)
    slot = lax.rem(step, 2)

    def copies(st, sl):
        return (_page_copies(pt_ref, k_hbm, kbuf_ref, ksem_ref, st, sl, group, n_pages)
                + _page_copies(pt_ref, v_hbm, vbuf_ref, vsem_ref, st, sl, group, n_pages))

    @pl.when(step == 0)
    def _():
        for cp in copies(step, slot):
            cp.start()

    @pl.when(step + 1 < pl.num_programs(0))
    def _():
        for cp in copies(step + 1, 1 - slot):
            cp.start()

    for cp in copies(step, slot):
        cp.wait()

    for j in range(group):
        k_t = _row_pages(kbuf_ref, slot, j)
        lg = _dot(q_ref[j], k_t) + mask_ref[j:j + 1, :]
        lg_n = lnew_ref[j] + mnew_ref[j]
        m = jnp.maximum(jnp.max(lg, axis=1, keepdims=True), lg_n)
        p = jnp.exp(lg - m)
        p_n = jnp.exp(lg_n - m)
        l = jnp.sum(p, axis=1, keepdims=True) + p_n
        acc = _nt_dot(p.astype(BF16), _row_pages(vbuf_ref, slot, j)) + p_n * vnew_ref[j]
        o_ref[j] = acc / l


def _attn_sample(page_table, qpad3, mask3, lnew3, mnew3, vnew3, k_t, v_t):
    nb, n_pages = page_table.shape
    n_keys = n_pages * PAGE
    g = ATTN_GROUP
    per_s = lambda s, pt: (s, 0, 0)
    return pl.pallas_call(
        functools.partial(_attn_sample_kernel, group=g, n_pages=n_pages),
        out_shape=jax.ShapeDtypeStruct((nb, N_HEADS, LANES), F32),
        grid_spec=pltpu.PrefetchScalarGridSpec(
            num_scalar_prefetch=1,
            grid=(nb // g,),
            in_specs=[
                pl.BlockSpec((g, N_HEADS, LANES), per_s),
                pl.BlockSpec((g, n_keys), lambda s, pt: (s, 0)),
                pl.BlockSpec((g, N_HEADS, 1), per_s),
                pl.BlockSpec((g, N_HEADS, 1), per_s),
                pl.BlockSpec((g, 1, LANES), per_s),
                pl.BlockSpec(memory_space=pl.ANY),
                pl.BlockSpec(memory_space=pl.ANY),
            ],
            out_specs=pl.BlockSpec((g, N_HEADS, LANES), per_s),
            scratch_shapes=[
                pltpu.VMEM((2, g, n_pages, LANES, PAGE), F32),
                pltpu.VMEM((2, g, n_pages, LANES, PAGE), F32),
                pltpu.SemaphoreType.DMA((2,)),
                pltpu.SemaphoreType.DMA((2,)),
            ],
        ),
        compiler_params=pltpu.CompilerParams(
            dimension_semantics=("arbitrary",), vmem_limit_bytes=VMEM_LIMIT),
        name="attn_sample",
    )(page_table, qpad3, mask3, lnew3, mnew3, vnew3, k_t, v_t)


def _out_proj_sample_kernel(x_ref, a_ref, sga_ref, p_ref, ada_ref, w_ref, o_ref):
    lo = _lane_iota((x_ref.shape[0], LANES)) < HEAD_DIM
    pairs = []
    for hp in range(N_HEADS // 2):
        even = a_ref[:, (2 * hp) * LANES:(2 * hp + 1) * LANES]
        odd = a_ref[:, (2 * hp + 1) * LANES:(2 * hp + 2) * LANES]
        if (2 * hp) // (N_HEADS // KV_HEADS) == 0:
            pairs.append(jnp.where(lo, even, pltpu.roll(odd, HEAD_DIM, axis=1)))
        else:
            pairs.append(jnp.where(lo, pltpu.roll(even, HEAD_DIM, axis=1), odd))
    a = jnp.concatenate(pairs, axis=1)
    ag = (a * sga_ref[...].astype(F32)).astype(BF16)
    y = _dot(ag, w_ref[0:ATTN_W, :]) + _dot(p_ref[...], w_ref[ATTN_W:ATTN_W + POOL_W, :])
    o_ref[...] = x_ref[...] + ada_ref[:, 2 * D_MODEL:3 * D_MODEL] * y


def _out_proj_sample(x, a, sga, pg, ada_s, w_out_b):
    return pl.pallas_call(
        _out_proj_sample_kernel,
        out_shape=jax.ShapeDtypeStruct(x.shape, F32),
        compiler_params=pltpu.CompilerParams(vmem_limit_bytes=VMEM_LIMIT),
        name="out_proj_sample",
    )(x, a, sga, pg, ada_s, w_out_b)


def kernel(x_prompt, x_sample, cache_k, cache_v, cache_kidx, state_pool, page_table, c_prompt, c_sample,
           norm_w, w_ada, b_ada, w_in, q_norm_w, k_norm_w, w_pool, pool_scale, w_out):
    bp, s, _ = x_prompt.shape
    bs = x_sample.shape[0]
    assert w_in.shape[0] == 1 and x_sample.shape[1] == 1, "single layer, single decode token"
    n_phys = cache_k.shape[1]

    w_in_t = jnp.transpose(w_in[0])
    w_out_b = w_out[0].astype(BF16)
    zero_blk = jnp.zeros((POOL_GC, POOL_GC), w_pool.dtype)
    wpool_b = jnp.stack([jnp.block([[w_pool[0, 2 * p], zero_blk], [zero_blk, w_pool[0, 2 * p + 1]]])
                         for p in range(len(POOL_WINDOWS) // 2)]).astype(BF16)
    qnw = jnp.tile(q_norm_w[0], N_HEADS)[None, :] * (HEAD_DIM ** -0.5)
    knw = jnp.tile(k_norm_w[0], KV_HEADS)[None, :]
    nw = norm_w[0][None, :]
    pscale = pool_scale[0][None, :]

    ada_p, ada_s = _ada(c_prompt, c_sample, w_ada, b_ada)
    ada_p = ada_p.reshape(bp, 1, 3 * D_MODEL)

    (qs, k_t_p, kk, v_t_p, vtx, qi, ki_t_p, kik, wit, sga, pg, ulast, w_in_b) = _proj_prompt(
        x_prompt, ada_p, nw, w_in_t, qnw, knw, wpool_b, pscale)
    ag = _attn_prompt(qi, wit, qs, sga, kik, kk, vtx, bp, s)
    y_prompt = _out_proj_prompt(x_prompt, ag, pg, ada_p, w_out_b)

    hist_t = jnp.transpose(state_pool[0], (1, 0, 2))
    (qpad, k_s, v_s, ki_s, qi_s, wi_s, snew, lnew, sga_s, pg_s, pool_s) = _proj_sample(
        x_sample[:, 0, :], ada_s, nw, w_in_b, qnw, knw, wpool_b, pscale, hist_t)
    kidx_t = jnp.transpose(cache_kidx[0], (0, 2, 1))
    k_t = jnp.transpose(cache_k[0], (0, 2, 3, 1)).reshape(n_phys, LANES, PAGE)
    v_t = jnp.transpose(cache_v[0], (0, 2, 3, 1)).reshape(n_phys, LANES, PAGE)
    scores = _score_sample(page_table, qi_s.reshape(bs, IDX_HEADS, IDX_DIM), wi_s.reshape(bs, IDX_HEADS, 1),
                           kidx_t)
    mask, mnew = _select_sample(scores, snew)
    o_s = _attn_sample(
        page_table, qpad.reshape(bs, N_HEADS, LANES), mask,
        lnew[:, :N_HEADS].reshape(bs, N_HEADS, 1),
        jnp.broadcast_to(mnew[:, :1], (bs, N_HEADS)).reshape(bs, N_HEADS, 1),
        v_s.reshape(bs, 1, LANES), k_t, v_t)
    y_sample = _out_proj_sample(x_sample[:, 0, :], o_s.reshape(bs, N_HEADS * LANES), sga_s, pg_s, ada_s, w_out_b)

    to_heads = lambda a: jnp.transpose(a.reshape(bp, KV_HEADS, HEAD_DIM, s), (0, 3, 1, 2))[None]
    return (
        y_prompt,
        y_sample[:, None, :],
        to_heads(k_t_p),
        to_heads(v_t_p),
        jnp.transpose(ki_t_p, (0, 2, 1))[None],
        ulast[:, 1:, :][None],
        k_s.reshape(1, bs, 1, KV_HEADS, HEAD_DIM),
        v_s.reshape(1, bs, 1, KV_HEADS, HEAD_DIM),
        ki_s.reshape(1, bs, 1, IDX_DIM),
        jnp.transpose(pool_s, (1, 0, 2))[None],
    )
```

```python
import functools

import jax
import jax.numpy as jnp
import numpy as np
from jax import lax
from jax.experimental import pallas as pl
from jax.experimental.pallas import tpu as pltpu

F32 = jnp.float32
BF16 = jnp.bfloat16
I32 = jnp.int32

D_MODEL = 1024
ATTN_W = 512
POOL_W = 512
HEAD_DIM = 64
N_HEADS = 8
KV_HEADS = 2
IDX_HEADS = 8
IDX_DIM = 64
TOPK = 256
POOL_WINDOWS = (2, 4, 8, 16)
POOL_GC = 128
POOL_HIST = 15
EPS = 1e-6
PAGE = 128

LANES = 128
SUBLANES = 8
BF16_ROWS = 16
KEY_CHUNK = 256
TQ = 128
TM = 512
TM_OUT = 1024
VT_ROWS = HEAD_DIM + BF16_ROWS
SEARCH_PEEL = 16
SCORE_GROUP = 8
ATTN_GROUP = 8
INT_MIN = np.int32(-2 ** 31)
VMEM_LIMIT = 56 * 1024 * 1024

C_Q, C_K, C_V, C_QI, C_KW, C_GA, C_U, C_GP, N_PROJ = 0, 512, 640, 768, 1280, 1408, 1920, 2432, 2944


def _nt_dot(a, b):
    return lax.dot_general(a, b, (((1,), (1,)), ((), ())), preferred_element_type=F32)


def _dot(a, b):
    return jnp.dot(a, b, preferred_element_type=F32)


def _silu(z):
    return z / (1.0 + jnp.exp(-z))


def _lane_iota(shape):
    return lax.broadcasted_iota(I32, shape, len(shape) - 1)


def _seg_ones(n, seg):
    r = lax.broadcasted_iota(I32, (n, n), 0) // seg
    c = lax.broadcasted_iota(I32, (n, n), 1) // seg
    return jnp.where(r == c, 1.0, 0.0).astype(BF16)


def _head_rms(z, seg_mat, w):
    n = seg_mat.shape[0]
    sq = (z * z).astype(BF16)
    ss = jnp.concatenate([_dot(sq[:, j:j + n], seg_mat) for j in range(0, z.shape[1], n)], axis=1)
    return z * lax.rsqrt(ss * (1.0 / HEAD_DIM) + EPS) * w


def _float_of_rank(u):
    key = u ^ INT_MIN
    bits = key ^ ((key >> 31) & np.int32(0x7FFFFFFF))
    return lax.bitcast_convert_type(bits, F32)


def _kth_largest(count_ge, shape, peel=0):
    def bit_body(it, carry):
        ans, t = carry
        bit = jnp.left_shift(jnp.int32(1), 31 - it)
        nxt = lax.shift_right_logical(bit, jnp.int32(1))
        cand = ans | bit
        t_if_kept, t_if_dropped = _float_of_rank(cand | nxt), _float_of_rank(ans | nxt)
        keep = count_ge(t) >= float(TOPK)
        return jnp.where(keep, cand, ans), jnp.where(keep, t_if_kept, t_if_dropped)

    carry = (jnp.zeros(shape, I32), jnp.zeros(shape, F32))
    for it in range(peel):
        carry = bit_body(jnp.int32(it), carry)
    ans, _ = lax.fori_loop(peel, 32, bit_body, carry)
    return jnp.where(ans == 0, -jnp.inf, _float_of_rank(ans))


def _fold_rows(x, op):
    parts = [x[r:r + SUBLANES] for r in range(0, x.shape[0], SUBLANES)]
    while len(parts) > 1:
        parts = [op(parts[i], parts[i + 1]) for i in range(0, len(parts), 2)]
    return parts[0]


def _ada_kernel(cp_ref, cs_ref, w_ref, b_ref, op_ref, os_ref):
    w = w_ref[0].astype(BF16)
    op_ref[...] = _dot(_silu(cp_ref[...]).astype(BF16), w) + b_ref[...]
    os_ref[...] = _dot(_silu(cs_ref[...]).astype(BF16), w) + b_ref[...]


def _ada(c_prompt, c_sample, w_ada, b_ada):
    bp, bs = c_prompt.shape[0], c_sample.shape[0]
    return pl.pallas_call(
        _ada_kernel,
        out_shape=(jax.ShapeDtypeStruct((bp, 3 * D_MODEL), F32), jax.ShapeDtypeStruct((bs, 3 * D_MODEL), F32)),
        grid=(3,),
        in_specs=[
            pl.BlockSpec((bp, D_MODEL), lambda j: (0, 0)),
            pl.BlockSpec((bs, D_MODEL), lambda j: (0, 0)),
            pl.BlockSpec((1, D_MODEL, D_MODEL), lambda j: (0, 0, j)),
            pl.BlockSpec((1, D_MODEL), lambda j: (0, j)),
        ],
        out_specs=(pl.BlockSpec((bp, D_MODEL), lambda j: (0, j)), pl.BlockSpec((bs, D_MODEL), lambda j: (0, j))),
        compiler_params=pltpu.CompilerParams(dimension_semantics=("arbitrary",)),
        name="ada_ln",
    )(c_prompt, c_sample, w_ada, b_ada)


def _modulated_norm(x, norm_w, scale, shift):
    ms = jnp.mean(x * x, axis=-1, keepdims=True)
    return (x * lax.rsqrt(ms + EPS)) * norm_w * (1.0 + scale) + shift


def _pool_mix(d, wpool_ref, pscale, gp):
    db = d.astype(BF16)
    wide = 2 * POOL_GC
    y = jnp.concatenate([_dot(db[:, p * wide:(p + 1) * wide], wpool_ref[p])
                         for p in range(len(POOL_WINDOWS) // 2)], axis=1)
    return y * pscale * _silu(gp)


def _permuted_weight_block(wt_ref, g):
    cut = C_KW + IDX_DIM + IDX_HEADS
    c0 = g * LANES
    if c0 + LANES <= cut:
        blk = wt_ref[c0:c0 + LANES, :]
    elif c0 < cut:
        blk = jnp.concatenate([wt_ref[c0:cut, :], jnp.zeros((c0 + LANES - cut, D_MODEL), F32)], axis=0)
    else:
        r0 = c0 - (C_GA - cut)
        blk = wt_ref[r0:r0 + LANES, :]
    return blk.T.astype(BF16)


def _proj_prompt_kernel(x_ref, ada_ref, nw_ref, wt_ref, qnw_ref, knw_ref, wpool_ref, ps_ref,
                        qs_ref, kt_ref, kk_ref, vt_ref, vtx_ref, qi_ref, kit_ref, kik_ref,
                        wit_ref, sga_ref, pg_ref, ulast_ref, w_ref,
                        ext_ref):
    t = pl.program_id(1)

    @pl.when((pl.program_id(0) == 0) & (t == 0))
    def _():
        for g in range(N_PROJ // LANES):
            w_ref[:, g * LANES:(g + 1) * LANES] = _permuted_weight_block(wt_ref, g)

    ada = ada_ref[0]
    shift = ada[:, 0:D_MODEL]
    scale = ada[:, D_MODEL:2 * D_MODEL]
    sub_rows = KEY_CHUNK
    lo = _lane_iota((sub_rows, LANES)) < HEAD_DIM
    seg256 = _seg_ones(2 * LANES, HEAD_DIM)
    seg128 = _seg_ones(LANES, HEAD_DIM)
    ones = jnp.ones((BF16_ROWS, KEY_CHUNK), BF16)

    @pl.when(t == 0)
    def _():
        ext_ref[0:16, :] = jnp.zeros((16, POOL_W), F32)

    for sub in range(TM // sub_rows):
        rows = slice(sub * sub_rows, (sub + 1) * sub_rows)
        hb = _modulated_norm(x_ref[0, rows, :], nw_ref[...], scale, shift).astype(BF16)

        q = _dot(hb, w_ref[:, C_Q:C_Q + ATTN_W])
        qs_ref[rows, :] = _head_rms(q, seg256, qnw_ref[...]).astype(BF16)

        kv = _dot(hb, w_ref[:, C_K:C_K + 2 * LANES])
        k = _head_rms(kv[:, 0:LANES], seg128, knw_ref[...])
        kt_ref[0, :, rows] = k.T
        k_sw = pltpu.roll(k, HEAD_DIM, axis=1)
        kk_ref[0, rows, :] = jnp.where(lo, k, k_sw).astype(BF16)
        kk_ref[1, rows, :] = jnp.where(lo, k_sw, k).astype(BF16)

        v_t = kv[:, LANES:2 * LANES].T
        vt_ref[0, :, rows] = v_t
        for g in range(KV_HEADS):
            vtx_ref[0, g, sub, 0:HEAD_DIM, :] = v_t[g * HEAD_DIM:(g + 1) * HEAD_DIM].astype(BF16)
            vtx_ref[0, g, sub, HEAD_DIM:VT_ROWS, :] = ones

        qi_ref[rows, :] = _dot(hb, w_ref[:, C_QI:C_QI + ATTN_W]).astype(BF16)
        kw = _dot(hb, w_ref[:, C_KW:C_KW + LANES])
        kw_t = kw.T
        kit_ref[0, :, rows] = kw_t[0:IDX_DIM]
        wi_t = kw_t[IDX_DIM:IDX_DIM + IDX_HEADS] * ((IDX_HEADS * IDX_DIM) ** -0.5)
        for j in range(sub_rows // TQ):
            wit_ref[sub * (sub_rows // TQ) + j] = wi_t[:, j * TQ:(j + 1) * TQ]
        kik_ref[rows, :] = jnp.where(lo, kw, pltpu.roll(kw, HEAD_DIM, axis=1)).astype(BF16)

        sga_ref[rows, :] = _silu(_dot(hb, w_ref[:, C_GA:C_GA + ATTN_W])).astype(BF16)

        u = _dot(hb, w_ref[:, C_U:C_U + POOL_W])
        gp = _dot(hb, w_ref[:, C_GP:C_GP + POOL_W])
        base = 16 + sub * sub_rows
        ext_ref[base:base + sub_rows, :] = u
        pos = t * TM + sub * sub_rows + lax.broadcasted_iota(I32, (sub_rows, POOL_GC), 0)
        ds = []
        for g, w in enumerate(POOL_WINDOWS):
            cs = slice(g * POOL_GC, (g + 1) * POOL_GC)
            s = ext_ref[base - 16:base + sub_rows, cs]
            k = 1
            while k < w:
                s = s + pltpu.roll(s, k, axis=0)
                k *= 2
            cnt = jnp.minimum(pos + 1, w).astype(F32)
            ds.append(s[16:16 + sub_rows] / cnt - u[:, cs])
        d = jnp.concatenate(ds, axis=1)
        pg_ref[rows, :] = _pool_mix(d, wpool_ref, ps_ref[...], gp).astype(BF16)

    tail = ext_ref[TM:TM + 16, :]
    ulast_ref[0] = tail
    ext_ref[0:16, :] = tail


def _proj_prompt(x, ada_p, norm_w, w_in_t, qnw, knw, wpool_b, pscale):
    b, s, _ = x.shape
    n = b * s
    nt = s // TM
    cpt = TM // KEY_CHUNK
    row = lambda bi, ti: (bi * nt + ti, 0)
    tok = lambda bi, ti: (bi, 0, ti)
    const2 = lambda bi, ti: (0, 0)
    const3 = lambda bi, ti: (0, 0, 0)
    out_shape = (
        jax.ShapeDtypeStruct((n, ATTN_W), BF16),
        jax.ShapeDtypeStruct((b, LANES, s), F32),
        jax.ShapeDtypeStruct((KV_HEADS, n, LANES), BF16),
        jax.ShapeDtypeStruct((b, LANES, s), F32),
        jax.ShapeDtypeStruct((b, KV_HEADS, s // KEY_CHUNK, VT_ROWS, KEY_CHUNK), BF16),
        jax.ShapeDtypeStruct((n, ATTN_W), BF16),
        jax.ShapeDtypeStruct((b, IDX_DIM, s), F32),
        jax.ShapeDtypeStruct((n, LANES), BF16),
        jax.ShapeDtypeStruct((n // TQ, IDX_HEADS, TQ), F32),
        jax.ShapeDtypeStruct((n, ATTN_W), BF16),
        jax.ShapeDtypeStruct((n, POOL_W), BF16),
        jax.ShapeDtypeStruct((b, 16, POOL_W), F32),
        jax.ShapeDtypeStruct((D_MODEL, N_PROJ), BF16),
    )
    out_specs = (
        pl.BlockSpec((TM, ATTN_W), row),
        pl.BlockSpec((1, LANES, TM), tok),
        pl.BlockSpec((KV_HEADS, TM, LANES), lambda bi, ti: (0, bi * nt + ti, 0)),
        pl.BlockSpec((1, LANES, TM), tok),
        pl.BlockSpec((1, KV_HEADS, cpt, VT_ROWS, KEY_CHUNK), lambda bi, ti: (bi, 0, ti, 0, 0)),
        pl.BlockSpec((TM, ATTN_W), row),
        pl.BlockSpec((1, IDX_DIM, TM), tok),
        pl.BlockSpec((TM, LANES), row),
        pl.BlockSpec((TM // TQ, IDX_HEADS, TQ), lambda bi, ti: (bi * nt + ti, 0, 0)),
        pl.BlockSpec((TM, ATTN_W), row),
        pl.BlockSpec((TM, POOL_W), row),
        pl.BlockSpec((1, 16, POOL_W), lambda bi, ti: (bi, 0, 0)),
        pl.BlockSpec((D_MODEL, N_PROJ), const2),
    )
    in_specs = [
        pl.BlockSpec((1, TM, D_MODEL), lambda bi, ti: (bi, ti, 0)),
        pl.BlockSpec((1, 1, 3 * D_MODEL), lambda bi, ti: (bi, 0, 0)),
        pl.BlockSpec((1, D_MODEL), const2),
        pl.BlockSpec(w_in_t.shape, const2),
        pl.BlockSpec((1, ATTN_W), const2),
        pl.BlockSpec((1, LANES), const2),
        pl.BlockSpec((2, 2 * POOL_GC, 2 * POOL_GC), const3),
        pl.BlockSpec((1, POOL_W), const2),
    ]
    return pl.pallas_call(
        _proj_prompt_kernel,
        out_shape=out_shape,
        grid=(b, nt),
        in_specs=in_specs,
        out_specs=out_specs,
        scratch_shapes=[pltpu.VMEM((16 + TM, POOL_W), F32)],
        compiler_params=pltpu.CompilerParams(
            dimension_semantics=("arbitrary", "arbitrary"), vmem_limit_bytes=VMEM_LIMIT),
        name="proj_prompt",
    )(x, ada_p, norm_w, w_in_t, qnw, knw, wpool_b, pscale)


def _attn_tile(nch, row0, wit, kik_ref, kk_ref, vtx_ref, sc_ref, lg_ref, wq_ref, acc_ref):
    shape2 = (KEY_CHUNK, TQ)
    n_pairs = N_HEADS // 2
    group_of = lambda hp: (2 * hp) // (N_HEADS // KV_HEADS)

    for c in range(nch):
        keys = slice(c * KEY_CHUNK, (c + 1) * KEY_CHUNK)
        kk = kik_ref[keys, :]
        acc = jnp.zeros(shape2, F32)
        for hp in range(n_pairs):
            s2 = _nt_dot(kk, wq_ref[0, hp])
            acc = acc + jnp.maximum(s2[:, 0:TQ], 0.0) * wit[2 * hp:2 * hp + 1, :]
            acc = acc + jnp.maximum(s2[:, TQ:2 * TQ], 0.0) * wit[2 * hp + 1:2 * hp + 2, :]
        if c == nch - 1:
            kpos = c * KEY_CHUNK + lax.broadcasted_iota(I32, shape2, 0)
            qpos = row0 + lax.broadcasted_iota(I32, shape2, 1)
            acc = jnp.where(kpos <= qpos, acc, -jnp.inf)
        sc_ref[c] = acc

    for c in range(nch):
        keys = slice(c * KEY_CHUNK, (c + 1) * KEY_CHUNK)
        for hp in range(n_pairs):
            lg_ref[hp, c] = _nt_dot(kk_ref[group_of(hp), keys, :], wq_ref[1, hp])

    def count_pass(pred):
        parts = [_fold_rows(pred(sc_ref[c]).astype(F32), jnp.add) for c in range(nch)]
        while len(parts) > 1:
            parts = [sum(parts[j:j + 2]) for j in range(0, len(parts), 2)]
        return jnp.sum(parts[0], axis=0, keepdims=True)

    if nch * KEY_CHUNK <= TOPK:
        thr = jnp.full((1, TQ), -jnp.inf, F32)
        need = jnp.zeros((1, TQ), F32)
    else:
        thr = _kth_largest(lambda t: count_pass(lambda x: x >= t), (1, TQ), peel=SEARCH_PEEL)
        need = float(TOPK) - count_pass(lambda x: x > thr)

    tri = jnp.where(lax.broadcasted_iota(I32, (KEY_CHUNK, KEY_CHUNK), 1)
                    <= lax.broadcasted_iota(I32, (KEY_CHUNK, KEY_CHUNK), 0), 1.0, 0.0).astype(BF16)

    m_part = [jnp.full((SUBLANES, TQ), -jnp.inf, F32) for _ in range(N_HEADS)]
    eq_before = jnp.zeros((1, TQ), F32)
    for c in range(nch):
        sc = sc_ref[c]
        eq = sc == thr
        rank = _dot(tri, jnp.where(eq, 1.0, 0.0).astype(BF16)) + eq_before
        sel = (sc > thr) | (eq & (rank <= need))
        if c == nch - 1:
            sel = sel & (sc > -jnp.inf)
        eq_before = rank[KEY_CHUNK - 1:KEY_CHUNK, :]
        for hp in range(n_pairs):
            for par in range(2):
                cols = slice(par * TQ, (par + 1) * TQ)
                lg = jnp.where(sel, lg_ref[hp, c, :, cols], -jnp.inf)
                lg_ref[hp, c, :, cols] = lg
                m_part[2 * hp + par] = jnp.maximum(m_part[2 * hp + par], _fold_rows(lg, jnp.maximum))
    m_rows = [jnp.max(m, axis=0, keepdims=True) for m in m_part]

    for hp in range(n_pairs):
        m2 = jnp.concatenate([m_rows[2 * hp], m_rows[2 * hp + 1]], axis=1)
        acc = jnp.zeros((VT_ROWS, 2 * TQ), F32)
        for c in range(nch):
            p2 = jnp.exp(lg_ref[hp, c] - m2).astype(BF16)
            acc = acc + _dot(vtx_ref[0, group_of(hp), c], p2)
        acc_ref[hp] = acc


def _attn_prompt_kernel(qi_ref, wit_ref, qs_ref, sga_ref, kik_ref, kk_ref, vtx_ref, ag_ref,
                        sc_ref, lg_ref, wq_ref, acc_ref):
    step = pl.program_id(1)
    n_pairs = N_HEADS // 2
    tiles_per_step = KEY_CHUNK // TQ
    nch = step + 1
    lo = _lane_iota((TQ, LANES)) < HEAD_DIM

    def one_tile(sub, carry):
        rows = pl.ds(pl.multiple_of(sub * TQ, TQ), TQ)
        for kind, ref in enumerate((qi_ref, qs_ref)):
            for hp in range(n_pairs):
                pair = ref[rows, hp * LANES:(hp + 1) * LANES].astype(F32)
                wq_ref[kind, hp, 0:TQ, :] = jnp.where(lo, pair, 0.0).astype(BF16)
                wq_ref[kind, hp, TQ:2 * TQ, :] = jnp.where(lo, 0.0, pair).astype(BF16)

        wit = wit_ref[sub]
        row0 = (step * tiles_per_step + sub) * TQ
        for n_static in range(1, sc_ref.shape[0] + 1):
            @pl.when(nch == n_static)
            def _(n_static=n_static):
                _attn_tile(n_static, row0, wit, kik_ref, kk_ref, vtx_ref, sc_ref, lg_ref, wq_ref, acc_ref)

        for hp in range(n_pairs):
            a = acc_ref[hp]
            o0 = a[0:HEAD_DIM, 0:TQ] / a[HEAD_DIM:HEAD_DIM + 1, 0:TQ]
            o1 = a[0:HEAD_DIM, TQ:2 * TQ] / a[HEAD_DIM:HEAD_DIM + 1, TQ:2 * TQ]
            pair = jnp.concatenate([o0, o1], axis=0).T
            cols = slice(hp * LANES, (hp + 1) * LANES)
            ag_ref[rows, cols] = (pair * sga_ref[rows, cols].astype(F32)).astype(BF16)
        return carry

    lax.fori_loop(0, tiles_per_step, one_tile, 0)


def _attn_prompt(qi, wit, qs, sga, kik, kk, vtx, b, s):
    n = b * s
    nkc = s // KEY_CHUNK
    tps = KEY_CHUNK // TQ
    row = lambda bi, st: (bi * nkc + st, 0)
    return pl.pallas_call(
        _attn_prompt_kernel,
        out_shape=jax.ShapeDtypeStruct((n, ATTN_W), BF16),
        grid=(b, nkc),
        in_specs=[
            pl.BlockSpec((KEY_CHUNK, ATTN_W), row),
            pl.BlockSpec((tps, IDX_HEADS, TQ), lambda bi, st: (bi * nkc + st, 0, 0)),
            pl.BlockSpec((KEY_CHUNK, ATTN_W), row),
            pl.BlockSpec((KEY_CHUNK, ATTN_W), row),
            pl.BlockSpec((s, LANES), lambda bi, st: (bi, 0)),
            pl.BlockSpec((KV_HEADS, s, LANES), lambda bi, st: (0, bi, 0)),
            pl.BlockSpec((1, KV_HEADS, nkc, VT_ROWS, KEY_CHUNK), lambda bi, st: (bi, 0, 0, 0, 0)),
        ],
        out_specs=pl.BlockSpec((KEY_CHUNK, ATTN_W), row),
        scratch_shapes=[
            pltpu.VMEM((nkc, KEY_CHUNK, TQ), F32),
            pltpu.VMEM((N_HEADS // 2, nkc, KEY_CHUNK, 2 * TQ), F32),
            pltpu.VMEM((2, N_HEADS // 2, 2 * TQ, LANES), BF16),
            pltpu.VMEM((N_HEADS // 2, VT_ROWS, 2 * TQ), F32),
        ],
        compiler_params=pltpu.CompilerParams(
            dimension_semantics=("arbitrary", "arbitrary"), vmem_limit_bytes=VMEM_LIMIT),
        name="attn_prompt",
    )(qi, wit, qs, sga, kik, kk, vtx)


def _out_proj_kernel(x_ref, a_ref, p_ref, gate_ref, w_ref, o_ref):
    y = _dot(a_ref[...], w_ref[0:ATTN_W, :]) + _dot(p_ref[...], w_ref[ATTN_W:ATTN_W + POOL_W, :])
    o_ref[0] = x_ref[0] + gate_ref[0] * y


def _out_proj_prompt(x, ag, pg, ada_p, w_out_b):
    b, s, _ = x.shape
    nt = s // TM_OUT
    row = lambda bi, ti: (bi * nt + ti, 0)
    return pl.pallas_call(
        _out_proj_kernel,
        out_shape=jax.ShapeDtypeStruct(x.shape, F32),
        grid=(b, nt),
        in_specs=[
            pl.BlockSpec((1, TM_OUT, D_MODEL), lambda bi, ti: (bi, ti, 0)),
            pl.BlockSpec((TM_OUT, ATTN_W), row),
            pl.BlockSpec((TM_OUT, POOL_W), row),
            pl.BlockSpec((1, 1, D_MODEL), lambda bi, ti: (bi, 0, 2)),
            pl.BlockSpec((D_MODEL, D_MODEL), lambda bi, ti: (0, 0)),
        ],
        out_specs=pl.BlockSpec((1, TM_OUT, D_MODEL), lambda bi, ti: (bi, ti, 0)),
        compiler_params=pltpu.CompilerParams(
            dimension_semantics=("arbitrary", "arbitrary"), vmem_limit_bytes=VMEM_LIMIT),
        name="out_proj_prompt",
    )(x, ag, pg, ada_p, w_out_b)


def _proj_sample_kernel(x_ref, ada_ref, nw_ref, w_ref, qnw_ref, knw_ref, wpool_ref, ps_ref, hist_ref,
                        qpad_ref, k_ref, v_ref, ki_ref, qi_ref, wi_ref, snew_ref, lnew_ref,
                        sga_ref, pg_ref, pool_ref):
    nb = x_ref.shape[0]
    x = x_ref[...]
    shift = ada_ref[:, 0:D_MODEL]
    scale = ada_ref[:, D_MODEL:2 * D_MODEL]
    hb = _modulated_norm(x, nw_ref[...], scale, shift).astype(BF16)

    lane = _lane_iota((nb, LANES))
    lo = lane < HEAD_DIM
    seg256 = _seg_ones(2 * LANES, HEAD_DIM)
    seg128 = _seg_ones(LANES, HEAD_DIM)
    head_sel = jnp.where(lax.broadcasted_iota(I32, (ATTN_W, LANES), 0) // HEAD_DIM
                         == lax.broadcasted_iota(I32, (ATTN_W, LANES), 1), 1.0, 0.0).astype(BF16)

    def head_sums(prod):
        hi = prod.astype(BF16)
        rest = (prod - hi.astype(F32)).astype(BF16)
        return _dot(hi, head_sel) + _dot(rest, head_sel)

    q = _head_rms(_dot(hb, w_ref[:, C_Q:C_Q + ATTN_W]), seg256, qnw_ref[...])
    qb = q.astype(BF16)
    k = _head_rms(_dot(hb, w_ref[:, C_K:C_K + LANES]), seg128, knw_ref[...])
    k_ref[...] = k
    v = _dot(hb, w_ref[:, C_V:C_V + LANES])
    v_ref[...] = v

    for hp in range(N_HEADS // 2):
        pair = q[:, hp * LANES:(hp + 1) * LANES]
        pair_sw = pltpu.roll(pair, HEAD_DIM, axis=1)
        g = (2 * hp) // (N_HEADS // KV_HEADS)
        if g == 0:
            h_even, h_odd = jnp.where(lo, pair, 0.0), jnp.where(lo, pair_sw, 0.0)
        else:
            h_even, h_odd = jnp.where(lo, 0.0, pair_sw), jnp.where(lo, 0.0, pair)
        qpad_ref[:, (2 * hp) * LANES:(2 * hp + 1) * LANES] = h_even.astype(BF16)
        qpad_ref[:, (2 * hp + 1) * LANES:(2 * hp + 2) * LANES] = h_odd.astype(BF16)

    kq = k.astype(BF16).astype(F32)
    kq_sw = pltpu.roll(kq, HEAD_DIM, axis=1)
    k0t = jnp.where(lo, kq, kq_sw)
    k1t = jnp.where(lo, kq_sw, kq)
    qf = qb.astype(F32)
    prod = jnp.concatenate([qf[:, 0:LANES] * k0t, qf[:, LANES:2 * LANES] * k0t,
                            qf[:, 2 * LANES:3 * LANES] * k1t, qf[:, 3 * LANES:4 * LANES] * k1t], axis=1)
    lnew_ref[...] = head_sums(prod)

    qi = _dot(hb, w_ref[:, C_QI:C_QI + ATTN_W])
    qib = qi.astype(BF16)
    qi_ref[...] = qib
    kw = _dot(hb, w_ref[:, C_KW:C_KW + LANES])
    kw_sw = pltpu.roll(kw, HEAD_DIM, axis=1)
    ki_ref[...] = kw[:, 0:IDX_DIM]
    wi_full = jnp.where(lane < IDX_HEADS, kw_sw, 0.0) * ((IDX_HEADS * IDX_DIM) ** -0.5)
    wi_ref[...] = wi_full[:, 0:IDX_HEADS]

    kib = kw.astype(BF16).astype(F32)
    kit = jnp.where(lo, kib, pltpu.roll(kib, HEAD_DIM, axis=1))
    qif = qib.astype(F32)
    prod_i = jnp.concatenate([qif[:, j * LANES:(j + 1) * LANES] * kit for j in range(4)], axis=1)
    s_new = jnp.maximum(head_sums(prod_i), 0.0) * wi_full
    snew_ref[...] = jnp.broadcast_to(jnp.sum(s_new, axis=1, keepdims=True), (nb, LANES))

    sga_ref[...] = _silu(_dot(hb, w_ref[:, C_GA:C_GA + ATTN_W])).astype(BF16)

    u = _dot(hb, w_ref[:, C_U:C_U + POOL_W])
    gp = _dot(hb, w_ref[:, C_GP:C_GP + POOL_W])
    for j in range(POOL_HIST - 1):
        pool_ref[j] = hist_ref[j + 1]
    pool_ref[POOL_HIST - 1] = u
    ds = []
    for g, w in enumerate(POOL_WINDOWS):
        cs = slice(g * POOL_GC, (g + 1) * POOL_GC)
        s = u[:, cs]
        for j in range(1, w):
            s = s + hist_ref[POOL_HIST - j, :, cs]
        ds.append(s / float(w) - u[:, cs])
    d = jnp.concatenate(ds, axis=1)
    pg_ref[...] = _pool_mix(d, wpool_ref, ps_ref[...], gp).astype(BF16)


def _proj_sample(x, ada_s, norm_w, w_in_b, qnw, knw, wpool_b, pscale, hist_t):
    nb = x.shape[0]
    out_shape = (
        jax.ShapeDtypeStruct((nb, N_HEADS * LANES), BF16),
        jax.ShapeDtypeStruct((nb, LANES), F32),
        jax.ShapeDtypeStruct((nb, LANES), F32),
        jax.ShapeDtypeStruct((nb, IDX_DIM), F32),
        jax.ShapeDtypeStruct((nb, ATTN_W), BF16),
        jax.ShapeDtypeStruct((nb, IDX_HEADS), F32),
        jax.ShapeDtypeStruct((nb, LANES), F32),
        jax.ShapeDtypeStruct((nb, LANES), F32),
        jax.ShapeDtypeStruct((nb, ATTN_W), BF16),
        jax.ShapeDtypeStruct((nb, POOL_W), BF16),
        jax.ShapeDtypeStruct((POOL_HIST, nb, POOL_W), F32),
    )
    return pl.pallas_call(
        _proj_sample_kernel,
        out_shape=out_shape,
        compiler_params=pltpu.CompilerParams(vmem_limit_bytes=VMEM_LIMIT),
        name="proj_sample",
    )(x, ada_s, norm_w, w_in_b, qnw, knw, wpool_b, pscale, hist_t)


def _page_copies(pt_ref, pages_hbm, buf_ref, sem_ref, step, slot, group, n_pages):
    copies = []
    for j in range(group):
        for p in range(n_pages):
            page = pt_ref[step * group + j, p]
            copies.append(pltpu.make_async_copy(pages_hbm.at[page], buf_ref.at[slot, j, p], sem_ref.at[slot]))
    return copies


def _row_pages(buf_ref, slot, j):
    return jnp.concatenate([buf_ref[slot, j, p].astype(BF16) for p in range(buf_ref.shape[2])], axis=1)


def _score_sample_kernel(pt_ref, qi_ref, wi_ref, kidx_hbm, o_ref, buf_ref, sem_ref, *, group, n_pages):
    step = pl.program_id(0)
    slot = lax.rem(step, 2)

    def copies(st, sl):
        return _page_copies(pt_ref, kidx_hbm, buf_ref, sem_ref, st, sl, group, n_pages)

    @pl.when(step == 0)
    def _():
        for cp in copies(step, slot):
            cp.start()

    @pl.when(step + 1 < pl.num_programs(0))
    def _():
        for cp in copies(step + 1, 1 - slot):
            cp.start()

    for cp in copies(step, slot):
        cp.wait()

    for j in range(group):
        ki_t = _row_pages(buf_ref, slot, j)
        s = _dot(qi_ref[j], ki_t)
        o_ref[j:j + 1, :] = jnp.sum(jnp.maximum(s, 0.0) * wi_ref[j], axis=0, keepdims=True)


def _score_sample(page_table, qi3, wi3, kidx_t):
    nb, n_pages = page_table.shape
    n_keys = n_pages * PAGE
    g = SCORE_GROUP
    return pl.pallas_call(
        functools.partial(_score_sample_kernel, group=g, n_pages=n_pages),
        out_shape=jax.ShapeDtypeStruct((nb, n_keys), F32),
        grid_spec=pltpu.PrefetchScalarGridSpec(
            num_scalar_prefetch=1,
            grid=(nb // g,),
            in_specs=[
                pl.BlockSpec((g, IDX_HEADS, IDX_DIM), lambda s, pt: (s, 0, 0)),
                pl.BlockSpec((g, IDX_HEADS, 1), lambda s, pt: (s, 0, 0)),
                pl.BlockSpec(memory_space=pl.ANY),
            ],
            out_specs=pl.BlockSpec((g, n_keys), lambda s, pt: (s, 0)),
            scratch_shapes=[
                pltpu.VMEM((2, g, n_pages, IDX_DIM, PAGE), F32),
                pltpu.SemaphoreType.DMA((2,)),
            ],
        ),
        compiler_params=pltpu.CompilerParams(
            dimension_semantics=("arbitrary",), vmem_limit_bytes=VMEM_LIMIT),
        name="score_sample",
    )(page_table, qi3, wi3, kidx_t)


def _select_sample_kernel(sc_ref, snew_ref, mask_ref, mnew_ref):
    nb, n_keys = sc_ref.shape
    nch = n_keys // KEY_CHUNK
    snew = snew_ref[...]
    ones_mat = jnp.ones((LANES, LANES), BF16)

    def count_pass(pred):
        acc = jnp.zeros((nb, LANES), F32)
        for c in range(n_keys // LANES):
            acc = acc + pred(sc_ref[:, c * LANES:(c + 1) * LANES]).astype(F32)
        return _dot(acc.astype(BF16), ones_mat) + pred(snew).astype(F32)

    thr = _kth_largest(lambda t: count_pass(lambda x: x >= t), (nb, LANES))
    need = float(TOPK) - count_pass(lambda x: x > thr)
    thr2 = jnp.concatenate([thr, thr], axis=1)
    need2 = jnp.concatenate([need, need], axis=1)
    tri = jnp.where(lax.broadcasted_iota(I32, (KEY_CHUNK, KEY_CHUNK), 0)
                    <= lax.broadcasted_iota(I32, (KEY_CHUNK, KEY_CHUNK), 1), 1.0, 0.0).astype(BF16)
    ones_cl = jnp.ones((KEY_CHUNK, LANES), BF16)

    eq_before = jnp.zeros((nb, LANES), F32)
    for c in range(nch):
        sc = sc_ref[:, c * KEY_CHUNK:(c + 1) * KEY_CHUNK]
        eq = sc == thr2
        eqb = jnp.where(eq, 1.0, 0.0).astype(BF16)
        rank = _dot(eqb, tri) + jnp.concatenate([eq_before, eq_before], axis=1)
        sel = (sc > thr2) | (eq & (rank <= need2))
        mask_ref[:, c * KEY_CHUNK:(c + 1) * KEY_CHUNK] = jnp.where(sel, 0.0, -jnp.inf)
        eq_before = eq_before + _dot(eqb, ones_cl)
    sel_new = (snew > thr) | ((snew == thr) & (eq_before + 1.0 <= need))
    mnew_ref[...] = jnp.where(sel_new, 0.0, -jnp.inf)


def _select_sample(scores, snew):
    nb, n_keys = scores.shape
    return pl.pallas_call(
        _select_sample_kernel,
        out_shape=(jax.ShapeDtypeStruct((nb, n_keys), F32), jax.ShapeDtypeStruct((nb, LANES), F32)),
        name="select_sample",
    )(scores, snew)


def _attn_sample_kernel(pt_ref, q_ref, mask_ref, lnew_ref, mnew_ref, vnew_ref, k_hbm, v_hbm,
                        o_ref, kbuf_ref, vbuf_ref, ksem_ref, vsem_ref, *, group, n_pages):
    step = pl.program_id(0)
    slot = lax.rem(step, 2)

    def copies(st, sl):
        return (_page_copies(pt_ref, k_hbm, kbuf_ref, ksem_ref, st, sl, group, n_pages)
                + _page_copies(pt_ref, v_hbm, vbuf_ref, vsem_ref, st, sl, group, n_pages))

    @pl.when(step == 0)
    def _():
        for cp in copies(step, slot):
            cp.start()

    @pl.when(step + 1 < pl.num_programs(0))
    def _():
        for cp in copies(step + 1, 1 - slot):
            cp.start()

    for cp in copies(step, slot):
        cp.wait()

    for j in range(group):
        k_t = _row_pages(kbuf_ref, slot, j)
        lg = _dot(q_ref[j], k_t) + mask_ref[j:j + 1, :]
        lg_n = lnew_ref[j] + mnew_ref[j]
        m = jnp.maximum(jnp.max(lg, axis=1, keepdims=True), lg_n)
        p = jnp.exp(lg - m)
        p_n = jnp.exp(lg_n - m)
        l = jnp.sum(p, axis=1, keepdims=True) + p_n
        acc = _nt_dot(p.astype(BF16), _row_pages(vbuf_ref, slot, j)) + p_n * vnew_ref[j]
        o_ref[j] = acc / l


def _attn_sample(page_table, qpad3, mask3, lnew3, mnew3, vnew3, k_t, v_t):
    nb, n_pages = page_table.shape
    n_keys = n_pages * PAGE
    g = ATTN_GROUP
    per_s = lambda s, pt: (s, 0, 0)
    return pl.pallas_call(
        functools.partial(_attn_sample_kernel, group=g, n_pages=n_pages),
        out_shape=jax.ShapeDtypeStruct((nb, N_HEADS, LANES), F32),
        grid_spec=pltpu.PrefetchScalarGridSpec(
            num_scalar_prefetch=1,
            grid=(nb // g,),
            in_specs=[
                pl.BlockSpec((g, N_HEADS, LANES), per_s),
                pl.BlockSpec((g, n_keys), lambda s, pt: (s, 0)),
                pl.BlockSpec((g, N_HEADS, 1), per_s),
                pl.BlockSpec((g, N_HEADS, 1), per_s),
                pl.BlockSpec((g, 1, LANES), per_s),
                pl.BlockSpec(memory_space=pl.ANY),
                pl.BlockSpec(memory_space=pl.ANY),
            ],
            out_specs=pl.BlockSpec((g, N_HEADS, LANES), per_s),
            scratch_shapes=[
                pltpu.VMEM((2, g, n_pages, LANES, PAGE), F32),
                pltpu.VMEM((2, g, n_pages, LANES, PAGE), F32),
                pltpu.SemaphoreType.DMA((2,)),
                pltpu.SemaphoreType.DMA((2,)),
            ],
        ),
        compiler_params=pltpu.CompilerParams(
            dimension_semantics=("arbitrary",), vmem_limit_bytes=VMEM_LIMIT),
        name="attn_sample",
    )(page_table, qpad3, mask3, lnew3, mnew3, vnew3, k_t, v_t)


def _out_proj_sample_kernel(x_ref, a_ref, sga_ref, p_ref, ada_ref, w_ref, o_ref):
    lo = _lane_iota((x_ref.shape[0], LANES)) < HEAD_DIM
    pairs = []
    for hp in range(N_HEADS // 2):
        even = a_ref[:, (2 * hp) * LANES:(2 * hp + 1) * LANES]
        odd = a_ref[:, (2 * hp + 1) * LANES:(2 * hp + 2) * LANES]
        if (2 * hp) // (N_HEADS // KV_HEADS) == 0:
            pairs.append(jnp.where(lo, even, pltpu.roll(odd, HEAD_DIM, axis=1)))
        else:
            pairs.append(jnp.where(lo, pltpu.roll(even, HEAD_DIM, axis=1), odd))
    a = jnp.concatenate(pairs, axis=1)
    ag = (a * sga_ref[...].astype(F32)).astype(BF16)
    y = _dot(ag, w_ref[0:ATTN_W, :]) + _dot(p_ref[...], w_ref[ATTN_W:ATTN_W + POOL_W, :])
    o_ref[...] = x_ref[...] + ada_ref[:, 2 * D_MODEL:3 * D_MODEL] * y


def _out_proj_sample(x, a, sga, pg, ada_s, w_out_b):
    return pl.pallas_call(
        _out_proj_sample_kernel,
        out_shape=jax.ShapeDtypeStruct(x.shape, F32),
        compiler_params=pltpu.CompilerParams(vmem_limit_bytes=VMEM_LIMIT),
        name="out_proj_sample",
    )(x, a, sga, pg, ada_s, w_out_b)


def kernel(x_prompt, x_sample, cache_k, cache_v, cache_kidx, state_pool, page_table, c_prompt, c_sample,
           norm_w, w_ada, b_ada, w_in, q_norm_w, k_norm_w, w_pool, pool_scale, w_out):
    bp, s, _ = x_prompt.shape
    bs = x_sample.shape[0]
    assert w_in.shape[0] == 1 and x_sample.shape[1] == 1, "single layer, single decode token"
    n_phys = cache_k.shape[1]

    w_in_t = jnp.transpose(w_in[0])
    w_out_b = w_out[0].astype(BF16)
    zero_blk = jnp.zeros((POOL_GC, POOL_GC), w_pool.dtype)
    wpool_b = jnp.stack([jnp.block([[w_pool[0, 2 * p], zero_blk], [zero_blk, w_pool[0, 2 * p + 1]]])
                         for p in range(len(POOL_WINDOWS) // 2)]).astype(BF16)
    qnw = jnp.tile(q_norm_w[0], N_HEADS)[None, :] * (HEAD_DIM ** -0.5)
    knw = jnp.tile(k_norm_w[0], KV_HEADS)[None, :]
    nw = norm_w[0][None, :]
    pscale = pool_scale[0][None, :]

    ada_p, ada_s = _ada(c_prompt, c_sample, w_ada, b_ada)
    ada_p = ada_p.reshape(bp, 1, 3 * D_MODEL)

    (qs, k_t_p, kk, v_t_p, vtx, qi, ki_t_p, kik, wit, sga, pg, ulast, w_in_b) = _proj_prompt(
        x_prompt, ada_p, nw, w_in_t, qnw, knw, wpool_b, pscale)
    ag = _attn_prompt(qi, wit, qs, sga, kik, kk, vtx, bp, s)
    y_prompt = _out_proj_prompt(x_prompt, ag, pg, ada_p, w_out_b)

    hist_t = jnp.transpose(state_pool[0], (1, 0, 2))
    (qpad, k_s, v_s, ki_s, qi_s, wi_s, snew, lnew, sga_s, pg_s, pool_s) = _proj_sample(
        x_sample[:, 0, :], ada_s, nw, w_in_b, qnw, knw, wpool_b, pscale, hist_t)
    kidx_t = jnp.transpose(cache_kidx[0], (0, 2, 1))
    k_t = jnp.transpose(cache_k[0], (0, 2, 3, 1)).reshape(n_phys, LANES, PAGE)
    v_t = jnp.transpose(cache_v[0], (0, 2, 3, 1)).reshape(n_phys, LANES, PAGE)
    scores = _score_sample(page_table, qi_s.reshape(bs, IDX_HEADS, IDX_DIM), wi_s.reshape(bs, IDX_HEADS, 1),
                           kidx_t)
    mask, mnew = _select_sample(scores, snew)
    o_s = _attn_sample(
        page_table, qpad.reshape(bs, N_HEADS, LANES), mask,
        lnew[:, :N_HEADS].reshape(bs, N_HEADS, 1),
        jnp.broadcast_to(mnew[:, :1], (bs, N_HEADS)).reshape(bs, N_HEADS, 1),
        v_s.reshape(bs, 1, LANES), k_t, v_t)
    y_sample = _out_proj_sample(x_sample[:, 0, :], o_s.reshape(bs, N_HEADS * LANES), sga_s, pg_s, ada_s, w_out_b)

    to_heads = lambda a: jnp.transpose(a.reshape(bp, KV_HEADS, HEAD_DIM, s), (0, 3, 1, 2))[None]
    return (
        y_prompt,
        y_sample[:, None, :],
        to_heads(k_t_p),
        to_heads(v_t_p),
        jnp.transpose(ki_t_p, (0, 2, 1))[None],
        ulast[:, 1:, :][None],
        k_s.reshape(1, bs, 1, KV_HEADS, HEAD_DIM),
        v_s.reshape(1, bs, 1, KV_HEADS, HEAD_DIM),
        ki_s.reshape(1, bs, 1, IDX_DIM),
        jnp.transpose(pool_s, (1, 0, 2))[None],
    )
```

```python
import functools

import jax
import jax.numpy as jnp
import numpy as np
from jax import lax
from jax.experimental import pallas as pl
from jax.experimental.pallas import tpu as pltpu

F32 = jnp.float32
BF16 = jnp.bfloat16
I32 = jnp.int32

D_MODEL = 1024
ATTN_W = 512
POOL_W = 512
HEAD_DIM = 64
N_HEADS = 8
KV_HEADS = 2
IDX_HEADS = 8
IDX_DIM = 64
TOPK = 256
POOL_WINDOWS = (2, 4, 8, 16)
POOL_GC = 128
POOL_HIST = 15
EPS = 1e-6
PAGE = 128

LANES = 128
SUBLANES = 8
BF16_ROWS = 16
KEY_CHUNK = 256
TQ = 128
TM = 512
TM_OUT = 1024
VT_ROWS = HEAD_DIM + BF16_ROWS
SEARCH_PEEL = 24
SCORE_GROUP = 8
ATTN_GROUP = 8
INT_MIN = np.int32(-2 ** 31)
VMEM_LIMIT = 56 * 1024 * 1024

C_Q, C_K, C_V, C_QI, C_KW, C_GA, C_U, C_GP, N_PROJ = 0, 512, 640, 768, 1280, 1408, 1920, 2432, 2944


def _nt_dot(a, b):
    return lax.dot_general(a, b, (((1,), (1,)), ((), ())), preferred_element_type=F32)


def _dot(a, b):
    return jnp.dot(a, b, preferred_element_type=F32)


def _silu(z):
    return z / (1.0 + jnp.exp(-z))


def _lane_iota(shape):
    return lax.broadcasted_iota(I32, shape, len(shape) - 1)


def _seg_ones(n, seg):
    r = lax.broadcasted_iota(I32, (n, n), 0) // seg
    c = lax.broadcasted_iota(I32, (n, n), 1) // seg
    return jnp.where(r == c, 1.0, 0.0).astype(BF16)


def _head_rms(z, seg_mat, w):
    n = seg_mat.shape[0]
    sq = (z * z).astype(BF16)
    ss = jnp.concatenate([_dot(sq[:, j:j + n], seg_mat) for j in range(0, z.shape[1], n)], axis=1)
    return z * lax.rsqrt(ss * (1.0 / HEAD_DIM) + EPS) * w


def _float_of_rank(u):
    key = u ^ INT_MIN
    bits = key ^ ((key >> 31) & np.int32(0x7FFFFFFF))
    return lax.bitcast_convert_type(bits, F32)


def _kth_largest(count_ge, shape, peel=0):
    def bit_body(it, carry):
        ans, t = carry
        bit = jnp.left_shift(jnp.int32(1), 31 - it)
        nxt = lax.shift_right_logical(bit, jnp.int32(1))
        cand = ans | bit
        t_if_kept, t_if_dropped = _float_of_rank(cand | nxt), _float_of_rank(ans | nxt)
        keep = count_ge(t) >= float(TOPK)
        return jnp.where(keep, cand, ans), jnp.where(keep, t_if_kept, t_if_dropped)

    carry = (jnp.zeros(shape, I32), jnp.zeros(shape, F32))
    for it in range(peel):
        carry = bit_body(jnp.int32(it), carry)
    ans, _ = lax.fori_loop(peel, 32, bit_body, carry)
    return jnp.where(ans == 0, -jnp.inf, _float_of_rank(ans))


def _fold_rows(x, op):
    parts = [x[r:r + SUBLANES] for r in range(0, x.shape[0], SUBLANES)]
    while len(parts) > 1:
        parts = [op(parts[i], parts[i + 1]) for i in range(0, len(parts), 2)]
    return parts[0]


def _ada_kernel(cp_ref, cs_ref, w_ref, b_ref, op_ref, os_ref):
    w = w_ref[0].astype(BF16)
    op_ref[...] = _dot(_silu(cp_ref[...]).astype(BF16), w) + b_ref[...]
    os_ref[...] = _dot(_silu(cs_ref[...]).astype(BF16), w) + b_ref[...]


def _ada(c_prompt, c_sample, w_ada, b_ada):
    bp, bs = c_prompt.shape[0], c_sample.shape[0]
    return pl.pallas_call(
        _ada_kernel,
        out_shape=(jax.ShapeDtypeStruct((bp, 3 * D_MODEL), F32), jax.ShapeDtypeStruct((bs, 3 * D_MODEL), F32)),
        grid=(3,),
        in_specs=[
            pl.BlockSpec((bp, D_MODEL), lambda j: (0, 0)),
            pl.BlockSpec((bs, D_MODEL), lambda j: (0, 0)),
            pl.BlockSpec((1, D_MODEL, D_MODEL), lambda j: (0, 0, j)),
            pl.BlockSpec((1, D_MODEL), lambda j: (0, j)),
        ],
        out_specs=(pl.BlockSpec((bp, D_MODEL), lambda j: (0, j)), pl.BlockSpec((bs, D_MODEL), lambda j: (0, j))),
        compiler_params=pltpu.CompilerParams(dimension_semantics=("arbitrary",)),
        name="ada_ln",
    )(c_prompt, c_sample, w_ada, b_ada)


def _modulated_norm(x, norm_w, scale, shift):
    ms = jnp.mean(x * x, axis=-1, keepdims=True)
    return (x * lax.rsqrt(ms + EPS)) * norm_w * (1.0 + scale) + shift


def _pool_mix(d, wpool_ref, pscale, gp):
    db = d.astype(BF16)
    wide = 2 * POOL_GC
    y = jnp.concatenate([_dot(db[:, p * wide:(p + 1) * wide], wpool_ref[p])
                         for p in range(len(POOL_WINDOWS) // 2)], axis=1)
    return y * pscale * _silu(gp)


def _permuted_weight_block(wt_ref, g):
    cut = C_KW + IDX_DIM + IDX_HEADS
    c0 = g * LANES
    if c0 + LANES <= cut:
        blk = wt_ref[c0:c0 + LANES, :]
    elif c0 < cut:
        blk = jnp.concatenate([wt_ref[c0:cut, :], jnp.zeros((c0 + LANES - cut, D_MODEL), F32)], axis=0)
    else:
        r0 = c0 - (C_GA - cut)
        blk = wt_ref[r0:r0 + LANES, :]
    return blk.T.astype(BF16)


def _proj_prompt_kernel(x_ref, ada_ref, nw_ref, wt_ref, qnw_ref, knw_ref, wpool_ref, ps_ref,
                        qs_ref, kt_ref, kk_ref, vt_ref, vtx_ref, qi_ref, kit_ref, kik_ref,
                        wit_ref, sga_ref, pg_ref, ulast_ref, w_ref,
                        ext_ref):
    t = pl.program_id(1)

    @pl.when((pl.program_id(0) == 0) & (t == 0))
    def _():
        for g in range(N_PROJ // LANES):
            w_ref[:, g * LANES:(g + 1) * LANES] = _permuted_weight_block(wt_ref, g)

    ada = ada_ref[0]
    shift = ada[:, 0:D_MODEL]
    scale = ada[:, D_MODEL:2 * D_MODEL]
    sub_rows = KEY_CHUNK
    lo = _lane_iota((sub_rows, LANES)) < HEAD_DIM
    seg256 = _seg_ones(2 * LANES, HEAD_DIM)
    seg128 = _seg_ones(LANES, HEAD_DIM)
    ones = jnp.ones((BF16_ROWS, KEY_CHUNK), BF16)

    @pl.when(t == 0)
    def _():
        ext_ref[0:16, :] = jnp.zeros((16, POOL_W), F32)

    for sub in range(TM // sub_rows):
        rows = slice(sub * sub_rows, (sub + 1) * sub_rows)
        hb = _modulated_norm(x_ref[0, rows, :], nw_ref[...], scale, shift).astype(BF16)

        q = _dot(hb, w_ref[:, C_Q:C_Q + ATTN_W])
        qs_ref[rows, :] = _head_rms(q, seg256, qnw_ref[...]).astype(BF16)

        kv = _dot(hb, w_ref[:, C_K:C_K + 2 * LANES])
        k = _head_rms(kv[:, 0:LANES], seg128, knw_ref[...])
        kt_ref[0, :, rows] = k.T
        k_sw = pltpu.roll(k, HEAD_DIM, axis=1)
        kk_ref[0, rows, :] = jnp.where(lo, k, k_sw).astype(BF16)
        kk_ref[1, rows, :] = jnp.where(lo, k_sw, k).astype(BF16)

        v_t = kv[:, LANES:2 * LANES].T
        vt_ref[0, :, rows] = v_t
        for g in range(KV_HEADS):
            vtx_ref[0, g, sub, 0:HEAD_DIM, :] = v_t[g * HEAD_DIM:(g + 1) * HEAD_DIM].astype(BF16)
            vtx_ref[0, g, sub, HEAD_DIM:VT_ROWS, :] = ones

        qi_ref[rows, :] = _dot(hb, w_ref[:, C_QI:C_QI + ATTN_W]).astype(BF16)
        kw = _dot(hb, w_ref[:, C_KW:C_KW + LANES])
        kw_t = kw.T
        kit_ref[0, :, rows] = kw_t[0:IDX_DIM]
        wi_t = kw_t[IDX_DIM:IDX_DIM + IDX_HEADS] * ((IDX_HEADS * IDX_DIM) ** -0.5)
        for j in range(sub_rows // TQ):
            wit_ref[sub * (sub_rows // TQ) + j] = wi_t[:, j * TQ:(j + 1) * TQ]
        kik_ref[rows, :] = jnp.where(lo, kw, pltpu.roll(kw, HEAD_DIM, axis=1)).astype(BF16)

        sga_ref[rows, :] = _silu(_dot(hb, w_ref[:, C_GA:C_GA + ATTN_W])).astype(BF16)

        u = _dot(hb, w_ref[:, C_U:C_U + POOL_W])
        gp = _dot(hb, w_ref[:, C_GP:C_GP + POOL_W])
        base = 16 + sub * sub_rows
        ext_ref[base:base + sub_rows, :] = u
        pos = t * TM + sub * sub_rows + lax.broadcasted_iota(I32, (sub_rows, POOL_GC), 0)
        ds = []
        for g, w in enumerate(POOL_WINDOWS):
            cs = slice(g * POOL_GC, (g + 1) * POOL_GC)
            s = ext_ref[base - 16:base + sub_rows, cs]
            k = 1
            while k < w:
                s = s + pltpu.roll(s, k, axis=0)
                k *= 2
            cnt = jnp.minimum(pos + 1, w).astype(F32)
            ds.append(s[16:16 + sub_rows] / cnt - u[:, cs])
        d = jnp.concatenate(ds, axis=1)
        pg_ref[rows, :] = _pool_mix(d, wpool_ref, ps_ref[...], gp).astype(BF16)

    tail = ext_ref[TM:TM + 16, :]
    ulast_ref[0] = tail
    ext_ref[0:16, :] = tail


def _proj_prompt(x, ada_p, norm_w, w_in_t, qnw, knw, wpool_b, pscale):
    b, s, _ = x.shape
    n = b * s
    nt = s // TM
    cpt = TM // KEY_CHUNK
    row = lambda bi, ti: (bi * nt + ti, 0)
    tok = lambda bi, ti: (bi, 0, ti)
    const2 = lambda bi, ti: (0, 0)
    const3 = lambda bi, ti: (0, 0, 0)
    out_shape = (
        jax.ShapeDtypeStruct((n, ATTN_W), BF16),
        jax.ShapeDtypeStruct((b, LANES, s), F32),
        jax.ShapeDtypeStruct((KV_HEADS, n, LANES), BF16),
        jax.ShapeDtypeStruct((b, LANES, s), F32),
        jax.ShapeDtypeStruct((b, KV_HEADS, s // KEY_CHUNK, VT_ROWS, KEY_CHUNK), BF16),
        jax.ShapeDtypeStruct((n, ATTN_W), BF16),
        jax.ShapeDtypeStruct((b, IDX_DIM, s), F32),
        jax.ShapeDtypeStruct((n, LANES), BF16),
        jax.ShapeDtypeStruct((n // TQ, IDX_HEADS, TQ), F32),
        jax.ShapeDtypeStruct((n, ATTN_W), BF16),
        jax.ShapeDtypeStruct((n, POOL_W), BF16),
        jax.ShapeDtypeStruct((b, 16, POOL_W), F32),
        jax.ShapeDtypeStruct((D_MODEL, N_PROJ), BF16),
    )
    out_specs = (
        pl.BlockSpec((TM, ATTN_W), row),
        pl.BlockSpec((1, LANES, TM), tok),
        pl.BlockSpec((KV_HEADS, TM, LANES), lambda bi, ti: (0, bi * nt + ti, 0)),
        pl.BlockSpec((1, LANES, TM), tok),
        pl.BlockSpec((1, KV_HEADS, cpt, VT_ROWS, KEY_CHUNK), lambda bi, ti: (bi, 0, ti, 0, 0)),
        pl.BlockSpec((TM, ATTN_W), row),
        pl.BlockSpec((1, IDX_DIM, TM), tok),
        pl.BlockSpec((TM, LANES), row),
        pl.BlockSpec((TM // TQ, IDX_HEADS, TQ), lambda bi, ti: (bi * nt + ti, 0, 0)),
        pl.BlockSpec((TM, ATTN_W), row),
        pl.BlockSpec((TM, POOL_W), row),
        pl.BlockSpec((1, 16, POOL_W), lambda bi, ti: (bi, 0, 0)),
        pl.BlockSpec((D_MODEL, N_PROJ), const2),
    )
    in_specs = [
        pl.BlockSpec((1, TM, D_MODEL), lambda bi, ti: (bi, ti, 0)),
        pl.BlockSpec((1, 1, 3 * D_MODEL), lambda bi, ti: (bi, 0, 0)),
        pl.BlockSpec((1, D_MODEL), const2),
        pl.BlockSpec(w_in_t.shape, const2),
        pl.BlockSpec((1, ATTN_W), const2),
        pl.BlockSpec((1, LANES), const2),
        pl.BlockSpec((2, 2 * POOL_GC, 2 * POOL_GC), const3),
        pl.BlockSpec((1, POOL_W), const2),
    ]
    return pl.pallas_call(
        _proj_prompt_kernel,
        out_shape=out_shape,
        grid=(b, nt),
        in_specs=in_specs,
        out_specs=out_specs,
        scratch_shapes=[pltpu.VMEM((16 + TM, POOL_W), F32)],
        compiler_params=pltpu.CompilerParams(
            dimension_semantics=("arbitrary", "arbitrary"), vmem_limit_bytes=VMEM_LIMIT),
        name="proj_prompt",
    )(x, ada_p, norm_w, w_in_t, qnw, knw, wpool_b, pscale)


def _attn_tile(nch, row0, wit, kik_ref, kk_ref, vtx_ref, sc_ref, lg_ref, wq_ref, acc_ref):
    shape2 = (KEY_CHUNK, TQ)
    n_pairs = N_HEADS // 2
    group_of = lambda hp: (2 * hp) // (N_HEADS // KV_HEADS)

    for c in range(nch):
        keys = slice(c * KEY_CHUNK, (c + 1) * KEY_CHUNK)
        kk = kik_ref[keys, :]
        acc = jnp.zeros(shape2, F32)
        for hp in range(n_pairs):
            s2 = _nt_dot(kk, wq_ref[0, hp])
            acc = acc + jnp.maximum(s2[:, 0:TQ], 0.0) * wit[2 * hp:2 * hp + 1, :]
            acc = acc + jnp.maximum(s2[:, TQ:2 * TQ], 0.0) * wit[2 * hp + 1:2 * hp + 2, :]
        if c == nch - 1:
            kpos = c * KEY_CHUNK + lax.broadcasted_iota(I32, shape2, 0)
            qpos = row0 + lax.broadcasted_iota(I32, shape2, 1)
            acc = jnp.where(kpos <= qpos, acc, -jnp.inf)
        sc_ref[c] = acc

    for c in range(nch):
        keys = slice(c * KEY_CHUNK, (c + 1) * KEY_CHUNK)
        for hp in range(n_pairs):
            lg_ref[hp, c] = _nt_dot(kk_ref[group_of(hp), keys, :], wq_ref[1, hp])

    def count_pass(pred):
        parts = [_fold_rows(pred(sc_ref[c]).astype(F32), jnp.add) for c in range(nch)]
        while len(parts) > 1:
            parts = [sum(parts[j:j + 2]) for j in range(0, len(parts), 2)]
        return jnp.sum(parts[0], axis=0, keepdims=True)

    if nch * KEY_CHUNK <= TOPK:
        thr = jnp.full((1, TQ), -jnp.inf, F32)
        need = jnp.zeros((1, TQ), F32)
    else:
        thr = _kth_largest(lambda t: count_pass(lambda x: x >= t), (1, TQ), peel=SEARCH_PEEL)
        need = float(TOPK) - count_pass(lambda x: x > thr)

    tri = jnp.where(lax.broadcasted_iota(I32, (KEY_CHUNK, KEY_CHUNK), 1)
                    <= lax.broadcasted_iota(I32, (KEY_CHUNK, KEY_CHUNK), 0), 1.0, 0.0).astype(BF16)

    m_part = [jnp.full((SUBLANES, TQ), -jnp.inf, F32) for _ in range(N_HEADS)]
    eq_before = jnp.zeros((1, TQ), F32)
    for c in range(nch):
        sc = sc_ref[c]
        eq = sc == thr
        rank = _dot(tri, jnp.where(eq, 1.0, 0.0).astype(BF16)) + eq_before
        sel = (sc > thr) | (eq & (rank <= need))
        if c == nch - 1:
            sel = sel & (sc > -jnp.inf)
        eq_before = rank[KEY_CHUNK - 1:KEY_CHUNK, :]
        for hp in range(n_pairs):
            for par in range(2):
                cols = slice(par * TQ, (par + 1) * TQ)
                lg = jnp.where(sel, lg_ref[hp, c, :, cols], -jnp.inf)
                lg_ref[hp, c, :, cols] = lg
                m_part[2 * hp + par] = jnp.maximum(m_part[2 * hp + par], _fold_rows(lg, jnp.maximum))
    m_rows = [jnp.max(m, axis=0, keepdims=True) for m in m_part]

    for hp in range(n_pairs):
        m2 = jnp.concatenate([m_rows[2 * hp], m_rows[2 * hp + 1]], axis=1)
        acc = jnp.zeros((VT_ROWS, 2 * TQ), F32)
        for c in range(nch):
            p2 = jnp.exp(lg_ref[hp, c] - m2).astype(BF16)
            acc = acc + _dot(vtx_ref[0, group_of(hp), c], p2)
        acc_ref[hp] = acc


def _attn_prompt_kernel(qi_ref, wit_ref, qs_ref, sga_ref, kik_ref, kk_ref, vtx_ref, ag_ref,
                        sc_ref, lg_ref, wq_ref, acc_ref):
    step = pl.program_id(1)
    n_pairs = N_HEADS // 2
    tiles_per_step = KEY_CHUNK // TQ
    nch = step + 1
    lo = _lane_iota((TQ, LANES)) < HEAD_DIM

    def one_tile(sub, carry):
        rows = pl.ds(pl.multiple_of(sub * TQ, TQ), TQ)
        for kind, ref in enumerate((qi_ref, qs_ref)):
            for hp in range(n_pairs):
                pair = ref[rows, hp * LANES:(hp + 1) * LANES].astype(F32)
                wq_ref[kind, hp, 0:TQ, :] = jnp.where(lo, pair, 0.0).astype(BF16)
                wq_ref[kind, hp, TQ:2 * TQ, :] = jnp.where(lo, 0.0, pair).astype(BF16)

        wit = wit_ref[sub]
        row0 = (step * tiles_per_step + sub) * TQ
        for n_static in range(1, sc_ref.shape[0] + 1):
            @pl.when(nch == n_static)
            def _(n_static=n_static):
                _attn_tile(n_static, row0, wit, kik_ref, kk_ref, vtx_ref, sc_ref, lg_ref, wq_ref, acc_ref)

        for hp in range(n_pairs):
            a = acc_ref[hp]
            o0 = a[0:HEAD_DIM, 0:TQ] / a[HEAD_DIM:HEAD_DIM + 1, 0:TQ]
            o1 = a[0:HEAD_DIM, TQ:2 * TQ] / a[HEAD_DIM:HEAD_DIM + 1, TQ:2 * TQ]
            pair = jnp.concatenate([o0, o1], axis=0).T
            cols = slice(hp * LANES, (hp + 1) * LANES)
            ag_ref[rows, cols] = (pair * sga_ref[rows, cols].astype(F32)).astype(BF16)
        return carry

    lax.fori_loop(0, tiles_per_step, one_tile, 0)


def _attn_prompt(qi, wit, qs, sga, kik, kk, vtx, b, s):
    n = b * s
    nkc = s // KEY_CHUNK
    tps = KEY_CHUNK // TQ
    row = lambda bi, st: (bi * nkc + st, 0)
    return pl.pallas_call(
        _attn_prompt_kernel,
        out_shape=jax.ShapeDtypeStruct((n, ATTN_W), BF16),
        grid=(b, nkc),
        in_specs=[
            pl.BlockSpec((KEY_CHUNK, ATTN_W), row),
            pl.BlockSpec((tps, IDX_HEADS, TQ), lambda bi, st: (bi * nkc + st, 0, 0)),
            pl.BlockSpec((KEY_CHUNK, ATTN_W), row),
            pl.BlockSpec((KEY_CHUNK, ATTN_W), row),
            pl.BlockSpec((s, LANES), lambda bi, st: (bi, 0)),
            pl.BlockSpec((KV_HEADS, s, LANES), lambda bi, st: (0, bi, 0)),
            pl.BlockSpec((1, KV_HEADS, nkc, VT_ROWS, KEY_CHUNK), lambda bi, st: (bi, 0, 0, 0, 0)),
        ],
        out_specs=pl.BlockSpec((KEY_CHUNK, ATTN_W), row),
        scratch_shapes=[
            pltpu.VMEM((nkc, KEY_CHUNK, TQ), F32),
            pltpu.VMEM((N_HEADS // 2, nkc, KEY_CHUNK, 2 * TQ), F32),
            pltpu.VMEM((2, N_HEADS // 2, 2 * TQ, LANES), BF16),
            pltpu.VMEM((N_HEADS // 2, VT_ROWS, 2 * TQ), F32),
        ],
        compiler_params=pltpu.CompilerParams(
            dimension_semantics=("arbitrary", "arbitrary"), vmem_limit_bytes=VMEM_LIMIT),
        name="attn_prompt",
    )(qi, wit, qs, sga, kik, kk, vtx)


def _out_proj_kernel(x_ref, a_ref, p_ref, gate_ref, w_ref, o_ref):
    y = _dot(a_ref[...], w_ref[0:ATTN_W, :]) + _dot(p_ref[...], w_ref[ATTN_W:ATTN_W + POOL_W, :])
    o_ref[0] = x_ref[0] + gate_ref[0] * y


def _out_proj_prompt(x, ag, pg, ada_p, w_out_b):
    b, s, _ = x.shape
    nt = s // TM_OUT
    row = lambda bi, ti: (bi * nt + ti, 0)
    return pl.pallas_call(
        _out_proj_kernel,
        out_shape=jax.ShapeDtypeStruct(x.shape, F32),
        grid=(b, nt),
        in_specs=[
            pl.BlockSpec((1, TM_OUT, D_MODEL), lambda bi, ti: (bi, ti, 0)),
            pl.BlockSpec((TM_OUT, ATTN_W), row),
            pl.BlockSpec((TM_OUT, POOL_W), row),
            pl.BlockSpec((1, 1, D_MODEL), lambda bi, ti: (bi, 0, 2)),
            pl.BlockSpec((D_MODEL, D_MODEL), lambda bi, ti: (0, 0)),
        ],
        out_specs=pl.BlockSpec((1, TM_OUT, D_MODEL), lambda bi, ti: (bi, ti, 0)),
        compiler_params=pltpu.CompilerParams(
            dimension_semantics=("arbitrary", "arbitrary"), vmem_limit_bytes=VMEM_LIMIT),
        name="out_proj_prompt",
    )(x, ag, pg, ada_p, w_out_b)


def _proj_sample_kernel(x_ref, ada_ref, nw_ref, w_ref, qnw_ref, knw_ref, wpool_ref, ps_ref, hist_ref,
                        qpad_ref, k_ref, v_ref, ki_ref, qi_ref, wi_ref, snew_ref, lnew_ref,
                        sga_ref, pg_ref, pool_ref):
    nb = x_ref.shape[0]
    x = x_ref[...]
    shift = ada_ref[:, 0:D_MODEL]
    scale = ada_ref[:, D_MODEL:2 * D_MODEL]
    hb = _modulated_norm(x, nw_ref[...], scale, shift).astype(BF16)

    lane = _lane_iota((nb, LANES))
    lo = lane < HEAD_DIM
    seg256 = _seg_ones(2 * LANES, HEAD_DIM)
    seg128 = _seg_ones(LANES, HEAD_DIM)
    head_sel = jnp.where(lax.broadcasted_iota(I32, (ATTN_W, LANES), 0) // HEAD_DIM
                         == lax.broadcasted_iota(I32, (ATTN_W, LANES), 1), 1.0, 0.0).astype(BF16)

    def head_sums(prod):
        hi = prod.astype(BF16)
        rest = (prod - hi.astype(F32)).astype(BF16)
        return _dot(hi, head_sel) + _dot(rest, head_sel)

    q = _head_rms(_dot(hb, w_ref[:, C_Q:C_Q + ATTN_W]), seg256, qnw_ref[...])
    qb = q.astype(BF16)
    k = _head_rms(_dot(hb, w_ref[:, C_K:C_K + LANES]), seg128, knw_ref[...])
    k_ref[...] = k
    v = _dot(hb, w_ref[:, C_V:C_V + LANES])
    v_ref[...] = v

    for hp in range(N_HEADS // 2):
        pair = q[:, hp * LANES:(hp + 1) * LANES]
        pair_sw = pltpu.roll(pair, HEAD_DIM, axis=1)
        g = (2 * hp) // (N_HEADS // KV_HEADS)
        if g == 0:
            h_even, h_odd = jnp.where(lo, pair, 0.0), jnp.where(lo, pair_sw, 0.0)
        else:
            h_even, h_odd = jnp.where(lo, 0.0, pair_sw), jnp.where(lo, 0.0, pair)
        qpad_ref[:, (2 * hp) * LANES:(2 * hp + 1) * LANES] = h_even.astype(BF16)
        qpad_ref[:, (2 * hp + 1) * LANES:(2 * hp + 2) * LANES] = h_odd.astype(BF16)

    kq = k.astype(BF16).astype(F32)
    kq_sw = pltpu.roll(kq, HEAD_DIM, axis=1)
    k0t = jnp.where(lo, kq, kq_sw)
    k1t = jnp.where(lo, kq_sw, kq)
    qf = qb.astype(F32)
    prod = jnp.concatenate([qf[:, 0:LANES] * k0t, qf[:, LANES:2 * LANES] * k0t,
                            qf[:, 2 * LANES:3 * LANES] * k1t, qf[:, 3 * LANES:4 * LANES] * k1t], axis=1)
    lnew_ref[...] = head_sums(prod)

    qi = _dot(hb, w_ref[:, C_QI:C_QI + ATTN_W])
    qib = qi.astype(BF16)
    qi_ref[...] = qib
    kw = _dot(hb, w_ref[:, C_KW:C_KW + LANES])
    kw_sw = pltpu.roll(kw, HEAD_DIM, axis=1)
    ki_ref[...] = kw[:, 0:IDX_DIM]
    wi_full = jnp.where(lane < IDX_HEADS, kw_sw, 0.0) * ((IDX_HEADS * IDX_DIM) ** -0.5)
    wi_ref[...] = wi_full[:, 0:IDX_HEADS]

    kib = kw.astype(BF16).astype(F32)
    kit = jnp.where(lo, kib, pltpu.roll(kib, HEAD_DIM, axis=1))
    qif = qib.astype(F32)
    prod_i = jnp.concatenate([qif[:, j * LANES:(j + 1) * LANES] * kit for j in range(4)], axis=1)
    s_new = jnp.maximum(head_sums(prod_i), 0.0) * wi_full
    snew_ref[...] = jnp.broadcast_to(jnp.sum(s_new, axis=1, keepdims=True), (nb, LANES))

    sga_ref[...] = _silu(_dot(hb, w_ref[:, C_GA:C_GA + ATTN_W])).astype(BF16)

    u = _dot(hb, w_ref[:, C_U:C_U + POOL_W])
    gp = _dot(hb, w_ref[:, C_GP:C_GP + POOL_W])
    for j in range(POOL_HIST - 1):
        pool_ref[j] = hist_ref[j + 1]
    pool_ref[POOL_HIST - 1] = u
    ds = []
    for g, w in enumerate(POOL_WINDOWS):
        cs = slice(g * POOL_GC, (g + 1) * POOL_GC)
        s = u[:, cs]
        for j in range(1, w):
            s = s + hist_ref[POOL_HIST - j, :, cs]
        ds.append(s / float(w) - u[:, cs])
    d = jnp.concatenate(ds, axis=1)
    pg_ref[...] = _pool_mix(d, wpool_ref, ps_ref[...], gp).astype(BF16)


def _proj_sample(x, ada_s, norm_w, w_in_b, qnw, knw, wpool_b, pscale, hist_t):
    nb = x.shape[0]
    out_shape = (
        jax.ShapeDtypeStruct((nb, N_HEADS * LANES), BF16),
        jax.ShapeDtypeStruct((nb, LANES), F32),
        jax.ShapeDtypeStruct((nb, LANES), F32),
        jax.ShapeDtypeStruct((nb, IDX_DIM), F32),
        jax.ShapeDtypeStruct((nb, ATTN_W), BF16),
        jax.ShapeDtypeStruct((nb, IDX_HEADS), F32),
        jax.ShapeDtypeStruct((nb, LANES), F32),
        jax.ShapeDtypeStruct((nb, LANES), F32),
        jax.ShapeDtypeStruct((nb, ATTN_W), BF16),
        jax.ShapeDtypeStruct((nb, POOL_W), BF16),
        jax.ShapeDtypeStruct((POOL_HIST, nb, POOL_W), F32),
    )
    return pl.pallas_call(
        _proj_sample_kernel,
        out_shape=out_shape,
        compiler_params=pltpu.CompilerParams(vmem_limit_bytes=VMEM_LIMIT),
        name="proj_sample",
    )(x, ada_s, norm_w, w_in_b, qnw, knw, wpool_b, pscale, hist_t)


def _page_copies(pt_ref, pages_hbm, buf_ref, sem_ref, step, slot, group, n_pages):
    copies = []
    for j in range(group):
        for p in range(n_pages):
            page = pt_ref[step * group + j, p]
            copies.append(pltpu.make_async_copy(pages_hbm.at[page], buf_ref.at[slot, j, p], sem_ref.at[slot]))
    return copies


def _row_pages(buf_ref, slot, j):
    return jnp.concatenate([buf_ref[slot, j, p].astype(BF16) for p in range(buf_ref.shape[2])], axis=1)


def _score_sample_kernel(pt_ref, qi_ref, wi_ref, kidx_hbm, o_ref, buf_ref, sem_ref, *, group, n_pages):
    step = pl.program_id(0)
    slot = lax.rem(step, 2)

    def copies(st, sl):
        return _page_copies(pt_ref, kidx_hbm, buf_ref, sem_ref, st, sl, group, n_pages)

    @pl.when(step == 0)
    def _():
        for cp in copies(step, slot):
            cp.start()

    @pl.when(step + 1 < pl.num_programs(0))
    def _():
        for cp in copies(step + 1, 1 - slot):
            cp.start()

    for cp in copies(step, slot):
        cp.wait()

    for j in range(group):
        ki_t = _row_pages(buf_ref, slot, j)
        s = _dot(qi_ref[j], ki_t)
        o_ref[j:j + 1, :] = jnp.sum(jnp.maximum(s, 0.0) * wi_ref[j], axis=0, keepdims=True)


def _score_sample(page_table, qi3, wi3, kidx_t):
    nb, n_pages = page_table.shape
    n_keys = n_pages * PAGE
    g = SCORE_GROUP
    return pl.pallas_call(
        functools.partial(_score_sample_kernel, group=g, n_pages=n_pages),
        out_shape=jax.ShapeDtypeStruct((nb, n_keys), F32),
        grid_spec=pltpu.PrefetchScalarGridSpec(
            num_scalar_prefetch=1,
            grid=(nb // g,),
            in_specs=[
                pl.BlockSpec((g, IDX_HEADS, IDX_DIM), lambda s, pt: (s, 0, 0)),
                pl.BlockSpec((g, IDX_HEADS, 1), lambda s, pt: (s, 0, 0)),
                pl.BlockSpec(memory_space=pl.ANY),
            ],
            out_specs=pl.BlockSpec((g, n_keys), lambda s, pt: (s, 0)),
            scratch_shapes=[
                pltpu.VMEM((2, g, n_pages, IDX_DIM, PAGE), F32),
                pltpu.SemaphoreType.DMA((2,)),
            ],
        ),
        compiler_params=pltpu.CompilerParams(
            dimension_semantics=("arbitrary",), vmem_limit_bytes=VMEM_LIMIT),
        name="score_sample",
    )(page_table, qi3, wi3, kidx_t)


def _select_sample_kernel(sc_ref, snew_ref, mask_ref, mnew_ref):
    nb, n_keys = sc_ref.shape
    nch = n_keys // KEY_CHUNK
    snew = snew_ref[...]
    ones_mat = jnp.ones((LANES, LANES), BF16)

    def count_pass(pred):
        acc = jnp.zeros((nb, LANES), F32)
        for c in range(n_keys // LANES):
            acc = acc + pred(sc_ref[:, c * LANES:(c + 1) * LANES]).astype(F32)
        return _dot(acc.astype(BF16), ones_mat) + pred(snew).astype(F32)

    thr = _kth_largest(lambda t: count_pass(lambda x: x >= t), (nb, LANES))
    need = float(TOPK) - count_pass(lambda x: x > thr)
    thr2 = jnp.concatenate([thr, thr], axis=1)
    need2 = jnp.concatenate([need, need], axis=1)
    tri = jnp.where(lax.broadcasted_iota(I32, (KEY_CHUNK, KEY_CHUNK), 0)
                    <= lax.broadcasted_iota(I32, (KEY_CHUNK, KEY_CHUNK), 1), 1.0, 0.0).astype(BF16)
    ones_cl = jnp.ones((KEY_CHUNK, LANES), BF16)

    eq_before = jnp.zeros((nb, LANES), F32)
    for c in range(nch):
        sc = sc_ref[:, c * KEY_CHUNK:(c + 1) * KEY_CHUNK]
        eq = sc == thr2
        eqb = jnp.where(eq, 1.0, 0.0).astype(BF16)
        rank = _dot(eqb, tri) + jnp.concatenate([eq_before, eq_before], axis=1)
        sel = (sc > thr2) | (eq & (rank <= need2))
        mask_ref[:, c * KEY_CHUNK:(c + 1) * KEY_CHUNK] = jnp.where(sel, 0.0, -jnp.inf)
        eq_before = eq_before + _dot(eqb, ones_cl)
    sel_new = (snew > thr) | ((snew == thr) & (eq_before + 1.0 <= need))
    mnew_ref[...] = jnp.where(sel_new, 0.0, -jnp.inf)


def _select_sample(scores, snew):
    nb, n_keys = scores.shape
    return pl.pallas_call(
        _select_sample_kernel,
        out_shape=(jax.ShapeDtypeStruct((nb, n_keys), F32), jax.ShapeDtypeStruct((nb, LANES), F32)),
        name="select_sample",
    )(scores, snew)


def _attn_sample_kernel(pt_ref, q_ref, mask_ref, lnew_ref, mnew_ref, vnew_ref, k_hbm, v_hbm,
                        o_ref, kbuf_ref, vbuf_ref, ksem_ref, vsem_ref, *, group, n_pages):
    step = pl.program_id(0)
    slot = lax.rem(step, 2)

    def copies(st, sl):
        return (_page_copies(pt_ref, k_hbm, kbuf_ref, ksem_ref, st, sl, group, n_pages)
                + _page_copies(pt_ref, v_hbm, vbuf_ref, vsem_ref, st, sl, group, n_pages))

    @pl.when(step == 0)
    def _():
        for cp in copies(step, slot):
            cp.start()

    @pl.when(step + 1 < pl.num_programs(0))
    def _():
        for cp in copies(step + 1, 1 - slot):
            cp.start()

    for cp in copies(step, slot):
        cp.wait()

    for j in range(group):
        k_t = _row_pages(kbuf_ref, slot, j)
        lg = _dot(q_ref[j], k_t) + mask_ref[j:j + 1, :]
        lg_n = lnew_ref[j] + mnew_ref[j]
        m = jnp.maximum(jnp.max(lg, axis=1, keepdims=True), lg_n)
        p = jnp.exp(lg - m)
        p_n = jnp.exp(lg_n - m)
        l = jnp.sum(p, axis=1, keepdims=True) + p_n
        acc = _nt_dot(p.astype(BF16), _row_pages(vbuf_ref, slot, j)) + p_n * vnew_ref[j]
        o_ref[j] = acc / l


def _attn_sample(page_table, qpad3, mask3, lnew3, mnew3, vnew3, k_t, v_t):
    nb, n_pages = page_table.shape
    n_keys = n_pages * PAGE
    g = ATTN_GROUP
    per_s = lambda s, pt: (s, 0, 0)
    return pl.pallas_call(
        functools.partial(_attn_sample_kernel, group=g, n_pages=n_pages),
        out_shape=jax.ShapeDtypeStruct((nb, N_HEADS, LANES), F32),
        grid_spec=pltpu.PrefetchScalarGridSpec(
            num_scalar_prefetch=1,
            grid=(nb // g,),
            in_specs=[
                pl.BlockSpec((g, N_HEADS, LANES), per_s),
                pl.BlockSpec((g, n_keys), lambda s, pt: (s, 0)),
                pl.BlockSpec((g, N_HEADS, 1), per_s),
                pl.BlockSpec((g, N_HEADS, 1), per_s),
                pl.BlockSpec((g, 1, LANES), per_s),
                pl.BlockSpec(memory_space=pl.ANY),
                pl.BlockSpec(memory_space=pl.ANY),
            ],
            out_specs=pl.BlockSpec((g, N_HEADS, LANES), per_s),
            scratch_shapes=[
                pltpu.VMEM((2, g, n_pages, LANES, PAGE), F32),
                pltpu.VMEM((2, g, n_pages, LANES, PAGE), F32),
                pltpu.SemaphoreType.DMA((2,)),
                pltpu.SemaphoreType.DMA((2,)),
            ],
        ),
        compiler_params=pltpu.CompilerParams(
            dimension_semantics=("arbitrary",), vmem_limit_bytes=VMEM_LIMIT),
        name="attn_sample",
    )(page_table, qpad3, mask3, lnew3, mnew3, vnew3, k_t, v_t)


def _out_proj_sample_kernel(x_ref, a_ref, sga_ref, p_ref, ada_ref, w_ref, o_ref):
    lo = _lane_iota((x_ref.shape[0], LANES)) < HEAD_DIM
    pairs = []
    for hp in range(N_HEADS // 2):
        even = a_ref[:, (2 * hp) * LANES:(2 * hp + 1) * LANES]
        odd = a_ref[:, (2 * hp + 1) * LANES:(2 * hp + 2) * LANES]
        if (2 * hp) // (N_HEADS // KV_HEADS) == 0:
            pairs.append(jnp.where(lo, even, pltpu.roll(odd, HEAD_DIM, axis=1)))
        else:
            pairs.append(jnp.where(lo, pltpu.roll(even, HEAD_DIM, axis=1), odd))
    a = jnp.concatenate(pairs, axis=1)
    ag = (a * sga_ref[...].astype(F32)).astype(BF16)
    y = _dot(ag, w_ref[0:ATTN_W, :]) + _dot(p_ref[...], w_ref[ATTN_W:ATTN_W + POOL_W, :])
    o_ref[...] = x_ref[...] + ada_ref[:, 2 * D_MODEL:3 * D_MODEL] * y


def _out_proj_sample(x, a, sga, pg, ada_s, w_out_b):
    return pl.pallas_call(
        _out_proj_sample_kernel,
        out_shape=jax.ShapeDtypeStruct(x.shape, F32),
        compiler_params=pltpu.CompilerParams(vmem_limit_bytes=VMEM_LIMIT),
        name="out_proj_sample",
    )(x, a, sga, pg, ada_s, w_out_b)


def kernel(x_prompt, x_sample, cache_k, cache_v, cache_kidx, state_pool, page_table, c_prompt, c_sample,
           norm_w, w_ada, b_ada, w_in, q_norm_w, k_norm_w, w_pool, pool_scale, w_out):
    bp, s, _ = x_prompt.shape
    bs = x_sample.shape[0]
    assert w_in.shape[0] == 1 and x_sample.shape[1] == 1, "single layer, single decode token"
    n_phys = cache_k.shape[1]

    w_in_t = jnp.transpose(w_in[0])
    w_out_b = w_out[0].astype(BF16)
    zero_blk = jnp.zeros((POOL_GC, POOL_GC), w_pool.dtype)
    wpool_b = jnp.stack([jnp.block([[w_pool[0, 2 * p], zero_blk], [zero_blk, w_pool[0, 2 * p + 1]]])
                         for p in range(len(POOL_WINDOWS) // 2)]).astype(BF16)
    qnw = jnp.tile(q_norm_w[0], N_HEADS)[None, :] * (HEAD_DIM ** -0.5)
    knw = jnp.tile(k_norm_w[0], KV_HEADS)[None, :]
    nw = norm_w[0][None, :]
    pscale = pool_scale[0][None, :]

    ada_p, ada_s = _ada(c_prompt, c_sample, w_ada, b_ada)
    ada_p = ada_p.reshape(bp, 1, 3 * D_MODEL)

    (qs, k_t_p, kk, v_t_p, vtx, qi, ki_t_p, kik, wit, sga, pg, ulast, w_in_b) = _proj_prompt(
        x_prompt, ada_p, nw, w_in_t, qnw, knw, wpool_b, pscale)
    ag = _attn_prompt(qi, wit, qs, sga, kik, kk, vtx, bp, s)
    y_prompt = _out_proj_prompt(x_prompt, ag, pg, ada_p, w_out_b)

    hist_t = jnp.transpose(state_pool[0], (1, 0, 2))
    (qpad, k_s, v_s, ki_s, qi_s, wi_s, snew, lnew, sga_s, pg_s, pool_s) = _proj_sample(
        x_sample[:, 0, :], ada_s, nw, w_in_b, qnw, knw, wpool_b, pscale, hist_t)
    kidx_t = jnp.transpose(cache_kidx[0], (0, 2, 1))
    k_t = jnp.transpose(cache_k[0], (0, 2, 3, 1)).reshape(n_phys, LANES, PAGE)
    v_t = jnp.transpose(cache_v[0], (0, 2, 3, 1)).reshape(n_phys, LANES, PAGE)
    scores = _score_sample(page_table, qi_s.reshape(bs, IDX_HEADS, IDX_DIM), wi_s.reshape(bs, IDX_HEADS, 1),
                           kidx_t)
    mask, mnew = _select_sample(scores, snew)
    o_s = _attn_sample(
        page_table, qpad.reshape(bs, N_HEADS, LANES), mask,
        lnew[:, :N_HEADS].reshape(bs, N_HEADS, 1),
        jnp.broadcast_to(mnew[:, :1], (bs, N_HEADS)).reshape(bs, N_HEADS, 1),
        v_s.reshape(bs, 1, LANES), k_t, v_t)
    y_sample = _out_proj_sample(x_sample[:, 0, :], o_s.reshape(bs, N_HEADS * LANES), sga_s, pg_s, ada_s, w_out_b)

    to_heads = lambda a: jnp.transpose(a.reshape(bp, KV_HEADS, HEAD_DIM, s), (0, 3, 1, 2))[None]
    return (
        y_prompt,
        y_sample[:, None, :],
        to_heads(k_t_p),
        to_heads(v_t_p),
        jnp.transpose(ki_t_p, (0, 2, 1))[None],
        ulast[:, 1:, :][None],
        k_s.reshape(1, bs, 1, KV_HEADS, HEAD_DIM),
        v_s.reshape(1, bs, 1, KV_HEADS, HEAD_DIM),
        ki_s.reshape(1, bs, 1, IDX_DIM),
        jnp.transpose(pool_s, (1, 0, 2))[None],
    )
```

```python
import functools

import jax
import jax.numpy as jnp
import numpy as np
from jax import lax
from jax.experimental import pallas as pl
from jax.experimental.pallas import tpu as pltpu

F32 = jnp.float32
BF16 = jnp.bfloat16
I32 = jnp.int32

D_MODEL = 1024
ATTN_W = 512
POOL_W = 512
HEAD_DIM = 64
N_HEADS = 8
KV_HEADS = 2
IDX_HEADS = 8
IDX_DIM = 64
TOPK = 256
POOL_WINDOWS = (2, 4, 8, 16)
POOL_GC = 128
POOL_HIST = 15
EPS = 1e-6
PAGE = 128

LANES = 128
SUBLANES = 8
BF16_ROWS = 16
KEY_CHUNK = 256
TQ = 128
TM = 512
TM_OUT = 1024
VT_ROWS = HEAD_DIM + BF16_ROWS
SEARCH_PEEL = 32
SCORE_GROUP = 8
ATTN_GROUP = 8
INT_MIN = np.int32(-2 ** 31)
VMEM_LIMIT = 56 * 1024 * 1024

C_Q, C_K, C_V, C_QI, C_KW, C_GA, C_U, C_GP, N_PROJ = 0, 512, 640, 768, 1280, 1408, 1920, 2432, 2944


def _nt_dot(a, b):
    return lax.dot_general(a, b, (((1,), (1,)), ((), ())), preferred_element_type=F32)


def _dot(a, b):
    return jnp.dot(a, b, preferred_element_type=F32)


def _silu(z):
    return z / (1.0 + jnp.exp(-z))


def _lane_iota(shape):
    return lax.broadcasted_iota(I32, shape, len(shape) - 1)


def _seg_ones(n, seg):
    r = lax.broadcasted_iota(I32, (n, n), 0) // seg
    c = lax.broadcasted_iota(I32, (n, n), 1) // seg
    return jnp.where(r == c, 1.0, 0.0).astype(BF16)


def _head_rms(z, seg_mat, w):
    n = seg_mat.shape[0]
    sq = (z * z).astype(BF16)
    ss = jnp.concatenate([_dot(sq[:, j:j + n], seg_mat) for j in range(0, z.shape[1], n)], axis=1)
    return z * lax.rsqrt(ss * (1.0 / HEAD_DIM) + EPS) * w


def _float_of_rank(u):
    key = u ^ INT_MIN
    bits = key ^ ((key >> 31) & np.int32(0x7FFFFFFF))
    return lax.bitcast_convert_type(bits, F32)


def _kth_largest(count_ge, shape, peel=0):
    def bit_body(it, carry):
        ans, t = carry
        bit = jnp.left_shift(jnp.int32(1), 31 - it)
        nxt = lax.shift_right_logical(bit, jnp.int32(1))
        cand = ans | bit
        t_if_kept, t_if_dropped = _float_of_rank(cand | nxt), _float_of_rank(ans | nxt)
        keep = count_ge(t) >= float(TOPK)
        return jnp.where(keep, cand, ans), jnp.where(keep, t_if_kept, t_if_dropped)

    carry = (jnp.zeros(shape, I32), jnp.zeros(shape, F32))
    for it in range(peel):
        carry = bit_body(jnp.int32(it), carry)
    ans, _ = lax.fori_loop(peel, 32, bit_body, carry)
    return jnp.where(ans == 0, -jnp.inf, _float_of_rank(ans))


def _fold_rows(x, op):
    parts = [x[r:r + SUBLANES] for r in range(0, x.shape[0], SUBLANES)]
    while len(parts) > 1:
        parts = [op(parts[i], parts[i + 1]) for i in range(0, len(parts), 2)]
    return parts[0]


def _ada_kernel(cp_ref, cs_ref, w_ref, b_ref, op_ref, os_ref):
    w = w_ref[0].astype(BF16)
    op_ref[...] = _dot(_silu(cp_ref[...]).astype(BF16), w) + b_ref[...]
    os_ref[...] = _dot(_silu(cs_ref[...]).astype(BF16), w) + b_ref[...]


def _ada(c_prompt, c_sample, w_ada, b_ada):
    bp, bs = c_prompt.shape[0], c_sample.shape[0]
    return pl.pallas_call(
        _ada_kernel,
        out_shape=(jax.ShapeDtypeStruct((bp, 3 * D_MODEL), F32), jax.ShapeDtypeStruct((bs, 3 * D_MODEL), F32)),
        grid=(3,),
        in_specs=[
            pl.BlockSpec((bp, D_MODEL), lambda j: (0, 0)),
            pl.BlockSpec((bs, D_MODEL), lambda j: (0, 0)),
            pl.BlockSpec((1, D_MODEL, D_MODEL), lambda j: (0, 0, j)),
            pl.BlockSpec((1, D_MODEL), lambda j: (0, j)),
        ],
        out_specs=(pl.BlockSpec((bp, D_MODEL), lambda j: (0, j)), pl.BlockSpec((bs, D_MODEL), lambda j: (0, j))),
        compiler_params=pltpu.CompilerParams(dimension_semantics=("arbitrary",)),
        name="ada_ln",
    )(c_prompt, c_sample, w_ada, b_ada)


def _modulated_norm(x, norm_w, scale, shift):
    ms = jnp.mean(x * x, axis=-1, keepdims=True)
    return (x * lax.rsqrt(ms + EPS)) * norm_w * (1.0 + scale) + shift


def _pool_mix(d, wpool_ref, pscale, gp):
    db = d.astype(BF16)
    wide = 2 * POOL_GC
    y = jnp.concatenate([_dot(db[:, p * wide:(p + 1) * wide], wpool_ref[p])
                         for p in range(len(POOL_WINDOWS) // 2)], axis=1)
    return y * pscale * _silu(gp)


def _permuted_weight_block(wt_ref, g):
    cut = C_KW + IDX_DIM + IDX_HEADS
    c0 = g * LANES
    if c0 + LANES <= cut:
        blk = wt_ref[c0:c0 + LANES, :]
    elif c0 < cut:
        blk = jnp.concatenate([wt_ref[c0:cut, :], jnp.zeros((c0 + LANES - cut, D_MODEL), F32)], axis=0)
    else:
        r0 = c0 - (C_GA - cut)
        blk = wt_ref[r0:r0 + LANES, :]
    return blk.T.astype(BF16)


def _proj_prompt_kernel(x_ref, ada_ref, nw_ref, wt_ref, qnw_ref, knw_ref, wpool_ref, ps_ref,
                        qs_ref, kt_ref, kk_ref, vt_ref, vtx_ref, qi_ref, kit_ref, kik_ref,
                        wit_ref, sga_ref, pg_ref, ulast_ref, w_ref,
                        ext_ref):
    t = pl.program_id(1)

    @pl.when((pl.program_id(0) == 0) & (t == 0))
    def _():
        for g in range(N_PROJ // LANES):
            w_ref[:, g * LANES:(g + 1) * LANES] = _permuted_weight_block(wt_ref, g)

    ada = ada_ref[0]
    shift = ada[:, 0:D_MODEL]
    scale = ada[:, D_MODEL:2 * D_MODEL]
    sub_rows = KEY_CHUNK
    lo = _lane_iota((sub_rows, LANES)) < HEAD_DIM
    seg256 = _seg_ones(2 * LANES, HEAD_DIM)
    seg128 = _seg_ones(LANES, HEAD_DIM)
    ones = jnp.ones((BF16_ROWS, KEY_CHUNK), BF16)

    @pl.when(t == 0)
    def _():
        ext_ref[0:16, :] = jnp.zeros((16, POOL_W), F32)

    for sub in range(TM // sub_rows):
        rows = slice(sub * sub_rows, (sub + 1) * sub_rows)
        hb = _modulated_norm(x_ref[0, rows, :], nw_ref[...], scale, shift).astype(BF16)

        q = _dot(hb, w_ref[:, C_Q:C_Q + ATTN_W])
        qs_ref[rows, :] = _head_rms(q, seg256, qnw_ref[...]).astype(BF16)

        kv = _dot(hb, w_ref[:, C_K:C_K + 2 * LANES])
        k = _head_rms(kv[:, 0:LANES], seg128, knw_ref[...])
        kt_ref[0, :, rows] = k.T
        k_sw = pltpu.roll(k, HEAD_DIM, axis=1)
        kk_ref[0, rows, :] = jnp.where(lo, k, k_sw).astype(BF16)
        kk_ref[1, rows, :] = jnp.where(lo, k_sw, k).astype(BF16)

        v_t = kv[:, LANES:2 * LANES].T
        vt_ref[0, :, rows] = v_t
        for g in range(KV_HEADS):
            vtx_ref[0, g, sub, 0:HEAD_DIM, :] = v_t[g * HEAD_DIM:(g + 1) * HEAD_DIM].astype(BF16)
            vtx_ref[0, g, sub, HEAD_DIM:VT_ROWS, :] = ones

        qi_ref[rows, :] = _dot(hb, w_ref[:, C_QI:C_QI + ATTN_W]).astype(BF16)
        kw = _dot(hb, w_ref[:, C_KW:C_KW + LANES])
        kw_t = kw.T
        kit_ref[0, :, rows] = kw_t[0:IDX_DIM]
        wi_t = kw_t[IDX_DIM:IDX_DIM + IDX_HEADS] * ((IDX_HEADS * IDX_DIM) ** -0.5)
        for j in range(sub_rows // TQ):
            wit_ref[sub * (sub_rows // TQ) + j] = wi_t[:, j * TQ:(j + 1) * TQ]
        kik_ref[rows, :] = jnp.where(lo, kw, pltpu.roll(kw, HEAD_DIM, axis=1)).astype(BF16)

        sga_ref[rows, :] = _silu(_dot(hb, w_ref[:, C_GA:C_GA + ATTN_W])).astype(BF16)

        u = _dot(hb, w_ref[:, C_U:C_U + POOL_W])
        gp = _dot(hb, w_ref[:, C_GP:C_GP + POOL_W])
        base = 16 + sub * sub_rows
        ext_ref[base:base + sub_rows, :] = u
        pos = t * TM + sub * sub_rows + lax.broadcasted_iota(I32, (sub_rows, POOL_GC), 0)
        ds = []
        for g, w in enumerate(POOL_WINDOWS):
            cs = slice(g * POOL_GC, (g + 1) * POOL_GC)
            s = ext_ref[base - 16:base + sub_rows, cs]
            k = 1
            while k < w:
                s = s + pltpu.roll(s, k, axis=0)
                k *= 2
            cnt = jnp.minimum(pos + 1, w).astype(F32)
            ds.append(s[16:16 + sub_rows] / cnt - u[:, cs])
        d = jnp.concatenate(ds, axis=1)
        pg_ref[rows, :] = _pool_mix(d, wpool_ref, ps_ref[...], gp).astype(BF16)

    tail = ext_ref[TM:TM + 16, :]
    ulast_ref[0] = tail
    ext_ref[0:16, :] = tail


def _proj_prompt(x, ada_p, norm_w, w_in_t, qnw, knw, wpool_b, pscale):
    b, s, _ = x.shape
    n = b * s
    nt = s // TM
    cpt = TM // KEY_CHUNK
    row = lambda bi, ti: (bi * nt + ti, 0)
    tok = lambda bi, ti: (bi, 0, ti)
    const2 = lambda bi, ti: (0, 0)
    const3 = lambda bi, ti: (0, 0, 0)
    out_shape = (
        jax.ShapeDtypeStruct((n, ATTN_W), BF16),
        jax.ShapeDtypeStruct((b, LANES, s), F32),
        jax.ShapeDtypeStruct((KV_HEADS, n, LANES), BF16),
        jax.ShapeDtypeStruct((b, LANES, s), F32),
        jax.ShapeDtypeStruct((b, KV_HEADS, s // KEY_CHUNK, VT_ROWS, KEY_CHUNK), BF16),
        jax.ShapeDtypeStruct((n, ATTN_W), BF16),
        jax.ShapeDtypeStruct((b, IDX_DIM, s), F32),
        jax.ShapeDtypeStruct((n, LANES), BF16),
        jax.ShapeDtypeStruct((n // TQ, IDX_HEADS, TQ), F32),
        jax.ShapeDtypeStruct((n, ATTN_W), BF16),
        jax.ShapeDtypeStruct((n, POOL_W), BF16),
        jax.ShapeDtypeStruct((b, 16, POOL_W), F32),
        jax.ShapeDtypeStruct((D_MODEL, N_PROJ), BF16),
    )
    out_specs = (
        pl.BlockSpec((TM, ATTN_W), row),
        pl.BlockSpec((1, LANES, TM), tok),
        pl.BlockSpec((KV_HEADS, TM, LANES), lambda bi, ti: (0, bi * nt + ti, 0)),
        pl.BlockSpec((1, LANES, TM), tok),
        pl.BlockSpec((1, KV_HEADS, cpt, VT_ROWS, KEY_CHUNK), lambda bi, ti: (bi, 0, ti, 0, 0)),
        pl.BlockSpec((TM, ATTN_W), row),
        pl.BlockSpec((1, IDX_DIM, TM), tok),
        pl.BlockSpec((TM, LANES), row),
        pl.BlockSpec((TM // TQ, IDX_HEADS, TQ), lambda bi, ti: (bi * nt + ti, 0, 0)),
        pl.BlockSpec((TM, ATTN_W), row),
        pl.BlockSpec((TM, POOL_W), row),
        pl.BlockSpec((1, 16, POOL_W), lambda bi, ti: (bi, 0, 0)),
        pl.BlockSpec((D_MODEL, N_PROJ), const2),
    )
    in_specs = [
        pl.BlockSpec((1, TM, D_MODEL), lambda bi, ti: (bi, ti, 0)),
        pl.BlockSpec((1, 1, 3 * D_MODEL), lambda bi, ti: (bi, 0, 0)),
        pl.BlockSpec((1, D_MODEL), const2),
        pl.BlockSpec(w_in_t.shape, const2),
        pl.BlockSpec((1, ATTN_W), const2),
        pl.BlockSpec((1, LANES), const2),
        pl.BlockSpec((2, 2 * POOL_GC, 2 * POOL_GC), const3),
        pl.BlockSpec((1, POOL_W), const2),
    ]
    return pl.pallas_call(
        _proj_prompt_kernel,
        out_shape=out_shape,
        grid=(b, nt),
        in_specs=in_specs,
        out_specs=out_specs,
        scratch_shapes=[pltpu.VMEM((16 + TM, POOL_W), F32)],
        compiler_params=pltpu.CompilerParams(
            dimension_semantics=("arbitrary", "arbitrary"), vmem_limit_bytes=VMEM_LIMIT),
        name="proj_prompt",
    )(x, ada_p, norm_w, w_in_t, qnw, knw, wpool_b, pscale)


def _attn_tile(nch, row0, wit, kik_ref, kk_ref, vtx_ref, sc_ref, lg_ref, wq_ref, acc_ref):
    shape2 = (KEY_CHUNK, TQ)
    n_pairs = N_HEADS // 2
    group_of = lambda hp: (2 * hp) // (N_HEADS // KV_HEADS)

    for c in range(nch):
        keys = slice(c * KEY_CHUNK, (c + 1) * KEY_CHUNK)
        kk = kik_ref[keys, :]
        acc = jnp.zeros(shape2, F32)
        for hp in range(n_pairs):
            s2 = _nt_dot(kk, wq_ref[0, hp])
            acc = acc + jnp.maximum(s2[:, 0:TQ], 0.0) * wit[2 * hp:2 * hp + 1, :]
            acc = acc + jnp.maximum(s2[:, TQ:2 * TQ], 0.0) * wit[2 * hp + 1:2 * hp + 2, :]
        if c == nch - 1:
            kpos = c * KEY_CHUNK + lax.broadcasted_iota(I32, shape2, 0)
            qpos = row0 + lax.broadcasted_iota(I32, shape2, 1)
            acc = jnp.where(kpos <= qpos, acc, -jnp.inf)
        sc_ref[c] = acc

    for c in range(nch):
        keys = slice(c * KEY_CHUNK, (c + 1) * KEY_CHUNK)
        for hp in range(n_pairs):
            lg_ref[hp, c] = _nt_dot(kk_ref[group_of(hp), keys, :], wq_ref[1, hp])

    def count_pass(pred):
        parts = [_fold_rows(pred(sc_ref[c]).astype(F32), jnp.add) for c in range(nch)]
        while len(parts) > 1:
            parts = [sum(parts[j:j + 2]) for j in range(0, len(parts), 2)]
        return jnp.sum(parts[0], axis=0, keepdims=True)

    if nch * KEY_CHUNK <= TOPK:
        thr = jnp.full((1, TQ), -jnp.inf, F32)
        need = jnp.zeros((1, TQ), F32)
    else:
        thr = _kth_largest(lambda t: count_pass(lambda x: x >= t), (1, TQ), peel=SEARCH_PEEL)
        need = float(TOPK) - count_pass(lambda x: x > thr)

    tri = jnp.where(lax.broadcasted_iota(I32, (KEY_CHUNK, KEY_CHUNK), 1)
                    <= lax.broadcasted_iota(I32, (KEY_CHUNK, KEY_CHUNK), 0), 1.0, 0.0).astype(BF16)

    m_part = [jnp.full((SUBLANES, TQ), -jnp.inf, F32) for _ in range(N_HEADS)]
    eq_before = jnp.zeros((1, TQ), F32)
    for c in range(nch):
        sc = sc_ref[c]
        eq = sc == thr
        rank = _dot(tri, jnp.where(eq, 1.0, 0.0).astype(BF16)) + eq_before
        sel = (sc > thr) | (eq & (rank <= need))
        if c == nch - 1:
            sel = sel & (sc > -jnp.inf)
        eq_before = rank[KEY_CHUNK - 1:KEY_CHUNK, :]
        for hp in range(n_pairs):
            for par in range(2):
                cols = slice(par * TQ, (par + 1) * TQ)
                lg = jnp.where(sel, lg_ref[hp, c, :, cols], -jnp.inf)
                lg_ref[hp, c, :, cols] = lg
                m_part[2 * hp + par] = jnp.maximum(m_part[2 * hp + par], _fold_rows(lg, jnp.maximum))
    m_rows = [jnp.max(m, axis=0, keepdims=True) for m in m_part]

    for hp in range(n_pairs):
        m2 = jnp.concatenate([m_rows[2 * hp], m_rows[2 * hp + 1]], axis=1)
        acc = jnp.zeros((VT_ROWS, 2 * TQ), F32)
        for c in range(nch):
            p2 = jnp.exp(lg_ref[hp, c] - m2).astype(BF16)
            acc = acc + _dot(vtx_ref[0, group_of(hp), c], p2)
        acc_ref[hp] = acc


def _attn_prompt_kernel(qi_ref, wit_ref, qs_ref, sga_ref, kik_ref, kk_ref, vtx_ref, ag_ref,
                        sc_ref, lg_ref, wq_ref, acc_ref):
    step = pl.program_id(1)
    n_pairs = N_HEADS // 2
    tiles_per_step = KEY_CHUNK // TQ
    nch = step + 1
    lo = _lane_iota((TQ, LANES)) < HEAD_DIM

    def one_tile(sub, carry):
        rows = pl.ds(pl.multiple_of(sub * TQ, TQ), TQ)
        for kind, ref in enumerate((qi_ref, qs_ref)):
            for hp in range(n_pairs):
                pair = ref[rows, hp * LANES:(hp + 1) * LANES].astype(F32)
                wq_ref[kind, hp, 0:TQ, :] = jnp.where(lo, pair, 0.0).astype(BF16)
                wq_ref[kind, hp, TQ:2 * TQ, :] = jnp.where(lo, 0.0, pair).astype(BF16)

        wit = wit_ref[sub]
        row0 = (step * tiles_per_step + sub) * TQ
        for n_static in range(1, sc_ref.shape[0] + 1):
            @pl.when(nch == n_static)
            def _(n_static=n_static):
                _attn_tile(n_static, row0, wit, kik_ref, kk_ref, vtx_ref, sc_ref, lg_ref, wq_ref, acc_ref)

        for hp in range(n_pairs):
            a = acc_ref[hp]
            o0 = a[0:HEAD_DIM, 0:TQ] / a[HEAD_DIM:HEAD_DIM + 1, 0:TQ]
            o1 = a[0:HEAD_DIM, TQ:2 * TQ] / a[HEAD_DIM:HEAD_DIM + 1, TQ:2 * TQ]
            pair = jnp.concatenate([o0, o1], axis=0).T
            cols = slice(hp * LANES, (hp + 1) * LANES)
            ag_ref[rows, cols] = (pair * sga_ref[rows, cols].astype(F32)).astype(BF16)
        return carry

    lax.fori_loop(0, tiles_per_step, one_tile, 0)


def _attn_prompt(qi, wit, qs, sga, kik, kk, vtx, b, s):
    n = b * s
    nkc = s // KEY_CHUNK
    tps = KEY_CHUNK // TQ
    row = lambda bi, st: (bi * nkc + st, 0)
    return pl.pallas_call(
        _attn_prompt_kernel,
        out_shape=jax.ShapeDtypeStruct((n, ATTN_W), BF16),
        grid=(b, nkc),
        in_specs=[
            pl.BlockSpec((KEY_CHUNK, ATTN_W), row),
            pl.BlockSpec((tps, IDX_HEADS, TQ), lambda bi, st: (bi * nkc + st, 0, 0)),
            pl.BlockSpec((KEY_CHUNK, ATTN_W), row),
            pl.BlockSpec((KEY_CHUNK, ATTN_W), row),
            pl.BlockSpec((s, LANES), lambda bi, st: (bi, 0)),
            pl.BlockSpec((KV_HEADS, s, LANES), lambda bi, st: (0, bi, 0)),
            pl.BlockSpec((1, KV_HEADS, nkc, VT_ROWS, KEY_CHUNK), lambda bi, st: (bi, 0, 0, 0, 0)),
        ],
        out_specs=pl.BlockSpec((KEY_CHUNK, ATTN_W), row),
        scratch_shapes=[
            pltpu.VMEM((nkc, KEY_CHUNK, TQ), F32),
            pltpu.VMEM((N_HEADS // 2, nkc, KEY_CHUNK, 2 * TQ), F32),
            pltpu.VMEM((2, N_HEADS // 2, 2 * TQ, LANES), BF16),
            pltpu.VMEM((N_HEADS // 2, VT_ROWS, 2 * TQ), F32),
        ],
        compiler_params=pltpu.CompilerParams(
            dimension_semantics=("arbitrary", "arbitrary"), vmem_limit_bytes=VMEM_LIMIT),
        name="attn_prompt",
    )(qi, wit, qs, sga, kik, kk, vtx)


def _out_proj_kernel(x_ref, a_ref, p_ref, gate_ref, w_ref, o_ref):
    y = _dot(a_ref[...], w_ref[0:ATTN_W, :]) + _dot(p_ref[...], w_ref[ATTN_W:ATTN_W + POOL_W, :])
    o_ref[0] = x_ref[0] + gate_ref[0] * y


def _out_proj_prompt(x, ag, pg, ada_p, w_out_b):
    b, s, _ = x.shape
    nt = s // TM_OUT
    row = lambda bi, ti: (bi * nt + ti, 0)
    return pl.pallas_call(
        _out_proj_kernel,
        out_shape=jax.ShapeDtypeStruct(x.shape, F32),
        grid=(b, nt),
        in_specs=[
            pl.BlockSpec((1, TM_OUT, D_MODEL), lambda bi, ti: (bi, ti, 0)),
            pl.BlockSpec((TM_OUT, ATTN_W), row),
            pl.BlockSpec((TM_OUT, POOL_W), row),
            pl.BlockSpec((1, 1, D_MODEL), lambda bi, ti: (bi, 0, 2)),
            pl.BlockSpec((D_MODEL, D_MODEL), lambda bi, ti: (0, 0)),
        ],
        out_specs=pl.BlockSpec((1, TM_OUT, D_MODEL), lambda bi, ti: (bi, ti, 0)),
        compiler_params=pltpu.CompilerParams(
            dimension_semantics=("arbitrary", "arbitrary"), vmem_limit_bytes=VMEM_LIMIT),
        name="out_proj_prompt",
    )(x, ag, pg, ada_p, w_out_b)


def _proj_sample_kernel(x_ref, ada_ref, nw_ref, w_ref, qnw_ref, knw_ref, wpool_ref, ps_ref, hist_ref,
                        qpad_ref, k_ref, v_ref, ki_ref, qi_ref, wi_ref, snew_ref, lnew_ref,
                        sga_ref, pg_ref, pool_ref):
    nb = x_ref.shape[0]
    x = x_ref[...]
    shift = ada_ref[:, 0:D_MODEL]
    scale = ada_ref[:, D_MODEL:2 * D_MODEL]
    hb = _modulated_norm(x, nw_ref[...], scale, shift).astype(BF16)

    lane = _lane_iota((nb, LANES))
    lo = lane < HEAD_DIM
    seg256 = _seg_ones(2 * LANES, HEAD_DIM)
    seg128 = _seg_ones(LANES, HEAD_DIM)
    head_sel = jnp.where(lax.broadcasted_iota(I32, (ATTN_W, LANES), 0) // HEAD_DIM
                         == lax.broadcasted_iota(I32, (ATTN_W, LANES), 1), 1.0, 0.0).astype(BF16)

    def head_sums(prod):
        hi = prod.astype(BF16)
        rest = (prod - hi.astype(F32)).astype(BF16)
        return _dot(hi, head_sel) + _dot(rest, head_sel)

    q = _head_rms(_dot(hb, w_ref[:, C_Q:C_Q + ATTN_W]), seg256, qnw_ref[...])
    qb = q.astype(BF16)
    k = _head_rms(_dot(hb, w_ref[:, C_K:C_K + LANES]), seg128, knw_ref[...])
    k_ref[...] = k
    v = _dot(hb, w_ref[:, C_V:C_V + LANES])
    v_ref[...] = v

    for hp in range(N_HEADS // 2):
        pair = q[:, hp * LANES:(hp + 1) * LANES]
        pair_sw = pltpu.roll(pair, HEAD_DIM, axis=1)
        g = (2 * hp) // (N_HEADS // KV_HEADS)
        if g == 0:
            h_even, h_odd = jnp.where(lo, pair, 0.0), jnp.where(lo, pair_sw, 0.0)
        else:
            h_even, h_odd = jnp.where(lo, 0.0, pair_sw), jnp.where(lo, 0.0, pair)
        qpad_ref[:, (2 * hp) * LANES:(2 * hp + 1) * LANES] = h_even.astype(BF16)
        qpad_ref[:, (2 * hp + 1) * LANES:(2 * hp + 2) * LANES] = h_odd.astype(BF16)

    kq = k.astype(BF16).astype(F32)
    kq_sw = pltpu.roll(kq, HEAD_DIM, axis=1)
    k0t = jnp.where(lo, kq, kq_sw)
    k1t = jnp.where(lo, kq_sw, kq)
    qf = qb.astype(F32)
    prod = jnp.concatenate([qf[:, 0:LANES] * k0t, qf[:, LANES:2 * LANES] * k0t,
                            qf[:, 2 * LANES:3 * LANES] * k1t, qf[:, 3 * LANES:4 * LANES] * k1t], axis=1)
    lnew_ref[...] = head_sums(prod)

    qi = _dot(hb, w_ref[:, C_QI:C_QI + ATTN_W])
    qib = qi.astype(BF16)
    qi_ref[...] = qib
    kw = _dot(hb, w_ref[:, C_KW:C_KW + LANES])
    kw_sw = pltpu.roll(kw, HEAD_DIM, axis=1)
    ki_ref[...] = kw[:, 0:IDX_DIM]
    wi_full = jnp.where(lane < IDX_HEADS, kw_sw, 0.0) * ((IDX_HEADS * IDX_DIM) ** -0.5)
    wi_ref[...] = wi_full[:, 0:IDX_HEADS]

    kib = kw.astype(BF16).astype(F32)
    kit = jnp.where(lo, kib, pltpu.roll(kib, HEAD_DIM, axis=1))
    qif = qib.astype(F32)
    prod_i = jnp.concatenate([qif[:, j * LANES:(j + 1) * LANES] * kit for j in range(4)], axis=1)
    s_new = jnp.maximum(head_sums(prod_i), 0.0) * wi_full
    snew_ref[...] = jnp.broadcast_to(jnp.sum(s_new, axis=1, keepdims=True), (nb, LANES))

    sga_ref[...] = _silu(_dot(hb, w_ref[:, C_GA:C_GA + ATTN_W])).astype(BF16)

    u = _dot(hb, w_ref[:, C_U:C_U + POOL_W])
    gp = _dot(hb, w_ref[:, C_GP:C_GP + POOL_W])
    for j in range(POOL_HIST - 1):
        pool_ref[j] = hist_ref[j + 1]
    pool_ref[POOL_HIST - 1] = u
    ds = []
    for g, w in enumerate(POOL_WINDOWS):
        cs = slice(g * POOL_GC, (g + 1) * POOL_GC)
        s = u[:, cs]
        for j in range(1, w):
            s = s + hist_ref[POOL_HIST - j, :, cs]
        ds.append(s / float(w) - u[:, cs])
    d = jnp.concatenate(ds, axis=1)
    pg_ref[...] = _pool_mix(d, wpool_ref, ps_ref[...], gp).astype(BF16)


def _proj_sample(x, ada_s, norm_w, w_in_b, qnw, knw, wpool_b, pscale, hist_t):
    nb = x.shape[0]
    out_shape = (
        jax.ShapeDtypeStruct((nb, N_HEADS * LANES), BF16),
        jax.ShapeDtypeStruct((nb, LANES), F32),
        jax.ShapeDtypeStruct((nb, LANES), F32),
        jax.ShapeDtypeStruct((nb, IDX_DIM), F32),
        jax.ShapeDtypeStruct((nb, ATTN_W), BF16),
        jax.ShapeDtypeStruct((nb, IDX_HEADS), F32),
        jax.ShapeDtypeStruct((nb, LANES), F32),
        jax.ShapeDtypeStruct((nb, LANES), F32),
        jax.ShapeDtypeStruct((nb, ATTN_W), BF16),
        jax.ShapeDtypeStruct((nb, POOL_W), BF16),
        jax.ShapeDtypeStruct((POOL_HIST, nb, POOL_W), F32),
    )
    return pl.pallas_call(
        _proj_sample_kernel,
        out_shape=out_shape,
        compiler_params=pltpu.CompilerParams(vmem_limit_bytes=VMEM_LIMIT),
        name="proj_sample",
    )(x, ada_s, norm_w, w_in_b, qnw, knw, wpool_b, pscale, hist_t)


def _page_copies(pt_ref, pages_hbm, buf_ref, sem_ref, step, slot, group, n_pages):
    copies = []
    for j in range(group):
        for p in range(n_pages):
            page = pt_ref[step * group + j, p]
            copies.append(pltpu.make_async_copy(pages_hbm.at[page], buf_ref.at[slot, j, p], sem_ref.at[slot]))
    return copies


def _row_pages(buf_ref, slot, j):
    return jnp.concatenate([buf_ref[slot, j, p].astype(BF16) for p in range(buf_ref.shape[2])], axis=1)


def _score_sample_kernel(pt_ref, qi_ref, wi_ref, kidx_hbm, o_ref, buf_ref, sem_ref, *, group, n_pages):
    step = pl.program_id(0)
    slot = lax.rem(step, 2)

    def copies(st, sl):
        return _page_copies(pt_ref, kidx_hbm, buf_ref, sem_ref, st, sl, group, n_pages)

    @pl.when(step == 0)
    def _():
        for cp in copies(step, slot):
            cp.start()

    @pl.when(step + 1 < pl.num_programs(0))
    def _():
        for cp in copies(step + 1, 1 - slot):
            cp.start()

    for cp in copies(step, slot):
        cp.wait()

    for j in range(group):
        ki_t = _row_pages(buf_ref, slot, j)
        s = _dot(qi_ref[j], ki_t)
        o_ref[j:j + 1, :] = jnp.sum(jnp.maximum(s, 0.0) * wi_ref[j], axis=0, keepdims=True)


def _score_sample(page_table, qi3, wi3, kidx_t):
    nb, n_pages = page_table.shape
    n_keys = n_pages * PAGE
    g = SCORE_GROUP
    return pl.pallas_call(
        functools.partial(_score_sample_kernel, group=g, n_pages=n_pages),
        out_shape=jax.ShapeDtypeStruct((nb, n_keys), F32),
        grid_spec=pltpu.PrefetchScalarGridSpec(
            num_scalar_prefetch=1,
            grid=(nb // g,),
            in_specs=[
                pl.BlockSpec((g, IDX_HEADS, IDX_DIM), lambda s, pt: (s, 0, 0)),
                pl.BlockSpec((g, IDX_HEADS, 1), lambda s, pt: (s, 0, 0)),
                pl.BlockSpec(memory_space=pl.ANY),
            ],
            out_specs=pl.BlockSpec((g, n_keys), lambda s, pt: (s, 0)),
            scratch_shapes=[
                pltpu.VMEM((2, g, n_pages, IDX_DIM, PAGE), F32),
                pltpu.SemaphoreType.DMA((2,)),
            ],
        ),
        compiler_params=pltpu.CompilerParams(
            dimension_semantics=("arbitrary",), vmem_limit_bytes=VMEM_LIMIT),
        name="score_sample",
    )(page_table, qi3, wi3, kidx_t)


def _select_sample_kernel(sc_ref, snew_ref, mask_ref, mnew_ref):
    nb, n_keys = sc_ref.shape
    nch = n_keys // KEY_CHUNK
    snew = snew_ref[...]
    ones_mat = jnp.ones((LANES, LANES), BF16)

    def count_pass(pred):
        acc = jnp.zeros((nb, LANES), F32)
        for c in range(n_keys // LANES):
            acc = acc + pred(sc_ref[:, c * LANES:(c + 1) * LANES]).astype(F32)
        return _dot(acc.astype(BF16), ones_mat) + pred(snew).astype(F32)

    thr = _kth_largest(lambda t: count_pass(lambda x: x >= t), (nb, LANES))
    need = float(TOPK) - count_pass(lambda x: x > thr)
    thr2 = jnp.concatenate([thr, thr], axis=1)
    need2 = jnp.concatenate([need, need], axis=1)
    tri = jnp.where(lax.broadcasted_iota(I32, (KEY_CHUNK, KEY_CHUNK), 0)
                    <= lax.broadcasted_iota(I32, (KEY_CHUNK, KEY_CHUNK), 1), 1.0, 0.0).astype(BF16)
    ones_cl = jnp.ones((KEY_CHUNK, LANES), BF16)

    eq_before = jnp.zeros((nb, LANES), F32)
    for c in range(nch):
        sc = sc_ref[:, c * KEY_CHUNK:(c + 1) * KEY_CHUNK]
        eq = sc == thr2
        eqb = jnp.where(eq, 1.0, 0.0).astype(BF16)
        rank = _dot(eqb, tri) + jnp.concatenate([eq_before, eq_before], axis=1)
        sel = (sc > thr2) | (eq & (rank <= need2))
        mask_ref[:, c * KEY_CHUNK:(c + 1) * KEY_CHUNK] = jnp.where(sel, 0.0, -jnp.inf)
        eq_before = eq_before + _dot(eqb, ones_cl)
    sel_new = (snew > thr) | ((snew == thr) & (eq_before + 1.0 <= need))
    mnew_ref[...] = jnp.where(sel_new, 0.0, -jnp.inf)


def _select_sample(scores, snew):
    nb, n_keys = scores.shape
    return pl.pallas_call(
        _select_sample_kernel,
        out_shape=(jax.ShapeDtypeStruct((nb, n_keys), F32), jax.ShapeDtypeStruct((nb, LANES), F32)),
        name="select_sample",
    )(scores, snew)


def _attn_sample_kernel(pt_ref, q_ref, mask_ref, lnew_ref, mnew_ref, vnew_ref, k_hbm, v_hbm,
                        o_ref, kbuf_ref, vbuf_ref, ksem_ref, vsem_ref, *, group, n_pages):
    step = pl.program_id(0)
    slot = lax.rem(step, 2)

    def copies(st, sl):
        return (_page_copies(pt_ref, k_hbm, kbuf_ref, ksem_ref, st, sl, group, n_pages)
                + _page_copies(pt_ref, v_hbm, vbuf_ref, vsem_ref, st, sl, group, n_pages))

    @pl.when(step == 0)
    def _():
        for cp in copies(step, slot):
            cp.start()

    @pl.when(step + 1 < pl.num_programs(0))
    def _():
        for cp in copies(step + 1, 1 - slot):
            cp.start()

    for cp in copies(step, slot):
        cp.wait()

    for j in range(group):
        k_t = _row_pages(kbuf_ref, slot, j)
        lg = _dot(q_ref[j], k_t) + mask_ref[j:j + 1, :]
        lg_n = lnew_ref[j] + mnew_ref[j]
        m = jnp.maximum(jnp.max(lg, axis=1, keepdims=True), lg_n)
        p = jnp.exp(lg - m)
        p_n = jnp.exp(lg_n - m)
        l = jnp.sum(p, axis=1, keepdims=True) + p_n
        acc = _nt_dot(p.astype(BF16), _row_pages(vbuf_ref, slot, j)) + p_n * vnew_ref[j]
        o_ref[j] = acc / l


def _attn_sample(page_table, qpad3, mask3, lnew3, mnew3, vnew3, k_t, v_t):
    nb, n_pages = page_table.shape
    n_keys = n_pages * PAGE
    g = ATTN_GROUP
    per_s = lambda s, pt: (s, 0, 0)
    return pl.pallas_call(
        functools.partial(_attn_sample_kernel, group=g, n_pages=n_pages),
        out_shape=jax.ShapeDtypeStruct((nb, N_HEADS, LANES), F32),
        grid_spec=pltpu.PrefetchScalarGridSpec(
            num_scalar_prefetch=1,
            grid=(nb // g,),
            in_specs=[
                pl.BlockSpec((g, N_HEADS, LANES), per_s),
                pl.BlockSpec((g, n_keys), lambda s, pt: (s, 0)),
                pl.BlockSpec((g, N_HEADS, 1), per_s),
                pl.BlockSpec((g, N_HEADS, 1), per_s),
                pl.BlockSpec((g, 1, LANES), per_s),
                pl.BlockSpec(memory_space=pl.ANY),
                pl.BlockSpec(memory_space=pl.ANY),
            ],
            out_specs=pl.BlockSpec((g, N_HEADS, LANES), per_s),
            scratch_shapes=[
                pltpu.VMEM((2, g, n_pages, LANES, PAGE), F32),
                pltpu.VMEM((2, g, n_pages, LANES, PAGE), F32),
                pltpu.SemaphoreType.DMA((2,)),
                pltpu.SemaphoreType.DMA((2,)),
            ],
        ),
        compiler_params=pltpu.CompilerParams(
            dimension_semantics=("arbitrary",), vmem_limit_bytes=VMEM_LIMIT),
        name="attn_sample",
    )(page_table, qpad3, mask3, lnew3, mnew3, vnew3, k_t, v_t)


def _out_proj_sample_kernel(x_ref, a_ref, sga_ref, p_ref, ada_ref, w_ref, o_ref):
    lo = _lane_iota((x_ref.shape[0], LANES)) < HEAD_DIM
    pairs = []
    for hp in range(N_HEADS // 2):
        even = a_ref[:, (2 * hp) * LANES:(2 * hp + 1) * LANES]
        odd = a_ref[:, (2 * hp + 1) * LANES:(2 * hp + 2) * LANES]
        if (2 * hp) // (N_HEADS // KV_HEADS) == 0:
            pairs.append(jnp.where(lo, even, pltpu.roll(odd, HEAD_DIM, axis=1)))
        else:
            pairs.append(jnp.where(lo, pltpu.roll(even, HEAD_DIM, axis=1), odd))
    a = jnp.concatenate(pairs, axis=1)
    ag = (a * sga_ref[...].astype(F32)).astype(BF16)
    y = _dot(ag, w_ref[0:ATTN_W, :]) + _dot(p_ref[...], w_ref[ATTN_W:ATTN_W + POOL_W, :])
    o_ref[...] = x_ref[...] + ada_ref[:, 2 * D_MODEL:3 * D_MODEL] * y


def _out_proj_sample(x, a, sga, pg, ada_s, w_out_b):
    return pl.pallas_call(
        _out_proj_sample_kernel,
        out_shape=jax.ShapeDtypeStruct(x.shape, F32),
        compiler_params=pltpu.CompilerParams(vmem_limit_bytes=VMEM_LIMIT),
        name="out_proj_sample",
    )(x, a, sga, pg, ada_s, w_out_b)


def kernel(x_prompt, x_sample, cache_k, cache_v, cache_kidx, state_pool, page_table, c_prompt, c_sample,
           norm_w, w_ada, b_ada, w_in, q_norm_w, k_norm_w, w_pool, pool_scale, w_out):
    bp, s, _ = x_prompt.shape
    bs = x_sample.shape[0]
    assert w_in.shape[0] == 1 and x_sample.shape[1] == 1, "single layer, single decode token"
    n_phys = cache_k.shape[1]

    w_in_t = jnp.transpose(w_in[0])
    w_out_b = w_out[0].astype(BF16)
    zero_blk = jnp.zeros((POOL_GC, POOL_GC), w_pool.dtype)
    wpool_b = jnp.stack([jnp.block([[w_pool[0, 2 * p], zero_blk], [zero_blk, w_pool[0, 2 * p + 1]]])
                         for p in range(len(POOL_WINDOWS) // 2)]).astype(BF16)
    qnw = jnp.tile(q_norm_w[0], N_HEADS)[None, :] * (HEAD_DIM ** -0.5)
    knw = jnp.tile(k_norm_w[0], KV_HEADS)[None, :]
    nw = norm_w[0][None, :]
    pscale = pool_scale[0][None, :]

    ada_p, ada_s = _ada(c_prompt, c_sample, w_ada, b_ada)
    ada_p = ada_p.reshape(bp, 1, 3 * D_MODEL)

    (qs, k_t_p, kk, v_t_p, vtx, qi, ki_t_p, kik, wit, sga, pg, ulast, w_in_b) = _proj_prompt(
        x_prompt, ada_p, nw, w_in_t, qnw, knw, wpool_b, pscale)
    ag = _attn_prompt(qi, wit, qs, sga, kik, kk, vtx, bp, s)
    y_prompt = _out_proj_prompt(x_prompt, ag, pg, ada_p, w_out_b)

    hist_t = jnp.transpose(state_pool[0], (1, 0, 2))
    (qpad, k_s, v_s, ki_s, qi_s, wi_s, snew, lnew, sga_s, pg_s, pool_s) = _proj_sample(
        x_sample[:, 0, :], ada_s, nw, w_in_b, qnw, knw, wpool_b, pscale, hist_t)
    kidx_t = jnp.transpose(cache_kidx[0], (0, 2, 1))
    k_t = jnp.transpose(cache_k[0], (0, 2, 3, 1)).reshape(n_phys, LANES, PAGE)
    v_t = jnp.transpose(cache_v[0], (0, 2, 3, 1)).reshape(n_phys, LANES, PAGE)
    scores = _score_sample(page_table, qi_s.reshape(bs, IDX_HEADS, IDX_DIM), wi_s.reshape(bs, IDX_HEADS, 1),
                           kidx_t)
    mask, mnew = _select_sample(scores, snew)
    o_s = _attn_sample(
        page_table, qpad.reshape(bs, N_HEADS, LANES), mask,
        lnew[:, :N_HEADS].reshape(bs, N_HEADS, 1),
        jnp.broadcast_to(mnew[:, :1], (bs, N_HEADS)).reshape(bs, N_HEADS, 1),
        v_s.reshape(bs, 1, LANES), k_t, v_t)
    y_sample = _out_proj_sample(x_sample[:, 0, :], o_s.reshape(bs, N_HEADS * LANES), sga_s, pg_s, ada_s, w_out_b)

    to_heads = lambda a: jnp.transpose(a.reshape(bp, KV_HEADS, HEAD_DIM, s), (0, 3, 1, 2))[None]
    return (
        y_prompt,
        y_sample[:, None, :],
        to_heads(k_t_p),
        to_heads(v_t_p),
        jnp.transpose(ki_t_p, (0, 2, 1))[None],
        ulast[:, 1:, :][None],
        k_s.reshape(1, bs, 1, KV_HEADS, HEAD_DIM),
        v_s.reshape(1, bs, 1, KV_HEADS, HEAD_DIM),
        ki_s.reshape(1, bs, 1, IDX_DIM),
        jnp.transpose(pool_s, (1, 0, 2))[None],
    )
```
